```python
import math
import jax, jax.numpy as jnp
from jax import lax
import numpy as np

D_MODEL = 1024
BATCH = 8
SEQ = 4096
DEPTH = 4

PLE_DIM = 256
HEAD_DIM = 64
N_GROUPS = 4
GROUP_WIDTH = D_MODEL // N_GROUPS
N_HEADS_G = GROUP_WIDTH // HEAD_DIM
MOBA_BLOCK = 256
MOBA_TOPK = 3
MOBA_QCHUNK = 64
ATTN_QBLOCK = 128
N_REL_BUCKETS = 32
REL_MAX_EXACT = 16
REL_MAX_DIST = 128
DIFF_QK_DIM = HEAD_DIM // 2
CONV_WIDTH = 4
GDN_CHUNK = 64
GLA_DK = HEAD_DIM // 2
GLA_GATE_RANK = 16
GLA_TAU = 16.0
GLA_CHUNK = 64
D_FF = -(-8 * D_MODEL // (3 * 256)) * 256
EPS = 1e-6

SPLIT_WIDTHS = (
    GROUP_WIDTH, GROUP_WIDTH, GROUP_WIDTH,
    GROUP_WIDTH, GROUP_WIDTH, GROUP_WIDTH,
    GROUP_WIDTH, GROUP_WIDTH, GROUP_WIDTH, GROUP_WIDTH, N_HEADS_G, N_HEADS_G,
    N_HEADS_G * GLA_DK, N_HEADS_G * GLA_DK, GROUP_WIDTH, GROUP_WIDTH, GLA_GATE_RANK,
)
D_IN_PROJ = sum(SPLIT_WIDTHS)

kernel_name = "hybrid_moba_diff_gdn_gla_trunk"


def _rmsnorm(x, g):
    xf = x.astype(jnp.float32)
    y = xf * lax.rsqrt(jnp.mean(xf * xf, axis=-1, keepdims=True) + EPS)
    return (y * g.astype(jnp.float32)).astype(x.dtype)


def _l2norm(x):
    return x * lax.rsqrt(jnp.sum(x * x, axis=-1, keepdims=True) + EPS)


def _heads(x, n):
    b, s, _ = x.shape
    return x.reshape(b, s, n, -1).transpose(0, 2, 1, 3)


def _merge(x):
    b, n, s, d = x.shape
    return x.transpose(0, 2, 1, 3).reshape(b, s, n * d)


def _rel_bucket(dist):
    n = jnp.maximum(dist, 0)
    nf = jnp.maximum(n, 1).astype(jnp.float32)
    large = REL_MAX_EXACT + (jnp.log(nf / REL_MAX_EXACT) / math.log(REL_MAX_DIST / REL_MAX_EXACT)
                             * (N_REL_BUCKETS - REL_MAX_EXACT)).astype(jnp.int32)
    large = jnp.minimum(large, N_REL_BUCKETS - 1)
    return jnp.where(n < REL_MAX_EXACT, n, large)


def _moba_attention(q, k, v, bias_tab):
    b, h, s, dh = q.shape
    nb = -(-s // MOBA_BLOCK)
    pad = nb * MOBA_BLOCK - s
    k_pad = jnp.pad(k, ((0, 0), (0, 0), (0, pad), (0, 0)))
    v_pad = jnp.pad(v, ((0, 0), (0, 0), (0, pad), (0, 0)))
    k_blk = k_pad.reshape(b, h, nb, MOBA_BLOCK, dh)
    v_blk = v_pad.reshape(b, h, nb, MOBA_BLOCK, dh)
    k_mean = jnp.mean(k_blk.astype(jnp.float32), axis=3)
    gate = jnp.einsum('bhsd,bhnd->bhsn', q.astype(jnp.float32), k_mean)
    past = jnp.arange(nb)[None, :] < (jnp.arange(s) // MOBA_BLOCK)[:, None]
    gate = jnp.where(past, gate, -jnp.inf)
    topk = min(MOBA_TOPK, nb)
    _, sel = lax.top_k(gate, topk)
    nq = s // MOBA_QCHUNK
    q_c = q.reshape(b, h, nq, MOBA_QCHUNK, dh).transpose(2, 0, 1, 3, 4)
    sel_c = sel.reshape(b, h, nq, MOBA_QCHUNK, topk).transpose(2, 0, 1, 3, 4)
    bi = jnp.arange(b)[:, None, None, None]
    hi = jnp.arange(h)[None, :, None, None]
    hi5 = jnp.arange(h)[None, :, None, None, None]
    offs = jnp.arange(MOBA_BLOCK)
    scale = dh ** -0.5
    n_sel = topk * MOBA_BLOCK

    def chunk(args):
        qc, sc, c = args
        q_pos = c * MOBA_QCHUNK + jnp.arange(MOBA_QCHUNK)
        own = (c * MOBA_QCHUNK) // MOBA_BLOCK
        kg = k_blk[bi, hi, sc]
        vg = v_blk[bi, hi, sc]
        kpos = sc[..., None] * MOBA_BLOCK + offs
        bias_sel = bias_tab[hi5, _rel_bucket(q_pos[:, None, None] - kpos)]
        s_sel = jnp.einsum('bhqd,bhqnld->bhqnl', qc, kg).astype(jnp.float32) * scale + bias_sel
        s_sel = jnp.where((sc < own)[..., None], s_sel, -jnp.inf).reshape(b, h, MOBA_QCHUNK, n_sel)
        k_own = lax.dynamic_slice_in_dim(k_pad, own * MOBA_BLOCK, MOBA_BLOCK, axis=2)
        v_own = lax.dynamic_slice_in_dim(v_pad, own * MOBA_BLOCK, MOBA_BLOCK, axis=2)
        dist = q_pos[:, None] - (own * MOBA_BLOCK + offs)[None, :]
        s_own = (jnp.einsum('bhqd,bhld->bhql', qc, k_own).astype(jnp.float32) * scale
                 + bias_tab[:, _rel_bucket(dist)])
        s_own = jnp.where(dist >= 0, s_own, -jnp.inf)
        probs = jax.nn.softmax(jnp.concatenate([s_sel, s_own], axis=-1), axis=-1).astype(v.dtype)
        p_sel = probs[..., :n_sel].reshape(b, h, MOBA_QCHUNK, topk, MOBA_BLOCK)
        return (jnp.einsum('bhqnl,bhqnld->bhqd', p_sel, vg)
                + jnp.einsum('bhql,bhld->bhqd', probs[..., n_sel:], v_own))

    out = lax.map(chunk, (q_c, sel_c, jnp.arange(nq)))
    return out.transpose(1, 2, 0, 3, 4).reshape(b, h, s, dh)


def _diff_attention(q1, q2, k1, k2, v, lam, bias_tab):
    b, h, s, dd = q1.shape
    nq = s // ATTN_QBLOCK
    kpos = jnp.arange(s)
    scale = dd ** -0.5

    def to_blocks(x):
        return x.reshape(b, h, nq, ATTN_QBLOCK, dd).transpose(2, 0, 1, 3, 4)

    def block(args):
        a1, a2, c = args
        q_pos = c * ATTN_QBLOCK + jnp.arange(ATTN_QBLOCK)
        dist = q_pos[:, None] - kpos[None, :]
        causal = dist >= 0
        bias = bias_tab[:, _rel_bucket(dist)]

        def probs(qb, kk):
            logits = jnp.einsum('bhqd,bhkd->bhqk', qb, kk).astype(jnp.float32) * scale + bias
            return jax.nn.softmax(jnp.where(causal, logits, -jnp.inf), axis=-1)

        attn = probs(a1, k1) - lam * probs(a2, k2)
        return jnp.einsum('bhqk,bhkd->bhqd', attn.astype(v.dtype), v)

    out = lax.map(block, (to_blocks(q1), to_blocks(q2), jnp.arange(nq)))
    return out.transpose(1, 2, 0, 3, 4).reshape(b, h, s, -1)


def _short_conv(x, w):
    c = x.shape[-1]
    return lax.conv_general_dilated(x, w[:, None, :].astype(x.dtype), window_strides=(1,),
                                    padding=[(CONV_WIDTH - 1, 0)],
                                    dimension_numbers=('NWC', 'WIO', 'NWC'),
                                    feature_group_count=c)


def _gated_delta_rule(q, k, v, g, beta):
    b, h, s, dk = q.shape
    dv = v.shape[-1]
    L = GDN_CHUNK
    nc = s // L
    chunk4 = lambda x: x.reshape(b, h, nc, L, x.shape[-1])
    qc = chunk4(q * dk ** -0.5)
    kc = chunk4(k)
    vc = chunk4(v)
    bc = beta.reshape(b, h, nc, L)
    gc = jnp.cumsum(g.reshape(b, h, nc, L), axis=-1)
    tril = jnp.tril(jnp.ones((L, L), dtype=bool))
    strict = jnp.tril(jnp.ones((L, L), dtype=bool), -1)
    decay = jnp.exp(jnp.where(tril, gc[..., :, None] - gc[..., None, :], -jnp.inf))
    kb = kc * bc[..., None]
    lower = jnp.where(strict, jnp.einsum('bhcid,bhcjd->bhcij', kb, kc) * decay, 0.0)
    eye = jnp.eye(L, dtype=jnp.float32)
    t_inv = lax.linalg.triangular_solve(lower + eye, jnp.broadcast_to(eye, lower.shape),
                                        left_side=True, lower=True, unit_diagonal=True)
    u = t_inv @ (vc * bc[..., None])
    w = t_inv @ (kb * jnp.exp(gc)[..., None])
    qk = jnp.where(tril, jnp.einsum('bhcid,bhcjd->bhcij', qc, kc) * decay, 0.0)
    q_dec = qc * jnp.exp(gc)[..., None]
    k_dec = kc * jnp.exp(gc[..., -1:] - gc)[..., None]
    g_last = jnp.exp(gc[..., -1])

    def step(state, xs):
        u_c, w_c, qk_c, qd_c, kd_c, gl_c = xs
        v_new = u_c - w_c @ state
        o = qd_c @ state + qk_c @ v_new
        state = state * gl_c[..., None, None] + jnp.swapaxes(kd_c, -1, -2) @ v_new
        return state, o

    xs = tuple(jnp.moveaxis(a, 2, 0) for a in (u, w, qk, q_dec, k_dec, g_last))
    _, o = lax.scan(step, jnp.zeros((b, h, dk, dv), jnp.float32), xs)
    return jnp.moveaxis(o, 0, 2).reshape(b, h, s, dv)


def _gla_chunked(q, k, v, log_a):
    b, h, s, dk = q.shape
    L = GLA_CHUNK
    nc = s // L
    tril = jnp.tril(jnp.ones((L, L), dtype=bool))

    def to_chunks(x):
        return jnp.moveaxis(x.reshape(b, h, nc, L, x.shape[-1]), 2, 0)

    qc = to_chunks(q * dk ** -0.5)
    kc = to_chunks(k)
    vc = to_chunks(v)
    bc = jnp.cumsum(to_chunks(log_a), axis=-2)

    def step(state, xs):
        q_c, k_c, v_c, b_c = xs
        rel = jnp.where(tril[:, :, None], b_c[..., :, None, :] - b_c[..., None, :, :], -jnp.inf)
        a_intra = jnp.einsum('bhid,bhjd,bhijd->bhij', q_c, k_c, jnp.exp(rel))
        b_last = b_c[..., -1:, :]
        o = (q_c * jnp.exp(b_c)) @ state + a_intra @ v_c
        state = (state * jnp.exp(b_last[..., 0, :])[..., None]
                 + jnp.swapaxes(k_c * jnp.exp(b_last - b_c), -1, -2) @ v_c)
        return state, o

    state0 = jnp.zeros((b, h, dk, v.shape[-1]), jnp.float32)
    _, o = lax.scan(step, state0, (qc, kc, vc, bc))
    return jnp.moveaxis(o, 0, 2).reshape(b, h, s, -1)


def setup_inputs(seed: int = 0) -> dict:
    key = jax.random.key(seed)
    ks = jax.random.split(key, 24)
    f32 = jnp.float32

    def nrm(k, shape, scale):
        return jax.random.normal(k, shape, f32) * scale

    def gain(k, shape):
        return 1.0 + 0.02 * jax.random.normal(k, shape, f32)

    dt = jnp.exp(jax.random.uniform(ks[9], (DEPTH, N_HEADS_G), f32, math.log(1e-3), math.log(1e-1)))
    return {
        "x": nrm(ks[0], (BATCH, SEQ, D_MODEL), 1.0),
        "p": nrm(ks[1], (DEPTH, BATCH, SEQ, PLE_DIM), 1.0),
        "norm_mix": gain(ks[2], (DEPTH, D_MODEL)),
        "w_in": nrm(ks[3], (DEPTH, D_MODEL, D_IN_PROJ), D_MODEL ** -0.5),
        "rel_bias": nrm(ks[4], (N_REL_BUCKETS, 2 * N_HEADS_G), 0.3),
        "diff_lambda": nrm(ks[5], (DEPTH, 4, DIFF_QK_DIM), 0.1),
        "diff_norm": gain(ks[6], (DEPTH, HEAD_DIM)),
        "gdn_conv": nrm(ks[7], (DEPTH, CONV_WIDTH, 3 * GROUP_WIDTH), CONV_WIDTH ** -0.5),
        "gdn_a_log": jnp.log(jax.random.uniform(ks[8], (DEPTH, N_HEADS_G), f32, 1.0, 16.0)),
        "gdn_dt_bias": dt + jnp.log(-jnp.expm1(-dt)),
        "gdn_norm": gain(ks[10], (DEPTH, HEAD_DIM)),
        "gla_w_alpha": nrm(ks[11], (DEPTH, GLA_GATE_RANK, N_HEADS_G * GLA_DK), GLA_GATE_RANK ** -0.5),
        "gla_b_alpha": nrm(ks[12], (DEPTH, N_HEADS_G * GLA_DK), 0.1),
        "gla_norm": gain(ks[13], (DEPTH, HEAD_DIM)),
        "w_out": nrm(ks[14], (DEPTH, D_MODEL, D_MODEL), D_MODEL ** -0.5),
        "norm_ffn": gain(ks[15], (DEPTH, D_MODEL)),
        "w_gate": nrm(ks[16], (DEPTH, D_MODEL, D_FF), D_MODEL ** -0.5),
        "w_up": nrm(ks[17], (DEPTH, D_MODEL, D_FF), D_MODEL ** -0.5),
        "w_down": nrm(ks[18], (DEPTH, D_FF, D_MODEL), D_FF ** -0.5),
        "norm_ple": gain(ks[19], (DEPTH, D_MODEL)),
        "w_ple_gate": nrm(ks[20], (DEPTH, D_MODEL, D_MODEL), D_MODEL ** -0.5),
        "w_ple_proj": nrm(ks[21], (DEPTH, PLE_DIM, D_MODEL), PLE_DIM ** -0.5),
        "final_norm": gain(ks[22], (D_MODEL,)),
    }


def reference(x, p, norm_mix, w_in, rel_bias, diff_lambda, diff_norm, gdn_conv, gdn_a_log,
              gdn_dt_bias, gdn_norm, gla_w_alpha, gla_b_alpha, gla_norm, w_out, norm_ffn,
              w_gate, w_up, w_down, norm_ple, w_ple_gate, w_ple_proj, final_norm):
    b, s, _ = x.shape
    hg = N_HEADS_G
    f32 = jnp.float32
    split_at = np.cumsum(SPLIT_WIDTHS)[:-1].tolist()
    bias_a = rel_bias[:, :hg].T
    bias_b = rel_bias[:, hg:].T
    h = x
    for l in range(DEPTH):
        xn = _rmsnorm(h, norm_mix[l])
        proj = xn @ w_in[l]
        (a_q, a_k, a_v, b_q, b_k, b_v, c_q, c_k, c_v, c_z, c_a, c_b,
         d_q, d_k, d_v, d_g, d_lr) = jnp.split(proj, split_at, axis=-1)

        y_a = _merge(_moba_attention(_heads(a_q, hg), _heads(a_k, hg), _heads(a_v, hg), bias_a))

        bq = b_q.reshape(b, s, hg, 2, DIFF_QK_DIM).transpose(0, 2, 3, 1, 4)
        bk = b_k.reshape(b, s, hg, 2, DIFF_QK_DIM).transpose(0, 2, 3, 1, 4)
        lam_init = 0.8 - 0.6 * math.exp(-0.3 * l)
        lq1, lk1, lq2, lk2 = diff_lambda[l].astype(f32)
        lam = jnp.exp(jnp.sum(lq1 * lk1)) - jnp.exp(jnp.sum(lq2 * lk2)) + lam_init
        o_b = _diff_attention(bq[:, :, 0], bq[:, :, 1], bk[:, :, 0], bk[:, :, 1],
                              _heads(b_v, hg), lam, bias_b)
        y_b = _merge(_rmsnorm(o_b, diff_norm[l]) * (1.0 - lam_init))

        conv = jax.nn.silu(_short_conv(jnp.concatenate([c_q, c_k, c_v], axis=-1), gdn_conv[l]))
        cq, ck, cv = jnp.split(conv, 3, axis=-1)
        cq = _l2norm(_heads(cq, hg).astype(f32))
        ck = _l2norm(_heads(ck, hg).astype(f32))
        g = -jnp.exp(gdn_a_log[l].astype(f32)) * jax.nn.softplus(c_a.astype(f32) + gdn_dt_bias[l])
        beta = jax.nn.sigmoid(c_b.astype(f32))
        o_c = _gated_delta_rule(cq, ck, _heads(cv, hg).astype(f32),
                                g.transpose(0, 2, 1), beta.transpose(0, 2, 1))
        y_c = _merge(_rmsnorm(o_c, gdn_norm[l]) * jax.nn.silu(_heads(c_z, hg)))

        log_a = jax.nn.log_sigmoid((d_lr @ gla_w_alpha[l] + gla_b_alpha[l]).astype(f32)) / GLA_TAU
        o_d = _gla_chunked(_heads(d_q, hg).astype(f32), _heads(d_k, hg).astype(f32),
                           _heads(d_v, hg).astype(f32), _heads(log_a, hg))
        y_d = _merge(_rmsnorm(o_d, gla_norm[l]) * jax.nn.silu(_heads(d_g, hg)))

        mixed = jnp.concatenate([y_a.astype(h.dtype), y_b.astype(h.dtype),
                                 y_c.astype(h.dtype), y_d.astype(h.dtype)], axis=-1)
        h = h + mixed @ w_out[l]

        hn = _rmsnorm(h, norm_ffn[l])
        h = h + (jax.nn.silu(hn @ w_gate[l]) * (hn @ w_up[l])) @ w_down[l]

        gate = jax.nn.sigmoid(_rmsnorm(h, norm_ple[l]) @ w_ple_gate[l])
        h = h + gate * (p[l] @ w_ple_proj[l])
    return _rmsnorm(h, final_norm)
```

```python
import functools
import math

import numpy as np
import jax
import jax.numpy as jnp
from jax import lax
from jax.experimental import pallas as pl
from jax.experimental.pallas import tpu as pltpu

F32 = jnp.float32
BF16 = jnp.bfloat16

HEAD_DIM = 64
N_HEADS_G = 4
GROUP_WIDTH = HEAD_DIM * N_HEADS_G
MOBA_BLOCK = 256
MOBA_TOPK = 3
N_REL_BUCKETS = 32
REL_MAX_EXACT = 16
REL_MAX_DIST = 128
DIFF_QK_DIM = HEAD_DIM // 2
CONV_WIDTH = 4
CHUNK = 64
GLA_DK = HEAD_DIM // 2
GLA_GATE_RANK = 16
GLA_TAU = 16.0
EPS = 1e-6

LANES = 128
ATTN_TILE = 256
SEQ_TILE = 1024
NEG = -1e30
VMEM_LIMIT = 56 * 1024 * 1024

N_ATTN = 6 * GROUP_WIDTH
N_GDN = 4 * GROUP_WIDTH
N_GLA = 2 * N_HEADS_G * GLA_DK + 2 * GROUP_WIDTH
N_MISC = LANES
MISC_A, MISC_B, MISC_LR = 0, N_HEADS_G, 2 * N_HEADS_G

_NT = (((1,), (1,)), ((), ()))
_TN = (((0,), (0,)), ((), ()))


def _dot(a, b):
    return jnp.dot(a, b, preferred_element_type=F32)


def _dot_nt(a, b):
    return lax.dot_general(a, b, _NT, preferred_element_type=F32)


def _dot_tn(a, b):
    return lax.dot_general(a, b, _TN, preferred_element_type=F32)


def _split_bf16(x, n):
    parts = []
    r = x
    for _ in range(n):
        hi = r.astype(BF16)
        parts.append(hi)
        r = r - hi.astype(F32)
    return parts


def _dot_exact_rhs(x, w, n=2):
    acc = None
    for part in _split_bf16(x, n):
        t = _dot(part, w)
        acc = t if acc is None else acc + t
    return acc


def _dot_exact_lhs(w, x, n=3):
    acc = None
    for part in _split_bf16(x, n):
        t = _dot(w, part)
        acc = t if acc is None else acc + t
    return acc


def _rms(x, g):
    return x * lax.rsqrt(jnp.mean(x * x, axis=-1, keepdims=True) + EPS) * g


def _sigmoid(x):
    return 1.0 / (1.0 + jnp.exp(-x))


def _silu(x):
    return x * _sigmoid(x)


def _inproj_kernel(x_ref, g_ref, w_ref, scale_ref, attn_ref, gdn_ref, gla_ref, misc_ref, kmean_ref):
    tm = x_ref.shape[0]
    xn = _rms(x_ref[...], g_ref[...]).astype(BF16)
    attn = _dot(xn, w_ref[:, 0:N_ATTN])
    k_moba = attn[:, GROUP_WIDTH:2 * GROUP_WIDTH]
    kmean_ref[0] = jnp.mean(k_moba.reshape(tm // MOBA_BLOCK, MOBA_BLOCK, GROUP_WIDTH), axis=1)
    attn_ref[...] = (attn * scale_ref[...]).astype(BF16)
    o = N_ATTN
    gdn_ref[...] = _dot(xn, w_ref[:, o:o + N_GDN])
    o += N_GDN
    gla_ref[...] = _dot(xn, w_ref[:, o:o + N_GLA])
    o += N_GLA
    misc_ref[...] = _dot(xn, w_ref[:, o:o + N_MISC])


def _inproj(h2d, g, w, scale, tm):
    m, d = h2d.shape
    n_all = w.shape[1]
    const = lambda i: (0, 0)
    return pl.pallas_call(
        _inproj_kernel,
        grid=(m // tm,),
        in_specs=[
            pl.BlockSpec((tm, d), lambda i: (i, 0)),
            pl.BlockSpec((1, d), const),
            pl.BlockSpec((d, n_all), const),
            pl.BlockSpec((1, N_ATTN), const),
        ],
        out_specs=[
            pl.BlockSpec((tm, N_ATTN), lambda i: (i, 0)),
            pl.BlockSpec((tm, N_GDN), lambda i: (i, 0)),
            pl.BlockSpec((tm, N_GLA), lambda i: (i, 0)),
            pl.BlockSpec((tm, N_MISC), lambda i: (i, 0)),
            pl.BlockSpec((1, tm // MOBA_BLOCK, GROUP_WIDTH), lambda i: (i, 0, 0)),
        ],
        out_shape=[
            jax.ShapeDtypeStruct((m, N_ATTN), BF16),
            jax.ShapeDtypeStruct((m, N_GDN), F32),
            jax.ShapeDtypeStruct((m, N_GLA), F32),
            jax.ShapeDtypeStruct((m, N_MISC), F32),
            jax.ShapeDtypeStruct((m // tm, tm // MOBA_BLOCK, GROUP_WIDTH), F32),
        ],
        compiler_params=pltpu.CompilerParams(
            dimension_semantics=("arbitrary",), vmem_limit_bytes=VMEM_LIMIT),
        name="inproj",
    )(h2d, g, w, scale)


def _lane_mask(lo, width):
    lane = lax.broadcasted_iota(jnp.int32, (1, LANES), 1)
    return (lane >= lo) & (lane < lo + width)


def _softmax_first(s, v):
    m = jnp.max(s, axis=-1, keepdims=True)
    p = jnp.exp(s - m)
    l = jnp.sum(p, axis=-1, keepdims=True)
    acc = _dot(p.astype(BF16), v)
    return m, l, acc


def _softmax_next(carry, s, v):
    m, l, acc = carry
    m_new = jnp.maximum(m, jnp.max(s, axis=-1, keepdims=True))
    alpha = jnp.exp(m - m_new)
    p = jnp.exp(s - m_new)
    l = alpha * l + jnp.sum(p, axis=-1, keepdims=True)
    acc = alpha * acc + _dot(p.astype(BF16), v)
    return m_new, l, acc


def _moba_kernel(q_ref, k_ref, v_ref, kmean_ref, bias_ref, o_ref, pen_ref):
    t = ATTN_TILE
    qi = pl.program_id(2)
    nblk = kmean_ref.shape[1]
    q = q_ref[0]
    zero = jnp.zeros_like(q)

    row = lax.broadcasted_iota(jnp.int32, (nblk, t), 0)
    eye = (lax.broadcasted_iota(jnp.int32, (t, t), 0)
           == lax.broadcasted_iota(jnp.int32, (t, t), 1)).astype(BF16)
    qms = []
    for hh in range(2):
        hmask = _lane_mask(HEAD_DIM * hh, HEAD_DIM)
        qm = jnp.where(hmask, q, zero)
        qms.append(qm)
        gate = _dot_nt(kmean_ref[0].astype(BF16), qm)
        gate = jnp.where(row < qi, gate, -jnp.inf)
        sel_t = jnp.zeros((nblk, t), F32)
        for j in range(nblk):
            gj = gate[j:j + 1, :]
            beats = jnp.where(gate > gj, 1.0, jnp.where((gate == gj) & (row < j), 1.0, 0.0))
            rank = jnp.sum(beats, axis=0, keepdims=True)
            sel_j = jnp.where(rank < MOBA_TOPK, jnp.where(j < qi, 1.0, 0.0), 0.0)
            sel_t = jnp.where(row == j, sel_j, sel_t)
        sel_pad = jnp.concatenate([sel_t, jnp.zeros((LANES - nblk, t), F32)], axis=0).astype(BF16)
        sel = _dot_nt(eye, sel_pad)
        for j in range(nblk):
            pen = jnp.where(sel[:, j:j + 1] > 0.5, 0.0, NEG)
            pen_ref[hh, j] = jnp.broadcast_to(pen, (t, LANES))

    def kv(j):
        start = pl.multiple_of(j * t, t)
        return k_ref[0, pl.ds(start, t), :], v_ref[0, pl.ds(start, t), :]

    k, v = kv(qi)
    carry = tuple(_softmax_first(_dot_nt(qms[hh], k) + bias_ref[hh, 0], v) for hh in range(2))

    jp = jnp.maximum(qi - 1, 0)
    k, v = kv(jp)
    new = []
    for hh in range(2):
        pen = pen_ref[hh, jp]
        s = _dot_nt(qms[hh], k) + bias_ref[hh, 1] + jnp.concatenate([pen, pen], axis=1)
        new.append(_softmax_next(carry[hh], s, v))
    carry = tuple(new)

    def far(j, carry):
        k, v = kv(j)
        new = []
        for hh in range(2):
            pen = pen_ref[hh, j]
            s = _dot_nt(qms[hh], k) + jnp.concatenate([pen, pen], axis=1)
            new.append(_softmax_next(carry[hh], s, v))
        return tuple(new)

    carry = lax.fori_loop(0, qi - 1, far, carry)
    outs = [acc / l for (_, l, acc) in carry]
    o_ref[0] = jnp.where(_lane_mask(0, HEAD_DIM), outs[0], outs[1]).astype(o_ref.dtype)


def _moba(attn, kmean, bias):
    b, s, _ = attn.shape
    t = ATTN_TILE
    nq = s // t
    nblk = kmean.shape[1]
    qcol, kcol, vcol = 0, GROUP_WIDTH // LANES, 2 * GROUP_WIDTH // LANES
    return pl.pallas_call(
        _moba_kernel,
        grid=(b, 2, nq),
        in_specs=[
            pl.BlockSpec((1, t, LANES), lambda bi, p, qi: (bi, qi, qcol + p)),
            pl.BlockSpec((1, s, LANES), lambda bi, p, qi: (bi, 0, kcol + p)),
            pl.BlockSpec((1, s, LANES), lambda bi, p, qi: (bi, 0, vcol + p)),
            pl.BlockSpec((1, nblk, LANES), lambda bi, p, qi: (bi, 0, p)),
            pl.BlockSpec((2, 2, t, t), lambda bi, p, qi: (p, 0, 0, 0)),
        ],
        out_specs=pl.BlockSpec((1, t, LANES), lambda bi, p, qi: (bi, qi, p)),
        out_shape=jax.ShapeDtypeStruct((b, s, GROUP_WIDTH), BF16),
        scratch_shapes=[pltpu.VMEM((2, nblk, t, LANES), F32)],
        compiler_params=pltpu.CompilerParams(
            dimension_semantics=("arbitrary", "arbitrary", "arbitrary"),
            vmem_limit_bytes=VMEM_LIMIT),
        name="moba",
    )(attn, attn, attn, kmean, bias)


def _diff_kernel(lam_ref, q_ref, k_ref, v_ref, bias_ref, gnorm_ref, o_ref):
    t = ATTN_TILE
    qi = pl.program_id(2)
    q = q_ref[0]
    zero = jnp.zeros_like(q)
    qms = [jnp.where(_lane_mask(HEAD_DIM * hh + DIFF_QK_DIM * mm, DIFF_QK_DIM), q, zero)
           for hh in range(2) for mm in range(2)]

    def kv(j):
        start = pl.multiple_of(j * t, t)
        return k_ref[0, pl.ds(start, t), :], v_ref[0, pl.ds(start, t), :]

    k, v = kv(qi)
    carry = tuple(_softmax_first(_dot_nt(qms[c], k) + bias_ref[c // 2, 0], v) for c in range(4))

    jp = jnp.maximum(qi - 1, 0)
    k, v = kv(jp)
    off = jnp.where(qi >= 1, 0.0, NEG)
    carry = tuple(_softmax_next(carry[c], _dot_nt(qms[c], k) + (bias_ref[c // 2, 1] + off), v)
                  for c in range(4))

    def far(j, carry):
        k, v = kv(j)
        return tuple(_softmax_next(carry[c], _dot_nt(qms[c], k), v) for c in range(4))

    carry = lax.fori_loop(0, qi - 1, far, carry)

    lam_p = lam_ref[...]
    lam_init = lam_p[4:5, 0:1]
    lam = (jnp.exp(jnp.sum(lam_p[0:1] * lam_p[1:2], axis=-1, keepdims=True))
           - jnp.exp(jnp.sum(lam_p[2:3] * lam_p[3:4], axis=-1, keepdims=True)) + lam_init)
    outs = []
    for hh in range(2):
        _, l1, a1 = carry[2 * hh]
        _, l2, a2 = carry[2 * hh + 1]
        o = a1 / l1 - lam * (a2 / l2)
        hmask = _lane_mask(HEAD_DIM * hh, HEAD_DIM)
        ms = jnp.sum(jnp.where(hmask, o * o, 0.0), axis=-1, keepdims=True) * (1.0 / HEAD_DIM)
        outs.append(o * lax.rsqrt(ms + EPS))
    y = jnp.where(_lane_mask(0, HEAD_DIM), outs[0], outs[1]) * gnorm_ref[...] * (1.0 - lam_init)
    o_ref[0] = y.astype(o_ref.dtype)


def _diff(attn, lam_p, bias, gnorm):
    b, s, _ = attn.shape
    t = ATTN_TILE
    nq = s // t
    base = 3 * GROUP_WIDTH // LANES
    qcol, kcol, vcol = base, base + GROUP_WIDTH // LANES, base + 2 * GROUP_WIDTH // LANES
    return pl.pallas_call(
        _diff_kernel,
        grid=(b, 2, nq),
        in_specs=[
            pl.BlockSpec(lam_p.shape, lambda bi, p, qi: (0, 0)),
            pl.BlockSpec((1, t, LANES), lambda bi, p, qi: (bi, qi, qcol + p)),
            pl.BlockSpec((1, s, LANES), lambda bi, p, qi: (bi, 0, kcol + p)),
            pl.BlockSpec((1, s, LANES), lambda bi, p, qi: (bi, 0, vcol + p)),
            pl.BlockSpec((2, 2, t, t), lambda bi, p, qi: (p, 0, 0, 0)),
            pl.BlockSpec((1, LANES), lambda bi, p, qi: (0, 0)),
        ],
        out_specs=pl.BlockSpec((1, t, LANES), lambda bi, p, qi: (bi, qi, p)),
        out_shape=jax.ShapeDtypeStruct((b, s, GROUP_WIDTH), BF16),
        compiler_params=pltpu.CompilerParams(
            dimension_semantics=("arbitrary", "arbitrary", "arbitrary"),
            vmem_limit_bytes=VMEM_LIMIT),
        name="diff_attn",
    )(lam_p, attn, attn, attn, bias, gnorm)


def _head_of_lane(n_lanes, width):
    return lax.broadcasted_iota(jnp.int32, (1, n_lanes), 1) // width


def _block_rows(x, lane_head, n_heads=N_HEADS_G):
    zero = jnp.zeros_like(x)
    return jnp.concatenate([jnp.where(lane_head == h, x, zero) for h in range(n_heads)], axis=0)


def _gdn_kernel(x_ref, misc_ref, conv_ref, hp_ref, gnorm_ref, o_ref, s_ref, tail_ref, g_ref, beta_ref):
    L = CHUNK
    W = GROUP_WIDTH
    seq = x_ref.shape[1]
    n_chunks = seq // L
    head_w = _head_of_lane(W, HEAD_DIM)
    head_all = _head_of_lane(LANES, 1)
    ri = lax.broadcasted_iota(jnp.int32, (L, W), 0)
    cj = lax.broadcasted_iota(jnp.int32, (L, W), 1) % HEAD_DIM
    lower = ri >= cj
    strict = ri > cj
    ident = jnp.where(ri == cj, 1.0, 0.0)
    ones_blk = (lax.broadcasted_iota(jnp.int32, (W, W), 0) // HEAD_DIM
                == lax.broadcasted_iota(jnp.int32, (W, W), 1) // HEAD_DIM)
    ones_seg = ones_blk.astype(BF16)
    tril = (lax.broadcasted_iota(jnp.int32, (L, L), 0)
            >= lax.broadcasted_iota(jnp.int32, (L, L), 1)).astype(BF16)

    hp = hp_ref[...]
    misc = misc_ref[0]
    sp_in = misc + hp[1:2]
    softplus = jnp.maximum(sp_in, 0.0) + jnp.log(1.0 + jnp.exp(-jnp.abs(sp_in)))
    g_tok = -jnp.exp(hp[0:1]) * softplus
    b_tok = _sigmoid(misc)
    exp_g = (lax.broadcasted_iota(jnp.int32, (LANES, W), 0) - MISC_A
             == lax.broadcasted_iota(jnp.int32, (LANES, W), 1) // HEAD_DIM).astype(BF16)
    exp_b = (lax.broadcasted_iota(jnp.int32, (LANES, W), 0) - MISC_B
             == lax.broadcasted_iota(jnp.int32, (LANES, W), 1) // HEAD_DIM).astype(BF16)
    del head_all
    g_ref[...] = _dot_exact_rhs(g_tok, exp_g, 3)
    beta_ref[...] = _dot_exact_rhs(b_tok, exp_b, 3)

    @pl.when(pl.program_id(1) == 0)
    def _():
        s_ref[...] = jnp.zeros_like(s_ref)
        tail_ref[...] = jnp.zeros_like(tail_ref)

    cw = conv_ref[...]

    def chunk(c, _):
        r0 = pl.multiple_of(c * L, L)
        x = x_ref[0, pl.ds(r0, L), :]
        qkv = x[:, 0:3 * W]
        xx = jnp.concatenate([tail_ref[...], qkv], axis=0)
        tail_ref[...] = qkv[L - 8:L, :]
        conv = cw[CONV_WIDTH - 1:CONV_WIDTH] * qkv
        for i in range(CONV_WIDTH - 1):
            lo = 8 - (CONV_WIDTH - 1) + i
            conv = conv + cw[i:i + 1] * xx[lo:lo + L, :]
        conv = _silu(conv)
        q = conv[:, 0:W]
        k = conv[:, W:2 * W]
        v = conv[:, 2 * W:3 * W]
        q = q * lax.rsqrt(_dot_exact_rhs(q * q, ones_seg) + EPS) * (HEAD_DIM ** -0.5)
        k = k * lax.rsqrt(_dot_exact_rhs(k * k, ones_seg) + EPS)

        beta = beta_ref[pl.ds(r0, L), :]
        gc = _dot_exact_lhs(tril, g_ref[pl.ds(r0, L), :])
        egc = jnp.exp(gc)
        gc_col = jnp.sum(gc * ident, axis=0, keepdims=True)
        gc_last = gc[L - 1:L, :]
        decay = jnp.exp(jnp.where(lower, gc - gc_col, -jnp.inf))

        kb = k * beta
        k_blk = _block_rows(k.astype(BF16), head_w)
        both = _dot_nt(jnp.concatenate([kb, q], axis=0).astype(BF16), k_blk)
        a_mat = jnp.where(strict, both[0:L] * decay, 0.0)
        qk = jnp.where(lower, both[L:2 * L] * decay, 0.0)

        p = -a_mat
        t_inv = ident + p
        p = _dot(p.astype(BF16), _block_rows(p.astype(BF16), head_w))
        for _ in range(4):
            p_blk = _block_rows(p.astype(BF16), head_w)
            prod = _dot(jnp.concatenate([t_inv, p], axis=0).astype(BF16), p_blk)
            t_inv = t_inv + prod[0:L]
            p = prod[L:2 * L]
        t_inv = t_inv + _dot(t_inv.astype(BF16), _block_rows(p.astype(BF16), head_w))
        t_bf = t_inv.astype(BF16)

        u = _dot(t_bf, _block_rows((v * beta).astype(BF16), head_w))
        w = _dot(t_bf, _block_rows((kb * egc).astype(BF16), head_w))

        state = s_ref[...]
        state_bf = state.astype(BF16)
        ws_qs = _dot(jnp.concatenate([w, q * egc], axis=0).astype(BF16), state_bf)
        v_new = u - ws_qs[0:L]
        o = ws_qs[L:2 * L] + _dot(qk.astype(BF16), _block_rows(v_new.astype(BF16), head_w))
        k_dec = k * jnp.exp(gc_last - gc)
        upd = _dot_tn(k_dec.astype(BF16), v_new.astype(BF16))
        s_ref[...] = state * jnp.exp(gc_last) + jnp.where(ones_blk, upd, 0.0)

        ms = _dot_exact_rhs(o * o, ones_seg) * (1.0 / HEAD_DIM)
        y = o * lax.rsqrt(ms + EPS) * gnorm_ref[...] * _silu(x[:, 3 * W:4 * W])
        o_ref[0, pl.ds(r0, L), :] = y.astype(o_ref.dtype)
        return 0

    lax.fori_loop(0, n_chunks, chunk, 0)


def _gdn(gdn, misc, conv_w, hp, gnorm):
    b, s, _ = gdn.shape
    ts = min(s, SEQ_TILE)
    const = lambda bi, si: (0, 0)
    return pl.pallas_call(
        _gdn_kernel,
        grid=(b, s // ts),
        in_specs=[
            pl.BlockSpec((1, ts, N_GDN), lambda bi, si: (bi, si, 0)),
            pl.BlockSpec((1, ts, N_MISC), lambda bi, si: (bi, si, 0)),
            pl.BlockSpec(conv_w.shape, const),
            pl.BlockSpec(hp.shape, const),
            pl.BlockSpec((1, GROUP_WIDTH), const),
        ],
        out_specs=pl.BlockSpec((1, ts, GROUP_WIDTH), lambda bi, si: (bi, si, 0)),
        out_shape=jax.ShapeDtypeStruct((b, s, GROUP_WIDTH), BF16),
        scratch_shapes=[
            pltpu.VMEM((GROUP_WIDTH, GROUP_WIDTH), F32),
            pltpu.VMEM((8, 3 * GROUP_WIDTH), F32),
            pltpu.VMEM((ts, GROUP_WIDTH), F32),
            pltpu.VMEM((ts, GROUP_WIDTH), F32),
        ],
        compiler_params=pltpu.CompilerParams(
            dimension_semantics=("arbitrary", "arbitrary"), vmem_limit_bytes=VMEM_LIMIT),
        name="gdn",
    )(gdn, misc, conv_w, hp, gnorm)


def _gla_kernel(x_ref, misc_ref, walpha_ref, balpha_ref, gnorm_ref, o_ref, s_ref, la_ref):
    L = CHUNK
    W = GROUP_WIDTH
    KW = N_HEADS_G * GLA_DK
    seq = x_ref.shape[1]
    n_chunks = seq // L
    head_k = _head_of_lane(KW, GLA_DK)
    head_v = _head_of_lane(W, HEAD_DIM)
    ri = lax.broadcasted_iota(jnp.int32, (L, W), 0)
    cj = lax.broadcasted_iota(jnp.int32, (L, W), 1) % HEAD_DIM
    lower = ri >= cj
    ones_blk = (lax.broadcasted_iota(jnp.int32, (W, W), 0) // HEAD_DIM
                == lax.broadcasted_iota(jnp.int32, (W, W), 1) // HEAD_DIM).astype(BF16)
    state_mask = (lax.broadcasted_iota(jnp.int32, (W, KW), 0) // HEAD_DIM
                  == lax.broadcasted_iota(jnp.int32, (W, KW), 1) // GLA_DK)
    tril = (lax.broadcasted_iota(jnp.int32, (L, L), 0)
            >= lax.broadcasted_iota(jnp.int32, (L, L), 1)).astype(BF16)

    pre = _dot(misc_ref[0].astype(BF16), walpha_ref[...]) + balpha_ref[...]
    log_sig = jnp.minimum(pre, 0.0) - jnp.log(1.0 + jnp.exp(-jnp.abs(pre)))
    la_ref[...] = log_sig * (1.0 / GLA_TAU)

    @pl.when(pl.program_id(1) == 0)
    def _():
        s_ref[...] = jnp.zeros_like(s_ref)

    def chunk(c, _):
        r0 = pl.multiple_of(c * L, L)
        x = x_ref[0, pl.ds(r0, L), :]
        q = x[:, 0:KW] * (GLA_DK ** -0.5)
        k = x[:, KW:2 * KW]
        v = x[:, 2 * KW:2 * KW + W]
        bc = _dot_exact_lhs(tril, la_ref[pl.ds(r0, L), :])
        b_last = bc[L - 1:L, :]
        qe = (q * jnp.exp(bc)).astype(BF16)
        ke = (k * jnp.exp(-bc)).astype(BF16)
        a_mat = jnp.where(lower, _dot_nt(qe, _block_rows(ke, head_k)), 0.0)
        state_t = s_ref[...]
        o = _dot_nt(qe, state_t.astype(BF16)) + _dot(a_mat.astype(BF16),
                                                    _block_rows(v.astype(BF16), head_v))
        k_dec = (k * jnp.exp(b_last - bc)).astype(BF16)
        upd = _dot_tn(v.astype(BF16), k_dec)
        s_ref[...] = state_t * jnp.exp(b_last) + jnp.where(state_mask, upd, 0.0)

        ms = _dot_exact_rhs(o * o, ones_blk) * (1.0 / HEAD_DIM)
        y = o * lax.rsqrt(ms + EPS) * gnorm_ref[...] * _silu(x[:, 2 * KW + W:2 * KW + 2 * W])
        o_ref[0, pl.ds(r0, L), :] = y.astype(o_ref.dtype)
        return 0

    lax.fori_loop(0, n_chunks, chunk, 0)


def _gla(gla, misc, walpha, balpha, gnorm):
    b, s, _ = gla.shape
    ts = min(s, SEQ_TILE)
    const = lambda bi, si: (0, 0)
    return pl.pallas_call(
        _gla_kernel,
        grid=(b, s // ts),
        in_specs=[
            pl.BlockSpec((1, ts, N_GLA), lambda bi, si: (bi, si, 0)),
            pl.BlockSpec((1, ts, N_MISC), lambda bi, si: (bi, si, 0)),
            pl.BlockSpec(walpha.shape, const),
            pl.BlockSpec(balpha.shape, const),
            pl.BlockSpec((1, GROUP_WIDTH), const),
        ],
        out_specs=pl.BlockSpec((1, ts, GROUP_WIDTH), lambda bi, si: (bi, si, 0)),
        out_shape=jax.ShapeDtypeStruct((b, s, GROUP_WIDTH), BF16),
        scratch_shapes=[
            pltpu.VMEM((GROUP_WIDTH, N_HEADS_G * GLA_DK), F32),
            pltpu.VMEM((ts, N_HEADS_G * GLA_DK), F32),
        ],
        compiler_params=pltpu.CompilerParams(
            dimension_semantics=("arbitrary", "arbitrary"), vmem_limit_bytes=VMEM_LIMIT),
        name="gla",
    )(gla, misc, walpha, balpha, gnorm)


def _ffn_chunks(d_ff):
    step = 4 * GROUP_WIDTH
    return [(lo, min(step, d_ff - lo)) for lo in range(0, d_ff, step)]


def _mlp_kernel(h_ref, ya_ref, yb_ref, yc_ref, yd_ref, wout_ref, gffn_ref, wg_ref, wu_ref, wd_ref,
                gple_ref, wpg_ref, p_ref, wpp_ref, gfin_ref, o_ref, act_ref, *, final):
    mixed = jnp.concatenate([ya_ref[...], yb_ref[...], yc_ref[...], yd_ref[...]], axis=-1)
    h = h_ref[...] + _dot(mixed, wout_ref[...])
    hn = _rms(h, gffn_ref[...]).astype(BF16)
    for lo, width in _ffn_chunks(wg_ref.shape[1]):
        gate = _dot(hn, wg_ref[:, lo:lo + width])
        up = _dot(hn, wu_ref[:, lo:lo + width])
        act_ref[:, lo:lo + width] = (_silu(gate) * up).astype(BF16)
    h = h + _dot(act_ref[...], wd_ref[...])
    gate = _sigmoid(_dot(_rms(h, gple_ref[...]).astype(BF16), wpg_ref[...]))
    h = h + gate * _dot(p_ref[...].astype(BF16), wpp_ref[...])
    if final:
        h = _rms(h, gfin_ref[...])
    o_ref[...] = h


def _mlp(h2d, ys, wout, gffn, wg, wu, wd, gple, wpg, p2d, wpp, gfin, tm, final):
    m, d = h2d.shape
    d_ff = wg.shape[1]
    const = lambda i: (0, 0)
    resident = lambda shape: pl.BlockSpec(shape, const, pipeline_mode=pl.Buffered(1))
    rows = lambda width: pl.BlockSpec((tm, width), lambda i: (i, 0))
    return pl.pallas_call(
        functools.partial(_mlp_kernel, final=final),
        grid=(m // tm,),
        in_specs=[rows(d)] + [rows(GROUP_WIDTH)] * 4 + [
            resident(wout.shape), resident(gffn.shape), resident(wg.shape), resident(wu.shape),
            resident(wd.shape), resident(gple.shape), resident(wpg.shape),
            rows(p2d.shape[1]), resident(wpp.shape), resident(gfin.shape),
        ],
        out_specs=rows(d),
        out_shape=jax.ShapeDtypeStruct((m, d), F32),
        scratch_shapes=[pltpu.VMEM((tm, d_ff), BF16)],
        compiler_params=pltpu.CompilerParams(
            dimension_semantics=("arbitrary",), vmem_limit_bytes=VMEM_LIMIT),
        name="mlp",
    )(h2d, *ys, wout, gffn, wg, wu, wd, gple, wpg, p2d, wpp, gfin)


def _rel_bucket_table(n_dist):
    n = np.arange(n_dist)
    nf = np.maximum(n, 1).astype(np.float32)
    large = REL_MAX_EXACT + (np.log(nf / REL_MAX_EXACT) / math.log(REL_MAX_DIST / REL_MAX_EXACT)
                             * (N_REL_BUCKETS - REL_MAX_EXACT)).astype(np.int32)
    large = np.minimum(large, N_REL_BUCKETS - 1)
    return np.where(n < REL_MAX_EXACT, n, large)


def _bias_tiles(rel_bias):
    t = ATTN_TILE
    bucket = _rel_bucket_table(2 * t)
    assert (bucket[t + 1:] == N_REL_BUCKETS - 1).all()
    per_dist = rel_bias[bucket, :] - rel_bias[N_REL_BUCKETS - 1][None, :]
    i = np.arange(t)[:, None]
    j = np.arange(t)[None, :]
    dist = np.stack([np.clip(i - j, 0, 2 * t - 1), i - j + t])
    tiles = jnp.transpose(per_dist[dist], (3, 0, 1, 2))
    causal = np.stack([i >= j, np.ones((t, t), bool)])
    return jnp.where(causal[None], tiles, NEG)


def _row(v, width=None):
    v = v.astype(F32).reshape(1, -1)
    if width is not None and v.shape[1] < width:
        v = jnp.pad(v, ((0, 0), (0, width - v.shape[1])))
    return v


def kernel(x, p, norm_mix, w_in, rel_bias, diff_lambda, diff_norm, gdn_conv, gdn_a_log, gdn_dt_bias,
           gdn_norm, gla_w_alpha, gla_b_alpha, gla_norm, w_out, norm_ffn, w_gate, w_up, w_down,
           norm_ple, w_ple_gate, w_ple_proj, final_norm):
    b, s, d = x.shape
    depth = w_in.shape[0]
    m = b * s
    tm = 512
    hg = N_HEADS_G
    assert s % ATTN_TILE == 0 and s % tm == 0 and tm % MOBA_BLOCK == 0

    bias = _bias_tiles(rel_bias.astype(F32))
    bias_a, bias_b = bias[:hg], bias[hg:]
    col_scale = np.ones((1, N_ATTN), np.float32)
    col_scale[:, 0:GROUP_WIDTH] = HEAD_DIM ** -0.5
    col_scale[:, 3 * GROUP_WIDTH:4 * GROUP_WIDTH] = DIFF_QK_DIM ** -0.5
    col_scale = jnp.asarray(col_scale)

    n_main = N_ATTN + N_GDN
    ab = 2 * hg
    gla_lo = n_main + ab
    gla_hi = gla_lo + N_GLA
    fin = _row(final_norm)
    h = x.reshape(m, d)
    for l in range(depth):
        wl = w_in[l]
        w1 = jnp.concatenate(
            [wl[:, :n_main], wl[:, gla_lo:gla_hi], wl[:, n_main:gla_lo], wl[:, gla_hi:],
             jnp.zeros((d, N_MISC - ab - GLA_GATE_RANK), wl.dtype)], axis=1).astype(BF16)
        attn, gdn, gla, misc, kmean = _inproj(h, _row(norm_mix[l]), w1, col_scale, tm)
        attn = attn.reshape(b, s, N_ATTN)
        misc = misc.reshape(b, s, N_MISC)
        kmean = kmean.reshape(b, s // MOBA_BLOCK, GROUP_WIDTH)

        y_a = _moba(attn, kmean, bias_a)

        lam_init = 0.8 - 0.6 * math.exp(-0.3 * l)
        lam_p = jnp.concatenate([diff_lambda[l].astype(F32),
                                 jnp.full((1, DIFF_QK_DIM), lam_init, F32)], axis=0)
        y_b = _diff(attn, lam_p, bias_b, _row(jnp.tile(diff_norm[l], 2)))

        hp = jnp.concatenate([_row(gdn_a_log[l], LANES), _row(gdn_dt_bias[l], LANES)], axis=0)
        y_c = _gdn(gdn.reshape(b, s, N_GDN), misc, gdn_conv[l].astype(F32), hp,
                   _row(jnp.tile(gdn_norm[l], hg)))

        walpha = jnp.zeros((N_MISC, hg * GLA_DK), F32).at[MISC_LR:MISC_LR + GLA_GATE_RANK].set(
            gla_w_alpha[l]).astype(BF16)
        y_d = _gla(gla.reshape(b, s, N_GLA), misc, walpha, _row(gla_b_alpha[l]),
                   _row(jnp.tile(gla_norm[l], hg)))

        ys = [y.reshape(m, GROUP_WIDTH) for y in (y_a, y_b, y_c, y_d)]
        h = _mlp(h, ys, w_out[l].astype(BF16), _row(norm_ffn[l]), w_gate[l].astype(BF16),
                 w_up[l].astype(BF16), w_down[l].astype(BF16), _row(norm_ple[l]),
                 w_ple_gate[l].astype(BF16), p[l].reshape(m, -1), w_ple_proj[l].astype(BF16),
                 fin, tm, final=(l == depth - 1))
    return h.reshape(b, s, d)
```

```python
import functools
import math

import numpy as np
import jax
import jax.numpy as jnp
from jax import lax
from jax.experimental import pallas as pl
from jax.experimental.pallas import tpu as pltpu

F32 = jnp.float32
BF16 = jnp.bfloat16

HEAD_DIM = 64
N_HEADS_G = 4
GROUP_WIDTH = HEAD_DIM * N_HEADS_G
MOBA_BLOCK = 256
MOBA_TOPK = 3
N_REL_BUCKETS = 32
REL_MAX_EXACT = 16
REL_MAX_DIST = 128
DIFF_QK_DIM = HEAD_DIM // 2
CONV_WIDTH = 4
CHUNK = 64
GLA_DK = HEAD_DIM // 2
GLA_GATE_RANK = 16
GLA_TAU = 16.0
EPS = 1e-6

LANES = 128
ATTN_TILE = 256
SEQ_TILE = 1024
NEG = -1e30
VMEM_LIMIT = 56 * 1024 * 1024

N_QK = 4 * GROUP_WIDTH
N_VT = 2 * GROUP_WIDTH
N_GDN = 4 * GROUP_WIDTH
N_GLA = 2 * N_HEADS_G * GLA_DK + 2 * GROUP_WIDTH
N_MISC = LANES
MISC_A, MISC_B, MISC_LR = 0, N_HEADS_G, 2 * N_HEADS_G

_NT = (((1,), (1,)), ((), ()))
_TN = (((0,), (0,)), ((), ()))


def _dot(a, b):
    return jnp.dot(a, b, preferred_element_type=F32)


def _dot_nt(a, b):
    return lax.dot_general(a, b, _NT, preferred_element_type=F32)


def _dot_tn(a, b):
    return lax.dot_general(a, b, _TN, preferred_element_type=F32)


def _split_bf16(x, n):
    parts = []
    r = x
    for _ in range(n):
        hi = r.astype(BF16)
        parts.append(hi)
        r = r - hi.astype(F32)
    return parts


def _dot_exact_rhs(x, w, n=2):
    acc = None
    for part in _split_bf16(x, n):
        t = _dot(part, w)
        acc = t if acc is None else acc + t
    return acc


def _dot_exact_lhs(w, x, n=3):
    acc = None
    for part in _split_bf16(x, n):
        t = _dot(w, part)
        acc = t if acc is None else acc + t
    return acc


def _rms(x, g):
    return x * lax.rsqrt(jnp.mean(x * x, axis=-1, keepdims=True) + EPS) * g


def _sigmoid(x):
    return 1.0 / (1.0 + jnp.exp(-x))


def _silu(x):
    return x * _sigmoid(x)


def _inproj_kernel(x_ref, g_ref, w_ref, wvt_ref, scale_ref,
                   qk_ref, vt_ref, gdn_ref, gla_ref, misc_ref, kmean_ref):
    tm = x_ref.shape[0]
    xn = _rms(x_ref[...], g_ref[...]).astype(BF16)
    qk = _dot(xn, w_ref[:, 0:N_QK])
    k_moba = qk[:, GROUP_WIDTH:2 * GROUP_WIDTH]
    kmean_ref[0] = jnp.mean(k_moba.reshape(tm // MOBA_BLOCK, MOBA_BLOCK, GROUP_WIDTH), axis=1)
    qk_ref[...] = (qk * scale_ref[...]).astype(BF16)
    vt = _dot_nt(wvt_ref[...], xn).astype(BF16)
    for i in range(tm // ATTN_TILE):
        vt_ref[0, i] = vt[:, i * ATTN_TILE:(i + 1) * ATTN_TILE]
    o = N_QK
    gdn_ref[...] = _dot(xn, w_ref[:, o:o + N_GDN])
    o += N_GDN
    gla_ref[...] = _dot(xn, w_ref[:, o:o + N_GLA])
    o += N_GLA
    misc_ref[...] = _dot(xn, w_ref[:, o:o + N_MISC])


def _inproj(h2d, g, w, wvt, scale, tm, seq):
    m, d = h2d.shape
    n_all = w.shape[1]
    per_seq = seq // tm
    const = lambda i: (0, 0)
    return pl.pallas_call(
        _inproj_kernel,
        grid=(m // tm,),
        in_specs=[
            pl.BlockSpec((tm, d), lambda i: (i, 0)),
            pl.BlockSpec((1, d), const),
            pl.BlockSpec((d, n_all), const),
            pl.BlockSpec((N_VT, d), const),
            pl.BlockSpec((1, N_QK), const),
        ],
        out_specs=[
            pl.BlockSpec((tm, N_QK), lambda i: (i, 0)),
            pl.BlockSpec((1, tm // ATTN_TILE, N_VT, ATTN_TILE),
                         lambda i: (i // per_seq, i % per_seq, 0, 0)),
            pl.BlockSpec((tm, N_GDN), lambda i: (i, 0)),
            pl.BlockSpec((tm, N_GLA), lambda i: (i, 0)),
            pl.BlockSpec((tm, N_MISC), lambda i: (i, 0)),
            pl.BlockSpec((1, tm // MOBA_BLOCK, GROUP_WIDTH), lambda i: (i, 0, 0)),
        ],
        out_shape=[
            jax.ShapeDtypeStruct((m, N_QK), BF16),
            jax.ShapeDtypeStruct((m // seq, seq // ATTN_TILE, N_VT, ATTN_TILE), BF16),
            jax.ShapeDtypeStruct((m, N_GDN), F32),
            jax.ShapeDtypeStruct((m, N_GLA), F32),
            jax.ShapeDtypeStruct((m, N_MISC), F32),
            jax.ShapeDtypeStruct((m // tm, tm // MOBA_BLOCK, GROUP_WIDTH), F32),
        ],
        compiler_params=pltpu.CompilerParams(
            dimension_semantics=("arbitrary",), vmem_limit_bytes=VMEM_LIMIT),
        name="inproj",
    )(h2d, g, w, wvt, scale)


def _lane_mask(lo, width):
    lane = lax.broadcasted_iota(jnp.int32, (1, LANES), 1)
    return (lane >= lo) & (lane < lo + width)


def _row_mask(lo, width):
    row = lax.broadcasted_iota(jnp.int32, (LANES, 1), 0)
    return (row >= lo) & (row < lo + width)


def _values_with_ones(vt, hh):
    return jnp.where(_row_mask(HEAD_DIM * hh, HEAD_DIM), vt, jnp.ones_like(vt))


def _attend_init(n_chains, t):
    return tuple((jnp.full((1, t), NEG, F32), jnp.zeros((LANES, t), F32)) for _ in range(n_chains))


def _attend(state, scores, vts):
    stats = []
    for (m, _), s in zip(state, scores):
        m_new = jnp.maximum(m, jnp.max(s, axis=0, keepdims=True))
        stats.append((m_new, jnp.exp(m - m_new), jnp.exp(s - m_new).astype(BF16)))
    return tuple((m_new, alpha * acc + _dot(vt, p))
                 for (m_new, alpha, p), (_, acc), vt in zip(stats, state, vts))


def _attend_pipelined(state, s_first, n_far, scores_fn, values_fn):
    last = jnp.maximum(n_far - 1, 0)

    def body(j, carry):
        state, s_cur = carry
        s_next = scores_fn(jnp.minimum(j + 1, last))
        return _attend(state, s_cur, values_fn(j)), s_next

    state, _ = lax.fori_loop(0, n_far, body, (state, s_first))
    return state


def _normalised(acc, hh):
    other = HEAD_DIM * (1 - hh)
    return acc / acc[other:other + 1, :]


def _moba_kernel(q_ref, k_ref, vt_ref, kmean_ref, bias_ref, o_ref, pen_ref):
    t = ATTN_TILE
    qi = pl.program_id(2)
    nblk = kmean_ref.shape[1]
    q = q_ref[0]
    zero = jnp.zeros_like(q)

    row = lax.broadcasted_iota(jnp.int32, (nblk, t), 0)
    qms = []
    for hh in range(2):
        hmask = _lane_mask(HEAD_DIM * hh, HEAD_DIM)
        qm = jnp.where(hmask, q, zero)
        qms.append(qm)
        gate = _dot_nt(kmean_ref[0].astype(BF16), qm)
        gate = jnp.where(row < qi, gate, -jnp.inf)
        sel_t = jnp.zeros((nblk, t), F32)
        for j in range(nblk):
            gj = gate[j:j + 1, :]
            beats = jnp.where(gate > gj, 1.0, jnp.where((gate == gj) & (row < j), 1.0, 0.0))
            rank = jnp.sum(beats, axis=0, keepdims=True)
            sel_j = jnp.where(rank < MOBA_TOPK, jnp.where(j < qi, 1.0, 0.0), 0.0)
            sel_t = jnp.where(row == j, sel_j, sel_t)
        pen = jnp.where(sel_t > 0.5, 0.0, NEG)
        for j in range(nblk):
            pen_ref[hh * nblk + j] = pen[j:j + 1, :]

    def keys(j):
        return k_ref[0, pl.ds(pl.multiple_of(j * t, t), t), :]

    def values(j):
        vt = vt_ref[0, j]
        return [_values_with_ones(vt, hh) for hh in range(2)]

    def far_scores(j):
        k = keys(j)
        return [_dot_nt(k, qms[hh]) + pen_ref[hh * nblk + j] for hh in range(2)]

    jp = jnp.maximum(qi - 1, 0)
    k_own, k_prev = keys(qi), keys(jp)
    s_own = [_dot_nt(k_own, qms[hh]) + bias_ref[hh, 0] for hh in range(2)]
    s_prev = [_dot_nt(k_prev, qms[hh]) + bias_ref[hh, 1] + pen_ref[hh * nblk + jp] for hh in range(2)]
    s_far = far_scores(0)
    state = _attend(_attend_init(2, t), s_own, values(qi))
    state = _attend(state, s_prev, values(jp))
    state = _attend_pipelined(state, s_far, qi - 1, far_scores, values)
    out_t = jnp.where(_row_mask(0, HEAD_DIM), _normalised(state[0][1], 0), _normalised(state[1][1], 1))
    o_ref[0] = out_t.T.astype(o_ref.dtype)


def _attn_specs(s, t, mixer):
    nkv = s // t
    qcol = mixer * 2 * GROUP_WIDTH // LANES
    kcol = qcol + GROUP_WIDTH // LANES
    vrow = mixer * GROUP_WIDTH // LANES
    return [
        pl.BlockSpec((1, t, LANES), lambda bi, p, qi: (bi, qi, qcol + p)),
        pl.BlockSpec((1, s, LANES), lambda bi, p, qi: (bi, 0, kcol + p)),
        pl.BlockSpec((1, nkv, LANES, t), lambda bi, p, qi: (bi, 0, vrow + p, 0)),
    ]


def _moba(qk, vt, kmean, bias_t):
    b, s, _ = qk.shape
    t = ATTN_TILE
    nq = s // t
    nblk = kmean.shape[1]
    return pl.pallas_call(
        _moba_kernel,
        grid=(b, 2, nq),
        in_specs=_attn_specs(s, t, 0) + [
            pl.BlockSpec((1, nblk, LANES), lambda bi, p, qi: (bi, 0, p)),
            pl.BlockSpec((2, 2, t, t), lambda bi, p, qi: (p, 0, 0, 0)),
        ],
        out_specs=pl.BlockSpec((1, t, LANES), lambda bi, p, qi: (bi, qi, p)),
        out_shape=jax.ShapeDtypeStruct((b, s, GROUP_WIDTH), BF16),
        scratch_shapes=[pltpu.VMEM((2 * nblk, 1, t), F32)],
        compiler_params=pltpu.CompilerParams(
            dimension_semantics=("arbitrary", "arbitrary", "arbitrary"),
            vmem_limit_bytes=VMEM_LIMIT),
        name="moba",
    )(qk, qk, vt, kmean, bias_t)


def _diff_kernel(lam_ref, q_ref, k_ref, vt_ref, bias_ref, gnorm_ref, o_ref):
    t = ATTN_TILE
    qi = pl.program_id(2)
    q = q_ref[0]
    zero = jnp.zeros_like(q)
    qms = [jnp.where(_lane_mask(HEAD_DIM * hh + DIFF_QK_DIM * mm, DIFF_QK_DIM), q, zero)
           for hh in range(2) for mm in range(2)]

    def keys(j):
        return k_ref[0, pl.ds(pl.multiple_of(j * t, t), t), :]

    def values(j):
        vt = vt_ref[0, j]
        vts = [_values_with_ones(vt, hh) for hh in range(2)]
        return [vts[c // 2] for c in range(4)]

    def far_scores(j):
        k = keys(j)
        return [_dot_nt(k, qms[c]) for c in range(4)]

    jp = jnp.maximum(qi - 1, 0)
    off = jnp.where(qi >= 1, 0.0, NEG)
    k_own, k_prev = keys(qi), keys(jp)
    s_own = [_dot_nt(k_own, qms[c]) + bias_ref[c // 2, 0] for c in range(4)]
    s_prev = [_dot_nt(k_prev, qms[c]) + (bias_ref[c // 2, 1] + off) for c in range(4)]
    s_far = far_scores(0)
    carry = _attend(_attend_init(4, t), s_own, values(qi))
    carry = _attend(carry, s_prev, values(jp))
    carry = _attend_pipelined(carry, s_far, qi - 1, far_scores, values)

    lam_p = lam_ref[...]
    lam_init = lam_p[4:5, 0:1]
    lam = (jnp.exp(jnp.sum(lam_p[0:1] * lam_p[1:2], axis=-1, keepdims=True))
           - jnp.exp(jnp.sum(lam_p[2:3] * lam_p[3:4], axis=-1, keepdims=True)) + lam_init)
    outs = []
    for hh in range(2):
        o = _normalised(carry[2 * hh][1], hh) - lam * _normalised(carry[2 * hh + 1][1], hh)
        hmask = _row_mask(HEAD_DIM * hh, HEAD_DIM)
        ms = jnp.sum(jnp.where(hmask, o * o, 0.0), axis=0, keepdims=True) * (1.0 / HEAD_DIM)
        outs.append(o * lax.rsqrt(ms + EPS))
    y_t = jnp.where(_row_mask(0, HEAD_DIM), outs[0], outs[1])
    o_ref[0] = (y_t.T * gnorm_ref[...] * (1.0 - lam_init)).astype(o_ref.dtype)


def _diff(qk, vt, lam_p, bias_t, gnorm):
    b, s, _ = qk.shape
    t = ATTN_TILE
    nq = s // t
    return pl.pallas_call(
        _diff_kernel,
        grid=(b, 2, nq),
        in_specs=[pl.BlockSpec(lam_p.shape, lambda bi, p, qi: (0, 0))] + _attn_specs(s, t, 1) + [
            pl.BlockSpec((2, 2, t, t), lambda bi, p, qi: (p, 0, 0, 0)),
            pl.BlockSpec((1, LANES), lambda bi, p, qi: (0, 0)),
        ],
        out_specs=pl.BlockSpec((1, t, LANES), lambda bi, p, qi: (bi, qi, p)),
        out_shape=jax.ShapeDtypeStruct((b, s, GROUP_WIDTH), BF16),
        compiler_params=pltpu.CompilerParams(
            dimension_semantics=("arbitrary", "arbitrary", "arbitrary"),
            vmem_limit_bytes=VMEM_LIMIT),
        name="diff_attn",
    )(lam_p, qk, qk, vt, bias_t, gnorm)


def _head_of_lane(n_lanes, width):
    return lax.broadcasted_iota(jnp.int32, (1, n_lanes), 1) // width


def _block_rows(x, lane_head, n_heads=N_HEADS_G):
    zero = jnp.zeros_like(x)
    return jnp.concatenate([jnp.where(lane_head == h, x, zero) for h in range(n_heads)], axis=0)


def _gdn_kernel(x_ref, misc_ref, conv_ref, hp_ref, gnorm_ref, o_ref, s_ref, tail_ref, g_ref, beta_ref):
    L = CHUNK
    W = GROUP_WIDTH
    seq = x_ref.shape[1]
    n_chunks = seq // L
    head_w = _head_of_lane(W, HEAD_DIM)
    head_all = _head_of_lane(LANES, 1)
    ri = lax.broadcasted_iota(jnp.int32, (L, W), 0)
    cj = lax.broadcasted_iota(jnp.int32, (L, W), 1) % HEAD_DIM
    lower = ri >= cj
    strict = ri > cj
    ident = jnp.where(ri == cj, 1.0, 0.0)
    ones_blk = (lax.broadcasted_iota(jnp.int32, (W, W), 0) // HEAD_DIM
                == lax.broadcasted_iota(jnp.int32, (W, W), 1) // HEAD_DIM)
    ones_seg = ones_blk.astype(BF16)
    tril = (lax.broadcasted_iota(jnp.int32, (L, L), 0)
            >= lax.broadcasted_iota(jnp.int32, (L, L), 1)).astype(BF16)

    hp = hp_ref[...]
    misc = misc_ref[0]
    sp_in = misc + hp[1:2]
    softplus = jnp.maximum(sp_in, 0.0) + jnp.log(1.0 + jnp.exp(-jnp.abs(sp_in)))
    g_tok = -jnp.exp(hp[0:1]) * softplus
    b_tok = _sigmoid(misc)
    exp_g = (lax.broadcasted_iota(jnp.int32, (LANES, W), 0) - MISC_A
             == lax.broadcasted_iota(jnp.int32, (LANES, W), 1) // HEAD_DIM).astype(BF16)
    exp_b = (lax.broadcasted_iota(jnp.int32, (LANES, W), 0) - MISC_B
             == lax.broadcasted_iota(jnp.int32, (LANES, W), 1) // HEAD_DIM).astype(BF16)
    del head_all
    g_ref[...] = _dot_exact_rhs(g_tok, exp_g, 3)
    beta_ref[...] = _dot_exact_rhs(b_tok, exp_b, 3)

    @pl.when(pl.program_id(1) == 0)
    def _():
        s_ref[...] = jnp.zeros_like(s_ref)
        tail_ref[...] = jnp.zeros_like(tail_ref)

    cw = conv_ref[...]

    def chunk(c, _):
        r0 = pl.multiple_of(c * L, L)
        x = x_ref[0, pl.ds(r0, L), :]
        qkv = x[:, 0:3 * W]
        xx = jnp.concatenate([tail_ref[...], qkv], axis=0)
        tail_ref[...] = qkv[L - 8:L, :]
        conv = cw[CONV_WIDTH - 1:CONV_WIDTH] * qkv
        for i in range(CONV_WIDTH - 1):
            lo = 8 - (CONV_WIDTH - 1) + i
            conv = conv + cw[i:i + 1] * xx[lo:lo + L, :]
        conv = _silu(conv)
        q = conv[:, 0:W]
        k = conv[:, W:2 * W]
        v = conv[:, 2 * W:3 * W]
        q = q * lax.rsqrt(_dot_exact_rhs(q * q, ones_seg) + EPS) * (HEAD_DIM ** -0.5)
        k = k * lax.rsqrt(_dot_exact_rhs(k * k, ones_seg) + EPS)

        beta = beta_ref[pl.ds(r0, L), :]
        gc = _dot_exact_lhs(tril, g_ref[pl.ds(r0, L), :])
        egc = jnp.exp(gc)
        gc_col = jnp.sum(gc * ident, axis=0, keepdims=True)
        gc_last = gc[L - 1:L, :]
        decay = jnp.exp(jnp.where(lower, gc - gc_col, -jnp.inf))

        kb = k * beta
        k_blk = _block_rows(k.astype(BF16), head_w)
        both = _dot_nt(jnp.concatenate([kb, q], axis=0).astype(BF16), k_blk)
        a_mat = jnp.where(strict, both[0:L] * decay, 0.0)
        qk = jnp.where(lower, both[L:2 * L] * decay, 0.0)

        p = -a_mat
        t_inv = ident + p
        p = _dot(p.astype(BF16), _block_rows(p.astype(BF16), head_w))
        for _ in range(4):
            p_blk = _block_rows(p.astype(BF16), head_w)
            prod = _dot(jnp.concatenate([t_inv, p], axis=0).astype(BF16), p_blk)
            t_inv = t_inv + prod[0:L]
            p = prod[L:2 * L]
        t_inv = t_inv + _dot(t_inv.astype(BF16), _block_rows(p.astype(BF16), head_w))
        t_bf = t_inv.astype(BF16)

        u = _dot(t_bf, _block_rows((v * beta).astype(BF16), head_w))
        w = _dot(t_bf, _block_rows((kb * egc).astype(BF16), head_w))

        state = s_ref[...]
        state_bf = state.astype(BF16)
        ws_qs = _dot(jnp.concatenate([w, q * egc], axis=0).astype(BF16), state_bf)
        v_new = u - ws_qs[0:L]
        o = ws_qs[L:2 * L] + _dot(qk.astype(BF16), _block_rows(v_new.astype(BF16), head_w))
        k_dec = k * jnp.exp(gc_last - gc)
        upd = _dot_tn(k_dec.astype(BF16), v_new.astype(BF16))
        s_ref[...] = state * jnp.exp(gc_last) + jnp.where(ones_blk, upd, 0.0)

        ms = _dot_exact_rhs(o * o, ones_seg) * (1.0 / HEAD_DIM)
        y = o * lax.rsqrt(ms + EPS) * gnorm_ref[...] * _silu(x[:, 3 * W:4 * W])
        o_ref[0, pl.ds(r0, L), :] = y.astype(o_ref.dtype)
        return 0

    lax.fori_loop(0, n_chunks, chunk, 0)


def _gdn(gdn, misc, conv_w, hp, gnorm):
    b, s, _ = gdn.shape
    ts = min(s, SEQ_TILE)
    const = lambda bi, si: (0, 0)
    return pl.pallas_call(
        _gdn_kernel,
        grid=(b, s // ts),
        in_specs=[
            pl.BlockSpec((1, ts, N_GDN), lambda bi, si: (bi, si, 0)),
            pl.BlockSpec((1, ts, N_MISC), lambda bi, si: (bi, si, 0)),
            pl.BlockSpec(conv_w.shape, const),
            pl.BlockSpec(hp.shape, const),
            pl.BlockSpec((1, GROUP_WIDTH), const),
        ],
        out_specs=pl.BlockSpec((1, ts, GROUP_WIDTH), lambda bi, si: (bi, si, 0)),
        out_shape=jax.ShapeDtypeStruct((b, s, GROUP_WIDTH), BF16),
        scratch_shapes=[
            pltpu.VMEM((GROUP_WIDTH, GROUP_WIDTH), F32),
            pltpu.VMEM((8, 3 * GROUP_WIDTH), F32),
            pltpu.VMEM((ts, GROUP_WIDTH), F32),
            pltpu.VMEM((ts, GROUP_WIDTH), F32),
        ],
        compiler_params=pltpu.CompilerParams(
            dimension_semantics=("arbitrary", "arbitrary"), vmem_limit_bytes=VMEM_LIMIT),
        name="gdn",
    )(gdn, misc, conv_w, hp, gnorm)


def _gla_kernel(x_ref, misc_ref, walpha_ref, balpha_ref, gnorm_ref, o_ref, s_ref, la_ref):
    L = CHUNK
    W = GROUP_WIDTH
    KW = N_HEADS_G * GLA_DK
    seq = x_ref.shape[1]
    n_chunks = seq // L
    head_k = _head_of_lane(KW, GLA_DK)
    head_v = _head_of_lane(W, HEAD_DIM)
    ri = lax.broadcasted_iota(jnp.int32, (L, W), 0)
    cj = lax.broadcasted_iota(jnp.int32, (L, W), 1) % HEAD_DIM
    lower = ri >= cj
    ones_blk = (lax.broadcasted_iota(jnp.int32, (W, W), 0) // HEAD_DIM
                == lax.broadcasted_iota(jnp.int32, (W, W), 1) // HEAD_DIM).astype(BF16)
    state_mask = (lax.broadcasted_iota(jnp.int32, (W, KW), 0) // HEAD_DIM
                  == lax.broadcasted_iota(jnp.int32, (W, KW), 1) // GLA_DK)
    tril = (lax.broadcasted_iota(jnp.int32, (L, L), 0)
            >= lax.broadcasted_iota(jnp.int32, (L, L), 1)).astype(BF16)

    pre = _dot(misc_ref[0].astype(BF16), walpha_ref[...]) + balpha_ref[...]
    log_sig = jnp.minimum(pre, 0.0) - jnp.log(1.0 + jnp.exp(-jnp.abs(pre)))
    la_ref[...] = log_sig * (1.0 / GLA_TAU)

    @pl.when(pl.program_id(1) == 0)
    def _():
        s_ref[...] = jnp.zeros_like(s_ref)

    def chunk(c, _):
        r0 = pl.multiple_of(c * L, L)
        x = x_ref[0, pl.ds(r0, L), :]
        q = x[:, 0:KW] * (GLA_DK ** -0.5)
        k = x[:, KW:2 * KW]
        v = x[:, 2 * KW:2 * KW + W]
        bc = _dot_exact_lhs(tril, la_ref[pl.ds(r0, L), :])
        b_last = bc[L - 1:L, :]
        qe = (q * jnp.exp(bc)).astype(BF16)
        ke = (k * jnp.exp(-bc)).astype(BF16)
        a_mat = jnp.where(lower, _dot_nt(qe, _block_rows(ke, head_k)), 0.0)
        state_t = s_ref[...]
        o = _dot_nt(qe, state_t.astype(BF16)) + _dot(a_mat.astype(BF16),
                                                    _block_rows(v.astype(BF16), head_v))
        k_dec = (k * jnp.exp(b_last - bc)).astype(BF16)
        upd = _dot_tn(v.astype(BF16), k_dec)
        s_ref[...] = state_t * jnp.exp(b_last) + jnp.where(state_mask, upd, 0.0)

        ms = _dot_exact_rhs(o * o, ones_blk) * (1.0 / HEAD_DIM)
        y = o * lax.rsqrt(ms + EPS) * gnorm_ref[...] * _silu(x[:, 2 * KW + W:2 * KW + 2 * W])
        o_ref[0, pl.ds(r0, L), :] = y.astype(o_ref.dtype)
        return 0

    lax.fori_loop(0, n_chunks, chunk, 0)


def _gla(gla, misc, walpha, balpha, gnorm):
    b, s, _ = gla.shape
    ts = min(s, SEQ_TILE)
    const = lambda bi, si: (0, 0)
    return pl.pallas_call(
        _gla_kernel,
        grid=(b, s // ts),
        in_specs=[
            pl.BlockSpec((1, ts, N_GLA), lambda bi, si: (bi, si, 0)),
            pl.BlockSpec((1, ts, N_MISC), lambda bi, si: (bi, si, 0)),
            pl.BlockSpec(walpha.shape, const),
            pl.BlockSpec(balpha.shape, const),
            pl.BlockSpec((1, GROUP_WIDTH), const),
        ],
        out_specs=pl.BlockSpec((1, ts, GROUP_WIDTH), lambda bi, si: (bi, si, 0)),
        out_shape=jax.ShapeDtypeStruct((b, s, GROUP_WIDTH), BF16),
        scratch_shapes=[
            pltpu.VMEM((GROUP_WIDTH, N_HEADS_G * GLA_DK), F32),
            pltpu.VMEM((ts, N_HEADS_G * GLA_DK), F32),
        ],
        compiler_params=pltpu.CompilerParams(
            dimension_semantics=("arbitrary", "arbitrary"), vmem_limit_bytes=VMEM_LIMIT),
        name="gla",
    )(gla, misc, walpha, balpha, gnorm)


def _ffn_chunks(d_ff):
    step = 4 * GROUP_WIDTH
    return [(lo, min(step, d_ff - lo)) for lo in range(0, d_ff, step)]


def _mlp_kernel(h_ref, ya_ref, yb_ref, yc_ref, yd_ref, wout_ref, gffn_ref, wg_ref, wu_ref, wd_ref,
                gple_ref, wpg_ref, p_ref, wpp_ref, gfin_ref, o_ref, act_ref, *, final):
    mixed = jnp.concatenate([ya_ref[...], yb_ref[...], yc_ref[...], yd_ref[...]], axis=-1)
    h = h_ref[...] + _dot(mixed, wout_ref[...])
    hn = _rms(h, gffn_ref[...]).astype(BF16)
    for lo, width in _ffn_chunks(wg_ref.shape[1]):
        gate = _dot(hn, wg_ref[:, lo:lo + width])
        up = _dot(hn, wu_ref[:, lo:lo + width])
        act_ref[:, lo:lo + width] = (_silu(gate) * up).astype(BF16)
    h = h + _dot(act_ref[...], wd_ref[...])
    gate = _sigmoid(_dot(_rms(h, gple_ref[...]).astype(BF16), wpg_ref[...]))
    h = h + gate * _dot(p_ref[...].astype(BF16), wpp_ref[...])
    if final:
        h = _rms(h, gfin_ref[...])
    o_ref[...] = h


def _mlp(h2d, ys, wout, gffn, wg, wu, wd, gple, wpg, p2d, wpp, gfin, tm, final):
    m, d = h2d.shape
    d_ff = wg.shape[1]
    const = lambda i: (0, 0)
    resident = lambda shape: pl.BlockSpec(shape, const, pipeline_mode=pl.Buffered(1))
    rows = lambda width: pl.BlockSpec((tm, width), lambda i: (i, 0))
    return pl.pallas_call(
        functools.partial(_mlp_kernel, final=final),
        grid=(m // tm,),
        in_specs=[rows(d)] + [rows(GROUP_WIDTH)] * 4 + [
            resident(wout.shape), resident(gffn.shape), resident(wg.shape), resident(wu.shape),
            resident(wd.shape), resident(gple.shape), resident(wpg.shape),
            rows(p2d.shape[1]), resident(wpp.shape), resident(gfin.shape),
        ],
        out_specs=rows(d),
        out_shape=jax.ShapeDtypeStruct((m, d), F32),
        scratch_shapes=[pltpu.VMEM((tm, d_ff), BF16)],
        compiler_params=pltpu.CompilerParams(
            dimension_semantics=("arbitrary",), vmem_limit_bytes=VMEM_LIMIT),
        name="mlp",
    )(h2d, *ys, wout, gffn, wg, wu, wd, gple, wpg, p2d, wpp, gfin)


def _rel_bucket_table(n_dist):
    n = np.arange(n_dist)
    nf = np.maximum(n, 1).astype(np.float32)
    large = REL_MAX_EXACT + (np.log(nf / REL_MAX_EXACT) / math.log(REL_MAX_DIST / REL_MAX_EXACT)
                             * (N_REL_BUCKETS - REL_MAX_EXACT)).astype(np.int32)
    large = np.minimum(large, N_REL_BUCKETS - 1)
    return np.where(n < REL_MAX_EXACT, n, large)


def _bias_tiles(rel_bias):
    t = ATTN_TILE
    bucket = _rel_bucket_table(2 * t)
    assert (bucket[t + 1:] == N_REL_BUCKETS - 1).all()
    per_dist = rel_bias[bucket, :] - rel_bias[N_REL_BUCKETS - 1][None, :]
    i = np.arange(t)[:, None]
    j = np.arange(t)[None, :]
    dist = np.stack([np.clip(i - j, 0, 2 * t - 1), i - j + t])
    tiles = jnp.transpose(per_dist[dist], (3, 0, 1, 2))
    causal = np.stack([i >= j, np.ones((t, t), bool)])
    return jnp.where(causal[None], tiles, NEG)


def _row(v, width=None):
    v = v.astype(F32).reshape(1, -1)
    if width is not None and v.shape[1] < width:
        v = jnp.pad(v, ((0, 0), (0, width - v.shape[1])))
    return v


def kernel(x, p, norm_mix, w_in, rel_bias, diff_lambda, diff_norm, gdn_conv, gdn_a_log, gdn_dt_bias,
           gdn_norm, gla_w_alpha, gla_b_alpha, gla_norm, w_out, norm_ffn, w_gate, w_up, w_down,
           norm_ple, w_ple_gate, w_ple_proj, final_norm):
    b, s, d = x.shape
    depth = w_in.shape[0]
    m = b * s
    tm = 512
    hg = N_HEADS_G
    assert s % ATTN_TILE == 0 and s % tm == 0 and tm % MOBA_BLOCK == 0

    bias_t = jnp.swapaxes(_bias_tiles(rel_bias.astype(F32)), -1, -2)
    bias_a, bias_b = bias_t[:hg], bias_t[hg:]
    gw = GROUP_WIDTH
    col_scale = np.ones((1, N_QK), np.float32)
    col_scale[:, 0:gw] = HEAD_DIM ** -0.5
    col_scale[:, 2 * gw:3 * gw] = DIFF_QK_DIM ** -0.5
    col_scale = jnp.asarray(col_scale)

    n_main = 6 * gw + N_GDN
    ab = 2 * hg
    gla_lo = n_main + ab
    gla_hi = gla_lo + N_GLA
    fin = _row(final_norm)
    h = x.reshape(m, d)
    for l in range(depth):
        wl = w_in[l]
        w1 = jnp.concatenate(
            [wl[:, 0:2 * gw], wl[:, 3 * gw:5 * gw], wl[:, 6 * gw:n_main], wl[:, gla_lo:gla_hi],
             wl[:, n_main:gla_lo], wl[:, gla_hi:],
             jnp.zeros((d, N_MISC - ab - GLA_GATE_RANK), wl.dtype)], axis=1).astype(BF16)
        wvt = jnp.concatenate([wl[:, 2 * gw:3 * gw], wl[:, 5 * gw:6 * gw]], axis=1).T.astype(BF16)
        qk, vt, gdn, gla, misc, kmean = _inproj(h, _row(norm_mix[l]), w1, wvt, col_scale, tm, s)
        qk = qk.reshape(b, s, N_QK)
        misc = misc.reshape(b, s, N_MISC)
        kmean = kmean.reshape(b, s // MOBA_BLOCK, GROUP_WIDTH)

        y_a = _moba(qk, vt, kmean, bias_a)

        lam_init = 0.8 - 0.6 * math.exp(-0.3 * l)
        lam_p = jnp.concatenate([diff_lambda[l].astype(F32),
                                 jnp.full((1, DIFF_QK_DIM), lam_init, F32)], axis=0)
        y_b = _diff(qk, vt, lam_p, bias_b, _row(jnp.tile(diff_norm[l], 2)))

        hp = jnp.concatenate([_row(gdn_a_log[l], LANES), _row(gdn_dt_bias[l], LANES)], axis=0)
        y_c = _gdn(gdn.reshape(b, s, N_GDN), misc, gdn_conv[l].astype(F32), hp,
                   _row(jnp.tile(gdn_norm[l], hg)))

        walpha = jnp.zeros((N_MISC, hg * GLA_DK), F32).at[MISC_LR:MISC_LR + GLA_GATE_RANK].set(
            gla_w_alpha[l]).astype(BF16)
        y_d = _gla(gla.reshape(b, s, N_GLA), misc, walpha, _row(gla_b_alpha[l]),
                   _row(jnp.tile(gla_norm[l], hg)))

        ys = [y.reshape(m, GROUP_WIDTH) for y in (y_a, y_b, y_c, y_d)]
        h = _mlp(h, ys, w_out[l].astype(BF16), _row(norm_ffn[l]), w_gate[l].astype(BF16),
                 w_up[l].astype(BF16), w_down[l].astype(BF16), _row(norm_ple[l]),
                 w_ple_gate[l].astype(BF16), p[l].reshape(m, -1), w_ple_proj[l].astype(BF16),
                 fin, tm, final=(l == depth - 1))
    return h.reshape(b, s, d)
```

```python
import functools
import math

import numpy as np
import jax
import jax.numpy as jnp
from jax import lax
from jax.experimental import pallas as pl
from jax.experimental.pallas import tpu as pltpu

F32 = jnp.float32
BF16 = jnp.bfloat16

HEAD_DIM = 64
N_HEADS_G = 4
GROUP_WIDTH = HEAD_DIM * N_HEADS_G
MOBA_BLOCK = 256
MOBA_TOPK = 3
N_REL_BUCKETS = 32
REL_MAX_EXACT = 16
REL_MAX_DIST = 128
DIFF_QK_DIM = HEAD_DIM // 2
CONV_WIDTH = 4
CHUNK = 64
GLA_DK = HEAD_DIM // 2
GLA_GATE_RANK = 16
GLA_TAU = 16.0
EPS = 1e-6

LANES = 128
ATTN_TILE = 256
SEQ_TILE = 1024
GDN_SEQ_TILE = 512
GDN_ROWS = 4
NEG = -1e30
VMEM_LIMIT = 56 * 1024 * 1024

N_QK = 4 * GROUP_WIDTH
N_VT = 2 * GROUP_WIDTH
N_GDN = 4 * GROUP_WIDTH
N_GLA = 2 * N_HEADS_G * GLA_DK + 2 * GROUP_WIDTH
N_MISC = LANES
MISC_A, MISC_B, MISC_LR = 0, N_HEADS_G, 2 * N_HEADS_G

_NT = (((1,), (1,)), ((), ()))
_TN = (((0,), (0,)), ((), ()))


def _dot(a, b):
    return jnp.dot(a, b, preferred_element_type=F32)


def _dot_nt(a, b):
    return lax.dot_general(a, b, _NT, preferred_element_type=F32)


def _dot_tn(a, b):
    return lax.dot_general(a, b, _TN, preferred_element_type=F32)


def _split_bf16(x, n):
    parts = []
    r = x
    for _ in range(n):
        hi = r.astype(BF16)
        parts.append(hi)
        r = r - hi.astype(F32)
    return parts


def _dot_exact_rhs(x, w, n=2):
    acc = None
    for part in _split_bf16(x, n):
        t = _dot(part, w)
        acc = t if acc is None else acc + t
    return acc


def _dot_exact_lhs(w, x, n=3):
    acc = None
    for part in _split_bf16(x, n):
        t = _dot(w, part)
        acc = t if acc is None else acc + t
    return acc


def _rms(x, g):
    return x * lax.rsqrt(jnp.mean(x * x, axis=-1, keepdims=True) + EPS) * g


def _sigmoid(x):
    return 1.0 / (1.0 + jnp.exp(-x))


def _silu(x):
    return x * _sigmoid(x)


def _inproj_kernel(x_ref, g_ref, w_ref, wvt_ref, scale_ref,
                   qk_ref, vt_ref, gdn_ref, gla_ref, misc_ref, kmean_ref):
    tm = x_ref.shape[0]
    xn = _rms(x_ref[...], g_ref[...]).astype(BF16)
    qk = _dot(xn, w_ref[:, 0:N_QK])
    k_moba = qk[:, GROUP_WIDTH:2 * GROUP_WIDTH]
    kmean_ref[0] = jnp.mean(k_moba.reshape(tm // MOBA_BLOCK, MOBA_BLOCK, GROUP_WIDTH), axis=1)
    qk_ref[...] = (qk * scale_ref[...]).astype(BF16)
    vt = _dot_nt(wvt_ref[...], xn).astype(BF16)
    for i in range(tm // ATTN_TILE):
        vt_ref[0, i] = vt[:, i * ATTN_TILE:(i + 1) * ATTN_TILE]
    o = N_QK
    gdn_ref[...] = _dot(xn, w_ref[:, o:o + N_GDN])
    o += N_GDN
    gla_ref[...] = _dot(xn, w_ref[:, o:o + N_GLA])
    o += N_GLA
    misc_ref[...] = _dot(xn, w_ref[:, o:o + N_MISC])


def _inproj(h2d, g, w, wvt, scale, tm, seq):
    m, d = h2d.shape
    n_all = w.shape[1]
    per_seq = seq // tm
    const = lambda i: (0, 0)
    return pl.pallas_call(
        _inproj_kernel,
        grid=(m // tm,),
        in_specs=[
            pl.BlockSpec((tm, d), lambda i: (i, 0)),
            pl.BlockSpec((1, d), const),
            pl.BlockSpec((d, n_all), const),
            pl.BlockSpec((N_VT, d), const),
            pl.BlockSpec((1, N_QK), const),
        ],
        out_specs=[
            pl.BlockSpec((tm, N_QK), lambda i: (i, 0)),
            pl.BlockSpec((1, tm // ATTN_TILE, N_VT, ATTN_TILE),
                         lambda i: (i // per_seq, i % per_seq, 0, 0)),
            pl.BlockSpec((tm, N_GDN), lambda i: (i, 0)),
            pl.BlockSpec((tm, N_GLA), lambda i: (i, 0)),
            pl.BlockSpec((tm, N_MISC), lambda i: (i, 0)),
            pl.BlockSpec((1, tm // MOBA_BLOCK, GROUP_WIDTH), lambda i: (i, 0, 0)),
        ],
        out_shape=[
            jax.ShapeDtypeStruct((m, N_QK), BF16),
            jax.ShapeDtypeStruct((m // seq, seq // ATTN_TILE, N_VT, ATTN_TILE), BF16),
            jax.ShapeDtypeStruct((m, N_GDN), F32),
            jax.ShapeDtypeStruct((m, N_GLA), F32),
            jax.ShapeDtypeStruct((m, N_MISC), F32),
            jax.ShapeDtypeStruct((m // tm, tm // MOBA_BLOCK, GROUP_WIDTH), F32),
        ],
        compiler_params=pltpu.CompilerParams(
            dimension_semantics=("arbitrary",), vmem_limit_bytes=VMEM_LIMIT),
        name="inproj",
    )(h2d, g, w, wvt, scale)


def _lane_mask(lo, width):
    lane = lax.broadcasted_iota(jnp.int32, (1, LANES), 1)
    return (lane >= lo) & (lane < lo + width)


def _row_mask(lo, width):
    row = lax.broadcasted_iota(jnp.int32, (LANES, 1), 0)
    return (row >= lo) & (row < lo + width)


def _values_with_ones(vt, hh):
    return jnp.where(_row_mask(HEAD_DIM * hh, HEAD_DIM), vt, jnp.ones_like(vt))


def _attend_init(n_chains, t):
    return tuple((jnp.full((1, t), NEG, F32), jnp.zeros((LANES, t), F32)) for _ in range(n_chains))


def _attend(state, scores, vts):
    stats = []
    for (m, _), s in zip(state, scores):
        m_new = jnp.maximum(m, jnp.max(s, axis=0, keepdims=True))
        stats.append((m_new, jnp.exp(m - m_new), jnp.exp(s - m_new).astype(BF16)))
    return tuple((m_new, alpha * acc + _dot(vt, p))
                 for (m_new, alpha, p), (_, acc), vt in zip(stats, state, vts))


def _attend_pipelined(state, s_first, n_far, scores_fn, values_fn, s_ref):
    last = jnp.maximum(n_far - 1, 0)
    n_chains = len(s_first)
    for c in range(n_chains):
        s_ref[c] = s_first[c]

    def body(j, state):
        s_next = scores_fn(jnp.minimum(j + 1, last))
        state = _attend(state, [s_ref[c] for c in range(n_chains)], values_fn(j))
        for c in range(n_chains):
            s_ref[c] = s_next[c]
        return state

    return lax.fori_loop(0, n_far, body, state)


def _normalised(acc, hh):
    other = HEAD_DIM * (1 - hh)
    return acc / acc[other:other + 1, :]


def _moba_kernel(q_ref, k_ref, vt_ref, kmean_ref, bias_ref, o_ref, pen_ref, s_ref):
    t = ATTN_TILE
    qi = pl.program_id(2)
    nblk = kmean_ref.shape[1]
    q = q_ref[0]
    zero = jnp.zeros_like(q)

    row = lax.broadcasted_iota(jnp.int32, (nblk, t), 0)
    qms = []
    for hh in range(2):
        hmask = _lane_mask(HEAD_DIM * hh, HEAD_DIM)
        qm = jnp.where(hmask, q, zero)
        qms.append(qm)
        gate = _dot_nt(kmean_ref[0].astype(BF16), qm)
        gate = jnp.where(row < qi, gate, -jnp.inf)
        sel_t = jnp.zeros((nblk, t), F32)
        for j in range(nblk):
            gj = gate[j:j + 1, :]
            beats = jnp.where(gate > gj, 1.0, jnp.where((gate == gj) & (row < j), 1.0, 0.0))
            rank = jnp.sum(beats, axis=0, keepdims=True)
            sel_j = jnp.where(rank < MOBA_TOPK, jnp.where(j < qi, 1.0, 0.0), 0.0)
            sel_t = jnp.where(row == j, sel_j, sel_t)
        pen = jnp.where(sel_t > 0.5, 0.0, NEG)
        for j in range(nblk):
            pen_ref[hh * nblk + j] = pen[j:j + 1, :]

    def keys(j):
        return k_ref[0, pl.ds(pl.multiple_of(j * t, t), t), :]

    def values(j):
        vt = vt_ref[0, j]
        return [_values_with_ones(vt, hh) for hh in range(2)]

    def far_scores(j):
        k = keys(j)
        return [_dot_nt(k, qms[hh]) + pen_ref[hh * nblk + j] for hh in range(2)]

    jp = jnp.maximum(qi - 1, 0)
    k_own, k_prev = keys(qi), keys(jp)
    s_own = [_dot_nt(k_own, qms[hh]) + bias_ref[hh, 0] for hh in range(2)]
    s_prev = [_dot_nt(k_prev, qms[hh]) + bias_ref[hh, 1] + pen_ref[hh * nblk + jp] for hh in range(2)]
    s_far = far_scores(0)
    state = _attend(_attend_init(2, t), s_own, values(qi))
    state = _attend(state, s_prev, values(jp))
    state = _attend_pipelined(state, s_far, qi - 1, far_scores, values, s_ref)
    out_t = jnp.where(_row_mask(0, HEAD_DIM), _normalised(state[0][1], 0), _normalised(state[1][1], 1))
    o_ref[0] = out_t.T.astype(o_ref.dtype)


def _attn_specs(s, t, mixer):
    nkv = s // t
    qcol = mixer * 2 * GROUP_WIDTH // LANES
    kcol = qcol + GROUP_WIDTH // LANES
    vrow = mixer * GROUP_WIDTH // LANES
    return [
        pl.BlockSpec((1, t, LANES), lambda bi, p, qi: (bi, qi, qcol + p)),
        pl.BlockSpec((1, s, LANES), lambda bi, p, qi: (bi, 0, kcol + p)),
        pl.BlockSpec((1, nkv, LANES, t), lambda bi, p, qi: (bi, 0, vrow + p, 0)),
    ]


def _moba(qk, vt, kmean, bias_t):
    b, s, _ = qk.shape
    t = ATTN_TILE
    nq = s // t
    nblk = kmean.shape[1]
    return pl.pallas_call(
        _moba_kernel,
        grid=(b, 2, nq),
        in_specs=_attn_specs(s, t, 0) + [
            pl.BlockSpec((1, nblk, LANES), lambda bi, p, qi: (bi, 0, p)),
            pl.BlockSpec((2, 2, t, t), lambda bi, p, qi: (p, 0, 0, 0)),
        ],
        out_specs=pl.BlockSpec((1, t, LANES), lambda bi, p, qi: (bi, qi, p)),
        out_shape=jax.ShapeDtypeStruct((b, s, GROUP_WIDTH), BF16),
        scratch_shapes=[pltpu.VMEM((2 * nblk, 1, t), F32),
                        pltpu.VMEM((2, t, t), F32)],
        compiler_params=pltpu.CompilerParams(
            dimension_semantics=("arbitrary", "arbitrary", "arbitrary"),
            vmem_limit_bytes=VMEM_LIMIT),
        name="moba",
    )(qk, qk, vt, kmean, bias_t)


def _diff_kernel(lam_ref, q_ref, k_ref, vt_ref, bias_ref, gnorm_ref, o_ref, s_ref):
    t = ATTN_TILE
    qi = pl.program_id(2)
    q = q_ref[0]
    zero = jnp.zeros_like(q)
    qms = [jnp.where(_lane_mask(HEAD_DIM * hh + DIFF_QK_DIM * mm, DIFF_QK_DIM), q, zero)
           for hh in range(2) for mm in range(2)]

    def keys(j):
        return k_ref[0, pl.ds(pl.multiple_of(j * t, t), t), :]

    def values(j):
        vt = vt_ref[0, j]
        vts = [_values_with_ones(vt, hh) for hh in range(2)]
        return [vts[c // 2] for c in range(4)]

    def far_scores(j):
        k = keys(j)
        return [_dot_nt(k, qms[c]) for c in range(4)]

    jp = jnp.maximum(qi - 1, 0)
    off = jnp.where(qi >= 1, 0.0, NEG)
    k_own, k_prev = keys(qi), keys(jp)
    s_own = [_dot_nt(k_own, qms[c]) + bias_ref[c // 2, 0] for c in range(4)]
    s_prev = [_dot_nt(k_prev, qms[c]) + (bias_ref[c // 2, 1] + off) for c in range(4)]
    s_far = far_scores(0)
    carry = _attend(_attend_init(4, t), s_own, values(qi))
    carry = _attend(carry, s_prev, values(jp))
    carry = _attend_pipelined(carry, s_far, qi - 1, far_scores, values, s_ref)

    lam_p = lam_ref[...]
    lam_init = lam_p[4:5, 0:1]
    lam = (jnp.exp(jnp.sum(lam_p[0:1] * lam_p[1:2], axis=-1, keepdims=True))
           - jnp.exp(jnp.sum(lam_p[2:3] * lam_p[3:4], axis=-1, keepdims=True)) + lam_init)
    outs = []
    for hh in range(2):
        o = _normalised(carry[2 * hh][1], hh) - lam * _normalised(carry[2 * hh + 1][1], hh)
        hmask = _row_mask(HEAD_DIM * hh, HEAD_DIM)
        ms = jnp.sum(jnp.where(hmask, o * o, 0.0), axis=0, keepdims=True) * (1.0 / HEAD_DIM)
        outs.append(o * lax.rsqrt(ms + EPS))
    y_t = jnp.where(_row_mask(0, HEAD_DIM), outs[0], outs[1])
    o_ref[0] = (y_t.T * gnorm_ref[...] * (1.0 - lam_init)).astype(o_ref.dtype)


def _diff(qk, vt, lam_p, bias_t, gnorm):
    b, s, _ = qk.shape
    t = ATTN_TILE
    nq = s // t
    return pl.pallas_call(
        _diff_kernel,
        grid=(b, 2, nq),
        in_specs=[pl.BlockSpec(lam_p.shape, lambda bi, p, qi: (0, 0))] + _attn_specs(s, t, 1) + [
            pl.BlockSpec((2, 2, t, t), lambda bi, p, qi: (p, 0, 0, 0)),
            pl.BlockSpec((1, LANES), lambda bi, p, qi: (0, 0)),
        ],
        out_specs=pl.BlockSpec((1, t, LANES), lambda bi, p, qi: (bi, qi, p)),
        out_shape=jax.ShapeDtypeStruct((b, s, GROUP_WIDTH), BF16),
        scratch_shapes=[pltpu.VMEM((4, t, t), F32)],
        compiler_params=pltpu.CompilerParams(
            dimension_semantics=("arbitrary", "arbitrary", "arbitrary"),
            vmem_limit_bytes=VMEM_LIMIT),
        name="diff_attn",
    )(lam_p, qk, qk, vt, bias_t, gnorm)


def _head_of_lane(n_lanes, width):
    return lax.broadcasted_iota(jnp.int32, (1, n_lanes), 1) // width


def _block_rows(x, lane_head, n_heads=N_HEADS_G):
    zero = jnp.zeros_like(x)
    return jnp.concatenate([jnp.where(lane_head == h, x, zero) for h in range(n_heads)], axis=0)


def _gdn_kernel(x_ref, misc_ref, conv_ref, hp_ref, gnorm_ref, o_ref, s_ref, tail_ref, g_ref, beta_ref):
    L = CHUNK
    W = GROUP_WIDTH
    rows = range(x_ref.shape[0])
    seq = x_ref.shape[1]
    n_chunks = seq // L
    head_w = _head_of_lane(W, HEAD_DIM)
    head_all = _head_of_lane(LANES, 1)
    ri = lax.broadcasted_iota(jnp.int32, (L, W), 0)
    cj = lax.broadcasted_iota(jnp.int32, (L, W), 1) % HEAD_DIM
    lower = ri >= cj
    strict = ri > cj
    ident = jnp.where(ri == cj, 1.0, 0.0)
    ones_blk = (lax.broadcasted_iota(jnp.int32, (W, W), 0) // HEAD_DIM
                == lax.broadcasted_iota(jnp.int32, (W, W), 1) // HEAD_DIM)
    ones_seg = ones_blk.astype(BF16)
    tril = (lax.broadcasted_iota(jnp.int32, (L, L), 0)
            >= lax.broadcasted_iota(jnp.int32, (L, L), 1)).astype(BF16)

    hp = hp_ref[...]
    exp_g = (lax.broadcasted_iota(jnp.int32, (LANES, W), 0) - MISC_A
             == lax.broadcasted_iota(jnp.int32, (LANES, W), 1) // HEAD_DIM).astype(BF16)
    exp_b = (lax.broadcasted_iota(jnp.int32, (LANES, W), 0) - MISC_B
             == lax.broadcasted_iota(jnp.int32, (LANES, W), 1) // HEAD_DIM).astype(BF16)
    del head_all
    for r in rows:
        misc = misc_ref[r]
        sp_in = misc + hp[1:2]
        softplus = jnp.maximum(sp_in, 0.0) + jnp.log(1.0 + jnp.exp(-jnp.abs(sp_in)))
        g_tok = -jnp.exp(hp[0:1]) * softplus
        b_tok = _sigmoid(misc)
        g_ref[r] = _dot_exact_rhs(g_tok, exp_g, 3)
        beta_ref[r] = _dot_exact_rhs(b_tok, exp_b, 3)

    @pl.when(pl.program_id(1) == 0)
    def _():
        s_ref[...] = jnp.zeros_like(s_ref)
        tail_ref[...] = jnp.zeros_like(tail_ref)

    cw = conv_ref[...]
    blk = lambda a: _block_rows(a.astype(BF16), head_w)
    bf = lambda a: a.astype(BF16)

    def prepare(r, r0):
        x = x_ref[r, pl.ds(r0, L), :]
        qkv = x[:, 0:3 * W]
        xx = jnp.concatenate([tail_ref[r], qkv], axis=0)
        tail_ref[r] = qkv[L - 8:L, :]
        conv = cw[CONV_WIDTH - 1:CONV_WIDTH] * qkv
        for i in range(CONV_WIDTH - 1):
            lo = 8 - (CONV_WIDTH - 1) + i
            conv = conv + cw[i:i + 1] * xx[lo:lo + L, :]
        conv = _silu(conv)
        return conv[:, 0:W], conv[:, W:2 * W], conv[:, 2 * W:3 * W], x[:, 3 * W:4 * W]

    def decays(gc):
        gc_col = jnp.sum(gc * ident, axis=0, keepdims=True)
        return jnp.exp(jnp.where(lower, gc - gc_col, -jnp.inf))

    def chunk(c, _):
        r0 = pl.multiple_of(c * L, L)
        q, k, v, z = zip(*[prepare(r, r0) for r in rows])
        ssq = [_dot_exact_rhs(jnp.concatenate([q[r] * q[r], k[r] * k[r]], axis=0), ones_seg)
               for r in rows]
        q = [q[r] * lax.rsqrt(ssq[r][0:L] + EPS) * (HEAD_DIM ** -0.5) for r in rows]
        k = [k[r] * lax.rsqrt(ssq[r][L:2 * L] + EPS) for r in rows]
        beta = [beta_ref[r, pl.ds(r0, L), :] for r in rows]
        gc = [_dot_exact_lhs(tril, g_ref[r, pl.ds(r0, L), :]) for r in rows]
        egc = [jnp.exp(gc[r]) for r in rows]
        gc_last = [gc[r][L - 1:L, :] for r in rows]
        decay = [decays(gc[r]) for r in rows]
        kb = [k[r] * beta[r] for r in rows]
        both = [_dot_nt(bf(jnp.concatenate([kb[r], q[r]], axis=0)), blk(k[r])) for r in rows]
        qk = [jnp.where(lower, both[r][L:2 * L] * decay[r], 0.0) for r in rows]

        p = [jnp.where(strict, -both[r][0:L] * decay[r], 0.0) for r in rows]
        t_inv = [ident + p[r] for r in rows]
        p = [_dot(bf(p[r]), blk(p[r])) for r in rows]
        for _ in range(4):
            prod = [_dot(bf(jnp.concatenate([t_inv[r], p[r]], axis=0)), blk(p[r])) for r in rows]
            t_inv = [t_inv[r] + prod[r][0:L] for r in rows]
            p = [prod[r][L:2 * L] for r in rows]
        t_inv = [t_inv[r] + _dot(bf(t_inv[r]), blk(p[r])) for r in rows]

        u = [_dot(bf(t_inv[r]), blk(v[r] * beta[r])) for r in rows]
        w = [_dot(bf(t_inv[r]), blk(kb[r] * egc[r])) for r in rows]

        state = [s_ref[r] for r in rows]
        ws_qs = [_dot(bf(jnp.concatenate([w[r], q[r] * egc[r]], axis=0)), bf(state[r])) for r in rows]
        v_new = [u[r] - ws_qs[r][0:L] for r in rows]
        o = [ws_qs[r][L:2 * L] + _dot(bf(qk[r]), blk(v_new[r])) for r in rows]
        upd = [_dot_tn(bf(k[r] * jnp.exp(gc_last[r] - gc[r])), bf(v_new[r])) for r in rows]
        for r in rows:
            s_ref[r] = state[r] * jnp.exp(gc_last[r]) + jnp.where(ones_blk, upd[r], 0.0)

        ms = [_dot_exact_rhs(o[r] * o[r], ones_seg) * (1.0 / HEAD_DIM) for r in rows]
        for r in rows:
            y = o[r] * lax.rsqrt(ms[r] + EPS) * gnorm_ref[...] * _silu(z[r])
            o_ref[r, pl.ds(r0, L), :] = y.astype(o_ref.dtype)
        return 0

    lax.fori_loop(0, n_chunks, chunk, 0)


def _gdn(gdn, misc, conv_w, hp, gnorm):
    b, s, _ = gdn.shape
    ts = min(s, GDN_SEQ_TILE)
    nr = math.gcd(b, GDN_ROWS)
    const = lambda bi, si: (0, 0)
    return pl.pallas_call(
        _gdn_kernel,
        grid=(b // nr, s // ts),
        in_specs=[
            pl.BlockSpec((nr, ts, N_GDN), lambda bi, si: (bi, si, 0)),
            pl.BlockSpec((nr, ts, N_MISC), lambda bi, si: (bi, si, 0)),
            pl.BlockSpec(conv_w.shape, const),
            pl.BlockSpec(hp.shape, const),
            pl.BlockSpec((1, GROUP_WIDTH), const),
        ],
        out_specs=pl.BlockSpec((nr, ts, GROUP_WIDTH), lambda bi, si: (bi, si, 0)),
        out_shape=jax.ShapeDtypeStruct((b, s, GROUP_WIDTH), BF16),
        scratch_shapes=[
            pltpu.VMEM((nr, GROUP_WIDTH, GROUP_WIDTH), F32),
            pltpu.VMEM((nr, 8, 3 * GROUP_WIDTH), F32),
            pltpu.VMEM((nr, ts, GROUP_WIDTH), F32),
            pltpu.VMEM((nr, ts, GROUP_WIDTH), F32),
        ],
        compiler_params=pltpu.CompilerParams(
            dimension_semantics=("arbitrary", "arbitrary"), vmem_limit_bytes=VMEM_LIMIT),
        name="gdn",
    )(gdn, misc, conv_w, hp, gnorm)


def _gla_kernel(x_ref, misc_ref, walpha_ref, balpha_ref, gnorm_ref, o_ref, s_ref, la_ref):
    L = CHUNK
    W = GROUP_WIDTH
    KW = N_HEADS_G * GLA_DK
    seq = x_ref.shape[1]
    n_chunks = seq // L
    head_k = _head_of_lane(KW, GLA_DK)
    head_v = _head_of_lane(W, HEAD_DIM)
    ri = lax.broadcasted_iota(jnp.int32, (L, W), 0)
    cj = lax.broadcasted_iota(jnp.int32, (L, W), 1) % HEAD_DIM
    lower = ri >= cj
    ones_blk = (lax.broadcasted_iota(jnp.int32, (W, W), 0) // HEAD_DIM
                == lax.broadcasted_iota(jnp.int32, (W, W), 1) // HEAD_DIM).astype(BF16)
    state_mask = (lax.broadcasted_iota(jnp.int32, (W, KW), 0) // HEAD_DIM
                  == lax.broadcasted_iota(jnp.int32, (W, KW), 1) // GLA_DK)
    tril = (lax.broadcasted_iota(jnp.int32, (L, L), 0)
            >= lax.broadcasted_iota(jnp.int32, (L, L), 1)).astype(BF16)

    pre = _dot(misc_ref[0].astype(BF16), walpha_ref[...]) + balpha_ref[...]
    log_sig = jnp.minimum(pre, 0.0) - jnp.log(1.0 + jnp.exp(-jnp.abs(pre)))
    la_ref[...] = log_sig * (1.0 / GLA_TAU)

    @pl.when(pl.program_id(1) == 0)
    def _():
        s_ref[...] = jnp.zeros_like(s_ref)

    def chunk(c, _):
        r0 = pl.multiple_of(c * L, L)
        x = x_ref[0, pl.ds(r0, L), :]
        q = x[:, 0:KW] * (GLA_DK ** -0.5)
        k = x[:, KW:2 * KW]
        v = x[:, 2 * KW:2 * KW + W]
        bc = _dot_exact_lhs(tril, la_ref[pl.ds(r0, L), :])
        b_last = bc[L - 1:L, :]
        qe = (q * jnp.exp(bc)).astype(BF16)
        ke = (k * jnp.exp(-bc)).astype(BF16)
        a_mat = jnp.where(lower, _dot_nt(qe, _block_rows(ke, head_k)), 0.0)
        state_t = s_ref[...]
        o = _dot_nt(qe, state_t.astype(BF16)) + _dot(a_mat.astype(BF16),
                                                    _block_rows(v.astype(BF16), head_v))
        k_dec = (k * jnp.exp(b_last - bc)).astype(BF16)
        upd = _dot_tn(v.astype(BF16), k_dec)
        s_ref[...] = state_t * jnp.exp(b_last) + jnp.where(state_mask, upd, 0.0)

        ms = _dot_exact_rhs(o * o, ones_blk) * (1.0 / HEAD_DIM)
        y = o * lax.rsqrt(ms + EPS) * gnorm_ref[...] * _silu(x[:, 2 * KW + W:2 * KW + 2 * W])
        o_ref[0, pl.ds(r0, L), :] = y.astype(o_ref.dtype)
        return 0

    lax.fori_loop(0, n_chunks, chunk, 0)


def _gla(gla, misc, walpha, balpha, gnorm):
    b, s, _ = gla.shape
    ts = min(s, SEQ_TILE)
    const = lambda bi, si: (0, 0)
    return pl.pallas_call(
        _gla_kernel,
        grid=(b, s // ts),
        in_specs=[
            pl.BlockSpec((1, ts, N_GLA), lambda bi, si: (bi, si, 0)),
            pl.BlockSpec((1, ts, N_MISC), lambda bi, si: (bi, si, 0)),
            pl.BlockSpec(walpha.shape, const),
            pl.BlockSpec(balpha.shape, const),
            pl.BlockSpec((1, GROUP_WIDTH), const),
        ],
        out_specs=pl.BlockSpec((1, ts, GROUP_WIDTH), lambda bi, si: (bi, si, 0)),
        out_shape=jax.ShapeDtypeStruct((b, s, GROUP_WIDTH), BF16),
        scratch_shapes=[
            pltpu.VMEM((GROUP_WIDTH, N_HEADS_G * GLA_DK), F32),
            pltpu.VMEM((ts, N_HEADS_G * GLA_DK), F32),
        ],
        compiler_params=pltpu.CompilerParams(
            dimension_semantics=("arbitrary", "arbitrary"), vmem_limit_bytes=VMEM_LIMIT),
        name="gla",
    )(gla, misc, walpha, balpha, gnorm)


def _ffn_chunks(d_ff):
    step = 4 * GROUP_WIDTH
    return [(lo, min(step, d_ff - lo)) for lo in range(0, d_ff, step)]


def _mlp_kernel(h_ref, ya_ref, yb_ref, yc_ref, yd_ref, wout_ref, gffn_ref, wg_ref, wu_ref, wd_ref,
                gple_ref, wpg_ref, p_ref, wpp_ref, gfin_ref, o_ref, act_ref, *, final):
    mixed = jnp.concatenate([ya_ref[...], yb_ref[...], yc_ref[...], yd_ref[...]], axis=-1)
    h = h_ref[...] + _dot(mixed, wout_ref[...])
    hn = _rms(h, gffn_ref[...]).astype(BF16)
    for lo, width in _ffn_chunks(wg_ref.shape[1]):
        gate = _dot(hn, wg_ref[:, lo:lo + width])
        up = _dot(hn, wu_ref[:, lo:lo + width])
        act_ref[:, lo:lo + width] = (_silu(gate) * up).astype(BF16)
    h = h + _dot(act_ref[...], wd_ref[...])
    gate = _sigmoid(_dot(_rms(h, gple_ref[...]).astype(BF16), wpg_ref[...]))
    h = h + gate * _dot(p_ref[...].astype(BF16), wpp_ref[...])
    if final:
        h = _rms(h, gfin_ref[...])
    o_ref[...] = h


def _mlp(h2d, ys, wout, gffn, wg, wu, wd, gple, wpg, p2d, wpp, gfin, tm, final):
    m, d = h2d.shape
    d_ff = wg.shape[1]
    const = lambda i: (0, 0)
    resident = lambda shape: pl.BlockSpec(shape, const, pipeline_mode=pl.Buffered(1))
    rows = lambda width: pl.BlockSpec((tm, width), lambda i: (i, 0))
    return pl.pallas_call(
        functools.partial(_mlp_kernel, final=final),
        grid=(m // tm,),
        in_specs=[rows(d)] + [rows(GROUP_WIDTH)] * 4 + [
            resident(wout.shape), resident(gffn.shape), resident(wg.shape), resident(wu.shape),
            resident(wd.shape), resident(gple.shape), resident(wpg.shape),
            rows(p2d.shape[1]), resident(wpp.shape), resident(gfin.shape),
        ],
        out_specs=rows(d),
        out_shape=jax.ShapeDtypeStruct((m, d), F32),
        scratch_shapes=[pltpu.VMEM((tm, d_ff), BF16)],
        compiler_params=pltpu.CompilerParams(
            dimension_semantics=("arbitrary",), vmem_limit_bytes=VMEM_LIMIT),
        name="mlp",
    )(h2d, *ys, wout, gffn, wg, wu, wd, gple, wpg, p2d, wpp, gfin)


def _rel_bucket_table(n_dist):
    n = np.arange(n_dist)
    nf = np.maximum(n, 1).astype(np.float32)
    large = REL_MAX_EXACT + (np.log(nf / REL_MAX_EXACT) / math.log(REL_MAX_DIST / REL_MAX_EXACT)
                             * (N_REL_BUCKETS - REL_MAX_EXACT)).astype(np.int32)
    large = np.minimum(large, N_REL_BUCKETS - 1)
    return np.where(n < REL_MAX_EXACT, n, large)


def _bias_tiles(rel_bias):
    t = ATTN_TILE
    bucket = _rel_bucket_table(2 * t)
    assert (bucket[t + 1:] == N_REL_BUCKETS - 1).all()
    per_dist = rel_bias[bucket, :] - rel_bias[N_REL_BUCKETS - 1][None, :]
    i = np.arange(t)[:, None]
    j = np.arange(t)[None, :]
    dist = np.stack([np.clip(i - j, 0, 2 * t - 1), i - j + t])
    tiles = jnp.transpose(per_dist[dist], (3, 0, 1, 2))
    causal = np.stack([i >= j, np.ones((t, t), bool)])
    return jnp.where(causal[None], tiles, NEG)


def _row(v, width=None):
    v = v.astype(F32).reshape(1, -1)
    if width is not None and v.shape[1] < width:
        v = jnp.pad(v, ((0, 0), (0, width - v.shape[1])))
    return v


def kernel(x, p, norm_mix, w_in, rel_bias, diff_lambda, diff_norm, gdn_conv, gdn_a_log, gdn_dt_bias,
           gdn_norm, gla_w_alpha, gla_b_alpha, gla_norm, w_out, norm_ffn, w_gate, w_up, w_down,
           norm_ple, w_ple_gate, w_ple_proj, final_norm):
    b, s, d = x.shape
    depth = w_in.shape[0]
    m = b * s
    tm = 512
    hg = N_HEADS_G
    assert s % ATTN_TILE == 0 and s % tm == 0 and tm % MOBA_BLOCK == 0

    bias_t = jnp.swapaxes(_bias_tiles(rel_bias.astype(F32)), -1, -2)
    bias_a, bias_b = bias_t[:hg], bias_t[hg:]
    gw = GROUP_WIDTH
    col_scale = np.ones((1, N_QK), np.float32)
    col_scale[:, 0:gw] = HEAD_DIM ** -0.5
    col_scale[:, 2 * gw:3 * gw] = DIFF_QK_DIM ** -0.5
    col_scale = jnp.asarray(col_scale)

    n_main = 6 * gw + N_GDN
    ab = 2 * hg
    gla_lo = n_main + ab
    gla_hi = gla_lo + N_GLA
    fin = _row(final_norm)
    h = x.reshape(m, d)
    for l in range(depth):
        wl = w_in[l]
        w1 = jnp.concatenate(
            [wl[:, 0:2 * gw], wl[:, 3 * gw:5 * gw], wl[:, 6 * gw:n_main], wl[:, gla_lo:gla_hi],
             wl[:, n_main:gla_lo], wl[:, gla_hi:],
             jnp.zeros((d, N_MISC - ab - GLA_GATE_RANK), wl.dtype)], axis=1).astype(BF16)
        wvt = jnp.concatenate([wl[:, 2 * gw:3 * gw], wl[:, 5 * gw:6 * gw]], axis=1).T.astype(BF16)
        qk, vt, gdn, gla, misc, kmean = _inproj(h, _row(norm_mix[l]), w1, wvt, col_scale, tm, s)
        qk = qk.reshape(b, s, N_QK)
        misc = misc.reshape(b, s, N_MISC)
        kmean = kmean.reshape(b, s // MOBA_BLOCK, GROUP_WIDTH)

        y_a = _moba(qk, vt, kmean, bias_a)

        lam_init = 0.8 - 0.6 * math.exp(-0.3 * l)
        lam_p = jnp.concatenate([diff_lambda[l].astype(F32),
                                 jnp.full((1, DIFF_QK_DIM), lam_init, F32)], axis=0)
        y_b = _diff(qk, vt, lam_p, bias_b, _row(jnp.tile(diff_norm[l], 2)))

        hp = jnp.concatenate([_row(gdn_a_log[l], LANES), _row(gdn_dt_bias[l], LANES)], axis=0)
        y_c = _gdn(gdn.reshape(b, s, N_GDN), misc, gdn_conv[l].astype(F32), hp,
                   _row(jnp.tile(gdn_norm[l], hg)))

        walpha = jnp.zeros((N_MISC, hg * GLA_DK), F32).at[MISC_LR:MISC_LR + GLA_GATE_RANK].set(
            gla_w_alpha[l]).astype(BF16)
        y_d = _gla(gla.reshape(b, s, N_GLA), misc, walpha, _row(gla_b_alpha[l]),
                   _row(jnp.tile(gla_norm[l], hg)))

        ys = [y.reshape(m, GROUP_WIDTH) for y in (y_a, y_b, y_c, y_d)]
        h = _mlp(h, ys, w_out[l].astype(BF16), _row(norm_ffn[l]), w_gate[l].astype(BF16),
                 w_up[l].astype(BF16), w_down[l].astype(BF16), _row(norm_ple[l]),
                 w_ple_gate[l].astype(BF16), p[l].reshape(m, -1), w_ple_proj[l].astype(BF16),
                 fin, tm, final=(l == depth - 1))
    return h.reshape(b, s, d)
```

```python
import functools
import math

import numpy as np
import jax
import jax.numpy as jnp
from jax import lax
from jax.experimental import pallas as pl
from jax.experimental.pallas import tpu as pltpu

F32 = jnp.float32
BF16 = jnp.bfloat16

HEAD_DIM = 64
N_HEADS_G = 4
GROUP_WIDTH = HEAD_DIM * N_HEADS_G
MOBA_BLOCK = 256
MOBA_TOPK = 3
N_REL_BUCKETS = 32
REL_MAX_EXACT = 16
REL_MAX_DIST = 128
DIFF_QK_DIM = HEAD_DIM // 2
CONV_WIDTH = 4
CHUNK = 64
GLA_DK = HEAD_DIM // 2
GLA_GATE_RANK = 16
GLA_TAU = 16.0
EPS = 1e-6

LANES = 128
ATTN_TILE = 256
SEQ_TILE = 1024
GDN_SEQ_TILE = 512
GDN_ROWS = 4
NEG = -1e30
LOG2E = math.log2(math.e)
VMEM_LIMIT = 56 * 1024 * 1024

N_QK = 4 * GROUP_WIDTH
N_VT = 2 * GROUP_WIDTH
N_GDN = 4 * GROUP_WIDTH
N_GLA = 2 * N_HEADS_G * GLA_DK + 2 * GROUP_WIDTH
N_MISC = LANES
MISC_A, MISC_B, MISC_LR = 0, N_HEADS_G, 2 * N_HEADS_G

_NT = (((1,), (1,)), ((), ()))
_TN = (((0,), (0,)), ((), ()))


def _dot(a, b):
    return jnp.dot(a, b, preferred_element_type=F32)


def _dot_nt(a, b):
    return lax.dot_general(a, b, _NT, preferred_element_type=F32)


def _dot_tn(a, b):
    return lax.dot_general(a, b, _TN, preferred_element_type=F32)


def _split_bf16(x, n):
    parts = []
    r = x
    for _ in range(n):
        hi = r.astype(BF16)
        parts.append(hi)
        r = r - hi.astype(F32)
    return parts


def _dot_exact_rhs(x, w, n=2):
    acc = None
    for part in _split_bf16(x, n):
        t = _dot(part, w)
        acc = t if acc is None else acc + t
    return acc


def _dot_exact_lhs(w, x, n=3):
    acc = None
    for part in _split_bf16(x, n):
        t = _dot(w, part)
        acc = t if acc is None else acc + t
    return acc


def _rms(x, g):
    return x * lax.rsqrt(jnp.mean(x * x, axis=-1, keepdims=True) + EPS) * g


def _sigmoid(x):
    return 1.0 / (1.0 + jnp.exp(-x))


def _silu(x):
    return x * _sigmoid(x)


def _inproj_kernel(x_ref, g_ref, w_ref, wvt_ref, scale_ref,
                   qk_ref, vt_ref, gdn_ref, gla_ref, misc_ref, kmean_ref):
    tm = x_ref.shape[0]
    xn = _rms(x_ref[...], g_ref[...]).astype(BF16)
    qk = _dot(xn, w_ref[:, 0:N_QK])
    k_moba = qk[:, GROUP_WIDTH:2 * GROUP_WIDTH]
    kmean_ref[0] = jnp.mean(k_moba.reshape(tm // MOBA_BLOCK, MOBA_BLOCK, GROUP_WIDTH), axis=1)
    qk_ref[...] = (qk * scale_ref[...]).astype(BF16)
    vt = _dot_nt(wvt_ref[...], xn).astype(BF16)
    for i in range(tm // ATTN_TILE):
        vt_ref[0, i] = vt[:, i * ATTN_TILE:(i + 1) * ATTN_TILE]
    o = N_QK
    gdn_ref[...] = _dot(xn, w_ref[:, o:o + N_GDN])
    o += N_GDN
    gla_ref[...] = _dot(xn, w_ref[:, o:o + N_GLA])
    o += N_GLA
    misc_ref[...] = _dot(xn, w_ref[:, o:o + N_MISC])


def _inproj(h2d, g, w, wvt, scale, tm, seq):
    m, d = h2d.shape
    n_all = w.shape[1]
    per_seq = seq // tm
    const = lambda i: (0, 0)
    return pl.pallas_call(
        _inproj_kernel,
        grid=(m // tm,),
        in_specs=[
            pl.BlockSpec((tm, d), lambda i: (i, 0)),
            pl.BlockSpec((1, d), const),
            pl.BlockSpec((d, n_all), const),
            pl.BlockSpec((N_VT, d), const),
            pl.BlockSpec((1, N_QK), const),
        ],
        out_specs=[
            pl.BlockSpec((tm, N_QK), lambda i: (i, 0)),
            pl.BlockSpec((1, tm // ATTN_TILE, N_VT, ATTN_TILE),
                         lambda i: (i // per_seq, i % per_seq, 0, 0)),
            pl.BlockSpec((tm, N_GDN), lambda i: (i, 0)),
            pl.BlockSpec((tm, N_GLA), lambda i: (i, 0)),
            pl.BlockSpec((tm, N_MISC), lambda i: (i, 0)),
            pl.BlockSpec((1, tm // MOBA_BLOCK, GROUP_WIDTH), lambda i: (i, 0, 0)),
        ],
        out_shape=[
            jax.ShapeDtypeStruct((m, N_QK), BF16),
            jax.ShapeDtypeStruct((m // seq, seq // ATTN_TILE, N_VT, ATTN_TILE), BF16),
            jax.ShapeDtypeStruct((m, N_GDN), F32),
            jax.ShapeDtypeStruct((m, N_GLA), F32),
            jax.ShapeDtypeStruct((m, N_MISC), F32),
            jax.ShapeDtypeStruct((m // tm, tm // MOBA_BLOCK, GROUP_WIDTH), F32),
        ],
        compiler_params=pltpu.CompilerParams(
            dimension_semantics=("arbitrary",), vmem_limit_bytes=VMEM_LIMIT),
        name="inproj",
    )(h2d, g, w, wvt, scale)


def _lane_mask(lo, width):
    lane = lax.broadcasted_iota(jnp.int32, (1, LANES), 1)
    return (lane >= lo) & (lane < lo + width)


ACC_ROWS = HEAD_DIM + 16
PIPE_UNROLL = 2


def _values_with_ones(vt, hh):
    head = vt[HEAD_DIM * hh:HEAD_DIM * (hh + 1), :]
    return jnp.concatenate([head, jnp.ones((ACC_ROWS - HEAD_DIM, vt.shape[1]), vt.dtype)], axis=0)


def _softmax_stage(ms, scores):
    ms_new, alphas, ps = [], [], []
    for m, s in zip(ms, scores):
        m_new = jnp.maximum(m, jnp.max(s, axis=0, keepdims=True))
        ms_new.append(m_new)
        alphas.append(jnp.exp2(m - m_new))
        ps.append(jnp.exp2(s - m_new).astype(BF16))
    return ms_new, alphas, ps


def _value_products(vts, ps):
    return [_dot(vt, p) for vt, p in zip(vts, ps)]


def _attend_all(n, t, n_far, far_scores, prev_scores, own_scores, values, j_prev, j_own,
                s_ref, p_ref, acc_ref):
    chains = range(n)
    n_slots = jnp.maximum(n_far, 2)

    def slot_scores(j):
        off = jnp.where(j < n_far, 0.0, NEG)
        return [s + off for s in far_scores(j)]

    s0, s1 = slot_scores(0), slot_scores(1)
    ms, alphas, ps = _softmax_stage([jnp.full((1, t), NEG, F32)] * n, s0)
    for c in chains:
        p_ref[c] = ps[c]
        s_ref[c] = s1[c]
        acc_ref[c] = jnp.zeros((ACC_ROWS, t), F32)

    def steps(first, unroll, carry):
        ms, alphas = carry
        ps = [p_ref[c] for c in chains]
        s_cur = [s_ref[c] for c in chains]
        for u in range(unroll):
            pv = _value_products(values(first + u), ps)
            s_new = far_scores(first + u + 2)
            ms, alphas_next, ps = _softmax_stage(ms, s_cur)
            for c in chains:
                acc_ref[c] = alphas[c] * acc_ref[c] + pv[c]
            alphas, s_cur = alphas_next, s_new
        for c in chains:
            p_ref[c] = ps[c]
            s_ref[c] = s_cur[c]
        return ms, alphas

    n_steps = n_slots - 2
    n_blocks = n_steps // PIPE_UNROLL
    carry = lax.fori_loop(0, n_blocks, lambda i, cr: steps(i * PIPE_UNROLL, PIPE_UNROLL, cr),
                          (ms, alphas))
    ms, alphas = lax.fori_loop(n_blocks * PIPE_UNROLL, n_steps, lambda i, cr: steps(i, 1, cr), carry)

    pv = _value_products(values(n_slots - 2), [p_ref[c] for c in chains])
    s_prev = prev_scores()
    ms, alphas1, ps1 = _softmax_stage(ms, [s_ref[c] for c in chains])
    accs = [alphas[c] * acc_ref[c] + pv[c] for c in chains]
    pv = _value_products(values(n_slots - 1), ps1)
    s_own = own_scores()
    ms, alphas2, ps2 = _softmax_stage(ms, s_prev)
    accs = [alphas1[c] * accs[c] + pv[c] for c in chains]
    pv = _value_products(values(j_prev), ps2)
    ms, alphas3, ps3 = _softmax_stage(ms, s_own)
    accs = [alphas2[c] * accs[c] + pv[c] for c in chains]
    pv = _value_products(values(j_own), ps3)
    return [alphas3[c] * accs[c] + pv[c] for c in chains]


def _normalised(acc):
    return acc[0:HEAD_DIM, :] / acc[HEAD_DIM:HEAD_DIM + 1, :]


def _moba_kernel(q_ref, k_ref, vt_ref, kmean_ref, bias_ref, o_ref, pen_ref, s_ref, p_ref, acc_ref):
    t = ATTN_TILE
    qi = pl.program_id(2)
    nblk = kmean_ref.shape[1]
    q = q_ref[0]
    zero = jnp.zeros_like(q)

    row = lax.broadcasted_iota(jnp.int32, (nblk, t), 0)
    qms = []
    for hh in range(2):
        hmask = _lane_mask(HEAD_DIM * hh, HEAD_DIM)
        qm = jnp.where(hmask, q, zero)
        qms.append(qm)
        gate = _dot_nt(kmean_ref[0].astype(BF16), qm)
        gate = jnp.where(row < qi, gate, -jnp.inf)
        sel_t = jnp.zeros((nblk, t), F32)
        for j in range(nblk):
            gj = gate[j:j + 1, :]
            beats = jnp.where(gate > gj, 1.0, jnp.where((gate == gj) & (row < j), 1.0, 0.0))
            rank = jnp.sum(beats, axis=0, keepdims=True)
            sel_j = jnp.where(rank < MOBA_TOPK, jnp.where(j < qi, 1.0, 0.0), 0.0)
            sel_t = jnp.where(row == j, sel_j, sel_t)
        pen = jnp.where(sel_t > 0.5, 0.0, NEG)
        for j in range(nblk):
            pen_ref[hh * nblk + j] = pen[j:j + 1, :]

    def keys(j):
        return k_ref[0, pl.ds(pl.multiple_of(j * t, t), t), :]

    def values(j):
        vt = vt_ref[0, j]
        return [_values_with_ones(vt, hh) for hh in range(2)]

    def far_scores(j):
        k = keys(j)
        return [_dot_nt(k, qms[hh]) + pen_ref[hh * nblk + j] for hh in range(2)]

    jp = jnp.maximum(qi - 1, 0)

    def prev_scores():
        k = keys(jp)
        return [_dot_nt(k, qms[hh]) + bias_ref[hh, 1] + pen_ref[hh * nblk + jp] for hh in range(2)]

    def own_scores():
        k = keys(qi)
        return [_dot_nt(k, qms[hh]) + bias_ref[hh, 0] for hh in range(2)]

    accs = _attend_all(2, t, jnp.maximum(qi - 1, 0), far_scores, prev_scores, own_scores, values,
                       jp, qi, s_ref, p_ref, acc_ref)
    out_t = jnp.concatenate([_normalised(accs[0]), _normalised(accs[1])], axis=0)
    o_ref[0] = out_t.T.astype(o_ref.dtype)


def _attn_specs(s, t, mixer):
    nkv = s // t
    qcol = mixer * 2 * GROUP_WIDTH // LANES
    kcol = qcol + GROUP_WIDTH // LANES
    vrow = mixer * GROUP_WIDTH // LANES
    return [
        pl.BlockSpec((1, t, LANES), lambda bi, p, qi: (bi, qi, qcol + p)),
        pl.BlockSpec((1, s, LANES), lambda bi, p, qi: (bi, 0, kcol + p)),
        pl.BlockSpec((1, nkv, LANES, t), lambda bi, p, qi: (bi, 0, vrow + p, 0)),
    ]


def _attn_scratch(n_chains, t):
    return [pltpu.VMEM((n_chains, t, t), F32),
            pltpu.VMEM((n_chains, t, t), BF16),
            pltpu.VMEM((n_chains, ACC_ROWS, t), F32)]


def _moba(qk, vt, kmean, bias_t):
    b, s, _ = qk.shape
    t = ATTN_TILE
    nq = s // t
    nblk = kmean.shape[1]
    return pl.pallas_call(
        _moba_kernel,
        grid=(b, 2, nq),
        in_specs=_attn_specs(s, t, 0) + [
            pl.BlockSpec((1, nblk, LANES), lambda bi, p, qi: (bi, 0, p)),
            pl.BlockSpec((2, 2, t, t), lambda bi, p, qi: (p, 0, 0, 0)),
        ],
        out_specs=pl.BlockSpec((1, t, LANES), lambda bi, p, qi: (bi, qi, p)),
        out_shape=jax.ShapeDtypeStruct((b, s, GROUP_WIDTH), BF16),
        scratch_shapes=[pltpu.VMEM((2 * nblk, 1, t), F32)]
        + _attn_scratch(2, t),
        compiler_params=pltpu.CompilerParams(
            dimension_semantics=("arbitrary", "arbitrary", "arbitrary"),
            vmem_limit_bytes=VMEM_LIMIT),
        name="moba",
    )(qk, qk, vt, kmean, bias_t)


def _diff_kernel(lam_ref, q_ref, k_ref, vt_ref, bias_ref, gnorm_ref, o_ref, s_ref, p_ref, acc_ref):
    t = ATTN_TILE
    qi = pl.program_id(2)
    q = q_ref[0]
    zero = jnp.zeros_like(q)
    qms = [jnp.where(_lane_mask(HEAD_DIM * hh + DIFF_QK_DIM * mm, DIFF_QK_DIM), q, zero)
           for hh in range(2) for mm in range(2)]

    def keys(j):
        return k_ref[0, pl.ds(pl.multiple_of(j * t, t), t), :]

    def values(j):
        vt = vt_ref[0, j]
        vts = [_values_with_ones(vt, hh) for hh in range(2)]
        return [vts[c // 2] for c in range(4)]

    def far_scores(j):
        k = keys(j)
        return [_dot_nt(k, qms[c]) for c in range(4)]

    jp = jnp.maximum(qi - 1, 0)
    off = jnp.where(qi >= 1, 0.0, NEG)

    def prev_scores():
        k = keys(jp)
        return [_dot_nt(k, qms[c]) + (bias_ref[c // 2, 1] + off) for c in range(4)]

    def own_scores():
        k = keys(qi)
        return [_dot_nt(k, qms[c]) + bias_ref[c // 2, 0] for c in range(4)]

    accs = _attend_all(4, t, jnp.maximum(qi - 1, 0), far_scores, prev_scores, own_scores, values,
                       jp, qi, s_ref, p_ref, acc_ref)

    lam_p = lam_ref[...]
    lam_init = lam_p[4:5, 0:1]
    lam = (jnp.exp(jnp.sum(lam_p[0:1] * lam_p[1:2], axis=-1, keepdims=True))
           - jnp.exp(jnp.sum(lam_p[2:3] * lam_p[3:4], axis=-1, keepdims=True)) + lam_init)
    outs = []
    for hh in range(2):
        o = _normalised(accs[2 * hh]) - lam * _normalised(accs[2 * hh + 1])
        ms = jnp.mean(o * o, axis=0, keepdims=True)
        outs.append(o * lax.rsqrt(ms + EPS))
    y_t = jnp.concatenate(outs, axis=0)
    o_ref[0] = (y_t.T * gnorm_ref[...] * (1.0 - lam_init)).astype(o_ref.dtype)


def _diff(qk, vt, lam_p, bias_t, gnorm):
    b, s, _ = qk.shape
    t = ATTN_TILE
    nq = s // t
    return pl.pallas_call(
        _diff_kernel,
        grid=(b, 2, nq),
        in_specs=[pl.BlockSpec(lam_p.shape, lambda bi, p, qi: (0, 0))] + _attn_specs(s, t, 1) + [
            pl.BlockSpec((2, 2, t, t), lambda bi, p, qi: (p, 0, 0, 0)),
            pl.BlockSpec((1, LANES), lambda bi, p, qi: (0, 0)),
        ],
        out_specs=pl.BlockSpec((1, t, LANES), lambda bi, p, qi: (bi, qi, p)),
        out_shape=jax.ShapeDtypeStruct((b, s, GROUP_WIDTH), BF16),
        scratch_shapes=_attn_scratch(4, t),
        compiler_params=pltpu.CompilerParams(
            dimension_semantics=("arbitrary", "arbitrary", "arbitrary"),
            vmem_limit_bytes=VMEM_LIMIT),
        name="diff_attn",
    )(lam_p, qk, qk, vt, bias_t, gnorm)


def _head_of_lane(n_lanes, width):
    return lax.broadcasted_iota(jnp.int32, (1, n_lanes), 1) // width


def _block_rows(x, lane_head, n_heads=N_HEADS_G):
    zero = jnp.zeros_like(x)
    return jnp.concatenate([jnp.where(lane_head == h, x, zero) for h in range(n_heads)], axis=0)


def _gdn_kernel(x_ref, misc_ref, conv_ref, hp_ref, gnorm_ref, o_ref, s_ref, tail_ref, g_ref, beta_ref):
    L = CHUNK
    W = GROUP_WIDTH
    rows = range(x_ref.shape[0])
    seq = x_ref.shape[1]
    n_chunks = seq // L
    head_w = _head_of_lane(W, HEAD_DIM)
    head_all = _head_of_lane(LANES, 1)
    ri = lax.broadcasted_iota(jnp.int32, (L, W), 0)
    cj = lax.broadcasted_iota(jnp.int32, (L, W), 1) % HEAD_DIM
    lower = ri >= cj
    strict = ri > cj
    ident = jnp.where(ri == cj, 1.0, 0.0)
    ones_blk = (lax.broadcasted_iota(jnp.int32, (W, W), 0) // HEAD_DIM
                == lax.broadcasted_iota(jnp.int32, (W, W), 1) // HEAD_DIM)
    ones_seg = ones_blk.astype(BF16)
    tril = (lax.broadcasted_iota(jnp.int32, (L, L), 0)
            >= lax.broadcasted_iota(jnp.int32, (L, L), 1)).astype(BF16)

    hp = hp_ref[...]
    exp_g = (lax.broadcasted_iota(jnp.int32, (LANES, W), 0) - MISC_A
             == lax.broadcasted_iota(jnp.int32, (LANES, W), 1) // HEAD_DIM).astype(BF16)
    exp_b = (lax.broadcasted_iota(jnp.int32, (LANES, W), 0) - MISC_B
             == lax.broadcasted_iota(jnp.int32, (LANES, W), 1) // HEAD_DIM).astype(BF16)
    del head_all
    for r in rows:
        misc = misc_ref[r]
        sp_in = misc + hp[1:2]
        softplus = jnp.maximum(sp_in, 0.0) + jnp.log(1.0 + jnp.exp(-jnp.abs(sp_in)))
        g_tok = -jnp.exp(hp[0:1]) * softplus
        b_tok = _sigmoid(misc)
        g_ref[r] = _dot_exact_rhs(g_tok, exp_g, 3)
        beta_ref[r] = _dot_exact_rhs(b_tok, exp_b, 3)

    @pl.when(pl.program_id(1) == 0)
    def _():
        s_ref[...] = jnp.zeros_like(s_ref)
        tail_ref[...] = jnp.zeros_like(tail_ref)

    cw = conv_ref[...]
    blk = lambda a: _block_rows(a.astype(BF16), head_w)
    bf = lambda a: a.astype(BF16)

    def prepare(r, r0):
        x = x_ref[r, pl.ds(r0, L), :]
        qkv = x[:, 0:3 * W]
        xx = jnp.concatenate([tail_ref[r], qkv], axis=0)
        tail_ref[r] = qkv[L - 8:L, :]
        conv = cw[CONV_WIDTH - 1:CONV_WIDTH] * qkv
        for i in range(CONV_WIDTH - 1):
            lo = 8 - (CONV_WIDTH - 1) + i
            conv = conv + cw[i:i + 1] * xx[lo:lo + L, :]
        conv = _silu(conv)
        return conv[:, 0:W], conv[:, W:2 * W], conv[:, 2 * W:3 * W], x[:, 3 * W:4 * W]

    def decays(gc):
        gc_col = jnp.sum(gc * ident, axis=0, keepdims=True)
        return jnp.exp(jnp.where(lower, gc - gc_col, -jnp.inf))

    def chunk(c, _):
        r0 = pl.multiple_of(c * L, L)
        q, k, v, z = zip(*[prepare(r, r0) for r in rows])
        ssq = [_dot_exact_rhs(jnp.concatenate([q[r] * q[r], k[r] * k[r]], axis=0), ones_seg)
               for r in rows]
        q = [q[r] * lax.rsqrt(ssq[r][0:L] + EPS) * (HEAD_DIM ** -0.5) for r in rows]
        k = [k[r] * lax.rsqrt(ssq[r][L:2 * L] + EPS) for r in rows]
        beta = [beta_ref[r, pl.ds(r0, L), :] for r in rows]
        gc = [_dot_exact_lhs(tril, g_ref[r, pl.ds(r0, L), :]) for r in rows]
        egc = [jnp.exp(gc[r]) for r in rows]
        gc_last = [gc[r][L - 1:L, :] for r in rows]
        decay = [decays(gc[r]) for r in rows]
        kb = [k[r] * beta[r] for r in rows]
        both = [_dot_nt(bf(jnp.concatenate([kb[r], q[r]], axis=0)), blk(k[r])) for r in rows]
        qk = [jnp.where(lower, both[r][L:2 * L] * decay[r], 0.0) for r in rows]

        p = [jnp.where(strict, -both[r][0:L] * decay[r], 0.0) for r in rows]
        t_inv = [ident + p[r] for r in rows]
        p = [_dot(bf(p[r]), blk(p[r])) for r in rows]
        for _ in range(4):
            prod = [_dot(bf(jnp.concatenate([t_inv[r], p[r]], axis=0)), blk(p[r])) for r in rows]
            t_inv = [t_inv[r] + prod[r][0:L] for r in rows]
            p = [prod[r][L:2 * L] for r in rows]
        t_inv = [t_inv[r] + _dot(bf(t_inv[r]), blk(p[r])) for r in rows]

        u = [_dot(bf(t_inv[r]), blk(v[r] * beta[r])) for r in rows]
        w = [_dot(bf(t_inv[r]), blk(kb[r] * egc[r])) for r in rows]

        state = [s_ref[r] for r in rows]
        ws_qs = [_dot(bf(jnp.concatenate([w[r], q[r] * egc[r]], axis=0)), bf(state[r])) for r in rows]
        v_new = [u[r] - ws_qs[r][0:L] for r in rows]
        o = [ws_qs[r][L:2 * L] + _dot(bf(qk[r]), blk(v_new[r])) for r in rows]
        upd = [_dot_tn(bf(k[r] * jnp.exp(gc_last[r] - gc[r])), bf(v_new[r])) for r in rows]
        for r in rows:
            s_ref[r] = state[r] * jnp.exp(gc_last[r]) + jnp.where(ones_blk, upd[r], 0.0)

        ms = [_dot_exact_rhs(o[r] * o[r], ones_seg) * (1.0 / HEAD_DIM) for r in rows]
        for r in rows:
            y = o[r] * lax.rsqrt(ms[r] + EPS) * gnorm_ref[...] * _silu(z[r])
            o_ref[r, pl.ds(r0, L), :] = y.astype(o_ref.dtype)
        return 0

    lax.fori_loop(0, n_chunks, chunk, 0)


def _gdn(gdn, misc, conv_w, hp, gnorm):
    b, s, _ = gdn.shape
    ts = min(s, GDN_SEQ_TILE)
    nr = math.gcd(b, GDN_ROWS)
    const = lambda bi, si: (0, 0)
    return pl.pallas_call(
        _gdn_kernel,
        grid=(b // nr, s // ts),
        in_specs=[
            pl.BlockSpec((nr, ts, N_GDN), lambda bi, si: (bi, si, 0)),
            pl.BlockSpec((nr, ts, N_MISC), lambda bi, si: (bi, si, 0)),
            pl.BlockSpec(conv_w.shape, const),
            pl.BlockSpec(hp.shape, const),
            pl.BlockSpec((1, GROUP_WIDTH), const),
        ],
        out_specs=pl.BlockSpec((nr, ts, GROUP_WIDTH), lambda bi, si: (bi, si, 0)),
        out_shape=jax.ShapeDtypeStruct((b, s, GROUP_WIDTH), BF16),
        scratch_shapes=[
            pltpu.VMEM((nr, GROUP_WIDTH, GROUP_WIDTH), F32),
            pltpu.VMEM((nr, 8, 3 * GROUP_WIDTH), F32),
            pltpu.VMEM((nr, ts, GROUP_WIDTH), F32),
            pltpu.VMEM((nr, ts, GROUP_WIDTH), F32),
        ],
        compiler_params=pltpu.CompilerParams(
            dimension_semantics=("arbitrary", "arbitrary"), vmem_limit_bytes=VMEM_LIMIT),
        name="gdn",
    )(gdn, misc, conv_w, hp, gnorm)


def _gla_kernel(x_ref, misc_ref, walpha_ref, balpha_ref, gnorm_ref, o_ref, s_ref, la_ref):
    L = CHUNK
    W = GROUP_WIDTH
    KW = N_HEADS_G * GLA_DK
    seq = x_ref.shape[1]
    n_chunks = seq // L
    head_k = _head_of_lane(KW, GLA_DK)
    head_v = _head_of_lane(W, HEAD_DIM)
    ri = lax.broadcasted_iota(jnp.int32, (L, W), 0)
    cj = lax.broadcasted_iota(jnp.int32, (L, W), 1) % HEAD_DIM
    lower = ri >= cj
    ones_blk = (lax.broadcasted_iota(jnp.int32, (W, W), 0) // HEAD_DIM
                == lax.broadcasted_iota(jnp.int32, (W, W), 1) // HEAD_DIM).astype(BF16)
    state_mask = (lax.broadcasted_iota(jnp.int32, (W, KW), 0) // HEAD_DIM
                  == lax.broadcasted_iota(jnp.int32, (W, KW), 1) // GLA_DK)
    tril = (lax.broadcasted_iota(jnp.int32, (L, L), 0)
            >= lax.broadcasted_iota(jnp.int32, (L, L), 1)).astype(BF16)

    pre = _dot(misc_ref[0].astype(BF16), walpha_ref[...]) + balpha_ref[...]
    log_sig = jnp.minimum(pre, 0.0) - jnp.log(1.0 + jnp.exp(-jnp.abs(pre)))
    la_ref[...] = log_sig * (1.0 / GLA_TAU)

    @pl.when(pl.program_id(1) == 0)
    def _():
        s_ref[...] = jnp.zeros_like(s_ref)

    def chunk(c, _):
        r0 = pl.multiple_of(c * L, L)
        x = x_ref[0, pl.ds(r0, L), :]
        q = x[:, 0:KW] * (GLA_DK ** -0.5)
        k = x[:, KW:2 * KW]
        v = x[:, 2 * KW:2 * KW + W]
        bc = _dot_exact_lhs(tril, la_ref[pl.ds(r0, L), :])
        b_last = bc[L - 1:L, :]
        qe = (q * jnp.exp(bc)).astype(BF16)
        ke = (k * jnp.exp(-bc)).astype(BF16)
        a_mat = jnp.where(lower, _dot_nt(qe, _block_rows(ke, head_k)), 0.0)
        state_t = s_ref[...]
        o = _dot_nt(qe, state_t.astype(BF16)) + _dot(a_mat.astype(BF16),
                                                    _block_rows(v.astype(BF16), head_v))
        k_dec = (k * jnp.exp(b_last - bc)).astype(BF16)
        upd = _dot_tn(v.astype(BF16), k_dec)
        s_ref[...] = state_t * jnp.exp(b_last) + jnp.where(state_mask, upd, 0.0)

        ms = _dot_exact_rhs(o * o, ones_blk) * (1.0 / HEAD_DIM)
        y = o * lax.rsqrt(ms + EPS) * gnorm_ref[...] * _silu(x[:, 2 * KW + W:2 * KW + 2 * W])
        o_ref[0, pl.ds(r0, L), :] = y.astype(o_ref.dtype)
        return 0

    lax.fori_loop(0, n_chunks, chunk, 0)


def _gla(gla, misc, walpha, balpha, gnorm):
    b, s, _ = gla.shape
    ts = min(s, SEQ_TILE)
    const = lambda bi, si: (0, 0)
    return pl.pallas_call(
        _gla_kernel,
        grid=(b, s // ts),
        in_specs=[
            pl.BlockSpec((1, ts, N_GLA), lambda bi, si: (bi, si, 0)),
            pl.BlockSpec((1, ts, N_MISC), lambda bi, si: (bi, si, 0)),
            pl.BlockSpec(walpha.shape, const),
            pl.BlockSpec(balpha.shape, const),
            pl.BlockSpec((1, GROUP_WIDTH), const),
        ],
        out_specs=pl.BlockSpec((1, ts, GROUP_WIDTH), lambda bi, si: (bi, si, 0)),
        out_shape=jax.ShapeDtypeStruct((b, s, GROUP_WIDTH), BF16),
        scratch_shapes=[
            pltpu.VMEM((GROUP_WIDTH, N_HEADS_G * GLA_DK), F32),
            pltpu.VMEM((ts, N_HEADS_G * GLA_DK), F32),
        ],
        compiler_params=pltpu.CompilerParams(
            dimension_semantics=("arbitrary", "arbitrary"), vmem_limit_bytes=VMEM_LIMIT),
        name="gla",
    )(gla, misc, walpha, balpha, gnorm)


def _ffn_chunks(d_ff):
    step = 4 * GROUP_WIDTH
    return [(lo, min(step, d_ff - lo)) for lo in range(0, d_ff, step)]


def _mlp_kernel(h_ref, ya_ref, yb_ref, yc_ref, yd_ref, wout_ref, gffn_ref, wg_ref, wu_ref, wd_ref,
                gple_ref, wpg_ref, p_ref, wpp_ref, gfin_ref, o_ref, act_ref, *, final):
    mixed = jnp.concatenate([ya_ref[...], yb_ref[...], yc_ref[...], yd_ref[...]], axis=-1)
    h = h_ref[...] + _dot(mixed, wout_ref[...])
    hn = _rms(h, gffn_ref[...]).astype(BF16)
    for lo, width in _ffn_chunks(wg_ref.shape[1]):
        gate = _dot(hn, wg_ref[:, lo:lo + width])
        up = _dot(hn, wu_ref[:, lo:lo + width])
        act_ref[:, lo:lo + width] = (_silu(gate) * up).astype(BF16)
    h = h + _dot(act_ref[...], wd_ref[...])
    gate = _sigmoid(_dot(_rms(h, gple_ref[...]).astype(BF16), wpg_ref[...]))
    h = h + gate * _dot(p_ref[...].astype(BF16), wpp_ref[...])
    if final:
        h = _rms(h, gfin_ref[...])
    o_ref[...] = h


def _mlp(h2d, ys, wout, gffn, wg, wu, wd, gple, wpg, p2d, wpp, gfin, tm, final):
    m, d = h2d.shape
    d_ff = wg.shape[1]
    const = lambda i: (0, 0)
    resident = lambda shape: pl.BlockSpec(shape, const, pipeline_mode=pl.Buffered(1))
    rows = lambda width: pl.BlockSpec((tm, width), lambda i: (i, 0))
    return pl.pallas_call(
        functools.partial(_mlp_kernel, final=final),
        grid=(m // tm,),
        in_specs=[rows(d)] + [rows(GROUP_WIDTH)] * 4 + [
            resident(wout.shape), resident(gffn.shape), resident(wg.shape), resident(wu.shape),
            resident(wd.shape), resident(gple.shape), resident(wpg.shape),
            rows(p2d.shape[1]), resident(wpp.shape), resident(gfin.shape),
        ],
        out_specs=rows(d),
        out_shape=jax.ShapeDtypeStruct((m, d), F32),
        scratch_shapes=[pltpu.VMEM((tm, d_ff), BF16)],
        compiler_params=pltpu.CompilerParams(
            dimension_semantics=("arbitrary",), vmem_limit_bytes=VMEM_LIMIT),
        name="mlp",
    )(h2d, *ys, wout, gffn, wg, wu, wd, gple, wpg, p2d, wpp, gfin)


def _rel_bucket_table(n_dist):
    n = np.arange(n_dist)
    nf = np.maximum(n, 1).astype(np.float32)
    large = REL_MAX_EXACT + (np.log(nf / REL_MAX_EXACT) / math.log(REL_MAX_DIST / REL_MAX_EXACT)
                             * (N_REL_BUCKETS - REL_MAX_EXACT)).astype(np.int32)
    large = np.minimum(large, N_REL_BUCKETS - 1)
    return np.where(n < REL_MAX_EXACT, n, large)


def _bias_tiles(rel_bias):
    t = ATTN_TILE
    bucket = _rel_bucket_table(2 * t)
    assert (bucket[t + 1:] == N_REL_BUCKETS - 1).all()
    per_dist = (rel_bias[bucket, :] - rel_bias[N_REL_BUCKETS - 1][None, :]).T * LOG2E
    own = jnp.concatenate([per_dist[:, :t], jnp.full_like(per_dist[:, :t], NEG)], axis=1)
    prev = jnp.concatenate([per_dist[:, t:], per_dist[:, :t]], axis=1)
    vec = jnp.stack([own, prev], axis=1)
    rolled = jnp.tile(vec, (1, 1, t))[:, :, :t * (2 * t - 1)].reshape(-1, 2, t, 2 * t - 1)
    return rolled[:, :, :, :t]


def _row(v, width=None):
    v = v.astype(F32).reshape(1, -1)
    if width is not None and v.shape[1] < width:
        v = jnp.pad(v, ((0, 0), (0, width - v.shape[1])))
    return v


def kernel(x, p, norm_mix, w_in, rel_bias, diff_lambda, diff_norm, gdn_conv, gdn_a_log, gdn_dt_bias,
           gdn_norm, gla_w_alpha, gla_b_alpha, gla_norm, w_out, norm_ffn, w_gate, w_up, w_down,
           norm_ple, w_ple_gate, w_ple_proj, final_norm):
    b, s, d = x.shape
    depth = w_in.shape[0]
    m = b * s
    tm = 512
    hg = N_HEADS_G
    assert s % ATTN_TILE == 0 and s % tm == 0 and tm % MOBA_BLOCK == 0

    bias_t = _bias_tiles(rel_bias.astype(F32))
    bias_a, bias_b = bias_t[:hg], bias_t[hg:]
    gw = GROUP_WIDTH
    col_scale = np.ones((1, N_QK), np.float32)
    col_scale[:, 0:gw] = HEAD_DIM ** -0.5
    col_scale[:, gw:2 * gw] = LOG2E
    col_scale[:, 2 * gw:3 * gw] = DIFF_QK_DIM ** -0.5 * LOG2E
    col_scale = jnp.asarray(col_scale)

    n_main = 6 * gw + N_GDN
    ab = 2 * hg
    gla_lo = n_main + ab
    gla_hi = gla_lo + N_GLA
    fin = _row(final_norm)
    h = x.reshape(m, d)
    for l in range(depth):
        wl = w_in[l]
        w1 = jnp.concatenate(
            [wl[:, 0:2 * gw], wl[:, 3 * gw:5 * gw], wl[:, 6 * gw:n_main], wl[:, gla_lo:gla_hi],
             wl[:, n_main:gla_lo], wl[:, gla_hi:],
             jnp.zeros((d, N_MISC - ab - GLA_GATE_RANK), wl.dtype)], axis=1).astype(BF16)
        wvt = jnp.concatenate([wl[:, 2 * gw:3 * gw], wl[:, 5 * gw:6 * gw]], axis=1).T.astype(BF16)
        qk, vt, gdn, gla, misc, kmean = _inproj(h, _row(norm_mix[l]), w1, wvt, col_scale, tm, s)
        qk = qk.reshape(b, s, N_QK)
        misc = misc.reshape(b, s, N_MISC)
        kmean = kmean.reshape(b, s // MOBA_BLOCK, GROUP_WIDTH)

        y_a = _moba(qk, vt, kmean, bias_a)

        lam_init = 0.8 - 0.6 * math.exp(-0.3 * l)
        lam_p = jnp.concatenate([diff_lambda[l].astype(F32),
                                 jnp.full((1, DIFF_QK_DIM), lam_init, F32)], axis=0)
        y_b = _diff(qk, vt, lam_p, bias_b, _row(jnp.tile(diff_norm[l], 2)))

        hp = jnp.concatenate([_row(gdn_a_log[l], LANES), _row(gdn_dt_bias[l], LANES)], axis=0)
        y_c = _gdn(gdn.reshape(b, s, N_GDN), misc, gdn_conv[l].astype(F32), hp,
                   _row(jnp.tile(gdn_norm[l], hg)))

        walpha = jnp.zeros((N_MISC, hg * GLA_DK), F32).at[MISC_LR:MISC_LR + GLA_GATE_RANK].set(
            gla_w_alpha[l]).astype(BF16)
        y_d = _gla(gla.reshape(b, s, N_GLA), misc, walpha, _row(gla_b_alpha[l]),
                   _row(jnp.tile(gla_norm[l], hg)))

        ys = [y.reshape(m, GROUP_WIDTH) for y in (y_a, y_b, y_c, y_d)]
        h = _mlp(h, ys, w_out[l].astype(BF16), _row(norm_ffn[l]), w_gate[l].astype(BF16),
                 w_up[l].astype(BF16), w_down[l].astype(BF16), _row(norm_ple[l]),
                 w_ple_gate[l].astype(BF16), p[l].reshape(m, -1), w_ple_proj[l].astype(BF16),
                 fin, tm, final=(l == depth - 1))
    return h.reshape(b, s, d)
```

```python
import functools
import math

import numpy as np
import jax
import jax.numpy as jnp
from jax import lax
from jax.experimental import pallas as pl
from jax.experimental.pallas import tpu as pltpu

F32 = jnp.float32
BF16 = jnp.bfloat16

HEAD_DIM = 64
N_HEADS_G = 4
GROUP_WIDTH = HEAD_DIM * N_HEADS_G
MOBA_BLOCK = 256
MOBA_TOPK = 3
N_REL_BUCKETS = 32
REL_MAX_EXACT = 16
REL_MAX_DIST = 128
DIFF_QK_DIM = HEAD_DIM // 2
CONV_WIDTH = 4
CHUNK = 64
GLA_DK = HEAD_DIM // 2
GLA_GATE_RANK = 16
GLA_TAU = 16.0
EPS = 1e-6

LANES = 128
ATTN_TILE = 256
REC_SEQ_TILE = 512
REC_ROWS = 4
NEG = -1e30
LOG2E = math.log2(math.e)
VMEM_LIMIT = 56 * 1024 * 1024

N_QK = 4 * GROUP_WIDTH
N_VT = 2 * GROUP_WIDTH
N_GDN = 4 * GROUP_WIDTH
N_GLA = 2 * N_HEADS_G * GLA_DK + 2 * GROUP_WIDTH
N_MISC = LANES
MISC_A, MISC_B, MISC_LR = 0, N_HEADS_G, 2 * N_HEADS_G

_NT = (((1,), (1,)), ((), ()))
_TN = (((0,), (0,)), ((), ()))


def _dot(a, b):
    return jnp.dot(a, b, preferred_element_type=F32)


def _dot_nt(a, b):
    return lax.dot_general(a, b, _NT, preferred_element_type=F32)


def _dot_tn(a, b):
    return lax.dot_general(a, b, _TN, preferred_element_type=F32)


def _split_bf16(x, n):
    parts = []
    r = x
    for _ in range(n):
        hi = r.astype(BF16)
        parts.append(hi)
        r = r - hi.astype(F32)
    return parts


def _dot_exact_rhs(x, w, n):
    acc = None
    for part in _split_bf16(x, n):
        t = _dot(part, w)
        acc = t if acc is None else acc + t
    return acc


def _dot_exact_lhs(w, x, n):
    acc = None
    for part in _split_bf16(x, n):
        t = _dot(w, part)
        acc = t if acc is None else acc + t
    return acc


def _rms(x, g):
    return x * lax.rsqrt(jnp.mean(x * x, axis=-1, keepdims=True) + EPS) * g


def _sigmoid(x):
    return 1.0 / (1.0 + jnp.exp(-x))


def _silu(x):
    return x * _sigmoid(x)


def _inproj_kernel(x_ref, g_ref, w_ref, wvt_ref, scale_ref,
                   qk_ref, vt_ref, gdn_ref, gla_ref, misc_ref, kmean_ref):
    tm = x_ref.shape[0]
    xn = _rms(x_ref[...], g_ref[...]).astype(BF16)
    qk = _dot(xn, w_ref[:, 0:N_QK])
    k_moba = qk[:, GROUP_WIDTH:2 * GROUP_WIDTH]
    kmean_ref[0] = jnp.mean(k_moba.reshape(tm // MOBA_BLOCK, MOBA_BLOCK, GROUP_WIDTH), axis=1)
    qk_ref[...] = (qk * scale_ref[...]).astype(BF16)
    vt = _dot_nt(wvt_ref[...], xn).astype(BF16)
    for i in range(tm // ATTN_TILE):
        vt_ref[0, i] = vt[:, i * ATTN_TILE:(i + 1) * ATTN_TILE]
    o = N_QK
    gdn_ref[...] = _dot(xn, w_ref[:, o:o + N_GDN])
    o += N_GDN
    gla_ref[...] = _dot(xn, w_ref[:, o:o + N_GLA])
    o += N_GLA
    misc_ref[...] = _dot(xn, w_ref[:, o:o + N_MISC])


def _inproj(h2d, g, w, wvt, scale, tm, seq):
    m, d = h2d.shape
    n_all = w.shape[1]
    per_seq = seq // tm
    const = lambda i: (0, 0)
    return pl.pallas_call(
        _inproj_kernel,
        grid=(m // tm,),
        in_specs=[
            pl.BlockSpec((tm, d), lambda i: (i, 0)),
            pl.BlockSpec((1, d), const),
            pl.BlockSpec((d, n_all), const),
            pl.BlockSpec((N_VT, d), const),
            pl.BlockSpec((1, N_QK), const),
        ],
        out_specs=[
            pl.BlockSpec((tm, N_QK), lambda i: (i, 0)),
            pl.BlockSpec((1, tm // ATTN_TILE, N_VT, ATTN_TILE),
                         lambda i: (i // per_seq, i % per_seq, 0, 0)),
            pl.BlockSpec((tm, N_GDN), lambda i: (i, 0)),
            pl.BlockSpec((tm, N_GLA), lambda i: (i, 0)),
            pl.BlockSpec((tm, N_MISC), lambda i: (i, 0)),
            pl.BlockSpec((1, tm // MOBA_BLOCK, GROUP_WIDTH), lambda i: (i, 0, 0)),
        ],
        out_shape=[
            jax.ShapeDtypeStruct((m, N_QK), BF16),
            jax.ShapeDtypeStruct((m // seq, seq // ATTN_TILE, N_VT, ATTN_TILE), BF16),
            jax.ShapeDtypeStruct((m, N_GDN), F32),
            jax.ShapeDtypeStruct((m, N_GLA), F32),
            jax.ShapeDtypeStruct((m, N_MISC), F32),
            jax.ShapeDtypeStruct((m // tm, tm // MOBA_BLOCK, GROUP_WIDTH), F32),
        ],
        compiler_params=pltpu.CompilerParams(
            dimension_semantics=("arbitrary",), vmem_limit_bytes=VMEM_LIMIT),
        name="inproj",
    )(h2d, g, w, wvt, scale)


def _lane_mask(lo, width):
    lane = lax.broadcasted_iota(jnp.int32, (1, LANES), 1)
    return (lane >= lo) & (lane < lo + width)


ACC_ROWS = HEAD_DIM + 16
PIPE_UNROLL = 2


def _values_with_ones(vt, hh):
    head = vt[HEAD_DIM * hh:HEAD_DIM * (hh + 1), :]
    return jnp.concatenate([head, jnp.ones((ACC_ROWS - HEAD_DIM, vt.shape[1]), vt.dtype)], axis=0)


def _softmax_stage(ms, scores, valids=None):
    ms_new, alphas, ps = [], [], []
    for c, (m, s) in enumerate(zip(ms, scores)):
        m_tile = jnp.max(s, axis=0, keepdims=True)
        if valids is None or valids[c] is None:
            m_new = jnp.maximum(m, m_tile)
            shift = m_new
        else:
            ok = valids[c] > 0.5
            m_new = jnp.maximum(m, jnp.where(ok, m_tile, NEG))
            shift = jnp.where(ok, m_new, -NEG)
        ms_new.append(m_new)
        alphas.append(jnp.exp2(m - m_new))
        ps.append(jnp.exp2(s - shift).astype(BF16))
    return ms_new, alphas, ps


def _value_products(vts, ps):
    return [_dot(vt, p) for vt, p in zip(vts, ps)]


def _attend_all(n, t, n_far, far_scores, far_valid, prev_scores, prev_valid, own_scores, values,
                j_prev, j_own, s_ref, p_ref, acc_ref):
    chains = range(n)
    n_slots = jnp.maximum(n_far, 2)

    def slot_valid(j):
        exists = jnp.where(j < n_far, 1.0, 0.0)
        rows = far_valid(j)
        if rows is None:
            return [jnp.full((1, t), exists, F32)] * n
        return [row * exists for row in rows]

    s0, s1 = far_scores(0), far_scores(1)
    ms, alphas, ps = _softmax_stage([jnp.full((1, t), NEG, F32)] * n, s0, slot_valid(0))
    for c in chains:
        p_ref[c] = ps[c]
        s_ref[c] = s1[c]
        acc_ref[c] = jnp.zeros((ACC_ROWS, t), F32)

    def steps(first, unroll, carry):
        ms, alphas = carry
        ps = [p_ref[c] for c in chains]
        s_cur = [s_ref[c] for c in chains]
        for u in range(unroll):
            pv = _value_products(values(first + u), ps)
            s_new = far_scores(first + u + 2)
            ms, alphas_next, ps = _softmax_stage(ms, s_cur, far_valid(first + u + 1))
            for c in chains:
                acc_ref[c] = alphas[c] * acc_ref[c] + pv[c]
            alphas, s_cur = alphas_next, s_new
        for c in chains:
            p_ref[c] = ps[c]
            s_ref[c] = s_cur[c]
        return ms, alphas

    n_steps = n_slots - 2
    n_blocks = n_steps // PIPE_UNROLL
    carry = lax.fori_loop(0, n_blocks, lambda i, cr: steps(i * PIPE_UNROLL, PIPE_UNROLL, cr),
                          (ms, alphas))
    ms, alphas = lax.fori_loop(n_blocks * PIPE_UNROLL, n_steps, lambda i, cr: steps(i, 1, cr), carry)

    pv = _value_products(values(n_slots - 2), [p_ref[c] for c in chains])
    s_prev = prev_scores()
    ms, alphas1, ps1 = _softmax_stage(ms, [s_ref[c] for c in chains], slot_valid(n_slots - 1))
    accs = [alphas[c] * acc_ref[c] + pv[c] for c in chains]
    pv = _value_products(values(n_slots - 1), ps1)
    s_own = own_scores()
    ms, alphas2, ps2 = _softmax_stage(ms, s_prev, prev_valid())
    accs = [alphas1[c] * accs[c] + pv[c] for c in chains]
    pv = _value_products(values(j_prev), ps2)
    ms, alphas3, ps3 = _softmax_stage(ms, s_own)
    accs = [alphas2[c] * accs[c] + pv[c] for c in chains]
    pv = _value_products(values(j_own), ps3)
    return [alphas3[c] * accs[c] + pv[c] for c in chains]


def _normalised(acc):
    return acc[0:HEAD_DIM, :] / acc[HEAD_DIM:HEAD_DIM + 1, :]


def _moba_kernel(q_ref, k_ref, vt_ref, kmean_ref, bias_ref, o_ref, sel_ref, s_ref, p_ref, acc_ref):
    t = ATTN_TILE
    qi = pl.program_id(2)
    nblk = kmean_ref.shape[1]
    q = q_ref[0]
    zero = jnp.zeros_like(q)

    row = lax.broadcasted_iota(jnp.int32, (nblk, t), 0)
    qms = []
    for hh in range(2):
        hmask = _lane_mask(HEAD_DIM * hh, HEAD_DIM)
        qm = jnp.where(hmask, q, zero)
        qms.append(qm)
        gate = _dot_nt(kmean_ref[0].astype(BF16), qm)
        gate = jnp.where(row < qi, gate, -jnp.inf)
        sel_t = jnp.zeros((nblk, t), F32)
        for j in range(nblk):
            gj = gate[j:j + 1, :]
            beats = jnp.where(gate > gj, 1.0, jnp.where((gate == gj) & (row < j), 1.0, 0.0))
            rank = jnp.sum(beats, axis=0, keepdims=True)
            sel_j = jnp.where(rank < MOBA_TOPK, jnp.where(j < qi, 1.0, 0.0), 0.0)
            sel_t = jnp.where(row == j, sel_j, sel_t)
        for j in range(nblk):
            sel_ref[hh * nblk + j] = sel_t[j:j + 1, :]

    def keys(j):
        return k_ref[0, pl.ds(pl.multiple_of(j * t, t), t), :]

    def values(j):
        vt = vt_ref[0, j]
        return [_values_with_ones(vt, hh) for hh in range(2)]

    def far_scores(j):
        k = keys(j)
        return [_dot_nt(k, qms[hh]) for hh in range(2)]

    def far_valid(j):
        return [sel_ref[hh * nblk + j] for hh in range(2)]

    jp = jnp.maximum(qi - 1, 0)

    def prev_scores():
        k = keys(jp)
        return [_dot_nt(k, qms[hh]) + bias_ref[hh, 1] for hh in range(2)]

    def own_scores():
        k = keys(qi)
        return [_dot_nt(k, qms[hh]) + bias_ref[hh, 0] for hh in range(2)]

    accs = _attend_all(2, t, jnp.maximum(qi - 1, 0), far_scores, far_valid, prev_scores,
                       lambda: far_valid(jp), own_scores, values, jp, qi, s_ref, p_ref, acc_ref)
    out_t = jnp.concatenate([_normalised(accs[0]), _normalised(accs[1])], axis=0)
    o_ref[0] = out_t.T.astype(o_ref.dtype)


def _attn_specs(s, t, mixer):
    nkv = s // t
    qcol = mixer * 2 * GROUP_WIDTH // LANES
    kcol = qcol + GROUP_WIDTH // LANES
    vrow = mixer * GROUP_WIDTH // LANES
    return [
        pl.BlockSpec((1, t, LANES), lambda bi, p, qi: (bi, qi, qcol + p)),
        pl.BlockSpec((1, s, LANES), lambda bi, p, qi: (bi, 0, kcol + p)),
        pl.BlockSpec((1, nkv, LANES, t), lambda bi, p, qi: (bi, 0, vrow + p, 0)),
    ]


def _attn_scratch(n_chains, t):
    return [pltpu.VMEM((n_chains, t, t), F32),
            pltpu.VMEM((n_chains, t, t), BF16),
            pltpu.VMEM((n_chains, ACC_ROWS, t), F32)]


def _moba(qk, vt, kmean, bias_t):
    b, s, _ = qk.shape
    t = ATTN_TILE
    nq = s // t
    nblk = kmean.shape[1]
    return pl.pallas_call(
        _moba_kernel,
        grid=(b, 2, nq),
        in_specs=_attn_specs(s, t, 0) + [
            pl.BlockSpec((1, nblk, LANES), lambda bi, p, qi: (bi, 0, p)),
            pl.BlockSpec((2, 2, t, t), lambda bi, p, qi: (p, 0, 0, 0)),
        ],
        out_specs=pl.BlockSpec((1, t, LANES), lambda bi, p, qi: (bi, qi, p)),
        out_shape=jax.ShapeDtypeStruct((b, s, GROUP_WIDTH), BF16),
        scratch_shapes=[pltpu.VMEM((2 * nblk, 1, t), F32)]
        + _attn_scratch(2, t),
        compiler_params=pltpu.CompilerParams(
            dimension_semantics=("arbitrary", "arbitrary", "arbitrary"),
            vmem_limit_bytes=VMEM_LIMIT),
        name="moba",
    )(qk, qk, vt, kmean, bias_t)


def _diff_kernel(lam_ref, q_ref, k_ref, vt_ref, bias_ref, gnorm_ref, o_ref, s_ref, p_ref, acc_ref):
    t = ATTN_TILE
    qi = pl.program_id(2)
    q = q_ref[0]
    zero = jnp.zeros_like(q)
    qms = [jnp.where(_lane_mask(HEAD_DIM * hh + DIFF_QK_DIM * mm, DIFF_QK_DIM), q, zero)
           for hh in range(2) for mm in range(2)]

    def keys(j):
        return k_ref[0, pl.ds(pl.multiple_of(j * t, t), t), :]

    def values(j):
        vt = vt_ref[0, j]
        vts = [_values_with_ones(vt, hh) for hh in range(2)]
        return [vts[c // 2] for c in range(4)]

    def far_scores(j):
        k = keys(j)
        return [_dot_nt(k, qms[c]) for c in range(4)]

    jp = jnp.maximum(qi - 1, 0)

    def prev_scores():
        k = keys(jp)
        return [_dot_nt(k, qms[c]) + bias_ref[c // 2, 1] for c in range(4)]

    def prev_valid():
        return [jnp.full((1, t), jnp.where(qi >= 1, 1.0, 0.0), F32)] * 4

    def own_scores():
        k = keys(qi)
        return [_dot_nt(k, qms[c]) + bias_ref[c // 2, 0] for c in range(4)]

    accs = _attend_all(4, t, jnp.maximum(qi - 1, 0), far_scores, lambda j: None, prev_scores,
                       prev_valid, own_scores, values, jp, qi, s_ref, p_ref, acc_ref)

    lam_p = lam_ref[...]
    lam_init = lam_p[4:5, 0:1]
    lam = (jnp.exp(jnp.sum(lam_p[0:1] * lam_p[1:2], axis=-1, keepdims=True))
           - jnp.exp(jnp.sum(lam_p[2:3] * lam_p[3:4], axis=-1, keepdims=True)) + lam_init)
    outs = []
    for hh in range(2):
        o = _normalised(accs[2 * hh]) - lam * _normalised(accs[2 * hh + 1])
        ms = jnp.mean(o * o, axis=0, keepdims=True)
        outs.append(o * lax.rsqrt(ms + EPS))
    y_t = jnp.concatenate(outs, axis=0)
    o_ref[0] = (y_t.T * gnorm_ref[...] * (1.0 - lam_init)).astype(o_ref.dtype)


def _diff(qk, vt, lam_p, bias_t, gnorm):
    b, s, _ = qk.shape
    t = ATTN_TILE
    nq = s // t
    return pl.pallas_call(
        _diff_kernel,
        grid=(b, 2, nq),
        in_specs=[pl.BlockSpec(lam_p.shape, lambda bi, p, qi: (0, 0))] + _attn_specs(s, t, 1) + [
            pl.BlockSpec((2, 2, t, t), lambda bi, p, qi: (p, 0, 0, 0)),
            pl.BlockSpec((1, LANES), lambda bi, p, qi: (0, 0)),
        ],
        out_specs=pl.BlockSpec((1, t, LANES), lambda bi, p, qi: (bi, qi, p)),
        out_shape=jax.ShapeDtypeStruct((b, s, GROUP_WIDTH), BF16),
        scratch_shapes=_attn_scratch(4, t),
        compiler_params=pltpu.CompilerParams(
            dimension_semantics=("arbitrary", "arbitrary", "arbitrary"),
            vmem_limit_bytes=VMEM_LIMIT),
        name="diff_attn",
    )(lam_p, qk, qk, vt, bias_t, gnorm)


def _head_of_lane(n_lanes, width):
    return lax.broadcasted_iota(jnp.int32, (1, n_lanes), 1) // width


def _block_rows(x, lane_head, n_heads=N_HEADS_G):
    zero = jnp.zeros_like(x)
    return jnp.concatenate([jnp.where(lane_head == h, x, zero) for h in range(n_heads)], axis=0)


def _gdn_kernel(x_ref, misc_ref, conv_ref, hp_ref, gnorm_ref, o_ref, s_ref, tail_ref, g_ref, beta_ref):
    L = CHUNK
    W = GROUP_WIDTH
    rows = range(x_ref.shape[0])
    seq = x_ref.shape[1]
    n_chunks = seq // L
    head_w = _head_of_lane(W, HEAD_DIM)
    head_all = _head_of_lane(LANES, 1)
    ri = lax.broadcasted_iota(jnp.int32, (L, W), 0)
    cj = lax.broadcasted_iota(jnp.int32, (L, W), 1) % HEAD_DIM
    lower = ri >= cj
    strict = ri > cj
    ident = jnp.where(ri == cj, 1.0, 0.0)
    ones_blk = (lax.broadcasted_iota(jnp.int32, (W, W), 0) // HEAD_DIM
                == lax.broadcasted_iota(jnp.int32, (W, W), 1) // HEAD_DIM)
    ones_seg = ones_blk.astype(BF16)
    tril = (lax.broadcasted_iota(jnp.int32, (L, L), 0)
            >= lax.broadcasted_iota(jnp.int32, (L, L), 1)).astype(BF16)

    hp = hp_ref[...]
    exp_g = (lax.broadcasted_iota(jnp.int32, (LANES, W), 0) - MISC_A
             == lax.broadcasted_iota(jnp.int32, (LANES, W), 1) // HEAD_DIM).astype(BF16)
    exp_b = (lax.broadcasted_iota(jnp.int32, (LANES, W), 0) - MISC_B
             == lax.broadcasted_iota(jnp.int32, (LANES, W), 1) // HEAD_DIM).astype(BF16)
    del head_all
    for r in rows:
        misc = misc_ref[r]
        sp_in = misc + hp[1:2]
        softplus = jnp.maximum(sp_in, 0.0) + jnp.log(1.0 + jnp.exp(-jnp.abs(sp_in)))
        g_tok = -jnp.exp(hp[0:1]) * softplus
        b_tok = _sigmoid(misc)
        g_ref[r] = _dot_exact_rhs(g_tok, exp_g, 2)
        beta_ref[r] = _dot_exact_rhs(b_tok, exp_b, 2)

    @pl.when(pl.program_id(1) == 0)
    def _():
        s_ref[...] = jnp.zeros_like(s_ref)
        tail_ref[...] = jnp.zeros_like(tail_ref)

    cw = conv_ref[...]
    blk = lambda a: _block_rows(a.astype(BF16), head_w)
    bf = lambda a: a.astype(BF16)

    def prepare(r, r0):
        x = x_ref[r, pl.ds(r0, L), :]
        qkv = x[:, 0:3 * W]
        xx = jnp.concatenate([tail_ref[r], qkv], axis=0)
        tail_ref[r] = qkv[L - 8:L, :]
        conv = cw[CONV_WIDTH - 1:CONV_WIDTH] * qkv
        for i in range(CONV_WIDTH - 1):
            lo = 8 - (CONV_WIDTH - 1) + i
            conv = conv + cw[i:i + 1] * xx[lo:lo + L, :]
        conv = _silu(conv)
        return conv[:, 0:W], conv[:, W:2 * W], conv[:, 2 * W:3 * W], x[:, 3 * W:4 * W]

    def decays(gc):
        gc_col = jnp.sum(gc * ident, axis=0, keepdims=True)
        return jnp.exp(jnp.where(lower, gc - gc_col, -jnp.inf))

    def chunk(c, _):
        r0 = pl.multiple_of(c * L, L)
        q, k, v, z = zip(*[prepare(r, r0) for r in rows])
        ssq = [_dot_exact_rhs(jnp.concatenate([q[r] * q[r], k[r] * k[r]], axis=0), ones_seg, 1)
               for r in rows]
        q = [q[r] * lax.rsqrt(ssq[r][0:L] + EPS) * (HEAD_DIM ** -0.5) for r in rows]
        k = [k[r] * lax.rsqrt(ssq[r][L:2 * L] + EPS) for r in rows]
        beta = [beta_ref[r, pl.ds(r0, L), :] for r in rows]
        gc = [_dot_exact_lhs(tril, g_ref[r, pl.ds(r0, L), :], 2) for r in rows]
        egc = [jnp.exp(gc[r]) for r in rows]
        gc_last = [gc[r][L - 1:L, :] for r in rows]
        decay = [decays(gc[r]) for r in rows]
        kb = [k[r] * beta[r] for r in rows]
        both = [_dot_nt(bf(jnp.concatenate([kb[r], q[r]], axis=0)), blk(k[r])) for r in rows]
        qk = [jnp.where(lower, both[r][L:2 * L] * decay[r], 0.0) for r in rows]

        p = [jnp.where(strict, -both[r][0:L] * decay[r], 0.0) for r in rows]
        t_inv = [ident + p[r] for r in rows]
        p = [_dot(bf(p[r]), blk(p[r])) for r in rows]
        for _ in range(4):
            prod = [_dot(bf(jnp.concatenate([t_inv[r], p[r]], axis=0)), blk(p[r])) for r in rows]
            t_inv = [t_inv[r] + prod[r][0:L] for r in rows]
            p = [prod[r][L:2 * L] for r in rows]
        t_inv = [t_inv[r] + _dot(bf(t_inv[r]), blk(p[r])) for r in rows]

        u = [_dot(bf(t_inv[r]), blk(v[r] * beta[r])) for r in rows]
        w = [_dot(bf(t_inv[r]), blk(kb[r] * egc[r])) for r in rows]

        state = [s_ref[r] for r in rows]
        ws_qs = [_dot(bf(jnp.concatenate([w[r], q[r] * egc[r]], axis=0)), bf(state[r])) for r in rows]
        v_new = [u[r] - ws_qs[r][0:L] for r in rows]
        o = [ws_qs[r][L:2 * L] + _dot(bf(qk[r]), blk(v_new[r])) for r in rows]
        upd = [_dot_tn(bf(k[r] * jnp.exp(gc_last[r] - gc[r])), bf(v_new[r])) for r in rows]
        for r in rows:
            s_ref[r] = state[r] * jnp.exp(gc_last[r]) + jnp.where(ones_blk, upd[r], 0.0)

        ms = [_dot_exact_rhs(o[r] * o[r], ones_seg, 1) * (1.0 / HEAD_DIM) for r in rows]
        for r in rows:
            y = o[r] * lax.rsqrt(ms[r] + EPS) * gnorm_ref[...] * _silu(z[r])
            o_ref[r, pl.ds(r0, L), :] = y.astype(o_ref.dtype)
        return 0

    lax.fori_loop(0, n_chunks, chunk, 0)


def _gdn(gdn, misc, conv_w, hp, gnorm):
    b, s, _ = gdn.shape
    ts = min(s, REC_SEQ_TILE)
    nr = math.gcd(b, REC_ROWS)
    const = lambda bi, si: (0, 0)
    return pl.pallas_call(
        _gdn_kernel,
        grid=(b // nr, s // ts),
        in_specs=[
            pl.BlockSpec((nr, ts, N_GDN), lambda bi, si: (bi, si, 0)),
            pl.BlockSpec((nr, ts, N_MISC), lambda bi, si: (bi, si, 0)),
            pl.BlockSpec(conv_w.shape, const),
            pl.BlockSpec(hp.shape, const),
            pl.BlockSpec((1, GROUP_WIDTH), const),
        ],
        out_specs=pl.BlockSpec((nr, ts, GROUP_WIDTH), lambda bi, si: (bi, si, 0)),
        out_shape=jax.ShapeDtypeStruct((b, s, GROUP_WIDTH), BF16),
        scratch_shapes=[
            pltpu.VMEM((nr, GROUP_WIDTH, GROUP_WIDTH), F32),
            pltpu.VMEM((nr, 8, 3 * GROUP_WIDTH), F32),
            pltpu.VMEM((nr, ts, GROUP_WIDTH), F32),
            pltpu.VMEM((nr, ts, GROUP_WIDTH), F32),
        ],
        compiler_params=pltpu.CompilerParams(
            dimension_semantics=("arbitrary", "arbitrary"), vmem_limit_bytes=VMEM_LIMIT),
        name="gdn",
    )(gdn, misc, conv_w, hp, gnorm)


def _gla_kernel(x_ref, misc_ref, walpha_ref, balpha_ref, gnorm_ref, o_ref, s_ref, la_ref):
    L = CHUNK
    W = GROUP_WIDTH
    KW = N_HEADS_G * GLA_DK
    rows = range(x_ref.shape[0])
    seq = x_ref.shape[1]
    n_chunks = seq // L
    head_k = _head_of_lane(KW, GLA_DK)
    head_v = _head_of_lane(W, HEAD_DIM)
    ri = lax.broadcasted_iota(jnp.int32, (L, W), 0)
    cj = lax.broadcasted_iota(jnp.int32, (L, W), 1) % HEAD_DIM
    lower = ri >= cj
    ones_blk = (lax.broadcasted_iota(jnp.int32, (W, W), 0) // HEAD_DIM
                == lax.broadcasted_iota(jnp.int32, (W, W), 1) // HEAD_DIM).astype(BF16)
    state_mask = (lax.broadcasted_iota(jnp.int32, (W, KW), 0) // HEAD_DIM
                  == lax.broadcasted_iota(jnp.int32, (W, KW), 1) // GLA_DK)
    tril = (lax.broadcasted_iota(jnp.int32, (L, L), 0)
            >= lax.broadcasted_iota(jnp.int32, (L, L), 1)).astype(BF16)

    for r in rows:
        pre = _dot(misc_ref[r].astype(BF16), walpha_ref[...]) + balpha_ref[...]
        log_sig = jnp.minimum(pre, 0.0) - jnp.log(1.0 + jnp.exp(-jnp.abs(pre)))
        la_ref[r] = log_sig * (1.0 / GLA_TAU)

    @pl.when(pl.program_id(1) == 0)
    def _():
        s_ref[...] = jnp.zeros_like(s_ref)

    bf = lambda a: a.astype(BF16)

    def chunk(c, _):
        r0 = pl.multiple_of(c * L, L)
        x = [x_ref[r, pl.ds(r0, L), :] for r in rows]
        k = [x[r][:, KW:2 * KW] for r in rows]
        v = [bf(x[r][:, 2 * KW:2 * KW + W]) for r in rows]
        bc = [_dot_exact_lhs(tril, la_ref[r, pl.ds(r0, L), :], 2) for r in rows]
        b_last = [bc[r][L - 1:L, :] for r in rows]
        qe = [bf(x[r][:, 0:KW] * (GLA_DK ** -0.5) * jnp.exp(bc[r])) for r in rows]
        ke = [bf(k[r] * jnp.exp(-bc[r])) for r in rows]
        a_mat = [jnp.where(lower, _dot_nt(qe[r], _block_rows(ke[r], head_k)), 0.0) for r in rows]
        state_t = [s_ref[r] for r in rows]
        o = [_dot_nt(qe[r], bf(state_t[r])) for r in rows]
        o = [o[r] + _dot(bf(a_mat[r]), _block_rows(v[r], head_v)) for r in rows]
        upd = [_dot_tn(v[r], bf(k[r] * jnp.exp(b_last[r] - bc[r]))) for r in rows]
        for r in rows:
            s_ref[r] = state_t[r] * jnp.exp(b_last[r]) + jnp.where(state_mask, upd[r], 0.0)

        ms = [_dot_exact_rhs(o[r] * o[r], ones_blk, 1) * (1.0 / HEAD_DIM) for r in rows]
        for r in rows:
            gate = _silu(x[r][:, 2 * KW + W:2 * KW + 2 * W])
            y = o[r] * lax.rsqrt(ms[r] + EPS) * gnorm_ref[...] * gate
            o_ref[r, pl.ds(r0, L), :] = y.astype(o_ref.dtype)
        return 0

    lax.fori_loop(0, n_chunks, chunk, 0)


def _gla(gla, misc, walpha, balpha, gnorm):
    b, s, _ = gla.shape
    ts = min(s, REC_SEQ_TILE)
    nr = math.gcd(b, REC_ROWS)
    const = lambda bi, si: (0, 0)
    return pl.pallas_call(
        _gla_kernel,
        grid=(b // nr, s // ts),
        in_specs=[
            pl.BlockSpec((nr, ts, N_GLA), lambda bi, si: (bi, si, 0)),
            pl.BlockSpec((nr, ts, N_MISC), lambda bi, si: (bi, si, 0)),
            pl.BlockSpec(walpha.shape, const),
            pl.BlockSpec(balpha.shape, const),
            pl.BlockSpec((1, GROUP_WIDTH), const),
        ],
        out_specs=pl.BlockSpec((nr, ts, GROUP_WIDTH), lambda bi, si: (bi, si, 0)),
        out_shape=jax.ShapeDtypeStruct((b, s, GROUP_WIDTH), BF16),
        scratch_shapes=[
            pltpu.VMEM((nr, GROUP_WIDTH, N_HEADS_G * GLA_DK), F32),
            pltpu.VMEM((nr, ts, N_HEADS_G * GLA_DK), F32),
        ],
        compiler_params=pltpu.CompilerParams(
            dimension_semantics=("arbitrary", "arbitrary"), vmem_limit_bytes=VMEM_LIMIT),
        name="gla",
    )(gla, misc, walpha, balpha, gnorm)


def _ffn_chunks(d_ff):
    step = 4 * GROUP_WIDTH
    return [(lo, min(step, d_ff - lo)) for lo in range(0, d_ff, step)]


def _mlp_kernel(h_ref, ya_ref, yb_ref, yc_ref, yd_ref, wout_ref, gffn_ref, wg_ref, wu_ref, wd_ref,
                gple_ref, wpg_ref, p_ref, wpp_ref, gfin_ref, o_ref, act_ref, *, final):
    mixed = jnp.concatenate([ya_ref[...], yb_ref[...], yc_ref[...], yd_ref[...]], axis=-1)
    h = h_ref[...] + _dot(mixed, wout_ref[...])
    hn = _rms(h, gffn_ref[...]).astype(BF16)
    for lo, width in _ffn_chunks(wg_ref.shape[1]):
        gate = _dot(hn, wg_ref[:, lo:lo + width])
        up = _dot(hn, wu_ref[:, lo:lo + width])
        act_ref[:, lo:lo + width] = (_silu(gate) * up).astype(BF16)
    h = h + _dot(act_ref[...], wd_ref[...])
    gate = _sigmoid(_dot(_rms(h, gple_ref[...]).astype(BF16), wpg_ref[...]))
    h = h + gate * _dot(p_ref[...].astype(BF16), wpp_ref[...])
    if final:
        h = _rms(h, gfin_ref[...])
    o_ref[...] = h


def _mlp(h2d, ys, wout, gffn, wg, wu, wd, gple, wpg, p2d, wpp, gfin, tm, final):
    m, d = h2d.shape
    d_ff = wg.shape[1]
    const = lambda i: (0, 0)
    resident = lambda shape: pl.BlockSpec(shape, const, pipeline_mode=pl.Buffered(1))
    rows = lambda width: pl.BlockSpec((tm, width), lambda i: (i, 0))
    return pl.pallas_call(
        functools.partial(_mlp_kernel, final=final),
        grid=(m // tm,),
        in_specs=[rows(d)] + [rows(GROUP_WIDTH)] * 4 + [
            resident(wout.shape), resident(gffn.shape), resident(wg.shape), resident(wu.shape),
            resident(wd.shape), resident(gple.shape), resident(wpg.shape),
            rows(p2d.shape[1]), resident(wpp.shape), resident(gfin.shape),
        ],
        out_specs=rows(d),
        out_shape=jax.ShapeDtypeStruct((m, d), F32),
        scratch_shapes=[pltpu.VMEM((tm, d_ff), BF16)],
        compiler_params=pltpu.CompilerParams(
            dimension_semantics=("arbitrary",), vmem_limit_bytes=VMEM_LIMIT),
        name="mlp",
    )(h2d, *ys, wout, gffn, wg, wu, wd, gple, wpg, p2d, wpp, gfin)


def _rel_bucket_table(n_dist):
    n = np.arange(n_dist)
    nf = np.maximum(n, 1).astype(np.float32)
    large = REL_MAX_EXACT + (np.log(nf / REL_MAX_EXACT) / math.log(REL_MAX_DIST / REL_MAX_EXACT)
                             * (N_REL_BUCKETS - REL_MAX_EXACT)).astype(np.int32)
    large = np.minimum(large, N_REL_BUCKETS - 1)
    return np.where(n < REL_MAX_EXACT, n, large)


def _bias_tiles(rel_bias):
    t = ATTN_TILE
    bucket = _rel_bucket_table(2 * t)
    assert (bucket[t + 1:] == N_REL_BUCKETS - 1).all()
    per_dist = (rel_bias[bucket, :] - rel_bias[N_REL_BUCKETS - 1][None, :]).T * LOG2E
    own = jnp.concatenate([per_dist[:, :t], jnp.full_like(per_dist[:, :t], NEG)], axis=1)
    prev = jnp.concatenate([per_dist[:, t:], per_dist[:, :t]], axis=1)
    vec = jnp.stack([own, prev], axis=1)
    rolled = jnp.tile(vec, (1, 1, t))[:, :, :t * (2 * t - 1)].reshape(-1, 2, t, 2 * t - 1)
    return rolled[:, :, :, :t]


def _row(v, width=None):
    v = v.astype(F32).reshape(1, -1)
    if width is not None and v.shape[1] < width:
        v = jnp.pad(v, ((0, 0), (0, width - v.shape[1])))
    return v


def kernel(x, p, norm_mix, w_in, rel_bias, diff_lambda, diff_norm, gdn_conv, gdn_a_log, gdn_dt_bias,
           gdn_norm, gla_w_alpha, gla_b_alpha, gla_norm, w_out, norm_ffn, w_gate, w_up, w_down,
           norm_ple, w_ple_gate, w_ple_proj, final_norm):
    b, s, d = x.shape
    depth = w_in.shape[0]
    m = b * s
    tm = 512
    hg = N_HEADS_G
    assert s % ATTN_TILE == 0 and s % tm == 0 and tm % MOBA_BLOCK == 0

    bias_t = _bias_tiles(rel_bias.astype(F32))
    bias_a, bias_b = bias_t[:hg], bias_t[hg:]
    gw = GROUP_WIDTH
    col_scale = np.ones((1, N_QK), np.float32)
    col_scale[:, 0:gw] = HEAD_DIM ** -0.5
    col_scale[:, gw:2 * gw] = LOG2E
    col_scale[:, 2 * gw:3 * gw] = DIFF_QK_DIM ** -0.5 * LOG2E
    col_scale = jnp.asarray(col_scale)

    n_main = 6 * gw + N_GDN
    ab = 2 * hg
    gla_lo = n_main + ab
    gla_hi = gla_lo + N_GLA
    fin = _row(final_norm)
    h = x.reshape(m, d)
    for l in range(depth):
        wl = w_in[l]
        w1 = jnp.concatenate(
            [wl[:, 0:2 * gw], wl[:, 3 * gw:5 * gw], wl[:, 6 * gw:n_main], wl[:, gla_lo:gla_hi],
             wl[:, n_main:gla_lo], wl[:, gla_hi:],
             jnp.zeros((d, N_MISC - ab - GLA_GATE_RANK), wl.dtype)], axis=1).astype(BF16)
        wvt = jnp.concatenate([wl[:, 2 * gw:3 * gw], wl[:, 5 * gw:6 * gw]], axis=1).T.astype(BF16)
        qk, vt, gdn, gla, misc, kmean = _inproj(h, _row(norm_mix[l]), w1, wvt, col_scale, tm, s)
        qk = qk.reshape(b, s, N_QK)
        misc = misc.reshape(b, s, N_MISC)
        kmean = kmean.reshape(b, s // MOBA_BLOCK, GROUP_WIDTH)

        y_a = _moba(qk, vt, kmean, bias_a)

        lam_init = 0.8 - 0.6 * math.exp(-0.3 * l)
        lam_p = jnp.concatenate([diff_lambda[l].astype(F32),
                                 jnp.full((1, DIFF_QK_DIM), lam_init, F32)], axis=0)
        y_b = _diff(qk, vt, lam_p, bias_b, _row(jnp.tile(diff_norm[l], 2)))

        hp = jnp.concatenate([_row(gdn_a_log[l], LANES), _row(gdn_dt_bias[l], LANES)], axis=0)
        y_c = _gdn(gdn.reshape(b, s, N_GDN), misc, gdn_conv[l].astype(F32), hp,
                   _row(jnp.tile(gdn_norm[l], hg)))

        walpha = jnp.zeros((N_MISC, hg * GLA_DK), F32).at[MISC_LR:MISC_LR + GLA_GATE_RANK].set(
            gla_w_alpha[l]).astype(BF16)
        y_d = _gla(gla.reshape(b, s, N_GLA), misc, walpha, _row(gla_b_alpha[l]),
                   _row(jnp.tile(gla_norm[l], hg)))

        ys = [y.reshape(m, GROUP_WIDTH) for y in (y_a, y_b, y_c, y_d)]
        h = _mlp(h, ys, w_out[l].astype(BF16), _row(norm_ffn[l]), w_gate[l].astype(BF16),
                 w_up[l].astype(BF16), w_down[l].astype(BF16), _row(norm_ple[l]),
                 w_ple_gate[l].astype(BF16), p[l].reshape(m, -1), w_ple_proj[l].astype(BF16),
                 fin, tm, final=(l == depth - 1))
    return h.reshape(b, s, d)
```

```python
import functools
import math

import numpy as np
import jax
import jax.numpy as jnp
from jax import lax
from jax.experimental import pallas as pl
from jax.experimental.pallas import tpu as pltpu

F32 = jnp.float32
BF16 = jnp.bfloat16

HEAD_DIM = 64
N_HEADS_G = 4
GROUP_WIDTH = HEAD_DIM * N_HEADS_G
MOBA_BLOCK = 256
MOBA_TOPK = 3
N_REL_BUCKETS = 32
REL_MAX_EXACT = 16
REL_MAX_DIST = 128
DIFF_QK_DIM = HEAD_DIM // 2
CONV_WIDTH = 4
CHUNK = 64
GLA_DK = HEAD_DIM // 2
GLA_GATE_RANK = 16
GLA_TAU = 16.0
EPS = 1e-6

LANES = 128
ATTN_TILE = 256
REC_SEQ_TILE = 256
REC_ROWS = 8
NEG = -1e30
LOG2E = math.log2(math.e)
VMEM_LIMIT = 56 * 1024 * 1024

N_QK = 4 * GROUP_WIDTH
N_VT = 2 * GROUP_WIDTH
N_GDN = 4 * GROUP_WIDTH
N_GLA = 2 * N_HEADS_G * GLA_DK + 2 * GROUP_WIDTH
N_MISC = LANES
MISC_A, MISC_B, MISC_LR = 0, N_HEADS_G, 2 * N_HEADS_G

_NT = (((1,), (1,)), ((), ()))
_TN = (((0,), (0,)), ((), ()))


def _dot(a, b):
    return jnp.dot(a, b, preferred_element_type=F32)


def _dot_nt(a, b):
    return lax.dot_general(a, b, _NT, preferred_element_type=F32)


def _dot_tn(a, b):
    return lax.dot_general(a, b, _TN, preferred_element_type=F32)


def _split_bf16(x, n):
    parts = []
    r = x
    for _ in range(n):
        hi = r.astype(BF16)
        parts.append(hi)
        r = r - hi.astype(F32)
    return parts


def _dot_exact_rhs(x, w, n):
    acc = None
    for part in _split_bf16(x, n):
        t = _dot(part, w)
        acc = t if acc is None else acc + t
    return acc


def _dot_exact_lhs(w, x, n):
    acc = None
    for part in _split_bf16(x, n):
        t = _dot(w, part)
        acc = t if acc is None else acc + t
    return acc


def _rms(x, g):
    return x * lax.rsqrt(jnp.mean(x * x, axis=-1, keepdims=True) + EPS) * g


def _sigmoid(x):
    return 1.0 / (1.0 + jnp.exp(-x))


def _silu(x):
    return x * _sigmoid(x)


def _inproj_kernel(x_ref, g_ref, w_ref, wvt_ref, scale_ref,
                   qk_ref, vt_ref, gdn_ref, gla_ref, misc_ref, kmean_ref):
    tm = x_ref.shape[0]
    xn = _rms(x_ref[...], g_ref[...]).astype(BF16)
    qk = _dot(xn, w_ref[:, 0:N_QK])
    k_moba = qk[:, GROUP_WIDTH:2 * GROUP_WIDTH]
    kmean_ref[0] = jnp.mean(k_moba.reshape(tm // MOBA_BLOCK, MOBA_BLOCK, GROUP_WIDTH), axis=1)
    qk_ref[...] = (qk * scale_ref[...]).astype(BF16)
    vt = _dot_nt(wvt_ref[...], xn).astype(BF16)
    for i in range(tm // ATTN_TILE):
        vt_ref[0, i] = vt[:, i * ATTN_TILE:(i + 1) * ATTN_TILE]
    o = N_QK
    gdn_ref[...] = _dot(xn, w_ref[:, o:o + N_GDN])
    o += N_GDN
    gla_ref[...] = _dot(xn, w_ref[:, o:o + N_GLA])
    o += N_GLA
    misc_ref[...] = _dot(xn, w_ref[:, o:o + N_MISC])


def _inproj(h2d, g, w, wvt, scale, tm, seq):
    m, d = h2d.shape
    n_all = w.shape[1]
    per_seq = seq // tm
    const = lambda i: (0, 0)
    return pl.pallas_call(
        _inproj_kernel,
        grid=(m // tm,),
        in_specs=[
            pl.BlockSpec((tm, d), lambda i: (i, 0)),
            pl.BlockSpec((1, d), const),
            pl.BlockSpec((d, n_all), const),
            pl.BlockSpec((N_VT, d), const),
            pl.BlockSpec((1, N_QK), const),
        ],
        out_specs=[
            pl.BlockSpec((tm, N_QK), lambda i: (i, 0)),
            pl.BlockSpec((1, tm // ATTN_TILE, N_VT, ATTN_TILE),
                         lambda i: (i // per_seq, i % per_seq, 0, 0)),
            pl.BlockSpec((tm, N_GDN), lambda i: (i, 0)),
            pl.BlockSpec((tm, N_GLA), lambda i: (i, 0)),
            pl.BlockSpec((tm, N_MISC), lambda i: (i, 0)),
            pl.BlockSpec((1, tm // MOBA_BLOCK, GROUP_WIDTH), lambda i: (i, 0, 0)),
        ],
        out_shape=[
            jax.ShapeDtypeStruct((m, N_QK), BF16),
            jax.ShapeDtypeStruct((m // seq, seq // ATTN_TILE, N_VT, ATTN_TILE), BF16),
            jax.ShapeDtypeStruct((m, N_GDN), F32),
            jax.ShapeDtypeStruct((m, N_GLA), F32),
            jax.ShapeDtypeStruct((m, N_MISC), F32),
            jax.ShapeDtypeStruct((m // tm, tm // MOBA_BLOCK, GROUP_WIDTH), F32),
        ],
        compiler_params=pltpu.CompilerParams(
            dimension_semantics=("arbitrary",), vmem_limit_bytes=VMEM_LIMIT),
        name="inproj",
    )(h2d, g, w, wvt, scale)


def _lane_mask(lo, width):
    lane = lax.broadcasted_iota(jnp.int32, (1, LANES), 1)
    return (lane >= lo) & (lane < lo + width)


ACC_ROWS = HEAD_DIM + 16
PIPE_UNROLL = 2


def _values_with_ones(vt, hh):
    head = vt[HEAD_DIM * hh:HEAD_DIM * (hh + 1), :]
    return jnp.concatenate([head, jnp.ones((ACC_ROWS - HEAD_DIM, vt.shape[1]), vt.dtype)], axis=0)


def _softmax_stage(ms, scores, valids=None):
    ms_new, alphas, ps = [], [], []
    for c, (m, s) in enumerate(zip(ms, scores)):
        m_tile = jnp.max(s, axis=0, keepdims=True)
        if valids is None or valids[c] is None:
            m_new = jnp.maximum(m, m_tile)
            shift = m_new
        else:
            ok = valids[c] > 0.5
            m_new = jnp.maximum(m, jnp.where(ok, m_tile, NEG))
            shift = jnp.where(ok, m_new, -NEG)
        ms_new.append(m_new)
        alphas.append(jnp.exp2(m - m_new))
        ps.append(jnp.exp2(s - shift).astype(BF16))
    return ms_new, alphas, ps


def _value_products(vts, ps):
    return [_dot(vt, p) for vt, p in zip(vts, ps)]


def _attend_all(n, t, n_far, far_scores, far_valid, prev_scores, prev_valid, own_scores, values,
                j_prev, j_own, s_ref, p_ref, acc_ref):
    chains = range(n)
    n_slots = jnp.maximum(n_far, 2)

    def slot_valid(j):
        exists = jnp.where(j < n_far, 1.0, 0.0)
        rows = far_valid(j)
        if rows is None:
            return [jnp.full((1, t), exists, F32)] * n
        return [row * exists for row in rows]

    s0, s1 = far_scores(0), far_scores(1)
    ms, alphas, ps = _softmax_stage([jnp.full((1, t), NEG, F32)] * n, s0, slot_valid(0))
    for c in chains:
        p_ref[c] = ps[c]
        s_ref[c] = s1[c]
        acc_ref[c] = jnp.zeros((ACC_ROWS, t), F32)

    def steps(first, unroll, carry):
        ms, alphas = carry
        ps = [p_ref[c] for c in chains]
        s_cur = [s_ref[c] for c in chains]
        for u in range(unroll):
            pv = _value_products(values(first + u), ps)
            s_new = far_scores(first + u + 2)
            ms, alphas_next, ps = _softmax_stage(ms, s_cur, far_valid(first + u + 1))
            for c in chains:
                acc_ref[c] = alphas[c] * acc_ref[c] + pv[c]
            alphas, s_cur = alphas_next, s_new
        for c in chains:
            p_ref[c] = ps[c]
            s_ref[c] = s_cur[c]
        return ms, alphas

    n_steps = n_slots - 2
    n_blocks = n_steps // PIPE_UNROLL
    carry = lax.fori_loop(0, n_blocks, lambda i, cr: steps(i * PIPE_UNROLL, PIPE_UNROLL, cr),
                          (ms, alphas))
    ms, alphas = lax.fori_loop(n_blocks * PIPE_UNROLL, n_steps, lambda i, cr: steps(i, 1, cr), carry)

    pv = _value_products(values(n_slots - 2), [p_ref[c] for c in chains])
    s_prev = prev_scores()
    ms, alphas1, ps1 = _softmax_stage(ms, [s_ref[c] for c in chains], slot_valid(n_slots - 1))
    accs = [alphas[c] * acc_ref[c] + pv[c] for c in chains]
    pv = _value_products(values(n_slots - 1), ps1)
    s_own = own_scores()
    ms, alphas2, ps2 = _softmax_stage(ms, s_prev, prev_valid())
    accs = [alphas1[c] * accs[c] + pv[c] for c in chains]
    pv = _value_products(values(j_prev), ps2)
    ms, alphas3, ps3 = _softmax_stage(ms, s_own)
    accs = [alphas2[c] * accs[c] + pv[c] for c in chains]
    pv = _value_products(values(j_own), ps3)
    return [alphas3[c] * accs[c] + pv[c] for c in chains]


def _normalised(acc):
    return acc[0:HEAD_DIM, :] / acc[HEAD_DIM:HEAD_DIM + 1, :]


def _moba_kernel(q_ref, k_ref, vt_ref, kmean_ref, bias_ref, o_ref, sel_ref, s_ref, p_ref, acc_ref):
    t = ATTN_TILE
    qi = pl.program_id(2)
    nblk = kmean_ref.shape[1]
    q = q_ref[0]
    zero = jnp.zeros_like(q)

    row = lax.broadcasted_iota(jnp.int32, (nblk, t), 0)
    qms = []
    for hh in range(2):
        hmask = _lane_mask(HEAD_DIM * hh, HEAD_DIM)
        qm = jnp.where(hmask, q, zero)
        qms.append(qm)
        gate = _dot_nt(kmean_ref[0].astype(BF16), qm)
        gate = jnp.where(row < qi, gate, -jnp.inf)
        sel_t = jnp.zeros((nblk, t), F32)
        for j in range(nblk):
            gj = gate[j:j + 1, :]
            beats = jnp.where(gate > gj, 1.0, jnp.where((gate == gj) & (row < j), 1.0, 0.0))
            rank = jnp.sum(beats, axis=0, keepdims=True)
            sel_j = jnp.where(rank < MOBA_TOPK, jnp.where(j < qi, 1.0, 0.0), 0.0)
            sel_t = jnp.where(row == j, sel_j, sel_t)
        for j in range(nblk):
            sel_ref[hh * nblk + j] = sel_t[j:j + 1, :]

    def keys(j):
        return k_ref[0, pl.ds(pl.multiple_of(j * t, t), t), :]

    def values(j):
        vt = vt_ref[0, j]
        return [_values_with_ones(vt, hh) for hh in range(2)]

    def far_scores(j):
        k = keys(j)
        return [_dot_nt(k, qms[hh]) for hh in range(2)]

    def far_valid(j):
        return [sel_ref[hh * nblk + j] for hh in range(2)]

    jp = jnp.maximum(qi - 1, 0)

    def prev_scores():
        k = keys(jp)
        return [_dot_nt(k, qms[hh]) + bias_ref[hh, 1] for hh in range(2)]

    def own_scores():
        k = keys(qi)
        return [_dot_nt(k, qms[hh]) + bias_ref[hh, 0] for hh in range(2)]

    accs = _attend_all(2, t, jnp.maximum(qi - 1, 0), far_scores, far_valid, prev_scores,
                       lambda: far_valid(jp), own_scores, values, jp, qi, s_ref, p_ref, acc_ref)
    out_t = jnp.concatenate([_normalised(accs[0]), _normalised(accs[1])], axis=0)
    o_ref[0] = out_t.T.astype(o_ref.dtype)


def _attn_specs(s, t, mixer):
    nkv = s // t
    qcol = mixer * 2 * GROUP_WIDTH // LANES
    kcol = qcol + GROUP_WIDTH // LANES
    vrow = mixer * GROUP_WIDTH // LANES
    return [
        pl.BlockSpec((1, t, LANES), lambda bi, p, qi: (bi, qi, qcol + p)),
        pl.BlockSpec((1, s, LANES), lambda bi, p, qi: (bi, 0, kcol + p)),
        pl.BlockSpec((1, nkv, LANES, t), lambda bi, p, qi: (bi, 0, vrow + p, 0)),
    ]


def _attn_scratch(n_chains, t):
    return [pltpu.VMEM((n_chains, t, t), F32),
            pltpu.VMEM((n_chains, t, t), BF16),
            pltpu.VMEM((n_chains, ACC_ROWS, t), F32)]


def _moba(qk, vt, kmean, bias_t):
    b, s, _ = qk.shape
    t = ATTN_TILE
    nq = s // t
    nblk = kmean.shape[1]
    return pl.pallas_call(
        _moba_kernel,
        grid=(b, 2, nq),
        in_specs=_attn_specs(s, t, 0) + [
            pl.BlockSpec((1, nblk, LANES), lambda bi, p, qi: (bi, 0, p)),
            pl.BlockSpec((2, 2, t, t), lambda bi, p, qi: (p, 0, 0, 0)),
        ],
        out_specs=pl.BlockSpec((1, t, LANES), lambda bi, p, qi: (bi, qi, p)),
        out_shape=jax.ShapeDtypeStruct((b, s, GROUP_WIDTH), BF16),
        scratch_shapes=[pltpu.VMEM((2 * nblk, 1, t), F32)]
        + _attn_scratch(2, t),
        compiler_params=pltpu.CompilerParams(
            dimension_semantics=("arbitrary", "arbitrary", "arbitrary"),
            vmem_limit_bytes=VMEM_LIMIT),
        name="moba",
    )(qk, qk, vt, kmean, bias_t)


def _diff_kernel(lam_ref, q_ref, k_ref, vt_ref, bias_ref, gnorm_ref, o_ref, s_ref, p_ref, acc_ref):
    t = ATTN_TILE
    qi = pl.program_id(2)
    q = q_ref[0]
    zero = jnp.zeros_like(q)
    qms = [jnp.where(_lane_mask(HEAD_DIM * hh + DIFF_QK_DIM * mm, DIFF_QK_DIM), q, zero)
           for hh in range(2) for mm in range(2)]

    def keys(j):
        return k_ref[0, pl.ds(pl.multiple_of(j * t, t), t), :]

    def values(j):
        vt = vt_ref[0, j]
        vts = [_values_with_ones(vt, hh) for hh in range(2)]
        return [vts[c // 2] for c in range(4)]

    def far_scores(j):
        k = keys(j)
        return [_dot_nt(k, qms[c]) for c in range(4)]

    jp = jnp.maximum(qi - 1, 0)

    def prev_scores():
        k = keys(jp)
        return [_dot_nt(k, qms[c]) + bias_ref[c // 2, 1] for c in range(4)]

    def prev_valid():
        return [jnp.full((1, t), jnp.where(qi >= 1, 1.0, 0.0), F32)] * 4

    def own_scores():
        k = keys(qi)
        return [_dot_nt(k, qms[c]) + bias_ref[c // 2, 0] for c in range(4)]

    lam_p = lam_ref[...]
    lam_init = lam_p[4:5, 0:1]
    lam = (jnp.exp(jnp.sum(lam_p[0:1] * lam_p[1:2], axis=-1, keepdims=True))
           - jnp.exp(jnp.sum(lam_p[2:3] * lam_p[3:4], axis=-1, keepdims=True)) + lam_init)

    accs = _attend_all(4, t, jnp.maximum(qi - 1, 0), far_scores, lambda j: None, prev_scores,
                       prev_valid, own_scores, values, jp, qi, s_ref, p_ref, acc_ref)
    outs = []
    for hh in range(2):
        o = _normalised(accs[2 * hh]) - lam * _normalised(accs[2 * hh + 1])
        ms = jnp.mean(o * o, axis=0, keepdims=True)
        outs.append(o * lax.rsqrt(ms + EPS))
    y_t = jnp.concatenate(outs, axis=0)
    o_ref[0] = (y_t.T * gnorm_ref[...] * (1.0 - lam_init)).astype(o_ref.dtype)


def _diff(qk, vt, lam_p, bias_t, gnorm):
    b, s, _ = qk.shape
    t = ATTN_TILE
    nq = s // t
    return pl.pallas_call(
        _diff_kernel,
        grid=(b, 2, nq),
        in_specs=[pl.BlockSpec(lam_p.shape, lambda bi, p, qi: (0, 0))] + _attn_specs(s, t, 1) + [
            pl.BlockSpec((2, 2, t, t), lambda bi, p, qi: (p, 0, 0, 0)),
            pl.BlockSpec((1, LANES), lambda bi, p, qi: (0, 0)),
        ],
        out_specs=pl.BlockSpec((1, t, LANES), lambda bi, p, qi: (bi, qi, p)),
        out_shape=jax.ShapeDtypeStruct((b, s, GROUP_WIDTH), BF16),
        scratch_shapes=_attn_scratch(4, t),
        compiler_params=pltpu.CompilerParams(
            dimension_semantics=("arbitrary", "arbitrary", "arbitrary"),
            vmem_limit_bytes=VMEM_LIMIT),
        name="diff_attn",
    )(lam_p, qk, qk, vt, bias_t, gnorm)


def _head_of_lane(n_lanes, width):
    return lax.broadcasted_iota(jnp.int32, (1, n_lanes), 1) // width


def _block_rows(x, lane_head, n_heads=N_HEADS_G):
    zero = jnp.zeros_like(x)
    return jnp.concatenate([jnp.where(lane_head == h, x, zero) for h in range(n_heads)], axis=0)


def _gdn_kernel(x_ref, misc_ref, conv_ref, hp_ref, gnorm_ref, o_ref, s_ref, tail_ref, g_ref, beta_ref):
    L = CHUNK
    W = GROUP_WIDTH
    rows = range(x_ref.shape[0])
    seq = x_ref.shape[1]
    n_chunks = seq // L
    head_w = _head_of_lane(W, HEAD_DIM)
    head_all = _head_of_lane(LANES, 1)
    ri = lax.broadcasted_iota(jnp.int32, (L, W), 0)
    cj = lax.broadcasted_iota(jnp.int32, (L, W), 1) % HEAD_DIM
    lower = ri >= cj
    strict = ri > cj
    ident = jnp.where(ri == cj, 1.0, 0.0)
    ones_blk = (lax.broadcasted_iota(jnp.int32, (W, W), 0) // HEAD_DIM
                == lax.broadcasted_iota(jnp.int32, (W, W), 1) // HEAD_DIM)
    ones_seg = ones_blk.astype(BF16)
    tril = (lax.broadcasted_iota(jnp.int32, (L, L), 0)
            >= lax.broadcasted_iota(jnp.int32, (L, L), 1)).astype(BF16)

    hp = hp_ref[...]
    exp_g = (lax.broadcasted_iota(jnp.int32, (LANES, W), 0) - MISC_A
             == lax.broadcasted_iota(jnp.int32, (LANES, W), 1) // HEAD_DIM).astype(BF16)
    exp_b = (lax.broadcasted_iota(jnp.int32, (LANES, W), 0) - MISC_B
             == lax.broadcasted_iota(jnp.int32, (LANES, W), 1) // HEAD_DIM).astype(BF16)
    del head_all
    for r in rows:
        misc = misc_ref[r]
        sp_in = misc + hp[1:2]
        softplus = jnp.maximum(sp_in, 0.0) + jnp.log(1.0 + jnp.exp(-jnp.abs(sp_in)))
        g_tok = -jnp.exp(hp[0:1]) * softplus
        b_tok = _sigmoid(misc)
        g_ref[r] = _dot_exact_rhs(g_tok, exp_g, 2)
        beta_ref[r] = _dot_exact_rhs(b_tok, exp_b, 2)

    @pl.when(pl.program_id(1) == 0)
    def _():
        s_ref[...] = jnp.zeros_like(s_ref)
        tail_ref[...] = jnp.zeros_like(tail_ref)

    cw = conv_ref[...]
    blk = lambda a: _block_rows(a.astype(BF16), head_w)
    bf = lambda a: a.astype(BF16)

    def prepare(r, r0):
        x = x_ref[r, pl.ds(r0, L), :]
        qkv = x[:, 0:3 * W]
        xx = jnp.concatenate([tail_ref[r], qkv], axis=0)
        tail_ref[r] = qkv[L - 8:L, :]
        conv = cw[CONV_WIDTH - 1:CONV_WIDTH] * qkv
        for i in range(CONV_WIDTH - 1):
            lo = 8 - (CONV_WIDTH - 1) + i
            conv = conv + cw[i:i + 1] * xx[lo:lo + L, :]
        conv = _silu(conv)
        return conv[:, 0:W], conv[:, W:2 * W], conv[:, 2 * W:3 * W], x[:, 3 * W:4 * W]

    def decays(gc):
        gc_col = jnp.sum(gc * ident, axis=0, keepdims=True)
        return jnp.exp(jnp.where(lower, gc - gc_col, -jnp.inf))

    def chunk(c, _):
        r0 = pl.multiple_of(c * L, L)
        q, k, v, z = zip(*[prepare(r, r0) for r in rows])
        ssq = [_dot_exact_rhs(jnp.concatenate([q[r] * q[r], k[r] * k[r]], axis=0), ones_seg, 1)
               for r in rows]
        q = [q[r] * lax.rsqrt(ssq[r][0:L] + EPS) * (HEAD_DIM ** -0.5) for r in rows]
        k = [k[r] * lax.rsqrt(ssq[r][L:2 * L] + EPS) for r in rows]
        beta = [beta_ref[r, pl.ds(r0, L), :] for r in rows]
        gc = [_dot_exact_lhs(tril, g_ref[r, pl.ds(r0, L), :], 2) for r in rows]
        egc = [jnp.exp(gc[r]) for r in rows]
        gc_last = [gc[r][L - 1:L, :] for r in rows]
        decay = [decays(gc[r]) for r in rows]
        kb = [k[r] * beta[r] for r in rows]
        both = [_dot_nt(bf(jnp.concatenate([kb[r], q[r]], axis=0)), blk(k[r])) for r in rows]
        qk = [jnp.where(lower, both[r][L:2 * L] * decay[r], 0.0) for r in rows]

        p = [jnp.where(strict, -both[r][0:L] * decay[r], 0.0) for r in rows]
        t_inv = [ident + p[r] for r in rows]
        p = [_dot(bf(p[r]), blk(p[r])) for r in rows]
        for _ in range(4):
            prod = [_dot(bf(jnp.concatenate([t_inv[r], p[r]], axis=0)), blk(p[r])) for r in rows]
            t_inv = [t_inv[r] + prod[r][0:L] for r in rows]
            p = [prod[r][L:2 * L] for r in rows]
        t_inv = [t_inv[r] + _dot(bf(t_inv[r]), blk(p[r])) for r in rows]

        u = [_dot(bf(t_inv[r]), blk(v[r] * beta[r])) for r in rows]
        w = [_dot(bf(t_inv[r]), blk(kb[r] * egc[r])) for r in rows]

        state = [s_ref[r] for r in rows]
        ws_qs = [_dot(bf(jnp.concatenate([w[r], q[r] * egc[r]], axis=0)), bf(state[r])) for r in rows]
        v_new = [u[r] - ws_qs[r][0:L] for r in rows]
        o = [ws_qs[r][L:2 * L] + _dot(bf(qk[r]), blk(v_new[r])) for r in rows]
        upd = [_dot_tn(bf(k[r] * jnp.exp(gc_last[r] - gc[r])), bf(v_new[r])) for r in rows]
        for r in rows:
            s_ref[r] = state[r] * jnp.exp(gc_last[r]) + jnp.where(ones_blk, upd[r], 0.0)

        ms = [_dot_exact_rhs(o[r] * o[r], ones_seg, 1) * (1.0 / HEAD_DIM) for r in rows]
        for r in rows:
            y = o[r] * lax.rsqrt(ms[r] + EPS) * gnorm_ref[...] * _silu(z[r])
            o_ref[r, pl.ds(r0, L), :] = y.astype(o_ref.dtype)
        return 0

    lax.fori_loop(0, n_chunks, chunk, 0)


def _gdn(gdn, misc, conv_w, hp, gnorm):
    b, s, _ = gdn.shape
    ts = min(s, REC_SEQ_TILE)
    nr = math.gcd(b, REC_ROWS)
    const = lambda bi, si: (0, 0)
    return pl.pallas_call(
        _gdn_kernel,
        grid=(b // nr, s // ts),
        in_specs=[
            pl.BlockSpec((nr, ts, N_GDN), lambda bi, si: (bi, si, 0)),
            pl.BlockSpec((nr, ts, N_MISC), lambda bi, si: (bi, si, 0)),
            pl.BlockSpec(conv_w.shape, const),
            pl.BlockSpec(hp.shape, const),
            pl.BlockSpec((1, GROUP_WIDTH), const),
        ],
        out_specs=pl.BlockSpec((nr, ts, GROUP_WIDTH), lambda bi, si: (bi, si, 0)),
        out_shape=jax.ShapeDtypeStruct((b, s, GROUP_WIDTH), BF16),
        scratch_shapes=[
            pltpu.VMEM((nr, GROUP_WIDTH, GROUP_WIDTH), F32),
            pltpu.VMEM((nr, 8, 3 * GROUP_WIDTH), F32),
            pltpu.VMEM((nr, ts, GROUP_WIDTH), F32),
            pltpu.VMEM((nr, ts, GROUP_WIDTH), F32),
        ],
        compiler_params=pltpu.CompilerParams(
            dimension_semantics=("arbitrary", "arbitrary"), vmem_limit_bytes=VMEM_LIMIT),
        name="gdn",
    )(gdn, misc, conv_w, hp, gnorm)


def _gla_kernel(x_ref, misc_ref, walpha_ref, balpha_ref, gnorm_ref, o_ref, s_ref, la_ref):
    L = CHUNK
    W = GROUP_WIDTH
    KW = N_HEADS_G * GLA_DK
    rows = range(x_ref.shape[0])
    seq = x_ref.shape[1]
    n_chunks = seq // L
    head_k = _head_of_lane(KW, GLA_DK)
    head_v = _head_of_lane(W, HEAD_DIM)
    ri = lax.broadcasted_iota(jnp.int32, (L, W), 0)
    cj = lax.broadcasted_iota(jnp.int32, (L, W), 1) % HEAD_DIM
    lower = ri >= cj
    ones_blk = (lax.broadcasted_iota(jnp.int32, (W, W), 0) // HEAD_DIM
                == lax.broadcasted_iota(jnp.int32, (W, W), 1) // HEAD_DIM).astype(BF16)
    state_mask = (lax.broadcasted_iota(jnp.int32, (W, KW), 0) // HEAD_DIM
                  == lax.broadcasted_iota(jnp.int32, (W, KW), 1) // GLA_DK)
    tril = (lax.broadcasted_iota(jnp.int32, (L, L), 0)
            >= lax.broadcasted_iota(jnp.int32, (L, L), 1)).astype(BF16)

    for r in rows:
        pre = _dot(misc_ref[r].astype(BF16), walpha_ref[...]) + balpha_ref[...]
        log_sig = jnp.minimum(pre, 0.0) - jnp.log(1.0 + jnp.exp(-jnp.abs(pre)))
        la_ref[r] = log_sig * (1.0 / GLA_TAU)

    @pl.when(pl.program_id(1) == 0)
    def _():
        s_ref[...] = jnp.zeros_like(s_ref)

    bf = lambda a: a.astype(BF16)

    def chunk(c, _):
        r0 = pl.multiple_of(c * L, L)
        x = [x_ref[r, pl.ds(r0, L), :] for r in rows]
        k = [x[r][:, KW:2 * KW] for r in rows]
        v = [bf(x[r][:, 2 * KW:2 * KW + W]) for r in rows]
        bc = [_dot_exact_lhs(tril, la_ref[r, pl.ds(r0, L), :], 2) for r in rows]
        b_last = [bc[r][L - 1:L, :] for r in rows]
        qe = [bf(x[r][:, 0:KW] * (GLA_DK ** -0.5) * jnp.exp(bc[r])) for r in rows]
        ke = [bf(k[r] * jnp.exp(-bc[r])) for r in rows]
        a_mat = [jnp.where(lower, _dot_nt(qe[r], _block_rows(ke[r], head_k)), 0.0) for r in rows]
        state_t = [s_ref[r] for r in rows]
        o = [_dot_nt(qe[r], bf(state_t[r])) for r in rows]
        o = [o[r] + _dot(bf(a_mat[r]), _block_rows(v[r], head_v)) for r in rows]
        upd = [_dot_tn(v[r], bf(k[r] * jnp.exp(b_last[r] - bc[r]))) for r in rows]
        for r in rows:
            s_ref[r] = state_t[r] * jnp.exp(b_last[r]) + jnp.where(state_mask, upd[r], 0.0)

        ms = [_dot_exact_rhs(o[r] * o[r], ones_blk, 1) * (1.0 / HEAD_DIM) for r in rows]
        for r in rows:
            gate = _silu(x[r][:, 2 * KW + W:2 * KW + 2 * W])
            y = o[r] * lax.rsqrt(ms[r] + EPS) * gnorm_ref[...] * gate
            o_ref[r, pl.ds(r0, L), :] = y.astype(o_ref.dtype)
        return 0

    lax.fori_loop(0, n_chunks, chunk, 0)


def _gla(gla, misc, walpha, balpha, gnorm):
    b, s, _ = gla.shape
    ts = min(s, REC_SEQ_TILE)
    nr = math.gcd(b, REC_ROWS)
    const = lambda bi, si: (0, 0)
    return pl.pallas_call(
        _gla_kernel,
        grid=(b // nr, s // ts),
        in_specs=[
            pl.BlockSpec((nr, ts, N_GLA), lambda bi, si: (bi, si, 0)),
            pl.BlockSpec((nr, ts, N_MISC), lambda bi, si: (bi, si, 0)),
            pl.BlockSpec(walpha.shape, const),
            pl.BlockSpec(balpha.shape, const),
            pl.BlockSpec((1, GROUP_WIDTH), const),
        ],
        out_specs=pl.BlockSpec((nr, ts, GROUP_WIDTH), lambda bi, si: (bi, si, 0)),
        out_shape=jax.ShapeDtypeStruct((b, s, GROUP_WIDTH), BF16),
        scratch_shapes=[
            pltpu.VMEM((nr, GROUP_WIDTH, N_HEADS_G * GLA_DK), F32),
            pltpu.VMEM((nr, ts, N_HEADS_G * GLA_DK), F32),
        ],
        compiler_params=pltpu.CompilerParams(
            dimension_semantics=("arbitrary", "arbitrary"), vmem_limit_bytes=VMEM_LIMIT),
        name="gla",
    )(gla, misc, walpha, balpha, gnorm)


def _ffn_chunks(d_ff):
    step = 4 * GROUP_WIDTH
    return [(lo, min(step, d_ff - lo)) for lo in range(0, d_ff, step)]


def _mlp_kernel(h_ref, ya_ref, yb_ref, yc_ref, yd_ref, wout_ref, gffn_ref, wg_ref, wu_ref, wd_ref,
                gple_ref, wpg_ref, p_ref, wpp_ref, gfin_ref, o_ref, act_ref, *, final):
    mixed = jnp.concatenate([ya_ref[...], yb_ref[...], yc_ref[...], yd_ref[...]], axis=-1)
    h = h_ref[...] + _dot(mixed, wout_ref[...])
    hn = _rms(h, gffn_ref[...]).astype(BF16)
    for lo, width in _ffn_chunks(wg_ref.shape[1]):
        gate = _dot(hn, wg_ref[:, lo:lo + width])
        up = _dot(hn, wu_ref[:, lo:lo + width])
        act_ref[:, lo:lo + width] = (_silu(gate) * up).astype(BF16)
    h = h + _dot(act_ref[...], wd_ref[...])
    gate = _sigmoid(_dot(_rms(h, gple_ref[...]).astype(BF16), wpg_ref[...]))
    h = h + gate * _dot(p_ref[...].astype(BF16), wpp_ref[...])
    if final:
        h = _rms(h, gfin_ref[...])
    o_ref[...] = h


def _mlp(h2d, ys, wout, gffn, wg, wu, wd, gple, wpg, p2d, wpp, gfin, tm, final):
    m, d = h2d.shape
    d_ff = wg.shape[1]
    const = lambda i: (0, 0)
    resident = lambda shape: pl.BlockSpec(shape, const, pipeline_mode=pl.Buffered(1))
    rows = lambda width: pl.BlockSpec((tm, width), lambda i: (i, 0))
    return pl.pallas_call(
        functools.partial(_mlp_kernel, final=final),
        grid=(m // tm,),
        in_specs=[rows(d)] + [rows(GROUP_WIDTH)] * 4 + [
            resident(wout.shape), resident(gffn.shape), resident(wg.shape), resident(wu.shape),
            resident(wd.shape), resident(gple.shape), resident(wpg.shape),
            rows(p2d.shape[1]), resident(wpp.shape), resident(gfin.shape),
        ],
        out_specs=rows(d),
        out_shape=jax.ShapeDtypeStruct((m, d), F32),
        scratch_shapes=[pltpu.VMEM((tm, d_ff), BF16)],
        compiler_params=pltpu.CompilerParams(
            dimension_semantics=("arbitrary",), vmem_limit_bytes=VMEM_LIMIT),
        name="mlp",
    )(h2d, *ys, wout, gffn, wg, wu, wd, gple, wpg, p2d, wpp, gfin)


def _rel_bucket_table(n_dist):
    n = np.arange(n_dist)
    nf = np.maximum(n, 1).astype(np.float32)
    large = REL_MAX_EXACT + (np.log(nf / REL_MAX_EXACT) / math.log(REL_MAX_DIST / REL_MAX_EXACT)
                             * (N_REL_BUCKETS - REL_MAX_EXACT)).astype(np.int32)
    large = np.minimum(large, N_REL_BUCKETS - 1)
    return np.where(n < REL_MAX_EXACT, n, large)


def _bias_tiles(rel_bias):
    t = ATTN_TILE
    bucket = _rel_bucket_table(2 * t)
    assert (bucket[t + 1:] == N_REL_BUCKETS - 1).all()
    per_dist = (rel_bias[bucket, :] - rel_bias[N_REL_BUCKETS - 1][None, :]).T * LOG2E
    own = jnp.concatenate([per_dist[:, :t], jnp.full_like(per_dist[:, :t], NEG)], axis=1)
    prev = jnp.concatenate([per_dist[:, t:], per_dist[:, :t]], axis=1)
    vec = jnp.stack([own, prev], axis=1)
    rolled = jnp.tile(vec, (1, 1, t))[:, :, :t * (2 * t - 1)].reshape(-1, 2, t, 2 * t - 1)
    return rolled[:, :, :, :t]


def _row(v, width=None):
    v = v.astype(F32).reshape(1, -1)
    if width is not None and v.shape[1] < width:
        v = jnp.pad(v, ((0, 0), (0, width - v.shape[1])))
    return v


def kernel(x, p, norm_mix, w_in, rel_bias, diff_lambda, diff_norm, gdn_conv, gdn_a_log, gdn_dt_bias,
           gdn_norm, gla_w_alpha, gla_b_alpha, gla_norm, w_out, norm_ffn, w_gate, w_up, w_down,
           norm_ple, w_ple_gate, w_ple_proj, final_norm):
    b, s, d = x.shape
    depth = w_in.shape[0]
    m = b * s
    tm = 512
    hg = N_HEADS_G
    assert s % ATTN_TILE == 0 and s % tm == 0 and tm % MOBA_BLOCK == 0

    bias_t = _bias_tiles(rel_bias.astype(F32))
    bias_a, bias_b = bias_t[:hg], bias_t[hg:]
    gw = GROUP_WIDTH
    col_scale = np.ones((1, N_QK), np.float32)
    col_scale[:, 0:gw] = HEAD_DIM ** -0.5
    col_scale[:, gw:2 * gw] = LOG2E
    col_scale[:, 2 * gw:3 * gw] = DIFF_QK_DIM ** -0.5 * LOG2E
    col_scale = jnp.asarray(col_scale)

    n_main = 6 * gw + N_GDN
    ab = 2 * hg
    gla_lo = n_main + ab
    gla_hi = gla_lo + N_GLA
    fin = _row(final_norm)
    h = x.reshape(m, d)
    for l in range(depth):
        wl = w_in[l]
        w1 = jnp.concatenate(
            [wl[:, 0:2 * gw], wl[:, 3 * gw:5 * gw], wl[:, 6 * gw:n_main], wl[:, gla_lo:gla_hi],
             wl[:, n_main:gla_lo], wl[:, gla_hi:],
             jnp.zeros((d, N_MISC - ab - GLA_GATE_RANK), wl.dtype)], axis=1).astype(BF16)
        wvt = jnp.concatenate([wl[:, 2 * gw:3 * gw], wl[:, 5 * gw:6 * gw]], axis=1).T.astype(BF16)
        qk, vt, gdn, gla, misc, kmean = _inproj(h, _row(norm_mix[l]), w1, wvt, col_scale, tm, s)
        qk = qk.reshape(b, s, N_QK)
        misc = misc.reshape(b, s, N_MISC)
        kmean = kmean.reshape(b, s // MOBA_BLOCK, GROUP_WIDTH)

        y_a = _moba(qk, vt, kmean, bias_a)

        lam_init = 0.8 - 0.6 * math.exp(-0.3 * l)
        lam_p = jnp.concatenate([diff_lambda[l].astype(F32),
                                 jnp.full((1, DIFF_QK_DIM), lam_init, F32)], axis=0)
        y_b = _diff(qk, vt, lam_p, bias_b, _row(jnp.tile(diff_norm[l], 2)))

        hp = jnp.concatenate([_row(gdn_a_log[l], LANES), _row(gdn_dt_bias[l], LANES)], axis=0)
        y_c = _gdn(gdn.reshape(b, s, N_GDN), misc, gdn_conv[l].astype(F32), hp,
                   _row(jnp.tile(gdn_norm[l], hg)))

        walpha = jnp.zeros((N_MISC, hg * GLA_DK), F32).at[MISC_LR:MISC_LR + GLA_GATE_RANK].set(
            gla_w_alpha[l]).astype(BF16)
        y_d = _gla(gla.reshape(b, s, N_GLA), misc, walpha, _row(gla_b_alpha[l]),
                   _row(jnp.tile(gla_norm[l], hg)))

        ys = [y.reshape(m, GROUP_WIDTH) for y in (y_a, y_b, y_c, y_d)]
        h = _mlp(h, ys, w_out[l].astype(BF16), _row(norm_ffn[l]), w_gate[l].astype(BF16),
                 w_up[l].astype(BF16), w_down[l].astype(BF16), _row(norm_ple[l]),
                 w_ple_gate[l].astype(BF16), p[l].reshape(m, -1), w_ple_proj[l].astype(BF16),
                 fin, tm, final=(l == depth - 1))
    return h.reshape(b, s, d)
```

```python
import functools
import math

import numpy as np
import jax
import jax.numpy as jnp
from jax import lax
from jax.experimental import pallas as pl
from jax.experimental.pallas import tpu as pltpu

F32 = jnp.float32
BF16 = jnp.bfloat16

HEAD_DIM = 64
N_HEADS_G = 4
GROUP_WIDTH = HEAD_DIM * N_HEADS_G
MOBA_BLOCK = 256
MOBA_TOPK = 3
N_REL_BUCKETS = 32
REL_MAX_EXACT = 16
REL_MAX_DIST = 128
DIFF_QK_DIM = HEAD_DIM // 2
CONV_WIDTH = 4
CHUNK = 64
GLA_DK = HEAD_DIM // 2
GLA_GATE_RANK = 16
GLA_TAU = 16.0
EPS = 1e-6

LANES = 128
ATTN_TILE = 256
REC_SEQ_TILE = 256
REC_ROWS = 8
NEG = -1e30
LOG2E = math.log2(math.e)
VMEM_LIMIT = 56 * 1024 * 1024

N_QK = 4 * GROUP_WIDTH
N_VT = 2 * GROUP_WIDTH
N_GDN = 4 * GROUP_WIDTH
N_GLA = 2 * N_HEADS_G * GLA_DK + 2 * GROUP_WIDTH
N_MISC = LANES
MISC_A, MISC_B, MISC_LR = 0, N_HEADS_G, 2 * N_HEADS_G

_NT = (((1,), (1,)), ((), ()))
_TN = (((0,), (0,)), ((), ()))


def _dot(a, b):
    return jnp.dot(a, b, preferred_element_type=F32)


def _dot_nt(a, b):
    return lax.dot_general(a, b, _NT, preferred_element_type=F32)


def _dot_tn(a, b):
    return lax.dot_general(a, b, _TN, preferred_element_type=F32)


def _split_bf16(x, n):
    parts = []
    r = x
    for _ in range(n):
        hi = r.astype(BF16)
        parts.append(hi)
        r = r - hi.astype(F32)
    return parts


def _dot_exact_rhs(x, w, n):
    acc = None
    for part in _split_bf16(x, n):
        t = _dot(part, w)
        acc = t if acc is None else acc + t
    return acc


def _dot_exact_lhs(w, x, n):
    acc = None
    for part in _split_bf16(x, n):
        t = _dot(w, part)
        acc = t if acc is None else acc + t
    return acc


def _rms(x, g):
    return x * lax.rsqrt(jnp.mean(x * x, axis=-1, keepdims=True) + EPS) * g


def _sigmoid(x):
    return 1.0 / (1.0 + jnp.exp(-x))


def _silu(x):
    return x * _sigmoid(x)


def _inproj_kernel(x_ref, g_ref, w_ref, wvt_ref, scale_ref,
                   qk_ref, vt_ref, gdn_ref, gla_ref, misc_ref, kmean_ref):
    tm = x_ref.shape[0]
    xn = _rms(x_ref[...], g_ref[...]).astype(BF16)
    qk = _dot(xn, w_ref[:, 0:N_QK])
    k_moba = qk[:, GROUP_WIDTH:2 * GROUP_WIDTH]
    kmean_ref[0] = jnp.mean(k_moba.reshape(tm // MOBA_BLOCK, MOBA_BLOCK, GROUP_WIDTH), axis=1)
    qk_ref[...] = (qk * scale_ref[...]).astype(BF16)
    vt = _dot_nt(wvt_ref[...], xn).astype(BF16)
    for i in range(tm // ATTN_TILE):
        vt_ref[0, i] = vt[:, i * ATTN_TILE:(i + 1) * ATTN_TILE]
    o = N_QK
    gdn_ref[...] = _dot(xn, w_ref[:, o:o + N_GDN])
    o += N_GDN
    gla_ref[...] = _dot(xn, w_ref[:, o:o + N_GLA])
    o += N_GLA
    misc_ref[...] = _dot(xn, w_ref[:, o:o + N_MISC])


def _inproj(h2d, g, w, wvt, scale, tm, seq):
    m, d = h2d.shape
    n_all = w.shape[1]
    per_seq = seq // tm
    const = lambda i: (0, 0)
    return pl.pallas_call(
        _inproj_kernel,
        grid=(m // tm,),
        in_specs=[
            pl.BlockSpec((tm, d), lambda i: (i, 0)),
            pl.BlockSpec((1, d), const),
            pl.BlockSpec((d, n_all), const),
            pl.BlockSpec((N_VT, d), const),
            pl.BlockSpec((1, N_QK), const),
        ],
        out_specs=[
            pl.BlockSpec((tm, N_QK), lambda i: (i, 0)),
            pl.BlockSpec((1, tm // ATTN_TILE, N_VT, ATTN_TILE),
                         lambda i: (i // per_seq, i % per_seq, 0, 0)),
            pl.BlockSpec((tm, N_GDN), lambda i: (i, 0)),
            pl.BlockSpec((tm, N_GLA), lambda i: (i, 0)),
            pl.BlockSpec((tm, N_MISC), lambda i: (i, 0)),
            pl.BlockSpec((1, tm // MOBA_BLOCK, GROUP_WIDTH), lambda i: (i, 0, 0)),
        ],
        out_shape=[
            jax.ShapeDtypeStruct((m, N_QK), BF16),
            jax.ShapeDtypeStruct((m // seq, seq // ATTN_TILE, N_VT, ATTN_TILE), BF16),
            jax.ShapeDtypeStruct((m, N_GDN), F32),
            jax.ShapeDtypeStruct((m, N_GLA), F32),
            jax.ShapeDtypeStruct((m, N_MISC), F32),
            jax.ShapeDtypeStruct((m // tm, tm // MOBA_BLOCK, GROUP_WIDTH), F32),
        ],
        compiler_params=pltpu.CompilerParams(
            dimension_semantics=("arbitrary",), vmem_limit_bytes=VMEM_LIMIT),
        name="inproj",
    )(h2d, g, w, wvt, scale)


def _lane_mask(lo, width):
    lane = lax.broadcasted_iota(jnp.int32, (1, LANES), 1)
    return (lane >= lo) & (lane < lo + width)


ACC_ROWS = HEAD_DIM + 16
PIPE_CHAIN_TILES = 8
ATTN_HEADS_PER_STEP = 4


def _values_with_ones(vt, hh):
    head = vt[HEAD_DIM * hh:HEAD_DIM * (hh + 1), :]
    return jnp.concatenate([head, jnp.ones((ACC_ROWS - HEAD_DIM, vt.shape[1]), vt.dtype)], axis=0)


def _softmax_stage(ms, scores, valids=None):
    ms_new, alphas, ps = [], [], []
    for c, (m, s) in enumerate(zip(ms, scores)):
        m_tile = jnp.max(s, axis=0, keepdims=True)
        if valids is None or valids[c] is None:
            m_new = jnp.maximum(m, m_tile)
            shift = m_new
        else:
            ok = valids[c] > 0.5
            m_new = jnp.maximum(m, jnp.where(ok, m_tile, NEG))
            shift = jnp.where(ok, m_new, -NEG)
        ms_new.append(m_new)
        alphas.append(jnp.exp2(m - m_new))
        ps.append(jnp.exp2(s - shift).astype(BF16))
    return ms_new, alphas, ps


def _value_products(vts, ps):
    return [_dot(vt, p) for vt, p in zip(vts, ps)]


def _attend_all(n, t, n_far, far_scores, far_valid, prev_scores, prev_valid, own_scores, values,
                j_prev, j_own, s_ref, p_ref, acc_ref):
    chains = range(n)
    n_slots = jnp.maximum(n_far, 2)

    def slot_valid(j):
        exists = jnp.where(j < n_far, 1.0, 0.0)
        rows = far_valid(j)
        if rows is None:
            return [jnp.full((1, t), exists, F32)] * n
        return [row * exists for row in rows]

    s0, s1 = far_scores(0), far_scores(1)
    ms, alphas, ps = _softmax_stage([jnp.full((1, t), NEG, F32)] * n, s0, slot_valid(0))
    for c in chains:
        p_ref[c] = ps[c]
        s_ref[c] = s1[c]
        acc_ref[c] = jnp.zeros((ACC_ROWS, t), F32)

    def steps(first, unroll, carry):
        ms, alphas = carry
        ps = [p_ref[c] for c in chains]
        s_cur = [s_ref[c] for c in chains]
        for u in range(unroll):
            pv = _value_products(values(first + u), ps)
            s_new = far_scores(first + u + 2)
            ms, alphas_next, ps = _softmax_stage(ms, s_cur, far_valid(first + u + 1))
            for c in chains:
                acc_ref[c] = alphas[c] * acc_ref[c] + pv[c]
            alphas, s_cur = alphas_next, s_new
        for c in chains:
            p_ref[c] = ps[c]
            s_ref[c] = s_cur[c]
        return ms, alphas

    n_steps = n_slots - 2
    unroll = max(1, PIPE_CHAIN_TILES // n)
    n_blocks = n_steps // unroll
    carry = lax.fori_loop(0, n_blocks, lambda i, cr: steps(i * unroll, unroll, cr), (ms, alphas))
    if unroll > 1:
        carry = lax.fori_loop(n_blocks * unroll, n_steps, lambda i, cr: steps(i, 1, cr), carry)
    ms, alphas = carry

    pv = _value_products(values(n_slots - 2), [p_ref[c] for c in chains])
    s_prev = prev_scores()
    ms, alphas1, ps1 = _softmax_stage(ms, [s_ref[c] for c in chains], slot_valid(n_slots - 1))
    accs = [alphas[c] * acc_ref[c] + pv[c] for c in chains]
    pv = _value_products(values(n_slots - 1), ps1)
    s_own = own_scores()
    ms, alphas2, ps2 = _softmax_stage(ms, s_prev, prev_valid())
    accs = [alphas1[c] * accs[c] + pv[c] for c in chains]
    pv = _value_products(values(j_prev), ps2)
    ms, alphas3, ps3 = _softmax_stage(ms, s_own)
    accs = [alphas2[c] * accs[c] + pv[c] for c in chains]
    pv = _value_products(values(j_own), ps3)
    return [alphas3[c] * accs[c] + pv[c] for c in chains]


def _normalised(acc):
    return acc[0:HEAD_DIM, :] / acc[HEAD_DIM:HEAD_DIM + 1, :]


def _moba_kernel(q_ref, k_ref, vt_ref, kmean_ref, bias_ref, o_ref, sel_ref, s_ref, p_ref, acc_ref):
    t = ATTN_TILE
    qi = pl.program_id(2)
    nblk = kmean_ref.shape[1]
    heads = range(q_ref.shape[2] // HEAD_DIM)
    q = q_ref[0]
    kmean = kmean_ref[0].astype(BF16)

    def pair_lanes(x, h):
        lo = LANES * (h // 2)
        return x[:, lo:lo + LANES]

    row = lax.broadcasted_iota(jnp.int32, (nblk, t), 0)
    qms = []
    for h in heads:
        qp = pair_lanes(q, h)
        qm = jnp.where(_lane_mask(HEAD_DIM * (h % 2), HEAD_DIM), qp, jnp.zeros_like(qp))
        qms.append(qm)
        gate = _dot_nt(pair_lanes(kmean, h), qm)
        gate = jnp.where(row < qi, gate, -jnp.inf)
        sel_t = jnp.zeros((nblk, t), F32)
        for j in range(nblk):
            gj = gate[j:j + 1, :]
            beats = jnp.where(gate > gj, 1.0, jnp.where((gate == gj) & (row < j), 1.0, 0.0))
            rank = jnp.sum(beats, axis=0, keepdims=True)
            sel_j = jnp.where(rank < MOBA_TOPK, jnp.where(j < qi, 1.0, 0.0), 0.0)
            sel_t = jnp.where(row == j, sel_j, sel_t)
        for j in range(nblk):
            sel_ref[h * nblk + j] = sel_t[j:j + 1, :]

    def keys(j):
        return k_ref[0, pl.ds(pl.multiple_of(j * t, t), t), :]

    def values(j):
        vt = vt_ref[0, j]
        return [_values_with_ones(vt, h) for h in heads]

    def far_scores(j):
        k = keys(j)
        return [_dot_nt(pair_lanes(k, h), qms[h]) for h in heads]

    def far_valid(j):
        return [sel_ref[h * nblk + j] for h in heads]

    jp = jnp.maximum(qi - 1, 0)

    def prev_scores():
        k = keys(jp)
        return [_dot_nt(pair_lanes(k, h), qms[h]) + bias_ref[h, 1] for h in heads]

    def own_scores():
        k = keys(qi)
        return [_dot_nt(pair_lanes(k, h), qms[h]) + bias_ref[h, 0] for h in heads]

    accs = _attend_all(len(heads), t, jnp.maximum(qi - 1, 0), far_scores, far_valid, prev_scores,
                       lambda: far_valid(jp), own_scores, values, jp, qi, s_ref, p_ref, acc_ref)
    out_t = jnp.concatenate([_normalised(acc) for acc in accs], axis=0)
    o_ref[0] = out_t.T.astype(o_ref.dtype)


def _attn_specs(s, t, mixer, width):
    nkv = s // t
    per = GROUP_WIDTH // width
    qcol = mixer * 2 * per
    kcol = qcol + per
    vrow = mixer * per
    return [
        pl.BlockSpec((1, t, width), lambda bi, g, qi: (bi, qi, qcol + g)),
        pl.BlockSpec((1, s, width), lambda bi, g, qi: (bi, 0, kcol + g)),
        pl.BlockSpec((1, nkv, width, t), lambda bi, g, qi: (bi, 0, vrow + g, 0)),
    ]


def _attn_scratch(n_chains, t):
    return [pltpu.VMEM((n_chains, t, t), F32),
            pltpu.VMEM((n_chains, t, t), BF16),
            pltpu.VMEM((n_chains, ACC_ROWS, t), F32)]


def _moba(qk, vt, kmean, bias_t):
    b, s, _ = qk.shape
    t = ATTN_TILE
    nq = s // t
    nblk = kmean.shape[1]
    n_heads = ATTN_HEADS_PER_STEP
    width = n_heads * HEAD_DIM
    return pl.pallas_call(
        _moba_kernel,
        grid=(b, GROUP_WIDTH // width, nq),
        in_specs=_attn_specs(s, t, 0, width) + [
            pl.BlockSpec((1, nblk, width), lambda bi, g, qi: (bi, 0, g)),
            pl.BlockSpec((n_heads, 2, t, t), lambda bi, g, qi: (g, 0, 0, 0)),
        ],
        out_specs=pl.BlockSpec((1, t, width), lambda bi, g, qi: (bi, qi, g)),
        out_shape=jax.ShapeDtypeStruct((b, s, GROUP_WIDTH), BF16),
        scratch_shapes=[pltpu.VMEM((n_heads * nblk, 1, t), F32)]
        + _attn_scratch(n_heads, t),
        compiler_params=pltpu.CompilerParams(
            dimension_semantics=("arbitrary", "arbitrary", "arbitrary"),
            vmem_limit_bytes=VMEM_LIMIT),
        name="moba",
    )(qk, qk, vt, kmean, bias_t)


def _diff_kernel(lam_ref, q_ref, k_ref, vt_ref, bias_ref, gnorm_ref, o_ref, s_ref, p_ref, acc_ref):
    t = ATTN_TILE
    qi = pl.program_id(2)
    heads = range(q_ref.shape[2] // HEAD_DIM)
    chains = [(h, mm) for h in heads for mm in range(2)]
    q = q_ref[0]

    def pair_lanes(x, h):
        lo = LANES * (h // 2)
        return x[:, lo:lo + LANES]

    qms = []
    for h, mm in chains:
        qp = pair_lanes(q, h)
        mask = _lane_mask(HEAD_DIM * (h % 2) + DIFF_QK_DIM * mm, DIFF_QK_DIM)
        qms.append(jnp.where(mask, qp, jnp.zeros_like(qp)))

    def keys(j):
        return k_ref[0, pl.ds(pl.multiple_of(j * t, t), t), :]

    def values(j):
        vt = vt_ref[0, j]
        vts = [_values_with_ones(vt, h) for h in heads]
        return [vts[h] for h, _ in chains]

    def far_scores(j):
        k = keys(j)
        return [_dot_nt(pair_lanes(k, h), qm) for (h, _), qm in zip(chains, qms)]

    jp = jnp.maximum(qi - 1, 0)

    def prev_scores():
        return [s + bias_ref[h, 1] for (h, _), s in zip(chains, far_scores(jp))]

    def prev_valid():
        return [jnp.full((1, t), jnp.where(qi >= 1, 1.0, 0.0), F32)] * len(chains)

    def own_scores():
        return [s + bias_ref[h, 0] for (h, _), s in zip(chains, far_scores(qi))]

    lam_p = lam_ref[...]
    lam_init = lam_p[4:5, 0:1]
    lam = (jnp.exp(jnp.sum(lam_p[0:1] * lam_p[1:2], axis=-1, keepdims=True))
           - jnp.exp(jnp.sum(lam_p[2:3] * lam_p[3:4], axis=-1, keepdims=True)) + lam_init)

    accs = _attend_all(len(chains), t, jnp.maximum(qi - 1, 0), far_scores, lambda j: None,
                       prev_scores, prev_valid, own_scores, values, jp, qi, s_ref, p_ref, acc_ref)
    outs = []
    for h in heads:
        o = _normalised(accs[2 * h]) - lam * _normalised(accs[2 * h + 1])
        ms = jnp.mean(o * o, axis=0, keepdims=True)
        outs.append(o * lax.rsqrt(ms + EPS))
    y_t = jnp.concatenate(outs, axis=0)
    o_ref[0] = (y_t.T * gnorm_ref[...] * (1.0 - lam_init)).astype(o_ref.dtype)


def _diff(qk, vt, lam_p, bias_t, gnorm):
    b, s, _ = qk.shape
    t = ATTN_TILE
    nq = s // t
    n_heads = ATTN_HEADS_PER_STEP
    width = n_heads * HEAD_DIM
    return pl.pallas_call(
        _diff_kernel,
        grid=(b, GROUP_WIDTH // width, nq),
        in_specs=[pl.BlockSpec(lam_p.shape, lambda bi, g, qi: (0, 0))]
        + _attn_specs(s, t, 1, width) + [
            pl.BlockSpec((n_heads, 2, t, t), lambda bi, g, qi: (g, 0, 0, 0)),
            pl.BlockSpec((1, width), lambda bi, g, qi: (0, g)),
        ],
        out_specs=pl.BlockSpec((1, t, width), lambda bi, g, qi: (bi, qi, g)),
        out_shape=jax.ShapeDtypeStruct((b, s, GROUP_WIDTH), BF16),
        scratch_shapes=_attn_scratch(2 * n_heads, t),
        compiler_params=pltpu.CompilerParams(
            dimension_semantics=("arbitrary", "arbitrary", "arbitrary"),
            vmem_limit_bytes=VMEM_LIMIT),
        name="diff_attn",
    )(lam_p, qk, qk, vt, bias_t, gnorm)


def _head_of_lane(n_lanes, width):
    return lax.broadcasted_iota(jnp.int32, (1, n_lanes), 1) // width


def _block_rows(x, lane_head, n_heads=N_HEADS_G):
    zero = jnp.zeros_like(x)
    return jnp.concatenate([jnp.where(lane_head == h, x, zero) for h in range(n_heads)], axis=0)


def _gdn_kernel(x_ref, misc_ref, conv_ref, hp_ref, gnorm_ref, o_ref, s_ref, tail_ref, g_ref, beta_ref):
    L = CHUNK
    W = GROUP_WIDTH
    rows = range(x_ref.shape[0])
    seq = x_ref.shape[1]
    n_chunks = seq // L
    head_w = _head_of_lane(W, HEAD_DIM)
    head_all = _head_of_lane(LANES, 1)
    ri = lax.broadcasted_iota(jnp.int32, (L, W), 0)
    cj = lax.broadcasted_iota(jnp.int32, (L, W), 1) % HEAD_DIM
    lower = ri >= cj
    strict = ri > cj
    ident = jnp.where(ri == cj, 1.0, 0.0)
    ones_blk = (lax.broadcasted_iota(jnp.int32, (W, W), 0) // HEAD_DIM
                == lax.broadcasted_iota(jnp.int32, (W, W), 1) // HEAD_DIM)
    ones_seg = ones_blk.astype(BF16)
    tril = (lax.broadcasted_iota(jnp.int32, (L, L), 0)
            >= lax.broadcasted_iota(jnp.int32, (L, L), 1)).astype(BF16)

    hp = hp_ref[...]
    exp_g = (lax.broadcasted_iota(jnp.int32, (LANES, W), 0) - MISC_A
             == lax.broadcasted_iota(jnp.int32, (LANES, W), 1) // HEAD_DIM).astype(BF16)
    exp_b = (lax.broadcasted_iota(jnp.int32, (LANES, W), 0) - MISC_B
             == lax.broadcasted_iota(jnp.int32, (LANES, W), 1) // HEAD_DIM).astype(BF16)
    del head_all
    for r in rows:
        misc = misc_ref[r]
        sp_in = misc + hp[1:2]
        softplus = jnp.maximum(sp_in, 0.0) + jnp.log(1.0 + jnp.exp(-jnp.abs(sp_in)))
        g_tok = -jnp.exp(hp[0:1]) * softplus
        b_tok = _sigmoid(misc)
        g_ref[r] = _dot_exact_rhs(g_tok, exp_g, 2)
        beta_ref[r] = _dot_exact_rhs(b_tok, exp_b, 2)

    @pl.when(pl.program_id(1) == 0)
    def _():
        s_ref[...] = jnp.zeros_like(s_ref)
        tail_ref[...] = jnp.zeros_like(tail_ref)

    cw = conv_ref[...]
    blk = lambda a: _block_rows(a.astype(BF16), head_w)
    bf = lambda a: a.astype(BF16)

    def prepare(r, r0):
        x = x_ref[r, pl.ds(r0, L), :]
        qkv = x[:, 0:3 * W]
        xx = jnp.concatenate([tail_ref[r], qkv], axis=0)
        tail_ref[r] = qkv[L - 8:L, :]
        conv = cw[CONV_WIDTH - 1:CONV_WIDTH] * qkv
        for i in range(CONV_WIDTH - 1):
            lo = 8 - (CONV_WIDTH - 1) + i
            conv = conv + cw[i:i + 1] * xx[lo:lo + L, :]
        conv = _silu(conv)
        return conv[:, 0:W], conv[:, W:2 * W], conv[:, 2 * W:3 * W], x[:, 3 * W:4 * W]

    def decays(gc):
        gc_col = jnp.sum(gc * ident, axis=0, keepdims=True)
        return jnp.exp(jnp.where(lower, gc - gc_col, -jnp.inf))

    def chunk(c, _):
        r0 = pl.multiple_of(c * L, L)
        q, k, v, z = zip(*[prepare(r, r0) for r in rows])
        ssq = [_dot_exact_rhs(jnp.concatenate([q[r] * q[r], k[r] * k[r]], axis=0), ones_seg, 1)
               for r in rows]
        q = [q[r] * lax.rsqrt(ssq[r][0:L] + EPS) * (HEAD_DIM ** -0.5) for r in rows]
        k = [k[r] * lax.rsqrt(ssq[r][L:2 * L] + EPS) for r in rows]
        beta = [beta_ref[r, pl.ds(r0, L), :] for r in rows]
        gc = [_dot_exact_lhs(tril, g_ref[r, pl.ds(r0, L), :], 2) for r in rows]
        egc = [jnp.exp(gc[r]) for r in rows]
        gc_last = [gc[r][L - 1:L, :] for r in rows]
        decay = [decays(gc[r]) for r in rows]
        kb = [k[r] * beta[r] for r in rows]
        both = [_dot_nt(bf(jnp.concatenate([kb[r], q[r]], axis=0)), blk(k[r])) for r in rows]
        qk = [jnp.where(lower, both[r][L:2 * L] * decay[r], 0.0) for r in rows]

        p = [jnp.where(strict, -both[r][0:L] * decay[r], 0.0) for r in rows]
        t_inv = [ident + p[r] for r in rows]
        p = [_dot(bf(p[r]), blk(p[r])) for r in rows]
        for _ in range(4):
            prod = [_dot(bf(jnp.concatenate([t_inv[r], p[r]], axis=0)), blk(p[r])) for r in rows]
            t_inv = [t_inv[r] + prod[r][0:L] for r in rows]
            p = [prod[r][L:2 * L] for r in rows]
        t_inv = [t_inv[r] + _dot(bf(t_inv[r]), blk(p[r])) for r in rows]

        u = [_dot(bf(t_inv[r]), blk(v[r] * beta[r])) for r in rows]
        w = [_dot(bf(t_inv[r]), blk(kb[r] * egc[r])) for r in rows]

        state = [s_ref[r] for r in rows]
        ws_qs = [_dot(bf(jnp.concatenate([w[r], q[r] * egc[r]], axis=0)), bf(state[r])) for r in rows]
        v_new = [u[r] - ws_qs[r][0:L] for r in rows]
        o = [ws_qs[r][L:2 * L] + _dot(bf(qk[r]), blk(v_new[r])) for r in rows]
        upd = [_dot_tn(bf(k[r] * jnp.exp(gc_last[r] - gc[r])), bf(v_new[r])) for r in rows]
        for r in rows:
            s_ref[r] = state[r] * jnp.exp(gc_last[r]) + jnp.where(ones_blk, upd[r], 0.0)

        ms = [_dot_exact_rhs(o[r] * o[r], ones_seg, 1) * (1.0 / HEAD_DIM) for r in rows]
        for r in rows:
            y = o[r] * lax.rsqrt(ms[r] + EPS) * gnorm_ref[...] * _silu(z[r])
            o_ref[r, pl.ds(r0, L), :] = y.astype(o_ref.dtype)
        return 0

    lax.fori_loop(0, n_chunks, chunk, 0)


def _gdn(gdn, misc, conv_w, hp, gnorm):
    b, s, _ = gdn.shape
    ts = min(s, REC_SEQ_TILE)
    nr = math.gcd(b, REC_ROWS)
    const = lambda bi, si: (0, 0)
    return pl.pallas_call(
        _gdn_kernel,
        grid=(b // nr, s // ts),
        in_specs=[
            pl.BlockSpec((nr, ts, N_GDN), lambda bi, si: (bi, si, 0)),
            pl.BlockSpec((nr, ts, N_MISC), lambda bi, si: (bi, si, 0)),
            pl.BlockSpec(conv_w.shape, const),
            pl.BlockSpec(hp.shape, const),
            pl.BlockSpec((1, GROUP_WIDTH), const),
        ],
        out_specs=pl.BlockSpec((nr, ts, GROUP_WIDTH), lambda bi, si: (bi, si, 0)),
        out_shape=jax.ShapeDtypeStruct((b, s, GROUP_WIDTH), BF16),
        scratch_shapes=[
            pltpu.VMEM((nr, GROUP_WIDTH, GROUP_WIDTH), F32),
            pltpu.VMEM((nr, 8, 3 * GROUP_WIDTH), F32),
            pltpu.VMEM((nr, ts, GROUP_WIDTH), F32),
            pltpu.VMEM((nr, ts, GROUP_WIDTH), F32),
        ],
        compiler_params=pltpu.CompilerParams(
            dimension_semantics=("arbitrary", "arbitrary"), vmem_limit_bytes=VMEM_LIMIT),
        name="gdn",
    )(gdn, misc, conv_w, hp, gnorm)


def _gla_kernel(x_ref, misc_ref, walpha_ref, balpha_ref, gnorm_ref, o_ref, s_ref, la_ref):
    L = CHUNK
    W = GROUP_WIDTH
    KW = N_HEADS_G * GLA_DK
    rows = range(x_ref.shape[0])
    seq = x_ref.shape[1]
    n_chunks = seq // L
    head_k = _head_of_lane(KW, GLA_DK)
    head_v = _head_of_lane(W, HEAD_DIM)
    ri = lax.broadcasted_iota(jnp.int32, (L, W), 0)
    cj = lax.broadcasted_iota(jnp.int32, (L, W), 1) % HEAD_DIM
    lower = ri >= cj
    ones_blk = (lax.broadcasted_iota(jnp.int32, (W, W), 0) // HEAD_DIM
                == lax.broadcasted_iota(jnp.int32, (W, W), 1) // HEAD_DIM).astype(BF16)
    state_mask = (lax.broadcasted_iota(jnp.int32, (W, KW), 0) // HEAD_DIM
                  == lax.broadcasted_iota(jnp.int32, (W, KW), 1) // GLA_DK)
    tril = (lax.broadcasted_iota(jnp.int32, (L, L), 0)
            >= lax.broadcasted_iota(jnp.int32, (L, L), 1)).astype(BF16)

    for r in rows:
        pre = _dot(misc_ref[r].astype(BF16), walpha_ref[...]) + balpha_ref[...]
        log_sig = jnp.minimum(pre, 0.0) - jnp.log(1.0 + jnp.exp(-jnp.abs(pre)))
        la_ref[r] = log_sig * (1.0 / GLA_TAU)

    @pl.when(pl.program_id(1) == 0)
    def _():
        s_ref[...] = jnp.zeros_like(s_ref)

    bf = lambda a: a.astype(BF16)

    def chunk(c, _):
        r0 = pl.multiple_of(c * L, L)
        x = [x_ref[r, pl.ds(r0, L), :] for r in rows]
        k = [x[r][:, KW:2 * KW] for r in rows]
        v = [bf(x[r][:, 2 * KW:2 * KW + W]) for r in rows]
        bc = [_dot_exact_lhs(tril, la_ref[r, pl.ds(r0, L), :], 2) for r in rows]
        b_last = [bc[r][L - 1:L, :] for r in rows]
        qe = [bf(x[r][:, 0:KW] * (GLA_DK ** -0.5) * jnp.exp(bc[r])) for r in rows]
        ke = [bf(k[r] * jnp.exp(-bc[r])) for r in rows]
        a_mat = [jnp.where(lower, _dot_nt(qe[r], _block_rows(ke[r], head_k)), 0.0) for r in rows]
        state_t = [s_ref[r] for r in rows]
        o = [_dot_nt(qe[r], bf(state_t[r])) for r in rows]
        o = [o[r] + _dot(bf(a_mat[r]), _block_rows(v[r], head_v)) for r in rows]
        upd = [_dot_tn(v[r], bf(k[r] * jnp.exp(b_last[r] - bc[r]))) for r in rows]
        for r in rows:
            s_ref[r] = state_t[r] * jnp.exp(b_last[r]) + jnp.where(state_mask, upd[r], 0.0)

        ms = [_dot_exact_rhs(o[r] * o[r], ones_blk, 1) * (1.0 / HEAD_DIM) for r in rows]
        for r in rows:
            gate = _silu(x[r][:, 2 * KW + W:2 * KW + 2 * W])
            y = o[r] * lax.rsqrt(ms[r] + EPS) * gnorm_ref[...] * gate
            o_ref[r, pl.ds(r0, L), :] = y.astype(o_ref.dtype)
        return 0

    lax.fori_loop(0, n_chunks, chunk, 0)


def _gla(gla, misc, walpha, balpha, gnorm):
    b, s, _ = gla.shape
    ts = min(s, REC_SEQ_TILE)
    nr = math.gcd(b, REC_ROWS)
    const = lambda bi, si: (0, 0)
    return pl.pallas_call(
        _gla_kernel,
        grid=(b // nr, s // ts),
        in_specs=[
            pl.BlockSpec((nr, ts, N_GLA), lambda bi, si: (bi, si, 0)),
            pl.BlockSpec((nr, ts, N_MISC), lambda bi, si: (bi, si, 0)),
            pl.BlockSpec(walpha.shape, const),
            pl.BlockSpec(balpha.shape, const),
            pl.BlockSpec((1, GROUP_WIDTH), const),
        ],
        out_specs=pl.BlockSpec((nr, ts, GROUP_WIDTH), lambda bi, si: (bi, si, 0)),
        out_shape=jax.ShapeDtypeStruct((b, s, GROUP_WIDTH), BF16),
        scratch_shapes=[
            pltpu.VMEM((nr, GROUP_WIDTH, N_HEADS_G * GLA_DK), F32),
            pltpu.VMEM((nr, ts, N_HEADS_G * GLA_DK), F32),
        ],
        compiler_params=pltpu.CompilerParams(
            dimension_semantics=("arbitrary", "arbitrary"), vmem_limit_bytes=VMEM_LIMIT),
        name="gla",
    )(gla, misc, walpha, balpha, gnorm)


def _ffn_chunks(d_ff):
    step = 4 * GROUP_WIDTH
    return [(lo, min(step, d_ff - lo)) for lo in range(0, d_ff, step)]


def _mlp_kernel(h_ref, ya_ref, yb_ref, yc_ref, yd_ref, wout_ref, gffn_ref, wg_ref, wu_ref, wd_ref,
                gple_ref, wpg_ref, p_ref, wpp_ref, gfin_ref, o_ref, act_ref, *, final):
    mixed = jnp.concatenate([ya_ref[...], yb_ref[...], yc_ref[...], yd_ref[...]], axis=-1)
    h = h_ref[...] + _dot(mixed, wout_ref[...])
    hn = _rms(h, gffn_ref[...]).astype(BF16)
    for lo, width in _ffn_chunks(wg_ref.shape[1]):
        gate = _dot(hn, wg_ref[:, lo:lo + width])
        up = _dot(hn, wu_ref[:, lo:lo + width])
        act_ref[:, lo:lo + width] = (_silu(gate) * up).astype(BF16)
    h = h + _dot(act_ref[...], wd_ref[...])
    gate = _sigmoid(_dot(_rms(h, gple_ref[...]).astype(BF16), wpg_ref[...]))
    h = h + gate * _dot(p_ref[...].astype(BF16), wpp_ref[...])
    if final:
        h = _rms(h, gfin_ref[...])
    o_ref[...] = h


def _mlp(h2d, ys, wout, gffn, wg, wu, wd, gple, wpg, p2d, wpp, gfin, tm, final):
    m, d = h2d.shape
    d_ff = wg.shape[1]
    const = lambda i: (0, 0)
    resident = lambda shape: pl.BlockSpec(shape, const, pipeline_mode=pl.Buffered(1))
    rows = lambda width: pl.BlockSpec((tm, width), lambda i: (i, 0))
    return pl.pallas_call(
        functools.partial(_mlp_kernel, final=final),
        grid=(m // tm,),
        in_specs=[rows(d)] + [rows(GROUP_WIDTH)] * 4 + [
            resident(wout.shape), resident(gffn.shape), resident(wg.shape), resident(wu.shape),
            resident(wd.shape), resident(gple.shape), resident(wpg.shape),
            rows(p2d.shape[1]), resident(wpp.shape), resident(gfin.shape),
        ],
        out_specs=rows(d),
        out_shape=jax.ShapeDtypeStruct((m, d), F32),
        scratch_shapes=[pltpu.VMEM((tm, d_ff), BF16)],
        compiler_params=pltpu.CompilerParams(
            dimension_semantics=("arbitrary",), vmem_limit_bytes=VMEM_LIMIT),
        name="mlp",
    )(h2d, *ys, wout, gffn, wg, wu, wd, gple, wpg, p2d, wpp, gfin)


def _rel_bucket_table(n_dist):
    n = np.arange(n_dist)
    nf = np.maximum(n, 1).astype(np.float32)
    large = REL_MAX_EXACT + (np.log(nf / REL_MAX_EXACT) / math.log(REL_MAX_DIST / REL_MAX_EXACT)
                             * (N_REL_BUCKETS - REL_MAX_EXACT)).astype(np.int32)
    large = np.minimum(large, N_REL_BUCKETS - 1)
    return np.where(n < REL_MAX_EXACT, n, large)


def _bias_tiles(rel_bias):
    t = ATTN_TILE
    bucket = _rel_bucket_table(2 * t)
    assert (bucket[t + 1:] == N_REL_BUCKETS - 1).all()
    per_dist = (rel_bias[bucket, :] - rel_bias[N_REL_BUCKETS - 1][None, :]).T * LOG2E
    own = jnp.concatenate([per_dist[:, :t], jnp.full_like(per_dist[:, :t], NEG)], axis=1)
    prev = jnp.concatenate([per_dist[:, t:], per_dist[:, :t]], axis=1)
    vec = jnp.stack([own, prev], axis=1)
    rolled = jnp.tile(vec, (1, 1, t))[:, :, :t * (2 * t - 1)].reshape(-1, 2, t, 2 * t - 1)
    return rolled[:, :, :, :t]


def _row(v, width=None):
    v = v.astype(F32).reshape(1, -1)
    if width is not None and v.shape[1] < width:
        v = jnp.pad(v, ((0, 0), (0, width - v.shape[1])))
    return v


def kernel(x, p, norm_mix, w_in, rel_bias, diff_lambda, diff_norm, gdn_conv, gdn_a_log, gdn_dt_bias,
           gdn_norm, gla_w_alpha, gla_b_alpha, gla_norm, w_out, norm_ffn, w_gate, w_up, w_down,
           norm_ple, w_ple_gate, w_ple_proj, final_norm):
    b, s, d = x.shape
    depth = w_in.shape[0]
    m = b * s
    tm = 512
    hg = N_HEADS_G
    assert s % ATTN_TILE == 0 and s % tm == 0 and tm % MOBA_BLOCK == 0

    bias_t = _bias_tiles(rel_bias.astype(F32))
    bias_a, bias_b = bias_t[:hg], bias_t[hg:]
    gw = GROUP_WIDTH
    col_scale = np.ones((1, N_QK), np.float32)
    col_scale[:, 0:gw] = HEAD_DIM ** -0.5
    col_scale[:, gw:2 * gw] = LOG2E
    col_scale[:, 2 * gw:3 * gw] = DIFF_QK_DIM ** -0.5 * LOG2E
    col_scale = jnp.asarray(col_scale)

    n_main = 6 * gw + N_GDN
    ab = 2 * hg
    gla_lo = n_main + ab
    gla_hi = gla_lo + N_GLA
    fin = _row(final_norm)
    h = x.reshape(m, d)
    for l in range(depth):
        wl = w_in[l]
        w1 = jnp.concatenate(
            [wl[:, 0:2 * gw], wl[:, 3 * gw:5 * gw], wl[:, 6 * gw:n_main], wl[:, gla_lo:gla_hi],
             wl[:, n_main:gla_lo], wl[:, gla_hi:],
             jnp.zeros((d, N_MISC - ab - GLA_GATE_RANK), wl.dtype)], axis=1).astype(BF16)
        wvt = jnp.concatenate([wl[:, 2 * gw:3 * gw], wl[:, 5 * gw:6 * gw]], axis=1).T.astype(BF16)
        qk, vt, gdn, gla, misc, kmean = _inproj(h, _row(norm_mix[l]), w1, wvt, col_scale, tm, s)
        qk = qk.reshape(b, s, N_QK)
        misc = misc.reshape(b, s, N_MISC)
        kmean = kmean.reshape(b, s // MOBA_BLOCK, GROUP_WIDTH)

        y_a = _moba(qk, vt, kmean, bias_a)

        lam_init = 0.8 - 0.6 * math.exp(-0.3 * l)
        lam_p = jnp.concatenate([diff_lambda[l].astype(F32),
                                 jnp.full((1, DIFF_QK_DIM), lam_init, F32)], axis=0)
        y_b = _diff(qk, vt, lam_p, bias_b, _row(jnp.tile(diff_norm[l], hg)))

        hp = jnp.concatenate([_row(gdn_a_log[l], LANES), _row(gdn_dt_bias[l], LANES)], axis=0)
        y_c = _gdn(gdn.reshape(b, s, N_GDN), misc, gdn_conv[l].astype(F32), hp,
                   _row(jnp.tile(gdn_norm[l], hg)))

        walpha = jnp.zeros((N_MISC, hg * GLA_DK), F32).at[MISC_LR:MISC_LR + GLA_GATE_RANK].set(
            gla_w_alpha[l]).astype(BF16)
        y_d = _gla(gla.reshape(b, s, N_GLA), misc, walpha, _row(gla_b_alpha[l]),
                   _row(jnp.tile(gla_norm[l], hg)))

        ys = [y.reshape(m, GROUP_WIDTH) for y in (y_a, y_b, y_c, y_d)]
        h = _mlp(h, ys, w_out[l].astype(BF16), _row(norm_ffn[l]), w_gate[l].astype(BF16),
                 w_up[l].astype(BF16), w_down[l].astype(BF16), _row(norm_ple[l]),
                 w_ple_gate[l].astype(BF16), p[l].reshape(m, -1), w_ple_proj[l].astype(BF16),
                 fin, tm, final=(l == depth - 1))
    return h.reshape(b, s, d)
```

```python
import functools
import math

import numpy as np
import jax
import jax.numpy as jnp
from jax import lax
from jax.experimental import pallas as pl
from jax.experimental.pallas import tpu as pltpu

F32 = jnp.float32
BF16 = jnp.bfloat16

HEAD_DIM = 64
N_HEADS_G = 4
GROUP_WIDTH = HEAD_DIM * N_HEADS_G
MOBA_BLOCK = 256
MOBA_TOPK = 3
N_REL_BUCKETS = 32
REL_MAX_EXACT = 16
REL_MAX_DIST = 128
DIFF_QK_DIM = HEAD_DIM // 2
CONV_WIDTH = 4
CHUNK = 64
GLA_DK = HEAD_DIM // 2
GLA_GATE_RANK = 16
GLA_TAU = 16.0
EPS = 1e-6

LANES = 128
ATTN_TILE = 256
REC_SEQ_TILE = 256
REC_ROWS = 8
NEG = -1e30
LOG2E = math.log2(math.e)
VMEM_LIMIT = 56 * 1024 * 1024

N_QK = 4 * GROUP_WIDTH
N_VT = 2 * GROUP_WIDTH
N_GDN = 4 * GROUP_WIDTH
N_GLA = 2 * N_HEADS_G * GLA_DK + 2 * GROUP_WIDTH
N_MISC = LANES
MISC_A, MISC_B, MISC_LR = 0, N_HEADS_G, 2 * N_HEADS_G

_NT = (((1,), (1,)), ((), ()))
_TN = (((0,), (0,)), ((), ()))


def _dot(a, b):
    return jnp.dot(a, b, preferred_element_type=F32)


def _dot_nt(a, b):
    return lax.dot_general(a, b, _NT, preferred_element_type=F32)


def _dot_tn(a, b):
    return lax.dot_general(a, b, _TN, preferred_element_type=F32)


def _split_bf16(x, n):
    parts = []
    r = x
    for _ in range(n):
        hi = r.astype(BF16)
        parts.append(hi)
        r = r - hi.astype(F32)
    return parts


def _dot_exact_rhs(x, w, n):
    acc = None
    for part in _split_bf16(x, n):
        t = _dot(part, w)
        acc = t if acc is None else acc + t
    return acc


def _dot_exact_lhs(w, x, n):
    acc = None
    for part in _split_bf16(x, n):
        t = _dot(w, part)
        acc = t if acc is None else acc + t
    return acc


def _rms(x, g):
    return x * lax.rsqrt(jnp.mean(x * x, axis=-1, keepdims=True) + EPS) * g


def _sigmoid(x):
    return 1.0 / (1.0 + jnp.exp(-x))


def _silu(x):
    return x * _sigmoid(x)


def _inproj_kernel(x_ref, g_ref, w_ref, wvt_ref, scale_ref,
                   qk_ref, vt_ref, gdn_ref, gla_ref, misc_ref, kmean_ref):
    tm = x_ref.shape[0]
    xn = _rms(x_ref[...], g_ref[...]).astype(BF16)
    qk = _dot(xn, w_ref[:, 0:N_QK])
    k_moba = qk[:, GROUP_WIDTH:2 * GROUP_WIDTH]
    kmean_ref[0] = jnp.mean(k_moba.reshape(tm // MOBA_BLOCK, MOBA_BLOCK, GROUP_WIDTH), axis=1)
    qk_ref[...] = (qk * scale_ref[...]).astype(BF16)
    vt = _dot_nt(wvt_ref[...], xn).astype(BF16)
    for i in range(tm // ATTN_TILE):
        vt_ref[0, i] = vt[:, i * ATTN_TILE:(i + 1) * ATTN_TILE]
    o = N_QK
    gdn_ref[...] = _dot(xn, w_ref[:, o:o + N_GDN])
    o += N_GDN
    gla_ref[...] = _dot(xn, w_ref[:, o:o + N_GLA])
    o += N_GLA
    misc_ref[...] = _dot(xn, w_ref[:, o:o + N_MISC])


def _inproj(h2d, g, w, wvt, scale, tm, seq):
    m, d = h2d.shape
    n_all = w.shape[1]
    per_seq = seq // tm
    const = lambda i: (0, 0)
    return pl.pallas_call(
        _inproj_kernel,
        grid=(m // tm,),
        in_specs=[
            pl.BlockSpec((tm, d), lambda i: (i, 0)),
            pl.BlockSpec((1, d), const),
            pl.BlockSpec((d, n_all), const),
            pl.BlockSpec((N_VT, d), const),
            pl.BlockSpec((1, N_QK), const),
        ],
        out_specs=[
            pl.BlockSpec((tm, N_QK), lambda i: (i, 0)),
            pl.BlockSpec((1, tm // ATTN_TILE, N_VT, ATTN_TILE),
                         lambda i: (i // per_seq, i % per_seq, 0, 0)),
            pl.BlockSpec((tm, N_GDN), lambda i: (i, 0)),
            pl.BlockSpec((tm, N_GLA), lambda i: (i, 0)),
            pl.BlockSpec((tm, N_MISC), lambda i: (i, 0)),
            pl.BlockSpec((1, tm // MOBA_BLOCK, GROUP_WIDTH), lambda i: (i, 0, 0)),
        ],
        out_shape=[
            jax.ShapeDtypeStruct((m, N_QK), BF16),
            jax.ShapeDtypeStruct((m // seq, seq // ATTN_TILE, N_VT, ATTN_TILE), BF16),
            jax.ShapeDtypeStruct((m, N_GDN), F32),
            jax.ShapeDtypeStruct((m, N_GLA), F32),
            jax.ShapeDtypeStruct((m, N_MISC), F32),
            jax.ShapeDtypeStruct((m // tm, tm // MOBA_BLOCK, GROUP_WIDTH), F32),
        ],
        compiler_params=pltpu.CompilerParams(
            dimension_semantics=("arbitrary",), vmem_limit_bytes=VMEM_LIMIT),
        name="inproj",
    )(h2d, g, w, wvt, scale)


def _lane_mask(lo, width):
    lane = lax.broadcasted_iota(jnp.int32, (1, LANES), 1)
    return (lane >= lo) & (lane < lo + width)


ACC_ROWS = HEAD_DIM + 16
PIPE_CHAIN_TILES = 8
ATTN_HEADS_PER_STEP = 4


def _values_with_ones(vt, hh):
    head = vt[HEAD_DIM * hh:HEAD_DIM * (hh + 1), :]
    return jnp.concatenate([head, jnp.ones((ACC_ROWS - HEAD_DIM, vt.shape[1]), vt.dtype)], axis=0)


def _tile_max(scores):
    return [jnp.max(s, axis=0, keepdims=True) for s in scores]


def _softmax_stage(ms, scores, tile_maxes, valids=None):
    ms_new, alphas, ps = [], [], []
    for c, (m, s, m_tile) in enumerate(zip(ms, scores, tile_maxes)):
        if valids is None or valids[c] is None:
            m_new = jnp.maximum(m, m_tile)
            shift = m_new
        else:
            ok = valids[c] > 0.5
            m_new = jnp.maximum(m, jnp.where(ok, m_tile, NEG))
            shift = jnp.where(ok, m_new, -NEG)
        ms_new.append(m_new)
        alphas.append(jnp.exp2(m - m_new))
        ps.append(jnp.exp2(s - shift).astype(BF16))
    return ms_new, alphas, ps


def _value_products(vts, ps):
    return [_dot(vt, p) for vt, p in zip(vts, ps)]


def _attend_all(n, t, n_far, far_scores, far_valid, prev_scores, prev_valid, own_scores, values,
                j_prev, j_own, s_ref, p_ref, acc_ref, first_scores=None):
    chains = range(n)
    n_slots = jnp.maximum(n_far, 2)

    def slot_valid(j):
        exists = jnp.where(j < n_far, 1.0, 0.0)
        rows = far_valid(j)
        if rows is None:
            return [jnp.full((1, t), exists, F32)] * n
        return [row * exists for row in rows]

    s0, s1 = first_scores if first_scores is not None else (far_scores(0), far_scores(1))
    ms, alphas, ps = _softmax_stage([jnp.full((1, t), NEG, F32)] * n, s0, _tile_max(s0),
                                    slot_valid(0))
    tmax = _tile_max(s1)
    for c in chains:
        p_ref[c] = ps[c]
        s_ref[c] = s1[c]
        acc_ref[c] = jnp.zeros((ACC_ROWS, t), F32)

    def steps(first, unroll, carry):
        ms, alphas, tmax = carry
        ps = [p_ref[c] for c in chains]
        s_cur = [s_ref[c] for c in chains]
        for u in range(unroll):
            pv = _value_products(values(first + u), ps)
            s_new = far_scores(first + u + 2)
            tmax_new = _tile_max(s_new)
            ms, alphas_next, ps = _softmax_stage(ms, s_cur, tmax, far_valid(first + u + 1))
            for c in chains:
                acc_ref[c] = alphas[c] * acc_ref[c] + pv[c]
            alphas, s_cur, tmax = alphas_next, s_new, tmax_new
        for c in chains:
            p_ref[c] = ps[c]
            s_ref[c] = s_cur[c]
        return ms, alphas, tmax

    n_steps = n_slots - 2
    unroll = max(1, PIPE_CHAIN_TILES // n)
    n_blocks = n_steps // unroll
    carry = lax.fori_loop(0, n_blocks, lambda i, cr: steps(i * unroll, unroll, cr),
                          (ms, alphas, tmax))
    if unroll > 1:
        carry = lax.fori_loop(n_blocks * unroll, n_steps, lambda i, cr: steps(i, 1, cr), carry)
    ms, alphas, tmax = carry

    pv = _value_products(values(n_slots - 2), [p_ref[c] for c in chains])
    s_prev = prev_scores()
    tmax_prev = _tile_max(s_prev)
    ms, alphas1, ps1 = _softmax_stage(ms, [s_ref[c] for c in chains], tmax,
                                      slot_valid(n_slots - 1))
    accs = [alphas[c] * acc_ref[c] + pv[c] for c in chains]
    pv = _value_products(values(n_slots - 1), ps1)
    s_own = own_scores()
    tmax_own = _tile_max(s_own)
    ms, alphas2, ps2 = _softmax_stage(ms, s_prev, tmax_prev, prev_valid())
    accs = [alphas1[c] * accs[c] + pv[c] for c in chains]
    pv = _value_products(values(j_prev), ps2)
    ms, alphas3, ps3 = _softmax_stage(ms, s_own, tmax_own)
    accs = [alphas2[c] * accs[c] + pv[c] for c in chains]
    pv = _value_products(values(j_own), ps3)
    return [alphas3[c] * accs[c] + pv[c] for c in chains]


def _normalised(acc):
    return acc[0:HEAD_DIM, :] / acc[HEAD_DIM:HEAD_DIM + 1, :]


def _moba_kernel(q_ref, k_ref, vt_ref, kmean_ref, bias_ref, o_ref, sel_ref, s_ref, p_ref, acc_ref):
    t = ATTN_TILE
    qi = pl.program_id(2)
    nblk = kmean_ref.shape[1]
    heads = range(q_ref.shape[2] // HEAD_DIM)
    q = q_ref[0]
    kmean = kmean_ref[0].astype(BF16)

    def pair_lanes(x, h):
        lo = LANES * (h // 2)
        return x[:, lo:lo + LANES]

    qms = []
    for h in heads:
        qp = pair_lanes(q, h)
        qms.append(jnp.where(_lane_mask(HEAD_DIM * (h % 2), HEAD_DIM), qp, jnp.zeros_like(qp)))

    def keys(j):
        return k_ref[0, pl.ds(pl.multiple_of(j * t, t), t), :]

    def far_scores(j):
        k = keys(j)
        return [_dot_nt(pair_lanes(k, h), qms[h]) for h in heads]

    gates = [_dot_nt(pair_lanes(kmean, h), qms[h]) for h in heads]
    first_scores = far_scores(0), far_scores(1)

    row = lax.broadcasted_iota(jnp.int32, (nblk, t), 0)
    for h in heads:
        gate = jnp.where(row < qi, gates[h], -jnp.inf)
        sel_t = jnp.zeros((nblk, t), F32)
        for j in range(nblk):
            gj = gate[j:j + 1, :]
            beats = jnp.where(gate > gj, 1.0, jnp.where((gate == gj) & (row < j), 1.0, 0.0))
            rank = jnp.sum(beats, axis=0, keepdims=True)
            sel_j = jnp.where(rank < MOBA_TOPK, jnp.where(j < qi, 1.0, 0.0), 0.0)
            sel_t = jnp.where(row == j, sel_j, sel_t)
        for j in range(nblk):
            sel_ref[h * nblk + j] = sel_t[j:j + 1, :]

    def values(j):
        vt = vt_ref[0, j]
        return [_values_with_ones(vt, h) for h in heads]

    def far_valid(j):
        return [sel_ref[h * nblk + j] for h in heads]

    jp = jnp.maximum(qi - 1, 0)

    def prev_scores():
        k = keys(jp)
        return [_dot_nt(pair_lanes(k, h), qms[h]) + bias_ref[h, 1] for h in heads]

    def own_scores():
        k = keys(qi)
        return [_dot_nt(pair_lanes(k, h), qms[h]) + bias_ref[h, 0] for h in heads]

    accs = _attend_all(len(heads), t, jnp.maximum(qi - 1, 0), far_scores, far_valid, prev_scores,
                       lambda: far_valid(jp), own_scores, values, jp, qi, s_ref, p_ref, acc_ref,
                       first_scores)
    out_t = jnp.concatenate([_normalised(acc) for acc in accs], axis=0)
    o_ref[0] = out_t.T.astype(o_ref.dtype)


def _attn_specs(s, t, mixer, width):
    nkv = s // t
    per = GROUP_WIDTH // width
    qcol = mixer * 2 * per
    kcol = qcol + per
    vrow = mixer * per
    return [
        pl.BlockSpec((1, t, width), lambda bi, g, qi: (bi, qi, qcol + g)),
        pl.BlockSpec((1, s, width), lambda bi, g, qi: (bi, 0, kcol + g)),
        pl.BlockSpec((1, nkv, width, t), lambda bi, g, qi: (bi, 0, vrow + g, 0)),
    ]


def _attn_scratch(n_chains, t):
    return [pltpu.VMEM((n_chains, t, t), F32),
            pltpu.VMEM((n_chains, t, t), BF16),
            pltpu.VMEM((n_chains, ACC_ROWS, t), F32)]


def _moba(qk, vt, kmean, bias_t):
    b, s, _ = qk.shape
    t = ATTN_TILE
    nq = s // t
    nblk = kmean.shape[1]
    n_heads = ATTN_HEADS_PER_STEP
    width = n_heads * HEAD_DIM
    return pl.pallas_call(
        _moba_kernel,
        grid=(b, GROUP_WIDTH // width, nq),
        in_specs=_attn_specs(s, t, 0, width) + [
            pl.BlockSpec((1, nblk, width), lambda bi, g, qi: (bi, 0, g)),
            pl.BlockSpec((n_heads, 2, t, t), lambda bi, g, qi: (g, 0, 0, 0)),
        ],
        out_specs=pl.BlockSpec((1, t, width), lambda bi, g, qi: (bi, qi, g)),
        out_shape=jax.ShapeDtypeStruct((b, s, GROUP_WIDTH), BF16),
        scratch_shapes=[pltpu.VMEM((n_heads * nblk, 1, t), F32)]
        + _attn_scratch(n_heads, t),
        compiler_params=pltpu.CompilerParams(
            dimension_semantics=("arbitrary", "arbitrary", "arbitrary"),
            vmem_limit_bytes=VMEM_LIMIT),
        name="moba",
    )(qk, qk, vt, kmean, bias_t)


def _diff_kernel(lam_ref, q_ref, k_ref, vt_ref, bias_ref, gnorm_ref, o_ref, s_ref, p_ref, acc_ref):
    t = ATTN_TILE
    qi = pl.program_id(2)
    heads = range(q_ref.shape[2] // HEAD_DIM)
    chains = [(h, mm) for h in heads for mm in range(2)]
    q = q_ref[0]

    def pair_lanes(x, h):
        lo = LANES * (h // 2)
        return x[:, lo:lo + LANES]

    qms = []
    for h, mm in chains:
        qp = pair_lanes(q, h)
        mask = _lane_mask(HEAD_DIM * (h % 2) + DIFF_QK_DIM * mm, DIFF_QK_DIM)
        qms.append(jnp.where(mask, qp, jnp.zeros_like(qp)))

    def keys(j):
        return k_ref[0, pl.ds(pl.multiple_of(j * t, t), t), :]

    def values(j):
        vt = vt_ref[0, j]
        vts = [_values_with_ones(vt, h) for h in heads]
        return [vts[h] for h, _ in chains]

    def far_scores(j):
        k = keys(j)
        return [_dot_nt(pair_lanes(k, h), qm) for (h, _), qm in zip(chains, qms)]

    jp = jnp.maximum(qi - 1, 0)

    def prev_scores():
        return [s + bias_ref[h, 1] for (h, _), s in zip(chains, far_scores(jp))]

    def prev_valid():
        return [jnp.full((1, t), jnp.where(qi >= 1, 1.0, 0.0), F32)] * len(chains)

    def own_scores():
        return [s + bias_ref[h, 0] for (h, _), s in zip(chains, far_scores(qi))]

    lam_p = lam_ref[...]
    lam_init = lam_p[4:5, 0:1]
    lam = (jnp.exp(jnp.sum(lam_p[0:1] * lam_p[1:2], axis=-1, keepdims=True))
           - jnp.exp(jnp.sum(lam_p[2:3] * lam_p[3:4], axis=-1, keepdims=True)) + lam_init)

    accs = _attend_all(len(chains), t, jnp.maximum(qi - 1, 0), far_scores, lambda j: None,
                       prev_scores, prev_valid, own_scores, values, jp, qi, s_ref, p_ref, acc_ref)
    outs = []
    for h in heads:
        o = _normalised(accs[2 * h]) - lam * _normalised(accs[2 * h + 1])
        ms = jnp.mean(o * o, axis=0, keepdims=True)
        outs.append(o * lax.rsqrt(ms + EPS))
    y_t = jnp.concatenate(outs, axis=0)
    o_ref[0] = (y_t.T * gnorm_ref[...] * (1.0 - lam_init)).astype(o_ref.dtype)


def _diff(qk, vt, lam_p, bias_t, gnorm):
    b, s, _ = qk.shape
    t = ATTN_TILE
    nq = s // t
    n_heads = ATTN_HEADS_PER_STEP
    width = n_heads * HEAD_DIM
    return pl.pallas_call(
        _diff_kernel,
        grid=(b, GROUP_WIDTH // width, nq),
        in_specs=[pl.BlockSpec(lam_p.shape, lambda bi, g, qi: (0, 0))]
        + _attn_specs(s, t, 1, width) + [
            pl.BlockSpec((n_heads, 2, t, t), lambda bi, g, qi: (g, 0, 0, 0)),
            pl.BlockSpec((1, width), lambda bi, g, qi: (0, g)),
        ],
        out_specs=pl.BlockSpec((1, t, width), lambda bi, g, qi: (bi, qi, g)),
        out_shape=jax.ShapeDtypeStruct((b, s, GROUP_WIDTH), BF16),
        scratch_shapes=_attn_scratch(2 * n_heads, t),
        compiler_params=pltpu.CompilerParams(
            dimension_semantics=("arbitrary", "arbitrary", "arbitrary"),
            vmem_limit_bytes=VMEM_LIMIT),
        name="diff_attn",
    )(lam_p, qk, qk, vt, bias_t, gnorm)


def _head_of_lane(n_lanes, width):
    return lax.broadcasted_iota(jnp.int32, (1, n_lanes), 1) // width


def _block_rows(x, lane_head, n_heads=N_HEADS_G):
    zero = jnp.zeros_like(x)
    return jnp.concatenate([jnp.where(lane_head == h, x, zero) for h in range(n_heads)], axis=0)


def _gdn_kernel(x_ref, misc_ref, conv_ref, hp_ref, gnorm_ref, o_ref, s_ref, tail_ref, g_ref, beta_ref):
    L = CHUNK
    W = GROUP_WIDTH
    rows = range(x_ref.shape[0])
    seq = x_ref.shape[1]
    n_chunks = seq // L
    head_w = _head_of_lane(W, HEAD_DIM)
    head_all = _head_of_lane(LANES, 1)
    ri = lax.broadcasted_iota(jnp.int32, (L, W), 0)
    cj = lax.broadcasted_iota(jnp.int32, (L, W), 1) % HEAD_DIM
    lower = ri >= cj
    strict = ri > cj
    ident = jnp.where(ri == cj, 1.0, 0.0)
    ones_blk = (lax.broadcasted_iota(jnp.int32, (W, W), 0) // HEAD_DIM
                == lax.broadcasted_iota(jnp.int32, (W, W), 1) // HEAD_DIM)
    ones_seg = ones_blk.astype(BF16)
    tril = (lax.broadcasted_iota(jnp.int32, (L, L), 0)
            >= lax.broadcasted_iota(jnp.int32, (L, L), 1)).astype(BF16)

    hp = hp_ref[...]
    exp_g = (lax.broadcasted_iota(jnp.int32, (LANES, W), 0) - MISC_A
             == lax.broadcasted_iota(jnp.int32, (LANES, W), 1) // HEAD_DIM).astype(BF16)
    exp_b = (lax.broadcasted_iota(jnp.int32, (LANES, W), 0) - MISC_B
             == lax.broadcasted_iota(jnp.int32, (LANES, W), 1) // HEAD_DIM).astype(BF16)
    del head_all
    for r in rows:
        misc = misc_ref[r]
        sp_in = misc + hp[1:2]
        softplus = jnp.maximum(sp_in, 0.0) + jnp.log(1.0 + jnp.exp(-jnp.abs(sp_in)))
        g_tok = -jnp.exp(hp[0:1]) * softplus
        b_tok = _sigmoid(misc)
        g_ref[r] = _dot_exact_rhs(g_tok, exp_g, 2)
        beta_ref[r] = _dot_exact_rhs(b_tok, exp_b, 2)

    @pl.when(pl.program_id(1) == 0)
    def _():
        s_ref[...] = jnp.zeros_like(s_ref)
        tail_ref[...] = jnp.zeros_like(tail_ref)

    cw = conv_ref[...]
    blk = lambda a: _block_rows(a.astype(BF16), head_w)
    bf = lambda a: a.astype(BF16)

    def prepare(r, r0):
        x = x_ref[r, pl.ds(r0, L), :]
        qkv = x[:, 0:3 * W]
        xx = jnp.concatenate([tail_ref[r], qkv], axis=0)
        tail_ref[r] = qkv[L - 8:L, :]
        conv = cw[CONV_WIDTH - 1:CONV_WIDTH] * qkv
        for i in range(CONV_WIDTH - 1):
            lo = 8 - (CONV_WIDTH - 1) + i
            conv = conv + cw[i:i + 1] * xx[lo:lo + L, :]
        conv = _silu(conv)
        return conv[:, 0:W], conv[:, W:2 * W], conv[:, 2 * W:3 * W], x[:, 3 * W:4 * W]

    def decays(gc):
        gc_col = jnp.sum(gc * ident, axis=0, keepdims=True)
        return jnp.exp(jnp.where(lower, gc - gc_col, -jnp.inf))

    def chunk(c, _):
        r0 = pl.multiple_of(c * L, L)
        q, k, v, z = zip(*[prepare(r, r0) for r in rows])
        ssq = [_dot_exact_rhs(jnp.concatenate([q[r] * q[r], k[r] * k[r]], axis=0), ones_seg, 1)
               for r in rows]
        q = [q[r] * lax.rsqrt(ssq[r][0:L] + EPS) * (HEAD_DIM ** -0.5) for r in rows]
        k = [k[r] * lax.rsqrt(ssq[r][L:2 * L] + EPS) for r in rows]
        beta = [beta_ref[r, pl.ds(r0, L), :] for r in rows]
        gc = [_dot_exact_lhs(tril, g_ref[r, pl.ds(r0, L), :], 2) for r in rows]
        egc = [jnp.exp(gc[r]) for r in rows]
        gc_last = [gc[r][L - 1:L, :] for r in rows]
        decay = [decays(gc[r]) for r in rows]
        kb = [k[r] * beta[r] for r in rows]
        both = [_dot_nt(bf(jnp.concatenate([kb[r], q[r]], axis=0)), blk(k[r])) for r in rows]
        qk = [jnp.where(lower, both[r][L:2 * L] * decay[r], 0.0) for r in rows]

        p = [jnp.where(strict, -both[r][0:L] * decay[r], 0.0) for r in rows]
        t_inv = [ident + p[r] for r in rows]
        p = [_dot(bf(p[r]), blk(p[r])) for r in rows]
        for _ in range(4):
            prod = [_dot(bf(jnp.concatenate([t_inv[r], p[r]], axis=0)), blk(p[r])) for r in rows]
            t_inv = [t_inv[r] + prod[r][0:L] for r in rows]
            p = [prod[r][L:2 * L] for r in rows]
        t_inv = [t_inv[r] + _dot(bf(t_inv[r]), blk(p[r])) for r in rows]

        u = [_dot(bf(t_inv[r]), blk(v[r] * beta[r])) for r in rows]
        w = [_dot(bf(t_inv[r]), blk(kb[r] * egc[r])) for r in rows]

        state = [s_ref[r] for r in rows]
        ws_qs = [_dot(bf(jnp.concatenate([w[r], q[r] * egc[r]], axis=0)), bf(state[r])) for r in rows]
        v_new = [u[r] - ws_qs[r][0:L] for r in rows]
        o = [ws_qs[r][L:2 * L] + _dot(bf(qk[r]), blk(v_new[r])) for r in rows]
        upd = [_dot_tn(bf(k[r] * jnp.exp(gc_last[r] - gc[r])), bf(v_new[r])) for r in rows]
        for r in rows:
            s_ref[r] = state[r] * jnp.exp(gc_last[r]) + jnp.where(ones_blk, upd[r], 0.0)

        ms = [_dot_exact_rhs(o[r] * o[r], ones_seg, 1) * (1.0 / HEAD_DIM) for r in rows]
        for r in rows:
            y = o[r] * lax.rsqrt(ms[r] + EPS) * gnorm_ref[...] * _silu(z[r])
            o_ref[r, pl.ds(r0, L), :] = y.astype(o_ref.dtype)
        return 0

    lax.fori_loop(0, n_chunks, chunk, 0)


def _gdn(gdn, misc, conv_w, hp, gnorm):
    b, s, _ = gdn.shape
    ts = min(s, REC_SEQ_TILE)
    nr = math.gcd(b, REC_ROWS)
    const = lambda bi, si: (0, 0)
    return pl.pallas_call(
        _gdn_kernel,
        grid=(b // nr, s // ts),
        in_specs=[
            pl.BlockSpec((nr, ts, N_GDN), lambda bi, si: (bi, si, 0)),
            pl.BlockSpec((nr, ts, N_MISC), lambda bi, si: (bi, si, 0)),
            pl.BlockSpec(conv_w.shape, const),
            pl.BlockSpec(hp.shape, const),
            pl.BlockSpec((1, GROUP_WIDTH), const),
        ],
        out_specs=pl.BlockSpec((nr, ts, GROUP_WIDTH), lambda bi, si: (bi, si, 0)),
        out_shape=jax.ShapeDtypeStruct((b, s, GROUP_WIDTH), BF16),
        scratch_shapes=[
            pltpu.VMEM((nr, GROUP_WIDTH, GROUP_WIDTH), F32),
            pltpu.VMEM((nr, 8, 3 * GROUP_WIDTH), F32),
            pltpu.VMEM((nr, ts, GROUP_WIDTH), F32),
            pltpu.VMEM((nr, ts, GROUP_WIDTH), F32),
        ],
        compiler_params=pltpu.CompilerParams(
            dimension_semantics=("arbitrary", "arbitrary"), vmem_limit_bytes=VMEM_LIMIT),
        name="gdn",
    )(gdn, misc, conv_w, hp, gnorm)


def _gla_kernel(x_ref, misc_ref, walpha_ref, balpha_ref, gnorm_ref, o_ref, s_ref, la_ref):
    L = CHUNK
    W = GROUP_WIDTH
    KW = N_HEADS_G * GLA_DK
    rows = range(x_ref.shape[0])
    seq = x_ref.shape[1]
    n_chunks = seq // L
    head_k = _head_of_lane(KW, GLA_DK)
    head_v = _head_of_lane(W, HEAD_DIM)
    ri = lax.broadcasted_iota(jnp.int32, (L, W), 0)
    cj = lax.broadcasted_iota(jnp.int32, (L, W), 1) % HEAD_DIM
    lower = ri >= cj
    ones_blk = (lax.broadcasted_iota(jnp.int32, (W, W), 0) // HEAD_DIM
                == lax.broadcasted_iota(jnp.int32, (W, W), 1) // HEAD_DIM).astype(BF16)
    state_mask = (lax.broadcasted_iota(jnp.int32, (W, KW), 0) // HEAD_DIM
                  == lax.broadcasted_iota(jnp.int32, (W, KW), 1) // GLA_DK)
    tril = (lax.broadcasted_iota(jnp.int32, (L, L), 0)
            >= lax.broadcasted_iota(jnp.int32, (L, L), 1)).astype(BF16)

    for r in rows:
        pre = _dot(misc_ref[r].astype(BF16), walpha_ref[...]) + balpha_ref[...]
        log_sig = jnp.minimum(pre, 0.0) - jnp.log(1.0 + jnp.exp(-jnp.abs(pre)))
        la_ref[r] = log_sig * (1.0 / GLA_TAU)

    @pl.when(pl.program_id(1) == 0)
    def _():
        s_ref[...] = jnp.zeros_like(s_ref)

    bf = lambda a: a.astype(BF16)

    def chunk(c, _):
        r0 = pl.multiple_of(c * L, L)
        x = [x_ref[r, pl.ds(r0, L), :] for r in rows]
        k = [x[r][:, KW:2 * KW] for r in rows]
        v = [bf(x[r][:, 2 * KW:2 * KW + W]) for r in rows]
        bc = [_dot_exact_lhs(tril, la_ref[r, pl.ds(r0, L), :], 2) for r in rows]
        b_last = [bc[r][L - 1:L, :] for r in rows]
        q = [x[r][:, 0:KW] * (GLA_DK ** -0.5) for r in rows]
        qe = [bf(q[r] * jnp.exp(bc[r])) for r in rows]
        b_mid = [bc[r] - bc[r][L // 2:L // 2 + 1, :] for r in rows]
        qm = [bf(q[r] * jnp.exp(b_mid[r])) for r in rows]
        km = [bf(k[r] * jnp.exp(-b_mid[r])) for r in rows]
        a_mat = [jnp.where(lower, _dot_nt(qm[r], _block_rows(km[r], head_k)), 0.0) for r in rows]
        state_t = [s_ref[r] for r in rows]
        o = [_dot_nt(qe[r], bf(state_t[r])) for r in rows]
        o = [o[r] + _dot(bf(a_mat[r]), _block_rows(v[r], head_v)) for r in rows]
        upd = [_dot_tn(v[r], bf(k[r] * jnp.exp(b_last[r] - bc[r]))) for r in rows]
        for r in rows:
            s_ref[r] = state_t[r] * jnp.exp(b_last[r]) + jnp.where(state_mask, upd[r], 0.0)

        ms = [_dot_exact_rhs(o[r] * o[r], ones_blk, 1) * (1.0 / HEAD_DIM) for r in rows]
        for r in rows:
            gate = _silu(x[r][:, 2 * KW + W:2 * KW + 2 * W])
            y = o[r] * lax.rsqrt(ms[r] + EPS) * gnorm_ref[...] * gate
            o_ref[r, pl.ds(r0, L), :] = y.astype(o_ref.dtype)
        return 0

    lax.fori_loop(0, n_chunks, chunk, 0)


def _gla(gla, misc, walpha, balpha, gnorm):
    b, s, _ = gla.shape
    ts = min(s, REC_SEQ_TILE)
    nr = math.gcd(b, REC_ROWS)
    const = lambda bi, si: (0, 0)
    return pl.pallas_call(
        _gla_kernel,
        grid=(b // nr, s // ts),
        in_specs=[
            pl.BlockSpec((nr, ts, N_GLA), lambda bi, si: (bi, si, 0)),
            pl.BlockSpec((nr, ts, N_MISC), lambda bi, si: (bi, si, 0)),
            pl.BlockSpec(walpha.shape, const),
            pl.BlockSpec(balpha.shape, const),
            pl.BlockSpec((1, GROUP_WIDTH), const),
        ],
        out_specs=pl.BlockSpec((nr, ts, GROUP_WIDTH), lambda bi, si: (bi, si, 0)),
        out_shape=jax.ShapeDtypeStruct((b, s, GROUP_WIDTH), BF16),
        scratch_shapes=[
            pltpu.VMEM((nr, GROUP_WIDTH, N_HEADS_G * GLA_DK), F32),
            pltpu.VMEM((nr, ts, N_HEADS_G * GLA_DK), F32),
        ],
        compiler_params=pltpu.CompilerParams(
            dimension_semantics=("arbitrary", "arbitrary"), vmem_limit_bytes=VMEM_LIMIT),
        name="gla",
    )(gla, misc, walpha, balpha, gnorm)


def _ffn_chunks(d_ff):
    step = 4 * GROUP_WIDTH
    return [(lo, min(step, d_ff - lo)) for lo in range(0, d_ff, step)]


def _mlp_kernel(h_ref, ya_ref, yb_ref, yc_ref, yd_ref, wout_ref, gffn_ref, wg_ref, wu_ref, wd_ref,
                gple_ref, wpg_ref, p_ref, wpp_ref, gfin_ref, o_ref, act_ref, *, final):
    mixed = jnp.concatenate([ya_ref[...], yb_ref[...], yc_ref[...], yd_ref[...]], axis=-1)
    h = h_ref[...] + _dot(mixed, wout_ref[...])
    hn = _rms(h, gffn_ref[...]).astype(BF16)
    for lo, width in _ffn_chunks(wg_ref.shape[1]):
        gate = _dot(hn, wg_ref[:, lo:lo + width])
        up = _dot(hn, wu_ref[:, lo:lo + width])
        act_ref[:, lo:lo + width] = (_silu(gate) * up).astype(BF16)
    h = h + _dot(act_ref[...], wd_ref[...])
    gate = _sigmoid(_dot(_rms(h, gple_ref[...]).astype(BF16), wpg_ref[...]))
    h = h + gate * _dot(p_ref[...].astype(BF16), wpp_ref[...])
    if final:
        h = _rms(h, gfin_ref[...])
    o_ref[...] = h


def _mlp(h2d, ys, wout, gffn, wg, wu, wd, gple, wpg, p2d, wpp, gfin, tm, final):
    m, d = h2d.shape
    d_ff = wg.shape[1]
    const = lambda i: (0, 0)
    resident = lambda shape: pl.BlockSpec(shape, const, pipeline_mode=pl.Buffered(1))
    rows = lambda width: pl.BlockSpec((tm, width), lambda i: (i, 0))
    return pl.pallas_call(
        functools.partial(_mlp_kernel, final=final),
        grid=(m // tm,),
        in_specs=[rows(d)] + [rows(GROUP_WIDTH)] * 4 + [
            resident(wout.shape), resident(gffn.shape), resident(wg.shape), resident(wu.shape),
            resident(wd.shape), resident(gple.shape), resident(wpg.shape),
            rows(p2d.shape[1]), resident(wpp.shape), resident(gfin.shape),
        ],
        out_specs=rows(d),
        out_shape=jax.ShapeDtypeStruct((m, d), F32),
        scratch_shapes=[pltpu.VMEM((tm, d_ff), BF16)],
        compiler_params=pltpu.CompilerParams(
            dimension_semantics=("arbitrary",), vmem_limit_bytes=VMEM_LIMIT),
        name="mlp",
    )(h2d, *ys, wout, gffn, wg, wu, wd, gple, wpg, p2d, wpp, gfin)


def _rel_bucket_table(n_dist):
    n = np.arange(n_dist)
    nf = np.maximum(n, 1).astype(np.float32)
    large = REL_MAX_EXACT + (np.log(nf / REL_MAX_EXACT) / math.log(REL_MAX_DIST / REL_MAX_EXACT)
                             * (N_REL_BUCKETS - REL_MAX_EXACT)).astype(np.int32)
    large = np.minimum(large, N_REL_BUCKETS - 1)
    return np.where(n < REL_MAX_EXACT, n, large)


def _bias_tiles(rel_bias):
    t = ATTN_TILE
    bucket = _rel_bucket_table(2 * t)
    assert (bucket[t + 1:] == N_REL_BUCKETS - 1).all()
    per_dist = (rel_bias[bucket, :] - rel_bias[N_REL_BUCKETS - 1][None, :]).T * LOG2E
    own = jnp.concatenate([per_dist[:, :t], jnp.full_like(per_dist[:, :t], NEG)], axis=1)
    prev = jnp.concatenate([per_dist[:, t:], per_dist[:, :t]], axis=1)
    vec = jnp.stack([own, prev], axis=1)
    rolled = jnp.tile(vec, (1, 1, t))[:, :, :t * (2 * t - 1)].reshape(-1, 2, t, 2 * t - 1)
    return rolled[:, :, :, :t]


def _row(v, width=None):
    v = v.astype(F32).reshape(1, -1)
    if width is not None and v.shape[1] < width:
        v = jnp.pad(v, ((0, 0), (0, width - v.shape[1])))
    return v


def kernel(x, p, norm_mix, w_in, rel_bias, diff_lambda, diff_norm, gdn_conv, gdn_a_log, gdn_dt_bias,
           gdn_norm, gla_w_alpha, gla_b_alpha, gla_norm, w_out, norm_ffn, w_gate, w_up, w_down,
           norm_ple, w_ple_gate, w_ple_proj, final_norm):
    b, s, d = x.shape
    depth = w_in.shape[0]
    m = b * s
    tm = 512
    hg = N_HEADS_G
    assert s % ATTN_TILE == 0 and s % tm == 0 and tm % MOBA_BLOCK == 0

    bias_t = _bias_tiles(rel_bias.astype(F32))
    bias_a, bias_b = bias_t[:hg], bias_t[hg:]
    gw = GROUP_WIDTH
    col_scale = np.ones((1, N_QK), np.float32)
    col_scale[:, 0:gw] = HEAD_DIM ** -0.5
    col_scale[:, gw:2 * gw] = LOG2E
    col_scale[:, 2 * gw:3 * gw] = DIFF_QK_DIM ** -0.5 * LOG2E
    col_scale = jnp.asarray(col_scale)

    n_main = 6 * gw + N_GDN
    ab = 2 * hg
    gla_lo = n_main + ab
    gla_hi = gla_lo + N_GLA
    fin = _row(final_norm)
    h = x.reshape(m, d)
    for l in range(depth):
        wl = w_in[l]
        w1 = jnp.concatenate(
            [wl[:, 0:2 * gw], wl[:, 3 * gw:5 * gw], wl[:, 6 * gw:n_main], wl[:, gla_lo:gla_hi],
             wl[:, n_main:gla_lo], wl[:, gla_hi:],
             jnp.zeros((d, N_MISC - ab - GLA_GATE_RANK), wl.dtype)], axis=1).astype(BF16)
        wvt = jnp.concatenate([wl[:, 2 * gw:3 * gw], wl[:, 5 * gw:6 * gw]], axis=1).T.astype(BF16)
        qk, vt, gdn, gla, misc, kmean = _inproj(h, _row(norm_mix[l]), w1, wvt, col_scale, tm, s)
        qk = qk.reshape(b, s, N_QK)
        misc = misc.reshape(b, s, N_MISC)
        kmean = kmean.reshape(b, s // MOBA_BLOCK, GROUP_WIDTH)

        y_a = _moba(qk, vt, kmean, bias_a)

        lam_init = 0.8 - 0.6 * math.exp(-0.3 * l)
        lam_p = jnp.concatenate([diff_lambda[l].astype(F32),
                                 jnp.full((1, DIFF_QK_DIM), lam_init, F32)], axis=0)
        y_b = _diff(qk, vt, lam_p, bias_b, _row(jnp.tile(diff_norm[l], hg)))

        hp = jnp.concatenate([_row(gdn_a_log[l], LANES), _row(gdn_dt_bias[l], LANES)], axis=0)
        y_c = _gdn(gdn.reshape(b, s, N_GDN), misc, gdn_conv[l].astype(F32), hp,
                   _row(jnp.tile(gdn_norm[l], hg)))

        walpha = jnp.zeros((N_MISC, hg * GLA_DK), F32).at[MISC_LR:MISC_LR + GLA_GATE_RANK].set(
            gla_w_alpha[l]).astype(BF16)
        y_d = _gla(gla.reshape(b, s, N_GLA), misc, walpha, _row(gla_b_alpha[l]),
                   _row(jnp.tile(gla_norm[l], hg)))

        ys = [y.reshape(m, GROUP_WIDTH) for y in (y_a, y_b, y_c, y_d)]
        h = _mlp(h, ys, w_out[l].astype(BF16), _row(norm_ffn[l]), w_gate[l].astype(BF16),
                 w_up[l].astype(BF16), w_down[l].astype(BF16), _row(norm_ple[l]),
                 w_ple_gate[l].astype(BF16), p[l].reshape(m, -1), w_ple_proj[l].astype(BF16),
                 fin, tm, final=(l == depth - 1))
    return h.reshape(b, s, d)
```

```python
import functools
import math

import numpy as np
import jax
import jax.numpy as jnp
from jax import lax
from jax.experimental import pallas as pl
from jax.experimental.pallas import tpu as pltpu

F32 = jnp.float32
BF16 = jnp.bfloat16

HEAD_DIM = 64
N_HEADS_G = 4
GROUP_WIDTH = HEAD_DIM * N_HEADS_G
MOBA_BLOCK = 256
MOBA_TOPK = 3
N_REL_BUCKETS = 32
REL_MAX_EXACT = 16
REL_MAX_DIST = 128
DIFF_QK_DIM = HEAD_DIM // 2
CONV_WIDTH = 4
CHUNK = 64
GLA_DK = HEAD_DIM // 2
GLA_GATE_RANK = 16
GLA_TAU = 16.0
EPS = 1e-6

LANES = 128
ATTN_TILE = 256
REC_SEQ_TILE = 256
REC_ROWS = 8
NEG = -1e30
LOG2E = math.log2(math.e)
VMEM_LIMIT = 56 * 1024 * 1024

N_QK = 4 * GROUP_WIDTH
N_VT = 2 * GROUP_WIDTH
N_GDN = 4 * GROUP_WIDTH
N_GLA = 2 * N_HEADS_G * GLA_DK + 2 * GROUP_WIDTH
N_MISC = LANES
MISC_A, MISC_B, MISC_LR = 0, N_HEADS_G, 2 * N_HEADS_G

_NT = (((1,), (1,)), ((), ()))
_TN = (((0,), (0,)), ((), ()))


def _dot(a, b):
    return jnp.dot(a, b, preferred_element_type=F32)


def _dot_nt(a, b):
    return lax.dot_general(a, b, _NT, preferred_element_type=F32)


def _dot_tn(a, b):
    return lax.dot_general(a, b, _TN, preferred_element_type=F32)


def _split_bf16(x, n):
    parts = []
    r = x
    for _ in range(n):
        hi = r.astype(BF16)
        parts.append(hi)
        r = r - hi.astype(F32)
    return parts


def _dot_exact_rhs(x, w, n):
    acc = None
    for part in _split_bf16(x, n):
        t = _dot(part, w)
        acc = t if acc is None else acc + t
    return acc


def _dot_exact_lhs(w, x, n):
    acc = None
    for part in _split_bf16(x, n):
        t = _dot(w, part)
        acc = t if acc is None else acc + t
    return acc


def _rms(x, g):
    return x * lax.rsqrt(jnp.mean(x * x, axis=-1, keepdims=True) + EPS) * g


def _sigmoid(x):
    return 1.0 / (1.0 + jnp.exp(-x))


def _silu(x):
    return x * _sigmoid(x)


def _inproj_kernel(x_ref, g_ref, w_ref, wvt_ref, scale_ref,
                   qk_ref, vt_ref, gdn_ref, gla_ref, misc_ref, kmean_ref):
    tm = x_ref.shape[0]
    xn = _rms(x_ref[...], g_ref[...]).astype(BF16)
    qk = _dot(xn, w_ref[:, 0:N_QK])
    k_moba = qk[:, GROUP_WIDTH:2 * GROUP_WIDTH]
    kmean_ref[0] = jnp.mean(k_moba.reshape(tm // MOBA_BLOCK, MOBA_BLOCK, GROUP_WIDTH), axis=1)
    qk_ref[...] = (qk * scale_ref[...]).astype(BF16)
    vt = _dot_nt(wvt_ref[...], xn).astype(BF16)
    for i in range(tm // ATTN_TILE):
        vt_ref[0, i] = vt[:, i * ATTN_TILE:(i + 1) * ATTN_TILE]
    o = N_QK
    gdn_ref[...] = _dot(xn, w_ref[:, o:o + N_GDN])
    o += N_GDN
    gla_ref[...] = _dot(xn, w_ref[:, o:o + N_GLA])
    o += N_GLA
    misc_ref[...] = _dot(xn, w_ref[:, o:o + N_MISC])


def _inproj(h2d, g, w, wvt, scale, tm, seq):
    m, d = h2d.shape
    n_all = w.shape[1]
    per_seq = seq // tm
    const = lambda i: (0, 0)
    return pl.pallas_call(
        _inproj_kernel,
        grid=(m // tm,),
        in_specs=[
            pl.BlockSpec((tm, d), lambda i: (i, 0)),
            pl.BlockSpec((1, d), const),
            pl.BlockSpec((d, n_all), const),
            pl.BlockSpec((N_VT, d), const),
            pl.BlockSpec((1, N_QK), const),
        ],
        out_specs=[
            pl.BlockSpec((tm, N_QK), lambda i: (i, 0)),
            pl.BlockSpec((1, tm // ATTN_TILE, N_VT, ATTN_TILE),
                         lambda i: (i // per_seq, i % per_seq, 0, 0)),
            pl.BlockSpec((tm, N_GDN), lambda i: (i, 0)),
            pl.BlockSpec((tm, N_GLA), lambda i: (i, 0)),
            pl.BlockSpec((tm, N_MISC), lambda i: (i, 0)),
            pl.BlockSpec((1, tm // MOBA_BLOCK, GROUP_WIDTH), lambda i: (i, 0, 0)),
        ],
        out_shape=[
            jax.ShapeDtypeStruct((m, N_QK), BF16),
            jax.ShapeDtypeStruct((m // seq, seq // ATTN_TILE, N_VT, ATTN_TILE), BF16),
            jax.ShapeDtypeStruct((m, N_GDN), F32),
            jax.ShapeDtypeStruct((m, N_GLA), F32),
            jax.ShapeDtypeStruct((m, N_MISC), F32),
            jax.ShapeDtypeStruct((m // tm, tm // MOBA_BLOCK, GROUP_WIDTH), F32),
        ],
        compiler_params=pltpu.CompilerParams(
            dimension_semantics=("arbitrary",), vmem_limit_bytes=VMEM_LIMIT),
        name="inproj",
    )(h2d, g, w, wvt, scale)


def _lane_mask(lo, width):
    lane = lax.broadcasted_iota(jnp.int32, (1, LANES), 1)
    return (lane >= lo) & (lane < lo + width)


ACC_ROWS = HEAD_DIM + 16
PIPE_CHAIN_TILES = 8
ATTN_HEADS_PER_STEP = 4


def _values_with_ones(vt, hh):
    head = vt[HEAD_DIM * hh:HEAD_DIM * (hh + 1), :]
    return jnp.concatenate([head, jnp.ones((ACC_ROWS - HEAD_DIM, vt.shape[1]), vt.dtype)], axis=0)


def _tile_max(scores):
    return [jnp.max(s, axis=0, keepdims=True) for s in scores]


def _softmax_stage(ms, scores, tile_maxes, valids=None):
    ms_new, alphas, ps = [], [], []
    for c, (m, s, m_tile) in enumerate(zip(ms, scores, tile_maxes)):
        if valids is None or valids[c] is None:
            m_new = jnp.maximum(m, m_tile)
            shift = m_new
        else:
            ok = valids[c] > 0.5
            m_new = jnp.maximum(m, jnp.where(ok, m_tile, NEG))
            shift = jnp.where(ok, m_new, -NEG)
        ms_new.append(m_new)
        alphas.append(jnp.exp2(m - m_new))
        ps.append(jnp.exp2(s - shift).astype(BF16))
    return ms_new, alphas, ps


def _value_products(vts, ps):
    return [_dot(vt, p) for vt, p in zip(vts, ps)]


def _attend_all(n, t, n_far, far_scores, far_valid, prev_scores, prev_valid, own_scores, values,
                j_prev, j_own, s_ref, p_ref, acc_ref, first_scores=None):
    chains = range(n)
    n_slots = jnp.maximum(n_far, 2)

    def slot_valid(j):
        exists = jnp.where(j < n_far, 1.0, 0.0)
        rows = far_valid(j)
        if rows is None:
            return [jnp.full((1, t), exists, F32)] * n
        return [row * exists for row in rows]

    s0, s1 = first_scores if first_scores is not None else (far_scores(0), far_scores(1))
    ms, alphas, ps = _softmax_stage([jnp.full((1, t), NEG, F32)] * n, s0, _tile_max(s0),
                                    slot_valid(0))
    tmax = _tile_max(s1)
    for c in chains:
        p_ref[c] = ps[c]
        s_ref[c] = s1[c]
        acc_ref[c] = jnp.zeros((ACC_ROWS, t), F32)

    def steps(first, unroll, carry):
        ms, alphas, tmax = carry
        ps = [p_ref[c] for c in chains]
        s_cur = [s_ref[c] for c in chains]
        for u in range(unroll):
            pv = _value_products(values(first + u), ps)
            s_new = far_scores(first + u + 2)
            tmax_new = _tile_max(s_new)
            ms, alphas_next, ps = _softmax_stage(ms, s_cur, tmax, far_valid(first + u + 1))
            for c in chains:
                acc_ref[c] = alphas[c] * acc_ref[c] + pv[c]
            alphas, s_cur, tmax = alphas_next, s_new, tmax_new
        for c in chains:
            p_ref[c] = ps[c]
            s_ref[c] = s_cur[c]
        return ms, alphas, tmax

    n_steps = n_slots - 2
    unroll = max(1, PIPE_CHAIN_TILES // n)
    n_blocks = n_steps // unroll
    carry = lax.fori_loop(0, n_blocks, lambda i, cr: steps(i * unroll, unroll, cr),
                          (ms, alphas, tmax))
    if unroll > 1:
        carry = lax.fori_loop(n_blocks * unroll, n_steps, lambda i, cr: steps(i, 1, cr), carry)
    ms, alphas, tmax = carry

    pv = _value_products(values(n_slots - 2), [p_ref[c] for c in chains])
    s_prev = prev_scores()
    tmax_prev = _tile_max(s_prev)
    ms, alphas1, ps1 = _softmax_stage(ms, [s_ref[c] for c in chains], tmax,
                                      slot_valid(n_slots - 1))
    accs = [alphas[c] * acc_ref[c] + pv[c] for c in chains]
    pv = _value_products(values(n_slots - 1), ps1)
    s_own = own_scores()
    tmax_own = _tile_max(s_own)
    ms, alphas2, ps2 = _softmax_stage(ms, s_prev, tmax_prev, prev_valid())
    accs = [alphas1[c] * accs[c] + pv[c] for c in chains]
    pv = _value_products(values(j_prev), ps2)
    ms, alphas3, ps3 = _softmax_stage(ms, s_own, tmax_own)
    accs = [alphas2[c] * accs[c] + pv[c] for c in chains]
    pv = _value_products(values(j_own), ps3)
    return [alphas3[c] * accs[c] + pv[c] for c in chains]


def _normalised(acc):
    return acc[0:HEAD_DIM, :] / acc[HEAD_DIM:HEAD_DIM + 1, :]


def _moba_kernel(q_ref, k_ref, vt_ref, kmean_ref, bias_ref, o_ref, sel_ref, s_ref, p_ref, acc_ref):
    t = ATTN_TILE
    qi = pl.program_id(2)
    nblk = kmean_ref.shape[1]
    heads = range(q_ref.shape[2] // HEAD_DIM)
    q = q_ref[0]
    kmean = kmean_ref[0].astype(BF16)

    def pair_lanes(x, h):
        lo = LANES * (h // 2)
        return x[:, lo:lo + LANES]

    qms = []
    for h in heads:
        qp = pair_lanes(q, h)
        qms.append(jnp.where(_lane_mask(HEAD_DIM * (h % 2), HEAD_DIM), qp, jnp.zeros_like(qp)))

    def keys(j):
        return k_ref[0, pl.ds(pl.multiple_of(j * t, t), t), :]

    def far_scores(j):
        k = keys(j)
        return [_dot_nt(pair_lanes(k, h), qms[h]) for h in heads]

    gates = [_dot_nt(pair_lanes(kmean, h), qms[h]) for h in heads]
    first_scores = far_scores(0), far_scores(1)

    row = lax.broadcasted_iota(jnp.int32, (nblk, t), 0)
    past = row < qi
    for h in heads:
        gate = jnp.where(past, gates[h], -jnp.inf)
        sel_t = jnp.zeros((nblk, t), F32)
        for _ in range(MOBA_TOPK):
            top = jnp.max(gate, axis=0, keepdims=True)
            first = jnp.min(jnp.where(gate == top, row, nblk), axis=0, keepdims=True)
            pick = row == first
            sel_t = jnp.where(pick, 1.0, sel_t)
            gate = jnp.where(pick, -jnp.inf, gate)
        sel_t = jnp.where(past, sel_t, 0.0)
        for j in range(nblk):
            sel_ref[h * nblk + j] = sel_t[j:j + 1, :]

    def values(j):
        vt = vt_ref[0, j]
        return [_values_with_ones(vt, h) for h in heads]

    def far_valid(j):
        return [sel_ref[h * nblk + j] for h in heads]

    jp = jnp.maximum(qi - 1, 0)

    def prev_scores():
        k = keys(jp)
        return [_dot_nt(pair_lanes(k, h), qms[h]) + bias_ref[h, 1] for h in heads]

    def own_scores():
        k = keys(qi)
        return [_dot_nt(pair_lanes(k, h), qms[h]) + bias_ref[h, 0] for h in heads]

    accs = _attend_all(len(heads), t, jnp.maximum(qi - 1, 0), far_scores, far_valid, prev_scores,
                       lambda: far_valid(jp), own_scores, values, jp, qi, s_ref, p_ref, acc_ref,
                       first_scores)
    out_t = jnp.concatenate([_normalised(acc) for acc in accs], axis=0)
    o_ref[0] = out_t.T.astype(o_ref.dtype)


def _attn_specs(s, t, mixer, width):
    nkv = s // t
    per = GROUP_WIDTH // width
    qcol = mixer * 2 * per
    kcol = qcol + per
    vrow = mixer * per
    return [
        pl.BlockSpec((1, t, width), lambda bi, g, qi: (bi, qi, qcol + g)),
        pl.BlockSpec((1, s, width), lambda bi, g, qi: (bi, 0, kcol + g)),
        pl.BlockSpec((1, nkv, width, t), lambda bi, g, qi: (bi, 0, vrow + g, 0)),
    ]


def _attn_scratch(n_chains, t):
    return [pltpu.VMEM((n_chains, t, t), F32),
            pltpu.VMEM((n_chains, t, t), BF16),
            pltpu.VMEM((n_chains, ACC_ROWS, t), F32)]


def _moba(qk, vt, kmean, bias_t):
    b, s, _ = qk.shape
    t = ATTN_TILE
    nq = s // t
    nblk = kmean.shape[1]
    n_heads = ATTN_HEADS_PER_STEP
    width = n_heads * HEAD_DIM
    return pl.pallas_call(
        _moba_kernel,
        grid=(b, GROUP_WIDTH // width, nq),
        in_specs=_attn_specs(s, t, 0, width) + [
            pl.BlockSpec((1, nblk, width), lambda bi, g, qi: (bi, 0, g)),
            pl.BlockSpec((n_heads, 2, t, t), lambda bi, g, qi: (g, 0, 0, 0)),
        ],
        out_specs=pl.BlockSpec((1, t, width), lambda bi, g, qi: (bi, qi, g)),
        out_shape=jax.ShapeDtypeStruct((b, s, GROUP_WIDTH), BF16),
        scratch_shapes=[pltpu.VMEM((n_heads * nblk, 1, t), F32)]
        + _attn_scratch(n_heads, t),
        compiler_params=pltpu.CompilerParams(
            dimension_semantics=("arbitrary", "arbitrary", "arbitrary"),
            vmem_limit_bytes=VMEM_LIMIT),
        name="moba",
    )(qk, qk, vt, kmean, bias_t)


def _diff_kernel(lam_ref, q_ref, k_ref, vt_ref, bias_ref, gnorm_ref, o_ref, s_ref, p_ref, acc_ref):
    t = ATTN_TILE
    qi = pl.program_id(2)
    heads = range(q_ref.shape[2] // HEAD_DIM)
    chains = [(h, mm) for h in heads for mm in range(2)]
    q = q_ref[0]

    def pair_lanes(x, h):
        lo = LANES * (h // 2)
        return x[:, lo:lo + LANES]

    masks = [_lane_mask(HEAD_DIM * (h % 2) + DIFF_QK_DIM * mm, DIFF_QK_DIM) for h, mm in chains]

    def keys(j):
        return k_ref[0, pl.ds(pl.multiple_of(j * t, t), t), :]

    def values(j):
        vt = vt_ref[0, j]
        vts = [_values_with_ones(vt, h) for h in heads]
        return [vts[h] for h, _ in chains]

    def far_scores(j):
        k = keys(j)
        out = []
        for pair in range(len(heads) // 2):
            kp = k[:, LANES * pair:LANES * (pair + 1)]
            zero = jnp.zeros_like(kp)
            cs = range(4 * pair, 4 * pair + 4)
            stacked = jnp.concatenate([jnp.where(masks[c], kp, zero) for c in cs], axis=0)
            s_all = _dot_nt(stacked, q[:, LANES * pair:LANES * (pair + 1)])
            out += [s_all[i * t:(i + 1) * t, :] for i in range(4)]
        return out

    jp = jnp.maximum(qi - 1, 0)

    def prev_scores():
        return [s + bias_ref[h, 1] for (h, _), s in zip(chains, far_scores(jp))]

    def prev_valid():
        return [jnp.full((1, t), jnp.where(qi >= 1, 1.0, 0.0), F32)] * len(chains)

    def own_scores():
        return [s + bias_ref[h, 0] for (h, _), s in zip(chains, far_scores(qi))]

    lam_p = lam_ref[...]
    lam_init = lam_p[4:5, 0:1]
    lam = (jnp.exp(jnp.sum(lam_p[0:1] * lam_p[1:2], axis=-1, keepdims=True))
           - jnp.exp(jnp.sum(lam_p[2:3] * lam_p[3:4], axis=-1, keepdims=True)) + lam_init)

    accs = _attend_all(len(chains), t, jnp.maximum(qi - 1, 0), far_scores, lambda j: None,
                       prev_scores, prev_valid, own_scores, values, jp, qi, s_ref, p_ref, acc_ref)
    outs = []
    for h in heads:
        o = _normalised(accs[2 * h]) - lam * _normalised(accs[2 * h + 1])
        ms = jnp.mean(o * o, axis=0, keepdims=True)
        outs.append(o * lax.rsqrt(ms + EPS))
    y_t = jnp.concatenate(outs, axis=0)
    o_ref[0] = (y_t.T * gnorm_ref[...] * (1.0 - lam_init)).astype(o_ref.dtype)


def _diff(qk, vt, lam_p, bias_t, gnorm):
    b, s, _ = qk.shape
    t = ATTN_TILE
    nq = s // t
    n_heads = ATTN_HEADS_PER_STEP
    width = n_heads * HEAD_DIM
    return pl.pallas_call(
        _diff_kernel,
        grid=(b, GROUP_WIDTH // width, nq),
        in_specs=[pl.BlockSpec(lam_p.shape, lambda bi, g, qi: (0, 0))]
        + _attn_specs(s, t, 1, width) + [
            pl.BlockSpec((n_heads, 2, t, t), lambda bi, g, qi: (g, 0, 0, 0)),
            pl.BlockSpec((1, width), lambda bi, g, qi: (0, g)),
        ],
        out_specs=pl.BlockSpec((1, t, width), lambda bi, g, qi: (bi, qi, g)),
        out_shape=jax.ShapeDtypeStruct((b, s, GROUP_WIDTH), BF16),
        scratch_shapes=_attn_scratch(2 * n_heads, t),
        compiler_params=pltpu.CompilerParams(
            dimension_semantics=("arbitrary", "arbitrary", "arbitrary"),
            vmem_limit_bytes=VMEM_LIMIT),
        name="diff_attn",
    )(lam_p, qk, qk, vt, bias_t, gnorm)


def _head_of_lane(n_lanes, width):
    return lax.broadcasted_iota(jnp.int32, (1, n_lanes), 1) // width


def _block_rows(x, lane_head, n_heads=N_HEADS_G):
    zero = jnp.zeros_like(x)
    return jnp.concatenate([jnp.where(lane_head == h, x, zero) for h in range(n_heads)], axis=0)


def _gdn_kernel(x_ref, misc_ref, conv_ref, hp_ref, gnorm_ref, o_ref, s_ref, tail_ref, g_ref, beta_ref):
    L = CHUNK
    W = GROUP_WIDTH
    rows = range(x_ref.shape[0])
    seq = x_ref.shape[1]
    n_chunks = seq // L
    head_w = _head_of_lane(W, HEAD_DIM)
    head_all = _head_of_lane(LANES, 1)
    ri = lax.broadcasted_iota(jnp.int32, (L, W), 0)
    cj = lax.broadcasted_iota(jnp.int32, (L, W), 1) % HEAD_DIM
    lower = ri >= cj
    strict = ri > cj
    ident = jnp.where(ri == cj, 1.0, 0.0)
    ones_blk = (lax.broadcasted_iota(jnp.int32, (W, W), 0) // HEAD_DIM
                == lax.broadcasted_iota(jnp.int32, (W, W), 1) // HEAD_DIM)
    ones_seg = ones_blk.astype(BF16)
    tril = (lax.broadcasted_iota(jnp.int32, (L, L), 0)
            >= lax.broadcasted_iota(jnp.int32, (L, L), 1)).astype(BF16)

    hp = hp_ref[...]
    exp_g = (lax.broadcasted_iota(jnp.int32, (LANES, W), 0) - MISC_A
             == lax.broadcasted_iota(jnp.int32, (LANES, W), 1) // HEAD_DIM).astype(BF16)
    exp_b = (lax.broadcasted_iota(jnp.int32, (LANES, W), 0) - MISC_B
             == lax.broadcasted_iota(jnp.int32, (LANES, W), 1) // HEAD_DIM).astype(BF16)
    del head_all
    for r in rows:
        misc = misc_ref[r]
        sp_in = misc + hp[1:2]
        softplus = jnp.maximum(sp_in, 0.0) + jnp.log(1.0 + jnp.exp(-jnp.abs(sp_in)))
        g_tok = -jnp.exp(hp[0:1]) * softplus
        b_tok = _sigmoid(misc)
        g_ref[r] = _dot_exact_rhs(g_tok, exp_g, 2)
        beta_ref[r] = _dot_exact_rhs(b_tok, exp_b, 2)

    @pl.when(pl.program_id(1) == 0)
    def _():
        s_ref[...] = jnp.zeros_like(s_ref)
        tail_ref[...] = jnp.zeros_like(tail_ref)

    cw = conv_ref[...]
    blk = lambda a: _block_rows(a.astype(BF16), head_w)
    bf = lambda a: a.astype(BF16)

    def prepare(r, r0):
        x = x_ref[r, pl.ds(r0, L), :]
        qkv = x[:, 0:3 * W]
        xx = jnp.concatenate([tail_ref[r], qkv], axis=0)
        tail_ref[r] = qkv[L - 8:L, :]
        conv = cw[CONV_WIDTH - 1:CONV_WIDTH] * qkv
        for i in range(CONV_WIDTH - 1):
            lo = 8 - (CONV_WIDTH - 1) + i
            conv = conv + cw[i:i + 1] * xx[lo:lo + L, :]
        conv = _silu(conv)
        return conv[:, 0:W], conv[:, W:2 * W], conv[:, 2 * W:3 * W], x[:, 3 * W:4 * W]

    def decays(gc):
        gc_col = jnp.sum(gc * ident, axis=0, keepdims=True)
        return jnp.exp(jnp.where(lower, gc - gc_col, -jnp.inf))

    def chunk(c, _):
        r0 = pl.multiple_of(c * L, L)
        q, k, v, z = zip(*[prepare(r, r0) for r in rows])
        ssq = [_dot_exact_rhs(jnp.concatenate([q[r] * q[r], k[r] * k[r]], axis=0), ones_seg, 1)
               for r in rows]
        q = [q[r] * lax.rsqrt(ssq[r][0:L] + EPS) * (HEAD_DIM ** -0.5) for r in rows]
        k = [k[r] * lax.rsqrt(ssq[r][L:2 * L] + EPS) for r in rows]
        beta = [beta_ref[r, pl.ds(r0, L), :] for r in rows]
        gc = [_dot_exact_lhs(tril, g_ref[r, pl.ds(r0, L), :], 2) for r in rows]
        egc = [jnp.exp(gc[r]) for r in rows]
        gc_last = [gc[r][L - 1:L, :] for r in rows]
        decay = [decays(gc[r]) for r in rows]
        kb = [k[r] * beta[r] for r in rows]
        both = [_dot_nt(bf(jnp.concatenate([kb[r], q[r]], axis=0)), blk(k[r])) for r in rows]
        qk = [jnp.where(lower, both[r][L:2 * L] * decay[r], 0.0) for r in rows]

        p = [jnp.where(strict, -both[r][0:L] * decay[r], 0.0) for r in rows]
        t_inv = [ident + p[r] for r in rows]
        p = [_dot(bf(p[r]), blk(p[r])) for r in rows]
        for _ in range(4):
            prod = [_dot(bf(jnp.concatenate([t_inv[r], p[r]], axis=0)), blk(p[r])) for r in rows]
            t_inv = [t_inv[r] + prod[r][0:L] for r in rows]
            p = [prod[r][L:2 * L] for r in rows]
        t_inv = [t_inv[r] + _dot(bf(t_inv[r]), blk(p[r])) for r in rows]

        u = [_dot(bf(t_inv[r]), blk(v[r] * beta[r])) for r in rows]
        w = [_dot(bf(t_inv[r]), blk(kb[r] * egc[r])) for r in rows]

        state = [s_ref[r] for r in rows]
        ws_qs = [_dot(bf(jnp.concatenate([w[r], q[r] * egc[r]], axis=0)), bf(state[r])) for r in rows]
        v_new = [u[r] - ws_qs[r][0:L] for r in rows]
        o = [ws_qs[r][L:2 * L] + _dot(bf(qk[r]), blk(v_new[r])) for r in rows]
        upd = [_dot_tn(bf(k[r] * jnp.exp(gc_last[r] - gc[r])), bf(v_new[r])) for r in rows]
        for r in rows:
            s_ref[r] = state[r] * jnp.exp(gc_last[r]) + jnp.where(ones_blk, upd[r], 0.0)

        ms = [_dot_exact_rhs(o[r] * o[r], ones_seg, 1) * (1.0 / HEAD_DIM) for r in rows]
        for r in rows:
            y = o[r] * lax.rsqrt(ms[r] + EPS) * gnorm_ref[...] * _silu(z[r])
            o_ref[r, pl.ds(r0, L), :] = y.astype(o_ref.dtype)
        return 0

    lax.fori_loop(0, n_chunks, chunk, 0)


def _gdn(gdn, misc, conv_w, hp, gnorm):
    b, s, _ = gdn.shape
    ts = min(s, REC_SEQ_TILE)
    nr = math.gcd(b, REC_ROWS)
    const = lambda bi, si: (0, 0)
    return pl.pallas_call(
        _gdn_kernel,
        grid=(b // nr, s // ts),
        in_specs=[
            pl.BlockSpec((nr, ts, N_GDN), lambda bi, si: (bi, si, 0)),
            pl.BlockSpec((nr, ts, N_MISC), lambda bi, si: (bi, si, 0)),
            pl.BlockSpec(conv_w.shape, const),
            pl.BlockSpec(hp.shape, const),
            pl.BlockSpec((1, GROUP_WIDTH), const),
        ],
        out_specs=pl.BlockSpec((nr, ts, GROUP_WIDTH), lambda bi, si: (bi, si, 0)),
        out_shape=jax.ShapeDtypeStruct((b, s, GROUP_WIDTH), BF16),
        scratch_shapes=[
            pltpu.VMEM((nr, GROUP_WIDTH, GROUP_WIDTH), F32),
            pltpu.VMEM((nr, 8, 3 * GROUP_WIDTH), F32),
            pltpu.VMEM((nr, ts, GROUP_WIDTH), F32),
            pltpu.VMEM((nr, ts, GROUP_WIDTH), F32),
        ],
        compiler_params=pltpu.CompilerParams(
            dimension_semantics=("arbitrary", "arbitrary"), vmem_limit_bytes=VMEM_LIMIT),
        name="gdn",
    )(gdn, misc, conv_w, hp, gnorm)


def _gla_kernel(x_ref, misc_ref, walpha_ref, balpha_ref, gnorm_ref, o_ref, s_ref, la_ref):
    L = CHUNK
    W = GROUP_WIDTH
    KW = N_HEADS_G * GLA_DK
    rows = range(x_ref.shape[0])
    seq = x_ref.shape[1]
    n_chunks = seq // L
    head_k = _head_of_lane(KW, GLA_DK)
    head_v = _head_of_lane(W, HEAD_DIM)
    ri = lax.broadcasted_iota(jnp.int32, (L, W), 0)
    cj = lax.broadcasted_iota(jnp.int32, (L, W), 1) % HEAD_DIM
    lower = ri >= cj
    ones_blk = (lax.broadcasted_iota(jnp.int32, (W, W), 0) // HEAD_DIM
                == lax.broadcasted_iota(jnp.int32, (W, W), 1) // HEAD_DIM).astype(BF16)
    state_mask = (lax.broadcasted_iota(jnp.int32, (W, KW), 0) // HEAD_DIM
                  == lax.broadcasted_iota(jnp.int32, (W, KW), 1) // GLA_DK)
    tril = (lax.broadcasted_iota(jnp.int32, (L, L), 0)
            >= lax.broadcasted_iota(jnp.int32, (L, L), 1)).astype(BF16)

    for r in rows:
        pre = _dot(misc_ref[r].astype(BF16), walpha_ref[...]) + balpha_ref[...]
        log_sig = jnp.minimum(pre, 0.0) - jnp.log(1.0 + jnp.exp(-jnp.abs(pre)))
        la_ref[r] = log_sig * (1.0 / GLA_TAU)

    @pl.when(pl.program_id(1) == 0)
    def _():
        s_ref[...] = jnp.zeros_like(s_ref)

    bf = lambda a: a.astype(BF16)

    def chunk(c, _):
        r0 = pl.multiple_of(c * L, L)
        x = [x_ref[r, pl.ds(r0, L), :] for r in rows]
        k = [x[r][:, KW:2 * KW] for r in rows]
        v = [bf(x[r][:, 2 * KW:2 * KW + W]) for r in rows]
        bc = [_dot_exact_lhs(tril, la_ref[r, pl.ds(r0, L), :], 2) for r in rows]
        b_last = [bc[r][L - 1:L, :] for r in rows]
        q = [x[r][:, 0:KW] * (GLA_DK ** -0.5) for r in rows]
        qe = [bf(q[r] * jnp.exp(bc[r])) for r in rows]
        b_mid = [bc[r] - bc[r][L // 2:L // 2 + 1, :] for r in rows]
        qm = [bf(q[r] * jnp.exp(b_mid[r])) for r in rows]
        km = [bf(k[r] * jnp.exp(-b_mid[r])) for r in rows]
        a_mat = [jnp.where(lower, _dot_nt(qm[r], _block_rows(km[r], head_k)), 0.0) for r in rows]
        state_t = [s_ref[r] for r in rows]
        o = [_dot_nt(qe[r], bf(state_t[r])) for r in rows]
        o = [o[r] + _dot(bf(a_mat[r]), _block_rows(v[r], head_v)) for r in rows]
        upd = [_dot_tn(v[r], bf(k[r] * jnp.exp(b_last[r] - bc[r]))) for r in rows]
        for r in rows:
            s_ref[r] = state_t[r] * jnp.exp(b_last[r]) + jnp.where(state_mask, upd[r], 0.0)

        ms = [_dot_exact_rhs(o[r] * o[r], ones_blk, 1) * (1.0 / HEAD_DIM) for r in rows]
        for r in rows:
            gate = _silu(x[r][:, 2 * KW + W:2 * KW + 2 * W])
            y = o[r] * lax.rsqrt(ms[r] + EPS) * gnorm_ref[...] * gate
            o_ref[r, pl.ds(r0, L), :] = y.astype(o_ref.dtype)
        return 0

    lax.fori_loop(0, n_chunks, chunk, 0)


def _gla(gla, misc, walpha, balpha, gnorm):
    b, s, _ = gla.shape
    ts = min(s, REC_SEQ_TILE)
    nr = math.gcd(b, REC_ROWS)
    const = lambda bi, si: (0, 0)
    return pl.pallas_call(
        _gla_kernel,
        grid=(b // nr, s // ts),
        in_specs=[
            pl.BlockSpec((nr, ts, N_GLA), lambda bi, si: (bi, si, 0)),
            pl.BlockSpec((nr, ts, N_MISC), lambda bi, si: (bi, si, 0)),
            pl.BlockSpec(walpha.shape, const),
            pl.BlockSpec(balpha.shape, const),
            pl.BlockSpec((1, GROUP_WIDTH), const),
        ],
        out_specs=pl.BlockSpec((nr, ts, GROUP_WIDTH), lambda bi, si: (bi, si, 0)),
        out_shape=jax.ShapeDtypeStruct((b, s, GROUP_WIDTH), BF16),
        scratch_shapes=[
            pltpu.VMEM((nr, GROUP_WIDTH, N_HEADS_G * GLA_DK), F32),
            pltpu.VMEM((nr, ts, N_HEADS_G * GLA_DK), F32),
        ],
        compiler_params=pltpu.CompilerParams(
            dimension_semantics=("arbitrary", "arbitrary"), vmem_limit_bytes=VMEM_LIMIT),
        name="gla",
    )(gla, misc, walpha, balpha, gnorm)


def _ffn_chunks(d_ff):
    step = 4 * GROUP_WIDTH
    return [(lo, min(step, d_ff - lo)) for lo in range(0, d_ff, step)]


def _mlp_kernel(h_ref, ya_ref, yb_ref, yc_ref, yd_ref, wout_ref, gffn_ref, wg_ref, wu_ref, wd_ref,
                gple_ref, wpg_ref, p_ref, wpp_ref, gfin_ref, o_ref, act_ref, *, final):
    mixed = jnp.concatenate([ya_ref[...], yb_ref[...], yc_ref[...], yd_ref[...]], axis=-1)
    h = h_ref[...] + _dot(mixed, wout_ref[...])
    hn = _rms(h, gffn_ref[...]).astype(BF16)
    for lo, width in _ffn_chunks(wg_ref.shape[1]):
        gate = _dot(hn, wg_ref[:, lo:lo + width])
        up = _dot(hn, wu_ref[:, lo:lo + width])
        act_ref[:, lo:lo + width] = (_silu(gate) * up).astype(BF16)
    h = h + _dot(act_ref[...], wd_ref[...])
    gate = _sigmoid(_dot(_rms(h, gple_ref[...]).astype(BF16), wpg_ref[...]))
    h = h + gate * _dot(p_ref[...].astype(BF16), wpp_ref[...])
    if final:
        h = _rms(h, gfin_ref[...])
    o_ref[...] = h


def _mlp(h2d, ys, wout, gffn, wg, wu, wd, gple, wpg, p2d, wpp, gfin, tm, final):
    m, d = h2d.shape
    d_ff = wg.shape[1]
    const = lambda i: (0, 0)
    resident = lambda shape: pl.BlockSpec(shape, const, pipeline_mode=pl.Buffered(1))
    rows = lambda width: pl.BlockSpec((tm, width), lambda i: (i, 0))
    return pl.pallas_call(
        functools.partial(_mlp_kernel, final=final),
        grid=(m // tm,),
        in_specs=[rows(d)] + [rows(GROUP_WIDTH)] * 4 + [
            resident(wout.shape), resident(gffn.shape), resident(wg.shape), resident(wu.shape),
            resident(wd.shape), resident(gple.shape), resident(wpg.shape),
            rows(p2d.shape[1]), resident(wpp.shape), resident(gfin.shape),
        ],
        out_specs=rows(d),
        out_shape=jax.ShapeDtypeStruct((m, d), F32),
        scratch_shapes=[pltpu.VMEM((tm, d_ff), BF16)],
        compiler_params=pltpu.CompilerParams(
            dimension_semantics=("arbitrary",), vmem_limit_bytes=VMEM_LIMIT),
        name="mlp",
    )(h2d, *ys, wout, gffn, wg, wu, wd, gple, wpg, p2d, wpp, gfin)


def _rel_bucket_table(n_dist):
    n = np.arange(n_dist)
    nf = np.maximum(n, 1).astype(np.float32)
    large = REL_MAX_EXACT + (np.log(nf / REL_MAX_EXACT) / math.log(REL_MAX_DIST / REL_MAX_EXACT)
                             * (N_REL_BUCKETS - REL_MAX_EXACT)).astype(np.int32)
    large = np.minimum(large, N_REL_BUCKETS - 1)
    return np.where(n < REL_MAX_EXACT, n, large)


def _bias_tiles(rel_bias):
    t = ATTN_TILE
    bucket = _rel_bucket_table(2 * t)
    assert (bucket[t + 1:] == N_REL_BUCKETS - 1).all()
    per_dist = (rel_bias[bucket, :] - rel_bias[N_REL_BUCKETS - 1][None, :]).T * LOG2E
    own = jnp.concatenate([per_dist[:, :t], jnp.full_like(per_dist[:, :t], NEG)], axis=1)
    prev = jnp.concatenate([per_dist[:, t:], per_dist[:, :t]], axis=1)
    vec = jnp.stack([own, prev], axis=1)
    rolled = jnp.tile(vec, (1, 1, t))[:, :, :t * (2 * t - 1)].reshape(-1, 2, t, 2 * t - 1)
    return rolled[:, :, :, :t]


def _row(v, width=None):
    v = v.astype(F32).reshape(1, -1)
    if width is not None and v.shape[1] < width:
        v = jnp.pad(v, ((0, 0), (0, width - v.shape[1])))
    return v


def kernel(x, p, norm_mix, w_in, rel_bias, diff_lambda, diff_norm, gdn_conv, gdn_a_log, gdn_dt_bias,
           gdn_norm, gla_w_alpha, gla_b_alpha, gla_norm, w_out, norm_ffn, w_gate, w_up, w_down,
           norm_ple, w_ple_gate, w_ple_proj, final_norm):
    b, s, d = x.shape
    depth = w_in.shape[0]
    m = b * s
    tm = 512
    hg = N_HEADS_G
    assert s % ATTN_TILE == 0 and s % tm == 0 and tm % MOBA_BLOCK == 0

    bias_t = _bias_tiles(rel_bias.astype(F32))
    bias_a, bias_b = bias_t[:hg], bias_t[hg:]
    gw = GROUP_WIDTH
    col_scale = np.ones((1, N_QK), np.float32)
    col_scale[:, 0:gw] = HEAD_DIM ** -0.5
    col_scale[:, gw:2 * gw] = LOG2E
    col_scale[:, 2 * gw:3 * gw] = DIFF_QK_DIM ** -0.5 * LOG2E
    col_scale = jnp.asarray(col_scale)

    n_main = 6 * gw + N_GDN
    ab = 2 * hg
    gla_lo = n_main + ab
    gla_hi = gla_lo + N_GLA
    fin = _row(final_norm)
    h = x.reshape(m, d)
    for l in range(depth):
        wl = w_in[l]
        w1 = jnp.concatenate(
            [wl[:, 0:2 * gw], wl[:, 3 * gw:5 * gw], wl[:, 6 * gw:n_main], wl[:, gla_lo:gla_hi],
             wl[:, n_main:gla_lo], wl[:, gla_hi:],
             jnp.zeros((d, N_MISC - ab - GLA_GATE_RANK), wl.dtype)], axis=1).astype(BF16)
        wvt = jnp.concatenate([wl[:, 2 * gw:3 * gw], wl[:, 5 * gw:6 * gw]], axis=1).T.astype(BF16)
        qk, vt, gdn, gla, misc, kmean = _inproj(h, _row(norm_mix[l]), w1, wvt, col_scale, tm, s)
        qk = qk.reshape(b, s, N_QK)
        misc = misc.reshape(b, s, N_MISC)
        kmean = kmean.reshape(b, s // MOBA_BLOCK, GROUP_WIDTH)

        y_a = _moba(qk, vt, kmean, bias_a)

        lam_init = 0.8 - 0.6 * math.exp(-0.3 * l)
        lam_p = jnp.concatenate([diff_lambda[l].astype(F32),
                                 jnp.full((1, DIFF_QK_DIM), lam_init, F32)], axis=0)
        y_b = _diff(qk, vt, lam_p, bias_b, _row(jnp.tile(diff_norm[l], hg)))

        hp = jnp.concatenate([_row(gdn_a_log[l], LANES), _row(gdn_dt_bias[l], LANES)], axis=0)
        y_c = _gdn(gdn.reshape(b, s, N_GDN), misc, gdn_conv[l].astype(F32), hp,
                   _row(jnp.tile(gdn_norm[l], hg)))

        walpha = jnp.zeros((N_MISC, hg * GLA_DK), F32).at[MISC_LR:MISC_LR + GLA_GATE_RANK].set(
            gla_w_alpha[l]).astype(BF16)
        y_d = _gla(gla.reshape(b, s, N_GLA), misc, walpha, _row(gla_b_alpha[l]),
                   _row(jnp.tile(gla_norm[l], hg)))

        ys = [y.reshape(m, GROUP_WIDTH) for y in (y_a, y_b, y_c, y_d)]
        h = _mlp(h, ys, w_out[l].astype(BF16), _row(norm_ffn[l]), w_gate[l].astype(BF16),
                 w_up[l].astype(BF16), w_down[l].astype(BF16), _row(norm_ple[l]),
                 w_ple_gate[l].astype(BF16), p[l].reshape(m, -1), w_ple_proj[l].astype(BF16),
                 fin, tm, final=(l == depth - 1))
    return h.reshape(b, s, d)
```

```python
import functools
import math

import numpy as np
import jax
import jax.numpy as jnp
from jax import lax
from jax.experimental import pallas as pl
from jax.experimental.pallas import tpu as pltpu

F32 = jnp.float32
BF16 = jnp.bfloat16

HEAD_DIM = 64
N_HEADS_G = 4
GROUP_WIDTH = HEAD_DIM * N_HEADS_G
MOBA_BLOCK = 256
MOBA_TOPK = 3
N_REL_BUCKETS = 32
REL_MAX_EXACT = 16
REL_MAX_DIST = 128
DIFF_QK_DIM = HEAD_DIM // 2
CONV_WIDTH = 4
CHUNK = 64
GLA_DK = HEAD_DIM // 2
GLA_GATE_RANK = 16
GLA_TAU = 16.0
EPS = 1e-6

LANES = 128
ATTN_TILE = 256
REC_SEQ_TILE = 256
REC_ROWS = 8
NEG = -1e30
LOG2E = math.log2(math.e)
VMEM_LIMIT = 56 * 1024 * 1024

N_QK = 4 * GROUP_WIDTH
N_VT = 2 * GROUP_WIDTH
N_GDN = 4 * GROUP_WIDTH
N_GLA = 2 * N_HEADS_G * GLA_DK + 2 * GROUP_WIDTH
N_MISC = LANES
MISC_A, MISC_B, MISC_LR = 0, N_HEADS_G, 2 * N_HEADS_G

_NT = (((1,), (1,)), ((), ()))
_TN = (((0,), (0,)), ((), ()))


def _dot(a, b):
    return jnp.dot(a, b, preferred_element_type=F32)


def _dot_nt(a, b):
    return lax.dot_general(a, b, _NT, preferred_element_type=F32)


def _dot_tn(a, b):
    return lax.dot_general(a, b, _TN, preferred_element_type=F32)


def _split_bf16(x, n):
    parts = []
    r = x
    for _ in range(n):
        hi = r.astype(BF16)
        parts.append(hi)
        r = r - hi.astype(F32)
    return parts


def _dot_exact_rhs(x, w, n):
    acc = None
    for part in _split_bf16(x, n):
        t = _dot(part, w)
        acc = t if acc is None else acc + t
    return acc


def _dot_exact_lhs(w, x, n):
    acc = None
    for part in _split_bf16(x, n):
        t = _dot(w, part)
        acc = t if acc is None else acc + t
    return acc


def _rms(x, g):
    return x * lax.rsqrt(jnp.mean(x * x, axis=-1, keepdims=True) + EPS) * g


def _sigmoid(x):
    return 1.0 / (1.0 + jnp.exp(-x))


def _silu(x):
    return x * _sigmoid(x)


def _inproj_kernel(x_ref, g_ref, w_ref, wvt_ref, scale_ref,
                   qk_ref, vt_ref, gdn_ref, gla_ref, misc_ref, kmean_ref):
    tm = x_ref.shape[0]
    xn = _rms(x_ref[...], g_ref[...]).astype(BF16)
    qk = _dot(xn, w_ref[:, 0:N_QK])
    k_moba = qk[:, GROUP_WIDTH:2 * GROUP_WIDTH]
    kmean_ref[0] = jnp.mean(k_moba.reshape(tm // MOBA_BLOCK, MOBA_BLOCK, GROUP_WIDTH), axis=1)
    qk_ref[...] = (qk * scale_ref[...]).astype(BF16)
    vt = _dot_nt(wvt_ref[...], xn).astype(BF16)
    for i in range(tm // ATTN_TILE):
        vt_ref[0, i] = vt[:, i * ATTN_TILE:(i + 1) * ATTN_TILE]
    o = N_QK
    gdn_ref[...] = _dot(xn, w_ref[:, o:o + N_GDN])
    o += N_GDN
    gla_ref[...] = _dot(xn, w_ref[:, o:o + N_GLA])
    o += N_GLA
    misc_ref[...] = _dot(xn, w_ref[:, o:o + N_MISC])


def _inproj(h2d, g, layer, w, wvt, scale, tm, seq):
    m, d = h2d.shape
    n_all = w.shape[2]
    per_seq = seq // tm
    const = lambda i: (0, 0)
    return pl.pallas_call(
        _inproj_kernel,
        grid=(m // tm,),
        in_specs=[
            pl.BlockSpec((tm, d), lambda i: (i, 0)),
            pl.BlockSpec((1, d), const),
            pl.BlockSpec((None, d, n_all), lambda i: (layer, 0, 0)),
            pl.BlockSpec((None, N_VT, d), lambda i: (layer, 0, 0)),
            pl.BlockSpec((1, N_QK), const),
        ],
        out_specs=[
            pl.BlockSpec((tm, N_QK), lambda i: (i, 0)),
            pl.BlockSpec((1, tm // ATTN_TILE, N_VT, ATTN_TILE),
                         lambda i: (i // per_seq, i % per_seq, 0, 0)),
            pl.BlockSpec((tm, N_GDN), lambda i: (i, 0)),
            pl.BlockSpec((tm, N_GLA), lambda i: (i, 0)),
            pl.BlockSpec((tm, N_MISC), lambda i: (i, 0)),
            pl.BlockSpec((1, tm // MOBA_BLOCK, GROUP_WIDTH), lambda i: (i, 0, 0)),
        ],
        out_shape=[
            jax.ShapeDtypeStruct((m, N_QK), BF16),
            jax.ShapeDtypeStruct((m // seq, seq // ATTN_TILE, N_VT, ATTN_TILE), BF16),
            jax.ShapeDtypeStruct((m, N_GDN), F32),
            jax.ShapeDtypeStruct((m, N_GLA), F32),
            jax.ShapeDtypeStruct((m, N_MISC), F32),
            jax.ShapeDtypeStruct((m // tm, tm // MOBA_BLOCK, GROUP_WIDTH), F32),
        ],
        compiler_params=pltpu.CompilerParams(
            dimension_semantics=("arbitrary",), vmem_limit_bytes=VMEM_LIMIT),
        name="inproj",
    )(h2d, g, w, wvt, scale)


def _lane_mask(lo, width):
    lane = lax.broadcasted_iota(jnp.int32, (1, LANES), 1)
    return (lane >= lo) & (lane < lo + width)


ACC_ROWS = HEAD_DIM + 16
PIPE_CHAIN_TILES = 8
ATTN_HEADS_PER_STEP = 4


def _values_with_ones(vt, hh):
    head = vt[HEAD_DIM * hh:HEAD_DIM * (hh + 1), :]
    return jnp.concatenate([head, jnp.ones((ACC_ROWS - HEAD_DIM, vt.shape[1]), vt.dtype)], axis=0)


def _tile_max(scores):
    return [jnp.max(s, axis=0, keepdims=True) for s in scores]


def _softmax_stage(ms, scores, tile_maxes, valids=None):
    ms_new, alphas, ps = [], [], []
    for c, (m, s, m_tile) in enumerate(zip(ms, scores, tile_maxes)):
        if valids is None or valids[c] is None:
            m_new = jnp.maximum(m, m_tile)
            shift = m_new
        else:
            ok = valids[c] > 0.5
            m_new = jnp.maximum(m, jnp.where(ok, m_tile, NEG))
            shift = jnp.where(ok, m_new, -NEG)
        ms_new.append(m_new)
        alphas.append(jnp.exp2(m - m_new))
        ps.append(jnp.exp2(s - shift).astype(BF16))
    return ms_new, alphas, ps


def _value_products(vts, ps):
    return [_dot(vt, p) for vt, p in zip(vts, ps)]


def _attend_all(n, t, n_far, far_scores, far_valid, prev_scores, prev_valid, own_scores, values,
                j_prev, j_own, s_ref, p_ref, acc_ref, first_scores=None):
    chains = range(n)
    n_slots = jnp.maximum(n_far, 2)

    def slot_valid(j):
        exists = jnp.where(j < n_far, 1.0, 0.0)
        rows = far_valid(j)
        if rows is None:
            return [jnp.full((1, t), exists, F32)] * n
        return [row * exists for row in rows]

    s0, s1 = first_scores if first_scores is not None else (far_scores(0), far_scores(1))
    ms, alphas, ps = _softmax_stage([jnp.full((1, t), NEG, F32)] * n, s0, _tile_max(s0),
                                    slot_valid(0))
    tmax = _tile_max(s1)
    for c in chains:
        p_ref[c] = ps[c]
        s_ref[c] = s1[c]
        acc_ref[c] = jnp.zeros((ACC_ROWS, t), F32)

    def steps(first, unroll, carry):
        ms, alphas, tmax = carry
        ps = [p_ref[c] for c in chains]
        s_cur = [s_ref[c] for c in chains]
        for u in range(unroll):
            pv = _value_products(values(first + u), ps)
            s_new = far_scores(first + u + 2)
            tmax_new = _tile_max(s_new)
            ms, alphas_next, ps = _softmax_stage(ms, s_cur, tmax, far_valid(first + u + 1))
            for c in chains:
                acc_ref[c] = alphas[c] * acc_ref[c] + pv[c]
            alphas, s_cur, tmax = alphas_next, s_new, tmax_new
        for c in chains:
            p_ref[c] = ps[c]
            s_ref[c] = s_cur[c]
        return ms, alphas, tmax

    n_steps = n_slots - 2
    unroll = max(1, PIPE_CHAIN_TILES // n)
    n_blocks = n_steps // unroll
    carry = lax.fori_loop(0, n_blocks, lambda i, cr: steps(i * unroll, unroll, cr),
                          (ms, alphas, tmax))
    if unroll > 1:
        carry = lax.fori_loop(n_blocks * unroll, n_steps, lambda i, cr: steps(i, 1, cr), carry)
    ms, alphas, tmax = carry

    pv = _value_products(values(n_slots - 2), [p_ref[c] for c in chains])
    s_prev = prev_scores()
    tmax_prev = _tile_max(s_prev)
    ms, alphas1, ps1 = _softmax_stage(ms, [s_ref[c] for c in chains], tmax,
                                      slot_valid(n_slots - 1))
    accs = [alphas[c] * acc_ref[c] + pv[c] for c in chains]
    pv = _value_products(values(n_slots - 1), ps1)
    s_own = own_scores()
    tmax_own = _tile_max(s_own)
    ms, alphas2, ps2 = _softmax_stage(ms, s_prev, tmax_prev, prev_valid())
    accs = [alphas1[c] * accs[c] + pv[c] for c in chains]
    pv = _value_products(values(j_prev), ps2)
    ms, alphas3, ps3 = _softmax_stage(ms, s_own, tmax_own)
    accs = [alphas2[c] * accs[c] + pv[c] for c in chains]
    pv = _value_products(values(j_own), ps3)
    return [alphas3[c] * accs[c] + pv[c] for c in chains]


def _normalised(acc):
    return acc[0:HEAD_DIM, :] / acc[HEAD_DIM:HEAD_DIM + 1, :]


def _moba_kernel(q_ref, k_ref, vt_ref, kmean_ref, bias_ref, o_ref, sel_ref, s_ref, p_ref, acc_ref):
    t = ATTN_TILE
    qi = pl.program_id(2)
    nblk = kmean_ref.shape[1]
    heads = range(q_ref.shape[2] // HEAD_DIM)
    q = q_ref[0]
    kmean = kmean_ref[0].astype(BF16)

    def pair_lanes(x, h):
        lo = LANES * (h // 2)
        return x[:, lo:lo + LANES]

    qms = []
    for h in heads:
        qp = pair_lanes(q, h)
        qms.append(jnp.where(_lane_mask(HEAD_DIM * (h % 2), HEAD_DIM), qp, jnp.zeros_like(qp)))

    def keys(j):
        return k_ref[0, pl.ds(pl.multiple_of(j * t, t), t), :]

    def far_scores(j):
        k = keys(j)
        return [_dot_nt(pair_lanes(k, h), qms[h]) for h in heads]

    gates = [_dot_nt(pair_lanes(kmean, h), qms[h]) for h in heads]
    first_scores = far_scores(0), far_scores(1)

    row = lax.broadcasted_iota(jnp.int32, (nblk, t), 0)
    past = row < qi
    for h in heads:
        gate = jnp.where(past, gates[h], -jnp.inf)
        sel_t = jnp.zeros((nblk, t), F32)
        for _ in range(MOBA_TOPK):
            top = jnp.max(gate, axis=0, keepdims=True)
            first = jnp.min(jnp.where(gate == top, row, nblk), axis=0, keepdims=True)
            pick = row == first
            sel_t = jnp.where(pick, 1.0, sel_t)
            gate = jnp.where(pick, -jnp.inf, gate)
        sel_t = jnp.where(past, sel_t, 0.0)
        for j in range(nblk):
            sel_ref[h * nblk + j] = sel_t[j:j + 1, :]

    def values(j):
        vt = vt_ref[0, j]
        return [_values_with_ones(vt, h) for h in heads]

    def far_valid(j):
        return [sel_ref[h * nblk + j] for h in heads]

    jp = jnp.maximum(qi - 1, 0)

    def prev_scores():
        k = keys(jp)
        return [_dot_nt(pair_lanes(k, h), qms[h]) + bias_ref[h, 1] for h in heads]

    def own_scores():
        k = keys(qi)
        return [_dot_nt(pair_lanes(k, h), qms[h]) + bias_ref[h, 0] for h in heads]

    accs = _attend_all(len(heads), t, jnp.maximum(qi - 1, 0), far_scores, far_valid, prev_scores,
                       lambda: far_valid(jp), own_scores, values, jp, qi, s_ref, p_ref, acc_ref,
                       first_scores)
    out_t = jnp.concatenate([_normalised(acc) for acc in accs], axis=0)
    o_ref[0] = out_t.T.astype(o_ref.dtype)


def _attn_specs(s, t, mixer, width):
    nkv = s // t
    per = GROUP_WIDTH // width
    qcol = mixer * 2 * per
    kcol = qcol + per
    vrow = mixer * per
    return [
        pl.BlockSpec((1, t, width), lambda bi, g, qi: (bi, qi, qcol + g)),
        pl.BlockSpec((1, s, width), lambda bi, g, qi: (bi, 0, kcol + g)),
        pl.BlockSpec((1, nkv, width, t), lambda bi, g, qi: (bi, 0, vrow + g, 0)),
    ]


def _attn_scratch(n_chains, t):
    return [pltpu.VMEM((n_chains, t, t), F32),
            pltpu.VMEM((n_chains, t, t), BF16),
            pltpu.VMEM((n_chains, ACC_ROWS, t), F32)]


def _moba(qk, vt, kmean, bias_t):
    b, s, _ = qk.shape
    t = ATTN_TILE
    nq = s // t
    nblk = kmean.shape[1]
    n_heads = ATTN_HEADS_PER_STEP
    width = n_heads * HEAD_DIM
    return pl.pallas_call(
        _moba_kernel,
        grid=(b, GROUP_WIDTH // width, nq),
        in_specs=_attn_specs(s, t, 0, width) + [
            pl.BlockSpec((1, nblk, width), lambda bi, g, qi: (bi, 0, g)),
            pl.BlockSpec((n_heads, 2, t, t), lambda bi, g, qi: (g, 0, 0, 0)),
        ],
        out_specs=pl.BlockSpec((1, t, width), lambda bi, g, qi: (bi, qi, g)),
        out_shape=jax.ShapeDtypeStruct((b, s, GROUP_WIDTH), BF16),
        scratch_shapes=[pltpu.VMEM((n_heads * nblk, 1, t), F32)]
        + _attn_scratch(n_heads, t),
        compiler_params=pltpu.CompilerParams(
            dimension_semantics=("arbitrary", "arbitrary", "arbitrary"),
            vmem_limit_bytes=VMEM_LIMIT),
        name="moba",
    )(qk, qk, vt, kmean, bias_t)


def _diff_kernel(lam_ref, q_ref, k_ref, vt_ref, bias_ref, gnorm_ref, o_ref, s_ref, p_ref, acc_ref):
    t = ATTN_TILE
    qi = pl.program_id(2)
    heads = range(q_ref.shape[2] // HEAD_DIM)
    chains = [(h, mm) for h in heads for mm in range(2)]
    q = q_ref[0]

    def pair_lanes(x, h):
        lo = LANES * (h // 2)
        return x[:, lo:lo + LANES]

    qms = []
    for h, mm in chains:
        qp = pair_lanes(q, h)
        mask = _lane_mask(HEAD_DIM * (h % 2) + DIFF_QK_DIM * mm, DIFF_QK_DIM)
        qms.append(jnp.where(mask, qp, jnp.zeros_like(qp)))

    def keys(j):
        return k_ref[0, pl.ds(pl.multiple_of(j * t, t), t), :]

    def values(j):
        vt = vt_ref[0, j]
        vts = [_values_with_ones(vt, h) for h in heads]
        return [vts[h] for h, _ in chains]

    def far_scores(j):
        k = keys(j)
        return [_dot_nt(pair_lanes(k, h), qm) for (h, _), qm in zip(chains, qms)]

    jp = jnp.maximum(qi - 1, 0)

    def prev_scores():
        return [s + bias_ref[h, 1] for (h, _), s in zip(chains, far_scores(jp))]

    def prev_valid():
        return [jnp.full((1, t), jnp.where(qi >= 1, 1.0, 0.0), F32)] * len(chains)

    def own_scores():
        return [s + bias_ref[h, 0] for (h, _), s in zip(chains, far_scores(qi))]

    lam_p = lam_ref[...]
    lam_init = lam_p[4:5, 0:1]
    lam = (jnp.exp(jnp.sum(lam_p[0:1] * lam_p[1:2], axis=-1, keepdims=True))
           - jnp.exp(jnp.sum(lam_p[2:3] * lam_p[3:4], axis=-1, keepdims=True)) + lam_init)

    accs = _attend_all(len(chains), t, jnp.maximum(qi - 1, 0), far_scores, lambda j: None,
                       prev_scores, prev_valid, own_scores, values, jp, qi, s_ref, p_ref, acc_ref)
    outs = []
    for h in heads:
        o = _normalised(accs[2 * h]) - lam * _normalised(accs[2 * h + 1])
        ms = jnp.mean(o * o, axis=0, keepdims=True)
        outs.append(o * lax.rsqrt(ms + EPS))
    y_t = jnp.concatenate(outs, axis=0)
    o_ref[0] = (y_t.T * gnorm_ref[...] * (1.0 - lam_init)).astype(o_ref.dtype)


def _diff(qk, vt, lam_p, bias_t, gnorm):
    b, s, _ = qk.shape
    t = ATTN_TILE
    nq = s // t
    n_heads = ATTN_HEADS_PER_STEP
    width = n_heads * HEAD_DIM
    return pl.pallas_call(
        _diff_kernel,
        grid=(b, GROUP_WIDTH // width, nq),
        in_specs=[pl.BlockSpec(lam_p.shape, lambda bi, g, qi: (0, 0))]
        + _attn_specs(s, t, 1, width) + [
            pl.BlockSpec((n_heads, 2, t, t), lambda bi, g, qi: (g, 0, 0, 0)),
            pl.BlockSpec((1, width), lambda bi, g, qi: (0, g)),
        ],
        out_specs=pl.BlockSpec((1, t, width), lambda bi, g, qi: (bi, qi, g)),
        out_shape=jax.ShapeDtypeStruct((b, s, GROUP_WIDTH), BF16),
        scratch_shapes=_attn_scratch(2 * n_heads, t),
        compiler_params=pltpu.CompilerParams(
            dimension_semantics=("arbitrary", "arbitrary", "arbitrary"),
            vmem_limit_bytes=VMEM_LIMIT),
        name="diff_attn",
    )(lam_p, qk, qk, vt, bias_t, gnorm)


def _head_of_lane(n_lanes, width):
    return lax.broadcasted_iota(jnp.int32, (1, n_lanes), 1) // width


def _block_rows(x, lane_head, n_heads=N_HEADS_G):
    zero = jnp.zeros_like(x)
    return jnp.concatenate([jnp.where(lane_head == h, x, zero) for h in range(n_heads)], axis=0)


def _gdn_kernel(x_ref, misc_ref, conv_ref, hp_ref, gnorm_ref, o_ref, s_ref, tail_ref, g_ref, beta_ref):
    L = CHUNK
    W = GROUP_WIDTH
    rows = range(x_ref.shape[0])
    seq = x_ref.shape[1]
    n_chunks = seq // L
    head_w = _head_of_lane(W, HEAD_DIM)
    head_all = _head_of_lane(LANES, 1)
    ri = lax.broadcasted_iota(jnp.int32, (L, W), 0)
    cj = lax.broadcasted_iota(jnp.int32, (L, W), 1) % HEAD_DIM
    lower = ri >= cj
    strict = ri > cj
    ident = jnp.where(ri == cj, 1.0, 0.0)
    ones_blk = (lax.broadcasted_iota(jnp.int32, (W, W), 0) // HEAD_DIM
                == lax.broadcasted_iota(jnp.int32, (W, W), 1) // HEAD_DIM)
    ones_seg = ones_blk.astype(BF16)
    tril = (lax.broadcasted_iota(jnp.int32, (L, L), 0)
            >= lax.broadcasted_iota(jnp.int32, (L, L), 1)).astype(BF16)

    hp = hp_ref[...]
    exp_g = (lax.broadcasted_iota(jnp.int32, (LANES, W), 0) - MISC_A
             == lax.broadcasted_iota(jnp.int32, (LANES, W), 1) // HEAD_DIM).astype(BF16)
    exp_b = (lax.broadcasted_iota(jnp.int32, (LANES, W), 0) - MISC_B
             == lax.broadcasted_iota(jnp.int32, (LANES, W), 1) // HEAD_DIM).astype(BF16)
    del head_all
    for r in rows:
        misc = misc_ref[r]
        sp_in = misc + hp[1:2]
        softplus = jnp.maximum(sp_in, 0.0) + jnp.log(1.0 + jnp.exp(-jnp.abs(sp_in)))
        g_tok = -jnp.exp(hp[0:1]) * softplus
        b_tok = _sigmoid(misc)
        g_ref[r] = _dot_exact_rhs(g_tok, exp_g, 2)
        beta_ref[r] = _dot_exact_rhs(b_tok, exp_b, 2)

    @pl.when(pl.program_id(1) == 0)
    def _():
        s_ref[...] = jnp.zeros_like(s_ref)
        tail_ref[...] = jnp.zeros_like(tail_ref)

    cw = conv_ref[...]
    blk = lambda a: _block_rows(a.astype(BF16), head_w)
    bf = lambda a: a.astype(BF16)

    def prepare(r, r0):
        x = x_ref[r, pl.ds(r0, L), :]
        qkv = x[:, 0:3 * W]
        xx = jnp.concatenate([tail_ref[r], qkv], axis=0)
        tail_ref[r] = qkv[L - 8:L, :]
        conv = cw[CONV_WIDTH - 1:CONV_WIDTH] * qkv
        for i in range(CONV_WIDTH - 1):
            lo = 8 - (CONV_WIDTH - 1) + i
            conv = conv + cw[i:i + 1] * xx[lo:lo + L, :]
        conv = _silu(conv)
        return conv[:, 0:W], conv[:, W:2 * W], conv[:, 2 * W:3 * W], x[:, 3 * W:4 * W]

    def decays(gc):
        gc_col = jnp.sum(gc * ident, axis=0, keepdims=True)
        return jnp.exp(jnp.where(lower, gc - gc_col, -jnp.inf))

    def chunk(c, _):
        r0 = pl.multiple_of(c * L, L)
        q, k, v, z = zip(*[prepare(r, r0) for r in rows])
        ssq = [_dot_exact_rhs(jnp.concatenate([q[r] * q[r], k[r] * k[r]], axis=0), ones_seg, 1)
               for r in rows]
        q = [q[r] * lax.rsqrt(ssq[r][0:L] + EPS) * (HEAD_DIM ** -0.5) for r in rows]
        k = [k[r] * lax.rsqrt(ssq[r][L:2 * L] + EPS) for r in rows]
        beta = [beta_ref[r, pl.ds(r0, L), :] for r in rows]
        gc = [_dot_exact_lhs(tril, g_ref[r, pl.ds(r0, L), :], 2) for r in rows]
        egc = [jnp.exp(gc[r]) for r in rows]
        gc_last = [gc[r][L - 1:L, :] for r in rows]
        decay = [decays(gc[r]) for r in rows]
        kb = [k[r] * beta[r] for r in rows]
        both = [_dot_nt(bf(jnp.concatenate([kb[r], q[r]], axis=0)), blk(k[r])) for r in rows]
        qk = [jnp.where(lower, both[r][L:2 * L] * decay[r], 0.0) for r in rows]

        p = [jnp.where(strict, -both[r][0:L] * decay[r], 0.0) for r in rows]
        t_inv = [ident + p[r] for r in rows]
        p = [_dot(bf(p[r]), blk(p[r])) for r in rows]
        for _ in range(4):
            prod = [_dot(bf(jnp.concatenate([t_inv[r], p[r]], axis=0)), blk(p[r])) for r in rows]
            t_inv = [t_inv[r] + prod[r][0:L] for r in rows]
            p = [prod[r][L:2 * L] for r in rows]
        t_inv = [t_inv[r] + _dot(bf(t_inv[r]), blk(p[r])) for r in rows]

        u = [_dot(bf(t_inv[r]), blk(v[r] * beta[r])) for r in rows]
        w = [_dot(bf(t_inv[r]), blk(kb[r] * egc[r])) for r in rows]

        state = [s_ref[r] for r in rows]
        ws_qs = [_dot(bf(jnp.concatenate([w[r], q[r] * egc[r]], axis=0)), bf(state[r])) for r in rows]
        v_new = [u[r] - ws_qs[r][0:L] for r in rows]
        o = [ws_qs[r][L:2 * L] + _dot(bf(qk[r]), blk(v_new[r])) for r in rows]
        upd = [_dot_tn(bf(k[r] * jnp.exp(gc_last[r] - gc[r])), bf(v_new[r])) for r in rows]
        for r in rows:
            s_ref[r] = state[r] * jnp.exp(gc_last[r]) + jnp.where(ones_blk, upd[r], 0.0)

        ms = [_dot_exact_rhs(o[r] * o[r], ones_seg, 1) * (1.0 / HEAD_DIM) for r in rows]
        for r in rows:
            y = o[r] * lax.rsqrt(ms[r] + EPS) * gnorm_ref[...] * _silu(z[r])
            o_ref[r, pl.ds(r0, L), :] = y.astype(o_ref.dtype)
        return 0

    lax.fori_loop(0, n_chunks, chunk, 0)


def _gdn(gdn, misc, conv_w, hp, gnorm):
    b, s, _ = gdn.shape
    ts = min(s, REC_SEQ_TILE)
    nr = math.gcd(b, REC_ROWS)
    const = lambda bi, si: (0, 0)
    return pl.pallas_call(
        _gdn_kernel,
        grid=(b // nr, s // ts),
        in_specs=[
            pl.BlockSpec((nr, ts, N_GDN), lambda bi, si: (bi, si, 0)),
            pl.BlockSpec((nr, ts, N_MISC), lambda bi, si: (bi, si, 0)),
            pl.BlockSpec(conv_w.shape, const),
            pl.BlockSpec(hp.shape, const),
            pl.BlockSpec((1, GROUP_WIDTH), const),
        ],
        out_specs=pl.BlockSpec((nr, ts, GROUP_WIDTH), lambda bi, si: (bi, si, 0)),
        out_shape=jax.ShapeDtypeStruct((b, s, GROUP_WIDTH), BF16),
        scratch_shapes=[
            pltpu.VMEM((nr, GROUP_WIDTH, GROUP_WIDTH), F32),
            pltpu.VMEM((nr, 8, 3 * GROUP_WIDTH), F32),
            pltpu.VMEM((nr, ts, GROUP_WIDTH), F32),
            pltpu.VMEM((nr, ts, GROUP_WIDTH), F32),
        ],
        compiler_params=pltpu.CompilerParams(
            dimension_semantics=("arbitrary", "arbitrary"), vmem_limit_bytes=VMEM_LIMIT),
        name="gdn",
    )(gdn, misc, conv_w, hp, gnorm)


def _gla_kernel(x_ref, misc_ref, walpha_ref, balpha_ref, gnorm_ref, o_ref, s_ref, la_ref):
    L = CHUNK
    W = GROUP_WIDTH
    KW = N_HEADS_G * GLA_DK
    rows = range(x_ref.shape[0])
    seq = x_ref.shape[1]
    n_chunks = seq // L
    head_k = _head_of_lane(KW, GLA_DK)
    head_v = _head_of_lane(W, HEAD_DIM)
    ri = lax.broadcasted_iota(jnp.int32, (L, W), 0)
    cj = lax.broadcasted_iota(jnp.int32, (L, W), 1) % HEAD_DIM
    lower = ri >= cj
    ones_blk = (lax.broadcasted_iota(jnp.int32, (W, W), 0) // HEAD_DIM
                == lax.broadcasted_iota(jnp.int32, (W, W), 1) // HEAD_DIM).astype(BF16)
    state_mask = (lax.broadcasted_iota(jnp.int32, (W, KW), 0) // HEAD_DIM
                  == lax.broadcasted_iota(jnp.int32, (W, KW), 1) // GLA_DK)
    tril = (lax.broadcasted_iota(jnp.int32, (L, L), 0)
            >= lax.broadcasted_iota(jnp.int32, (L, L), 1)).astype(BF16)

    for r in rows:
        pre = _dot(misc_ref[r].astype(BF16), walpha_ref[...]) + balpha_ref[...]
        log_sig = jnp.minimum(pre, 0.0) - jnp.log(1.0 + jnp.exp(-jnp.abs(pre)))
        la_ref[r] = log_sig * (1.0 / GLA_TAU)

    @pl.when(pl.program_id(1) == 0)
    def _():
        s_ref[...] = jnp.zeros_like(s_ref)

    bf = lambda a: a.astype(BF16)

    def chunk(c, _):
        r0 = pl.multiple_of(c * L, L)
        x = [x_ref[r, pl.ds(r0, L), :] for r in rows]
        k = [x[r][:, KW:2 * KW] for r in rows]
        v = [bf(x[r][:, 2 * KW:2 * KW + W]) for r in rows]
        bc = [_dot_exact_lhs(tril, la_ref[r, pl.ds(r0, L), :], 2) for r in rows]
        b_last = [bc[r][L - 1:L, :] for r in rows]
        q = [x[r][:, 0:KW] * (GLA_DK ** -0.5) for r in rows]
        qe = [bf(q[r] * jnp.exp(bc[r])) for r in rows]
        b_mid = [bc[r] - bc[r][L // 2:L // 2 + 1, :] for r in rows]
        qm = [bf(q[r] * jnp.exp(b_mid[r])) for r in rows]
        km = [bf(k[r] * jnp.exp(-b_mid[r])) for r in rows]
        a_mat = [jnp.where(lower, _dot_nt(qm[r], _block_rows(km[r], head_k)), 0.0) for r in rows]
        state_t = [s_ref[r] for r in rows]
        o = [_dot_nt(qe[r], bf(state_t[r])) for r in rows]
        o = [o[r] + _dot(bf(a_mat[r]), _block_rows(v[r], head_v)) for r in rows]
        upd = [_dot_tn(v[r], bf(k[r] * jnp.exp(b_last[r] - bc[r]))) for r in rows]
        for r in rows:
            s_ref[r] = state_t[r] * jnp.exp(b_last[r]) + jnp.where(state_mask, upd[r], 0.0)

        ms = [_dot_exact_rhs(o[r] * o[r], ones_blk, 1) * (1.0 / HEAD_DIM) for r in rows]
        for r in rows:
            gate = _silu(x[r][:, 2 * KW + W:2 * KW + 2 * W])
            y = o[r] * lax.rsqrt(ms[r] + EPS) * gnorm_ref[...] * gate
            o_ref[r, pl.ds(r0, L), :] = y.astype(o_ref.dtype)
        return 0

    lax.fori_loop(0, n_chunks, chunk, 0)


def _gla(gla, misc, walpha, balpha, gnorm):
    b, s, _ = gla.shape
    ts = min(s, REC_SEQ_TILE)
    nr = math.gcd(b, REC_ROWS)
    const = lambda bi, si: (0, 0)
    return pl.pallas_call(
        _gla_kernel,
        grid=(b // nr, s // ts),
        in_specs=[
            pl.BlockSpec((nr, ts, N_GLA), lambda bi, si: (bi, si, 0)),
            pl.BlockSpec((nr, ts, N_MISC), lambda bi, si: (bi, si, 0)),
            pl.BlockSpec(walpha.shape, const),
            pl.BlockSpec(balpha.shape, const),
            pl.BlockSpec((1, GROUP_WIDTH), const),
        ],
        out_specs=pl.BlockSpec((nr, ts, GROUP_WIDTH), lambda bi, si: (bi, si, 0)),
        out_shape=jax.ShapeDtypeStruct((b, s, GROUP_WIDTH), BF16),
        scratch_shapes=[
            pltpu.VMEM((nr, GROUP_WIDTH, N_HEADS_G * GLA_DK), F32),
            pltpu.VMEM((nr, ts, N_HEADS_G * GLA_DK), F32),
        ],
        compiler_params=pltpu.CompilerParams(
            dimension_semantics=("arbitrary", "arbitrary"), vmem_limit_bytes=VMEM_LIMIT),
        name="gla",
    )(gla, misc, walpha, balpha, gnorm)


def _ffn_chunks(d_ff):
    step = 4 * GROUP_WIDTH
    return [(lo, min(step, d_ff - lo)) for lo in range(0, d_ff, step)]


def _mlp_kernel(h_ref, ya_ref, yb_ref, yc_ref, yd_ref, wout_ref, gffn_ref, wg_ref, wu_ref, wd_ref,
                gple_ref, wpg_ref, p_ref, wpp_ref, gfin_ref, o_ref, act_ref, *, final):
    mixed = jnp.concatenate([ya_ref[...], yb_ref[...], yc_ref[...], yd_ref[...]], axis=-1)
    h = h_ref[...] + _dot(mixed, wout_ref[...])
    hn = _rms(h, gffn_ref[...]).astype(BF16)
    for lo, width in _ffn_chunks(wg_ref.shape[1]):
        gate = _dot(hn, wg_ref[:, lo:lo + width])
        up = _dot(hn, wu_ref[:, lo:lo + width])
        act_ref[:, lo:lo + width] = (_silu(gate) * up).astype(BF16)
    h = h + _dot(act_ref[...], wd_ref[...])
    gate = _sigmoid(_dot(_rms(h, gple_ref[...]).astype(BF16), wpg_ref[...]))
    h = h + gate * _dot(p_ref[...].astype(BF16), wpp_ref[...])
    if final:
        h = _rms(h, gfin_ref[...])
    o_ref[...] = h


def _mlp(h2d, ys, layer, wout, gffn, wg, wu, wd, gple, wpg, p3d, wpp, gfin, tm, final):
    m, d = h2d.shape
    d_ff = wg.shape[2]
    const = lambda i: (0, 0)
    resident = lambda shape: pl.BlockSpec(shape, const, pipeline_mode=pl.Buffered(1))
    stacked = lambda a: pl.BlockSpec((None,) + a.shape[1:], lambda i: (layer, 0, 0),
                                     pipeline_mode=pl.Buffered(1))
    rows = lambda width: pl.BlockSpec((tm, width), lambda i: (i, 0))
    return pl.pallas_call(
        functools.partial(_mlp_kernel, final=final),
        grid=(m // tm,),
        in_specs=[rows(d)] + [rows(GROUP_WIDTH)] * 4 + [
            stacked(wout), resident(gffn.shape), stacked(wg), stacked(wu),
            stacked(wd), resident(gple.shape), stacked(wpg),
            pl.BlockSpec((None, tm, p3d.shape[2]), lambda i: (layer, i, 0)), stacked(wpp),
            resident(gfin.shape),
        ],
        out_specs=rows(d),
        out_shape=jax.ShapeDtypeStruct((m, d), F32),
        scratch_shapes=[pltpu.VMEM((tm, d_ff), BF16)],
        compiler_params=pltpu.CompilerParams(
            dimension_semantics=("arbitrary",), vmem_limit_bytes=VMEM_LIMIT),
        name="mlp",
    )(h2d, *ys, wout, gffn, wg, wu, wd, gple, wpg, p3d, wpp, gfin)


def _rel_bucket_table(n_dist):
    n = np.arange(n_dist)
    nf = np.maximum(n, 1).astype(np.float32)
    large = REL_MAX_EXACT + (np.log(nf / REL_MAX_EXACT) / math.log(REL_MAX_DIST / REL_MAX_EXACT)
                             * (N_REL_BUCKETS - REL_MAX_EXACT)).astype(np.int32)
    large = np.minimum(large, N_REL_BUCKETS - 1)
    return np.where(n < REL_MAX_EXACT, n, large)


def _bias_tiles(rel_bias):
    t = ATTN_TILE
    bucket = _rel_bucket_table(2 * t)
    assert (bucket[t + 1:] == N_REL_BUCKETS - 1).all()
    per_dist = (rel_bias[bucket, :] - rel_bias[N_REL_BUCKETS - 1][None, :]).T * LOG2E
    own = jnp.concatenate([per_dist[:, :t], jnp.full_like(per_dist[:, :t], NEG)], axis=1)
    prev = jnp.concatenate([per_dist[:, t:], per_dist[:, :t]], axis=1)
    vec = jnp.stack([own, prev], axis=1)
    rolled = jnp.tile(vec, (1, 1, t))[:, :, :t * (2 * t - 1)].reshape(-1, 2, t, 2 * t - 1)
    return rolled[:, :, :, :t]


def _row(v, width=None):
    v = v.astype(F32).reshape(1, -1)
    if width is not None and v.shape[1] < width:
        v = jnp.pad(v, ((0, 0), (0, width - v.shape[1])))
    return v


def kernel(x, p, norm_mix, w_in, rel_bias, diff_lambda, diff_norm, gdn_conv, gdn_a_log, gdn_dt_bias,
           gdn_norm, gla_w_alpha, gla_b_alpha, gla_norm, w_out, norm_ffn, w_gate, w_up, w_down,
           norm_ple, w_ple_gate, w_ple_proj, final_norm):
    b, s, d = x.shape
    depth = w_in.shape[0]
    m = b * s
    tm = 512
    hg = N_HEADS_G
    assert s % ATTN_TILE == 0 and s % tm == 0 and tm % MOBA_BLOCK == 0

    bias_t = _bias_tiles(rel_bias.astype(F32))
    bias_a, bias_b = bias_t[:hg], bias_t[hg:]
    gw = GROUP_WIDTH
    col_scale = np.ones((1, N_QK), np.float32)
    col_scale[:, 0:gw] = HEAD_DIM ** -0.5
    col_scale[:, gw:2 * gw] = LOG2E
    col_scale[:, 2 * gw:3 * gw] = DIFF_QK_DIM ** -0.5 * LOG2E
    col_scale = jnp.asarray(col_scale)

    n_main = 6 * gw + N_GDN
    ab = 2 * hg
    gla_lo = n_main + ab
    gla_hi = gla_lo + N_GLA
    fin = _row(final_norm)
    w_out_bf, w_gate_bf, w_up_bf, w_down_bf, w_ple_gate_bf, w_ple_proj_bf = (
        w.astype(BF16) for w in (w_out, w_gate, w_up, w_down, w_ple_gate, w_ple_proj))
    p3d = p.reshape(depth, m, p.shape[-1])
    w1 = jnp.concatenate(
        [w_in[:, :, 0:2 * gw], w_in[:, :, 3 * gw:5 * gw], w_in[:, :, 6 * gw:n_main],
         w_in[:, :, gla_lo:gla_hi], w_in[:, :, n_main:gla_lo], w_in[:, :, gla_hi:],
         jnp.zeros((depth, d, N_MISC - ab - GLA_GATE_RANK), w_in.dtype)], axis=2).astype(BF16)
    wvt = jnp.swapaxes(jnp.concatenate([w_in[:, :, 2 * gw:3 * gw], w_in[:, :, 5 * gw:6 * gw]],
                                       axis=2), 1, 2).astype(BF16)
    h = x.reshape(m, d)
    for l in range(depth):
        qk, vt, gdn, gla, misc, kmean = _inproj(h, _row(norm_mix[l]), l, w1, wvt, col_scale, tm, s)
        qk = qk.reshape(b, s, N_QK)
        misc = misc.reshape(b, s, N_MISC)
        kmean = kmean.reshape(b, s // MOBA_BLOCK, GROUP_WIDTH)

        y_a = _moba(qk, vt, kmean, bias_a)

        lam_init = 0.8 - 0.6 * math.exp(-0.3 * l)
        lam_p = jnp.concatenate([diff_lambda[l].astype(F32),
                                 jnp.full((1, DIFF_QK_DIM), lam_init, F32)], axis=0)
        y_b = _diff(qk, vt, lam_p, bias_b, _row(jnp.tile(diff_norm[l], hg)))

        hp = jnp.concatenate([_row(gdn_a_log[l], LANES), _row(gdn_dt_bias[l], LANES)], axis=0)
        y_c = _gdn(gdn.reshape(b, s, N_GDN), misc, gdn_conv[l].astype(F32), hp,
                   _row(jnp.tile(gdn_norm[l], hg)))

        walpha = jnp.zeros((N_MISC, hg * GLA_DK), F32).at[MISC_LR:MISC_LR + GLA_GATE_RANK].set(
            gla_w_alpha[l]).astype(BF16)
        y_d = _gla(gla.reshape(b, s, N_GLA), misc, walpha, _row(gla_b_alpha[l]),
                   _row(jnp.tile(gla_norm[l], hg)))

        ys = [y.reshape(m, GROUP_WIDTH) for y in (y_a, y_b, y_c, y_d)]
        h = _mlp(h, ys, l, w_out_bf, _row(norm_ffn[l]), w_gate_bf, w_up_bf, w_down_bf,
                 _row(norm_ple[l]), w_ple_gate_bf, p3d, w_ple_proj_bf, fin, tm,
                 final=(l == depth - 1))
    return h.reshape(b, s, d)
```

```python
import functools
import math

import numpy as np
import jax
import jax.numpy as jnp
from jax import lax
from jax.experimental import pallas as pl
from jax.experimental.pallas import tpu as pltpu

F32 = jnp.float32
BF16 = jnp.bfloat16

HEAD_DIM = 64
N_HEADS_G = 4
GROUP_WIDTH = HEAD_DIM * N_HEADS_G
MOBA_BLOCK = 256
MOBA_TOPK = 3
N_REL_BUCKETS = 32
REL_MAX_EXACT = 16
REL_MAX_DIST = 128
DIFF_QK_DIM = HEAD_DIM // 2
CONV_WIDTH = 4
CHUNK = 64
GLA_DK = HEAD_DIM // 2
GLA_GATE_RANK = 16
GLA_TAU = 16.0
EPS = 1e-6

LANES = 128
F32_SUBLANES = 8
BF16_SUBLANES = 16
VMEM_LIMIT = 56 * 1024 * 1024
ROW_TILE = 512
ATTN_TILE = 256
ACC_ROWS = HEAD_DIM + BF16_SUBLANES
PIPE_CHAIN_TILES = 8
REC_SEQ_TILE = 256
REC_ROWS = 8
NEG = -1e30
LOG2E = math.log2(math.e)

N_QK = 4 * GROUP_WIDTH
N_VT = 2 * GROUP_WIDTH
N_GDN = 4 * GROUP_WIDTH
N_GLA = 2 * N_HEADS_G * GLA_DK + 2 * GROUP_WIDTH
N_MISC = LANES
MISC_A, MISC_B, MISC_LR = 0, N_HEADS_G, 2 * N_HEADS_G

_NT = (((1,), (1,)), ((), ()))
_TN = (((0,), (0,)), ((), ()))


def _dot(a, b):
    return jnp.dot(a, b, preferred_element_type=F32)


def _dot_nt(a, b):
    return lax.dot_general(a, b, _NT, preferred_element_type=F32)


def _dot_tn(a, b):
    return lax.dot_general(a, b, _TN, preferred_element_type=F32)


def _split_bf16(x, n):
    parts = []
    r = x
    for _ in range(n):
        hi = r.astype(BF16)
        parts.append(hi)
        r = r - hi.astype(F32)
    return parts


def _dot_exact_rhs(x, w, n):
    acc = None
    for part in _split_bf16(x, n):
        t = _dot(part, w)
        acc = t if acc is None else acc + t
    return acc


def _dot_exact_lhs(w, x, n):
    acc = None
    for part in _split_bf16(x, n):
        t = _dot(w, part)
        acc = t if acc is None else acc + t
    return acc


def _rms(x, g):
    return x * lax.rsqrt(jnp.mean(x * x, axis=-1, keepdims=True) + EPS) * g


def _sigmoid(x):
    return 1.0 / (1.0 + jnp.exp(-x))


def _silu(x):
    return x * _sigmoid(x)


def _inproj_kernel(x_ref, g_ref, w_ref, wvt_ref, scale_ref,
                   qk_ref, vt_ref, gdn_ref, gla_ref, misc_ref, kmean_ref):
    tm = x_ref.shape[0]
    xn = _rms(x_ref[...], g_ref[...]).astype(BF16)
    qk = _dot(xn, w_ref[:, 0:N_QK])
    k_moba = qk[:, GROUP_WIDTH:2 * GROUP_WIDTH]
    kmean_ref[0] = jnp.mean(k_moba.reshape(tm // MOBA_BLOCK, MOBA_BLOCK, GROUP_WIDTH), axis=1)
    qk_ref[...] = (qk * scale_ref[...]).astype(BF16)
    vt = _dot_nt(wvt_ref[...], xn).astype(BF16)
    for i in range(tm // ATTN_TILE):
        vt_ref[0, i] = vt[:, i * ATTN_TILE:(i + 1) * ATTN_TILE]
    o = N_QK
    gdn_ref[...] = _dot(xn, w_ref[:, o:o + N_GDN])
    o += N_GDN
    gla_ref[...] = _dot(xn, w_ref[:, o:o + N_GLA])
    o += N_GLA
    misc_ref[...] = _dot(xn, w_ref[:, o:o + N_MISC])


def _inproj(h2d, g, layer, w, wvt, scale, tm, seq):
    m, d = h2d.shape
    n_all = w.shape[2]
    per_seq = seq // tm
    const = lambda i: (0, 0)
    return pl.pallas_call(
        _inproj_kernel,
        grid=(m // tm,),
        in_specs=[
            pl.BlockSpec((tm, d), lambda i: (i, 0)),
            pl.BlockSpec((1, d), const),
            pl.BlockSpec((None, d, n_all), lambda i: (layer, 0, 0)),
            pl.BlockSpec((None, N_VT, d), lambda i: (layer, 0, 0)),
            pl.BlockSpec((1, N_QK), const),
        ],
        out_specs=[
            pl.BlockSpec((tm, N_QK), lambda i: (i, 0)),
            pl.BlockSpec((1, tm // ATTN_TILE, N_VT, ATTN_TILE),
                         lambda i: (i // per_seq, i % per_seq, 0, 0)),
            pl.BlockSpec((tm, N_GDN), lambda i: (i, 0)),
            pl.BlockSpec((tm, N_GLA), lambda i: (i, 0)),
            pl.BlockSpec((tm, N_MISC), lambda i: (i, 0)),
            pl.BlockSpec((1, tm // MOBA_BLOCK, GROUP_WIDTH), lambda i: (i, 0, 0)),
        ],
        out_shape=[
            jax.ShapeDtypeStruct((m, N_QK), BF16),
            jax.ShapeDtypeStruct((m // seq, seq // ATTN_TILE, N_VT, ATTN_TILE), BF16),
            jax.ShapeDtypeStruct((m, N_GDN), F32),
            jax.ShapeDtypeStruct((m, N_GLA), F32),
            jax.ShapeDtypeStruct((m, N_MISC), F32),
            jax.ShapeDtypeStruct((m // tm, tm // MOBA_BLOCK, GROUP_WIDTH), F32),
        ],
        compiler_params=pltpu.CompilerParams(
            dimension_semantics=("arbitrary",), vmem_limit_bytes=VMEM_LIMIT),
        name="inproj",
    )(h2d, g, w, wvt, scale)


def _lane_mask(lo, width):
    lane = lax.broadcasted_iota(jnp.int32, (1, LANES), 1)
    return (lane >= lo) & (lane < lo + width)


def _values_with_ones(vt, hh):
    head = vt[HEAD_DIM * hh:HEAD_DIM * (hh + 1), :]
    return jnp.concatenate([head, jnp.ones((ACC_ROWS - HEAD_DIM, vt.shape[1]), vt.dtype)], axis=0)


def _tile_max(scores):
    return [jnp.max(s, axis=0, keepdims=True) for s in scores]


def _softmax_stage(ms, scores, tile_maxes, valids=None):
    ms_new, alphas, ps = [], [], []
    for c, (m, s, m_tile) in enumerate(zip(ms, scores, tile_maxes)):
        if valids is None or valids[c] is None:
            m_new = jnp.maximum(m, m_tile)
            shift = m_new
        else:
            ok = valids[c] > 0.5
            m_new = jnp.maximum(m, jnp.where(ok, m_tile, NEG))
            shift = jnp.where(ok, m_new, -NEG)
        ms_new.append(m_new)
        alphas.append(jnp.exp2(m - m_new))
        ps.append(jnp.exp2(s - shift).astype(BF16))
    return ms_new, alphas, ps


def _value_products(vts, ps):
    return [_dot(vt, p) for vt, p in zip(vts, ps)]


def _attend_all(n, t, n_far, far_scores, far_valid, prev_scores, prev_valid, own_scores, values,
                j_prev, j_own, s_ref, p_ref, acc_ref, first_scores=None):
    chains = range(n)
    n_slots = jnp.maximum(n_far, 2)

    def slot_valid(j):
        exists = jnp.where(j < n_far, 1.0, 0.0)
        rows = far_valid(j)
        if rows is None:
            return [jnp.full((1, t), exists, F32)] * n
        return [row * exists for row in rows]

    s0, s1 = first_scores if first_scores is not None else (far_scores(0), far_scores(1))
    ms, alphas, ps = _softmax_stage([jnp.full((1, t), NEG, F32)] * n, s0, _tile_max(s0),
                                    slot_valid(0))
    tmax = _tile_max(s1)
    for c in chains:
        p_ref[c] = ps[c]
        s_ref[c] = s1[c]
        acc_ref[c] = jnp.zeros((ACC_ROWS, t), F32)

    def steps(first, unroll, carry):
        ms, alphas, tmax = carry
        ps = [p_ref[c] for c in chains]
        s_cur = [s_ref[c] for c in chains]
        for u in range(unroll):
            pv = _value_products(values(first + u), ps)
            s_new = far_scores(first + u + 2)
            tmax_new = _tile_max(s_new)
            ms, alphas_next, ps = _softmax_stage(ms, s_cur, tmax, far_valid(first + u + 1))
            for c in chains:
                acc_ref[c] = alphas[c] * acc_ref[c] + pv[c]
            alphas, s_cur, tmax = alphas_next, s_new, tmax_new
        for c in chains:
            p_ref[c] = ps[c]
            s_ref[c] = s_cur[c]
        return ms, alphas, tmax

    n_steps = n_slots - 2
    unroll = max(1, PIPE_CHAIN_TILES // n)
    n_blocks = n_steps // unroll
    carry = lax.fori_loop(0, n_blocks, lambda i, cr: steps(i * unroll, unroll, cr),
                          (ms, alphas, tmax))
    if unroll > 1:
        carry = lax.fori_loop(n_blocks * unroll, n_steps, lambda i, cr: steps(i, 1, cr), carry)
    ms, alphas, tmax = carry

    pv = _value_products(values(n_slots - 2), [p_ref[c] for c in chains])
    s_prev = prev_scores()
    tmax_prev = _tile_max(s_prev)
    ms, alphas1, ps1 = _softmax_stage(ms, [s_ref[c] for c in chains], tmax,
                                      slot_valid(n_slots - 1))
    accs = [alphas[c] * acc_ref[c] + pv[c] for c in chains]
    pv = _value_products(values(n_slots - 1), ps1)
    s_own = own_scores()
    tmax_own = _tile_max(s_own)
    ms, alphas2, ps2 = _softmax_stage(ms, s_prev, tmax_prev, prev_valid())
    accs = [alphas1[c] * accs[c] + pv[c] for c in chains]
    pv = _value_products(values(j_prev), ps2)
    ms, alphas3, ps3 = _softmax_stage(ms, s_own, tmax_own)
    accs = [alphas2[c] * accs[c] + pv[c] for c in chains]
    pv = _value_products(values(j_own), ps3)
    return [alphas3[c] * accs[c] + pv[c] for c in chains]


def _normalised(acc):
    return acc[0:HEAD_DIM, :] / acc[HEAD_DIM:HEAD_DIM + 1, :]


def _moba_kernel(q_ref, k_ref, vt_ref, kmean_ref, bias_ref, o_ref, sel_ref, s_ref, p_ref, acc_ref):
    t = ATTN_TILE
    qi = pl.program_id(2)
    nblk = kmean_ref.shape[1]
    heads = range(q_ref.shape[2] // HEAD_DIM)
    q = q_ref[0]
    kmean = kmean_ref[0].astype(BF16)

    def pair_lanes(x, h):
        lo = LANES * (h // 2)
        return x[:, lo:lo + LANES]

    qms = []
    for h in heads:
        qp = pair_lanes(q, h)
        qms.append(jnp.where(_lane_mask(HEAD_DIM * (h % 2), HEAD_DIM), qp, jnp.zeros_like(qp)))

    def keys(j):
        return k_ref[0, pl.ds(pl.multiple_of(j * t, t), t), :]

    def far_scores(j):
        k = keys(j)
        return [_dot_nt(pair_lanes(k, h), qms[h]) for h in heads]

    gates = [_dot_nt(pair_lanes(kmean, h), qms[h]) for h in heads]
    first_scores = far_scores(0), far_scores(1)

    row = lax.broadcasted_iota(jnp.int32, (nblk, t), 0)
    past = row < qi
    for h in heads:
        gate = jnp.where(past, gates[h], -jnp.inf)
        sel_t = jnp.zeros((nblk, t), F32)
        for _ in range(MOBA_TOPK):
            top = jnp.max(gate, axis=0, keepdims=True)
            first = jnp.min(jnp.where(gate == top, row, nblk), axis=0, keepdims=True)
            pick = row == first
            sel_t = jnp.where(pick, 1.0, sel_t)
            gate = jnp.where(pick, -jnp.inf, gate)
        sel_t = jnp.where(past, sel_t, 0.0)
        for j in range(nblk):
            sel_ref[h * nblk + j] = sel_t[j:j + 1, :]

    def values(j):
        vt = vt_ref[0, j]
        return [_values_with_ones(vt, h) for h in heads]

    def far_valid(j):
        return [sel_ref[h * nblk + j] for h in heads]

    jp = jnp.maximum(qi - 1, 0)

    def prev_scores():
        k = keys(jp)
        return [_dot_nt(pair_lanes(k, h), qms[h]) + bias_ref[h, 1] for h in heads]

    def own_scores():
        k = keys(qi)
        return [_dot_nt(pair_lanes(k, h), qms[h]) + bias_ref[h, 0] for h in heads]

    accs = _attend_all(len(heads), t, jnp.maximum(qi - 1, 0), far_scores, far_valid, prev_scores,
                       lambda: far_valid(jp), own_scores, values, jp, qi, s_ref, p_ref, acc_ref,
                       first_scores)
    out_t = jnp.concatenate([_normalised(acc) for acc in accs], axis=0)
    o_ref[0] = out_t.T.astype(o_ref.dtype)


def _attn_specs(s, t, mixer, width):
    nkv = s // t
    per = GROUP_WIDTH // width
    qcol = mixer * 2 * per
    kcol = qcol + per
    vrow = mixer * per
    return [
        pl.BlockSpec((1, t, width), lambda bi, g, qi: (bi, qi, qcol + g)),
        pl.BlockSpec((1, s, width), lambda bi, g, qi: (bi, 0, kcol + g)),
        pl.BlockSpec((1, nkv, width, t), lambda bi, g, qi: (bi, 0, vrow + g, 0)),
    ]


def _attn_scratch(n_chains, t):
    return [pltpu.VMEM((n_chains, t, t), F32),
            pltpu.VMEM((n_chains, t, t), BF16),
            pltpu.VMEM((n_chains, ACC_ROWS, t), F32)]


def _moba(qk, vt, kmean, bias_t):
    b, s, _ = qk.shape
    t = ATTN_TILE
    nq = s // t
    nblk = kmean.shape[1]
    n_heads = N_HEADS_G
    width = n_heads * HEAD_DIM
    return pl.pallas_call(
        _moba_kernel,
        grid=(b, GROUP_WIDTH // width, nq),
        in_specs=_attn_specs(s, t, 0, width) + [
            pl.BlockSpec((1, nblk, width), lambda bi, g, qi: (bi, 0, g)),
            pl.BlockSpec((n_heads, 2, t, t), lambda bi, g, qi: (g, 0, 0, 0)),
        ],
        out_specs=pl.BlockSpec((1, t, width), lambda bi, g, qi: (bi, qi, g)),
        out_shape=jax.ShapeDtypeStruct((b, s, GROUP_WIDTH), BF16),
        scratch_shapes=[pltpu.VMEM((n_heads * nblk, 1, t), F32)]
        + _attn_scratch(n_heads, t),
        compiler_params=pltpu.CompilerParams(
            dimension_semantics=("arbitrary", "arbitrary", "arbitrary"),
            vmem_limit_bytes=VMEM_LIMIT),
        name="moba",
    )(qk, qk, vt, kmean, bias_t)


def _diff_kernel(lam_ref, q_ref, k_ref, vt_ref, bias_ref, gnorm_ref, o_ref, s_ref, p_ref, acc_ref):
    t = ATTN_TILE
    qi = pl.program_id(2)
    heads = range(q_ref.shape[2] // HEAD_DIM)
    chains = [(h, mm) for h in heads for mm in range(2)]
    q = q_ref[0]

    def pair_lanes(x, h):
        lo = LANES * (h // 2)
        return x[:, lo:lo + LANES]

    qms = []
    for h, mm in chains:
        qp = pair_lanes(q, h)
        mask = _lane_mask(HEAD_DIM * (h % 2) + DIFF_QK_DIM * mm, DIFF_QK_DIM)
        qms.append(jnp.where(mask, qp, jnp.zeros_like(qp)))

    def keys(j):
        return k_ref[0, pl.ds(pl.multiple_of(j * t, t), t), :]

    def values(j):
        vt = vt_ref[0, j]
        vts = [_values_with_ones(vt, h) for h in heads]
        return [vts[h] for h, _ in chains]

    def far_scores(j):
        k = keys(j)
        return [_dot_nt(pair_lanes(k, h), qm) for (h, _), qm in zip(chains, qms)]

    jp = jnp.maximum(qi - 1, 0)

    def prev_scores():
        return [s + bias_ref[h, 1] for (h, _), s in zip(chains, far_scores(jp))]

    def prev_valid():
        return [jnp.full((1, t), jnp.where(qi >= 1, 1.0, 0.0), F32)] * len(chains)

    def own_scores():
        return [s + bias_ref[h, 0] for (h, _), s in zip(chains, far_scores(qi))]

    lam_p = lam_ref[...]
    lam_init = lam_p[4:5, 0:1]
    lam = (jnp.exp(jnp.sum(lam_p[0:1] * lam_p[1:2], axis=-1, keepdims=True))
           - jnp.exp(jnp.sum(lam_p[2:3] * lam_p[3:4], axis=-1, keepdims=True)) + lam_init)

    accs = _attend_all(len(chains), t, jnp.maximum(qi - 1, 0), far_scores, lambda j: None,
                       prev_scores, prev_valid, own_scores, values, jp, qi, s_ref, p_ref, acc_ref)
    outs = []
    for h in heads:
        o = _normalised(accs[2 * h]) - lam * _normalised(accs[2 * h + 1])
        ms = jnp.mean(o * o, axis=0, keepdims=True)
        outs.append(o * lax.rsqrt(ms + EPS))
    y_t = jnp.concatenate(outs, axis=0)
    o_ref[0] = (y_t.T * gnorm_ref[...] * (1.0 - lam_init)).astype(o_ref.dtype)


def _diff(qk, vt, lam_p, bias_t, gnorm):
    b, s, _ = qk.shape
    t = ATTN_TILE
    nq = s // t
    n_heads = N_HEADS_G
    width = n_heads * HEAD_DIM
    return pl.pallas_call(
        _diff_kernel,
        grid=(b, GROUP_WIDTH // width, nq),
        in_specs=[pl.BlockSpec(lam_p.shape, lambda bi, g, qi: (0, 0))]
        + _attn_specs(s, t, 1, width) + [
            pl.BlockSpec((n_heads, 2, t, t), lambda bi, g, qi: (g, 0, 0, 0)),
            pl.BlockSpec((1, width), lambda bi, g, qi: (0, g)),
        ],
        out_specs=pl.BlockSpec((1, t, width), lambda bi, g, qi: (bi, qi, g)),
        out_shape=jax.ShapeDtypeStruct((b, s, GROUP_WIDTH), BF16),
        scratch_shapes=_attn_scratch(2 * n_heads, t),
        compiler_params=pltpu.CompilerParams(
            dimension_semantics=("arbitrary", "arbitrary", "arbitrary"),
            vmem_limit_bytes=VMEM_LIMIT),
        name="diff_attn",
    )(lam_p, qk, qk, vt, bias_t, gnorm)


def _head_of_lane(n_lanes, width):
    return lax.broadcasted_iota(jnp.int32, (1, n_lanes), 1) // width


def _block_rows(x, lane_head, n_heads=N_HEADS_G):
    zero = jnp.zeros_like(x)
    return jnp.concatenate([jnp.where(lane_head == h, x, zero) for h in range(n_heads)], axis=0)


def _gdn_kernel(x_ref, misc_ref, conv_ref, hp_ref, gnorm_ref, o_ref, s_ref, tail_ref, g_ref, beta_ref):
    L = CHUNK
    W = GROUP_WIDTH
    rows = range(x_ref.shape[0])
    seq = x_ref.shape[1]
    n_chunks = seq // L
    head_w = _head_of_lane(W, HEAD_DIM)
    ri = lax.broadcasted_iota(jnp.int32, (L, W), 0)
    cj = lax.broadcasted_iota(jnp.int32, (L, W), 1) % HEAD_DIM
    lower = ri >= cj
    strict = ri > cj
    ident = jnp.where(ri == cj, 1.0, 0.0)
    ones_blk = (lax.broadcasted_iota(jnp.int32, (W, W), 0) // HEAD_DIM
                == lax.broadcasted_iota(jnp.int32, (W, W), 1) // HEAD_DIM)
    ones_seg = ones_blk.astype(BF16)
    tril = (lax.broadcasted_iota(jnp.int32, (L, L), 0)
            >= lax.broadcasted_iota(jnp.int32, (L, L), 1)).astype(BF16)

    hp = hp_ref[...]
    exp_g = (lax.broadcasted_iota(jnp.int32, (LANES, W), 0) - MISC_A
             == lax.broadcasted_iota(jnp.int32, (LANES, W), 1) // HEAD_DIM).astype(BF16)
    exp_b = (lax.broadcasted_iota(jnp.int32, (LANES, W), 0) - MISC_B
             == lax.broadcasted_iota(jnp.int32, (LANES, W), 1) // HEAD_DIM).astype(BF16)
    for r in rows:
        misc = misc_ref[r]
        sp_in = misc + hp[1:2]
        softplus = jnp.maximum(sp_in, 0.0) + jnp.log(1.0 + jnp.exp(-jnp.abs(sp_in)))
        g_tok = -jnp.exp(hp[0:1]) * softplus
        b_tok = _sigmoid(misc)
        g_ref[r] = _dot_exact_rhs(g_tok, exp_g, 2)
        beta_ref[r] = _dot_exact_rhs(b_tok, exp_b, 2)

    @pl.when(pl.program_id(1) == 0)
    def _():
        s_ref[...] = jnp.zeros_like(s_ref)
        tail_ref[...] = jnp.zeros_like(tail_ref)

    cw = conv_ref[...]
    blk = lambda a: _block_rows(a.astype(BF16), head_w)
    bf = lambda a: a.astype(BF16)

    def prepare(r, r0):
        x = x_ref[r, pl.ds(r0, L), :]
        qkv = x[:, 0:3 * W]
        xx = jnp.concatenate([tail_ref[r], qkv], axis=0)
        tail_ref[r] = qkv[L - F32_SUBLANES:L, :]
        conv = cw[CONV_WIDTH - 1:CONV_WIDTH] * qkv
        for i in range(CONV_WIDTH - 1):
            lo = F32_SUBLANES - (CONV_WIDTH - 1) + i
            conv = conv + cw[i:i + 1] * xx[lo:lo + L, :]
        conv = _silu(conv)
        return conv[:, 0:W], conv[:, W:2 * W], conv[:, 2 * W:3 * W], x[:, 3 * W:4 * W]

    def decays(gc):
        gc_col = jnp.sum(gc * ident, axis=0, keepdims=True)
        return jnp.exp(jnp.where(lower, gc - gc_col, -jnp.inf))

    def chunk(c, _):
        r0 = pl.multiple_of(c * L, L)
        q, k, v, z = zip(*[prepare(r, r0) for r in rows])
        ssq = [_dot_exact_rhs(jnp.concatenate([q[r] * q[r], k[r] * k[r]], axis=0), ones_seg, 1)
               for r in rows]
        q = [q[r] * lax.rsqrt(ssq[r][0:L] + EPS) * (HEAD_DIM ** -0.5) for r in rows]
        k = [k[r] * lax.rsqrt(ssq[r][L:2 * L] + EPS) for r in rows]
        beta = [beta_ref[r, pl.ds(r0, L), :] for r in rows]
        gc = [_dot_exact_lhs(tril, g_ref[r, pl.ds(r0, L), :], 2) for r in rows]
        egc = [jnp.exp(gc[r]) for r in rows]
        gc_last = [gc[r][L - 1:L, :] for r in rows]
        decay = [decays(gc[r]) for r in rows]
        kb = [k[r] * beta[r] for r in rows]
        both = [_dot_nt(bf(jnp.concatenate([kb[r], q[r]], axis=0)), blk(k[r])) for r in rows]
        qk = [jnp.where(lower, both[r][L:2 * L] * decay[r], 0.0) for r in rows]

        p = [jnp.where(strict, -both[r][0:L] * decay[r], 0.0) for r in rows]
        t_inv = [ident + p[r] for r in rows]
        p = [_dot(bf(p[r]), blk(p[r])) for r in rows]
        for _ in range(4):
            prod = [_dot(bf(jnp.concatenate([t_inv[r], p[r]], axis=0)), blk(p[r])) for r in rows]
            t_inv = [t_inv[r] + prod[r][0:L] for r in rows]
            p = [prod[r][L:2 * L] for r in rows]
        t_inv = [t_inv[r] + _dot(bf(t_inv[r]), blk(p[r])) for r in rows]

        u = [_dot(bf(t_inv[r]), blk(v[r] * beta[r])) for r in rows]
        w = [_dot(bf(t_inv[r]), blk(kb[r] * egc[r])) for r in rows]

        state = [s_ref[r] for r in rows]
        ws_qs = [_dot(bf(jnp.concatenate([w[r], q[r] * egc[r]], axis=0)), bf(state[r])) for r in rows]
        v_new = [u[r] - ws_qs[r][0:L] for r in rows]
        o = [ws_qs[r][L:2 * L] + _dot(bf(qk[r]), blk(v_new[r])) for r in rows]
        upd = [_dot_tn(bf(k[r] * jnp.exp(gc_last[r] - gc[r])), bf(v_new[r])) for r in rows]
        for r in rows:
            s_ref[r] = state[r] * jnp.exp(gc_last[r]) + jnp.where(ones_blk, upd[r], 0.0)

        ms = [_dot_exact_rhs(o[r] * o[r], ones_seg, 1) * (1.0 / HEAD_DIM) for r in rows]
        for r in rows:
            y = o[r] * lax.rsqrt(ms[r] + EPS) * gnorm_ref[...] * _silu(z[r])
            o_ref[r, pl.ds(r0, L), :] = y.astype(o_ref.dtype)
        return 0

    lax.fori_loop(0, n_chunks, chunk, 0)


def _gdn(gdn, misc, conv_w, hp, gnorm):
    b, s, _ = gdn.shape
    ts = min(s, REC_SEQ_TILE)
    nr = math.gcd(b, REC_ROWS)
    const = lambda bi, si: (0, 0)
    return pl.pallas_call(
        _gdn_kernel,
        grid=(b // nr, s // ts),
        in_specs=[
            pl.BlockSpec((nr, ts, N_GDN), lambda bi, si: (bi, si, 0)),
            pl.BlockSpec((nr, ts, N_MISC), lambda bi, si: (bi, si, 0)),
            pl.BlockSpec(conv_w.shape, const),
            pl.BlockSpec(hp.shape, const),
            pl.BlockSpec((1, GROUP_WIDTH), const),
        ],
        out_specs=pl.BlockSpec((nr, ts, GROUP_WIDTH), lambda bi, si: (bi, si, 0)),
        out_shape=jax.ShapeDtypeStruct((b, s, GROUP_WIDTH), BF16),
        scratch_shapes=[
            pltpu.VMEM((nr, GROUP_WIDTH, GROUP_WIDTH), F32),
            pltpu.VMEM((nr, F32_SUBLANES, 3 * GROUP_WIDTH), F32),
            pltpu.VMEM((nr, ts, GROUP_WIDTH), F32),
            pltpu.VMEM((nr, ts, GROUP_WIDTH), F32),
        ],
        compiler_params=pltpu.CompilerParams(
            dimension_semantics=("arbitrary", "arbitrary"), vmem_limit_bytes=VMEM_LIMIT),
        name="gdn",
    )(gdn, misc, conv_w, hp, gnorm)


def _gla_kernel(x_ref, misc_ref, walpha_ref, balpha_ref, gnorm_ref, o_ref, s_ref, la_ref):
    L = CHUNK
    W = GROUP_WIDTH
    KW = N_HEADS_G * GLA_DK
    rows = range(x_ref.shape[0])
    seq = x_ref.shape[1]
    n_chunks = seq // L
    head_k = _head_of_lane(KW, GLA_DK)
    head_v = _head_of_lane(W, HEAD_DIM)
    ri = lax.broadcasted_iota(jnp.int32, (L, W), 0)
    cj = lax.broadcasted_iota(jnp.int32, (L, W), 1) % HEAD_DIM
    lower = ri >= cj
    ones_blk = (lax.broadcasted_iota(jnp.int32, (W, W), 0) // HEAD_DIM
                == lax.broadcasted_iota(jnp.int32, (W, W), 1) // HEAD_DIM).astype(BF16)
    state_mask = (lax.broadcasted_iota(jnp.int32, (W, KW), 0) // HEAD_DIM
                  == lax.broadcasted_iota(jnp.int32, (W, KW), 1) // GLA_DK)
    tril = (lax.broadcasted_iota(jnp.int32, (L, L), 0)
            >= lax.broadcasted_iota(jnp.int32, (L, L), 1)).astype(BF16)

    for r in rows:
        pre = _dot(misc_ref[r].astype(BF16), walpha_ref[...]) + balpha_ref[...]
        log_sig = jnp.minimum(pre, 0.0) - jnp.log(1.0 + jnp.exp(-jnp.abs(pre)))
        la_ref[r] = log_sig * (1.0 / GLA_TAU)

    @pl.when(pl.program_id(1) == 0)
    def _():
        s_ref[...] = jnp.zeros_like(s_ref)

    bf = lambda a: a.astype(BF16)

    def chunk(c, _):
        r0 = pl.multiple_of(c * L, L)
        x = [x_ref[r, pl.ds(r0, L), :] for r in rows]
        k = [x[r][:, KW:2 * KW] for r in rows]
        v = [bf(x[r][:, 2 * KW:2 * KW + W]) for r in rows]
        bc = [_dot_exact_lhs(tril, la_ref[r, pl.ds(r0, L), :], 2) for r in rows]
        b_last = [bc[r][L - 1:L, :] for r in rows]
        q = [x[r][:, 0:KW] * (GLA_DK ** -0.5) for r in rows]
        qe = [bf(q[r] * jnp.exp(bc[r])) for r in rows]
        b_mid = [bc[r] - bc[r][L // 2:L // 2 + 1, :] for r in rows]
        qm = [bf(q[r] * jnp.exp(b_mid[r])) for r in rows]
        km = [bf(k[r] * jnp.exp(-b_mid[r])) for r in rows]
        a_mat = [jnp.where(lower, _dot_nt(qm[r], _block_rows(km[r], head_k)), 0.0) for r in rows]
        state_t = [s_ref[r] for r in rows]
        o = [_dot_nt(qe[r], bf(state_t[r])) for r in rows]
        o = [o[r] + _dot(bf(a_mat[r]), _block_rows(v[r], head_v)) for r in rows]
        upd = [_dot_tn(v[r], bf(k[r] * jnp.exp(b_last[r] - bc[r]))) for r in rows]
        for r in rows:
            s_ref[r] = state_t[r] * jnp.exp(b_last[r]) + jnp.where(state_mask, upd[r], 0.0)

        ms = [_dot_exact_rhs(o[r] * o[r], ones_blk, 1) * (1.0 / HEAD_DIM) for r in rows]
        for r in rows:
            gate = _silu(x[r][:, 2 * KW + W:2 * KW + 2 * W])
            y = o[r] * lax.rsqrt(ms[r] + EPS) * gnorm_ref[...] * gate
            o_ref[r, pl.ds(r0, L), :] = y.astype(o_ref.dtype)
        return 0

    lax.fori_loop(0, n_chunks, chunk, 0)


def _gla(gla, misc, walpha, balpha, gnorm):
    b, s, _ = gla.shape
    ts = min(s, REC_SEQ_TILE)
    nr = math.gcd(b, REC_ROWS)
    const = lambda bi, si: (0, 0)
    return pl.pallas_call(
        _gla_kernel,
        grid=(b // nr, s // ts),
        in_specs=[
            pl.BlockSpec((nr, ts, N_GLA), lambda bi, si: (bi, si, 0)),
            pl.BlockSpec((nr, ts, N_MISC), lambda bi, si: (bi, si, 0)),
            pl.BlockSpec(walpha.shape, const),
            pl.BlockSpec(balpha.shape, const),
            pl.BlockSpec((1, GROUP_WIDTH), const),
        ],
        out_specs=pl.BlockSpec((nr, ts, GROUP_WIDTH), lambda bi, si: (bi, si, 0)),
        out_shape=jax.ShapeDtypeStruct((b, s, GROUP_WIDTH), BF16),
        scratch_shapes=[
            pltpu.VMEM((nr, GROUP_WIDTH, N_HEADS_G * GLA_DK), F32),
            pltpu.VMEM((nr, ts, N_HEADS_G * GLA_DK), F32),
        ],
        compiler_params=pltpu.CompilerParams(
            dimension_semantics=("arbitrary", "arbitrary"), vmem_limit_bytes=VMEM_LIMIT),
        name="gla",
    )(gla, misc, walpha, balpha, gnorm)


def _ffn_chunks(d_ff):
    step = 4 * GROUP_WIDTH
    return [(lo, min(step, d_ff - lo)) for lo in range(0, d_ff, step)]


def _mlp_kernel(h_ref, ya_ref, yb_ref, yc_ref, yd_ref, wout_ref, gffn_ref, wg_ref, wu_ref, wd_ref,
                gple_ref, wpg_ref, p_ref, wpp_ref, gfin_ref, o_ref, act_ref, *, final):
    mixed = jnp.concatenate([ya_ref[...], yb_ref[...], yc_ref[...], yd_ref[...]], axis=-1)
    h = h_ref[...] + _dot(mixed, wout_ref[...])
    hn = _rms(h, gffn_ref[...]).astype(BF16)
    for lo, width in _ffn_chunks(wg_ref.shape[1]):
        gate = _dot(hn, wg_ref[:, lo:lo + width])
        up = _dot(hn, wu_ref[:, lo:lo + width])
        act_ref[:, lo:lo + width] = (_silu(gate) * up).astype(BF16)
    h = h + _dot(act_ref[...], wd_ref[...])
    gate = _sigmoid(_dot(_rms(h, gple_ref[...]).astype(BF16), wpg_ref[...]))
    h = h + gate * _dot(p_ref[...].astype(BF16), wpp_ref[...])
    if final:
        h = _rms(h, gfin_ref[...])
    o_ref[...] = h


def _mlp(h2d, ys, layer, wout, gffn, wg, wu, wd, gple, wpg, p3d, wpp, gfin, tm, final):
    m, d = h2d.shape
    d_ff = wg.shape[2]
    const = lambda i: (0, 0)
    resident = lambda shape: pl.BlockSpec(shape, const, pipeline_mode=pl.Buffered(1))
    stacked = lambda a: pl.BlockSpec((None,) + a.shape[1:], lambda i: (layer, 0, 0),
                                     pipeline_mode=pl.Buffered(1))
    rows = lambda width: pl.BlockSpec((tm, width), lambda i: (i, 0))
    return pl.pallas_call(
        functools.partial(_mlp_kernel, final=final),
        grid=(m // tm,),
        in_specs=[rows(d)] + [rows(GROUP_WIDTH)] * 4 + [
            stacked(wout), resident(gffn.shape), stacked(wg), stacked(wu),
            stacked(wd), resident(gple.shape), stacked(wpg),
            pl.BlockSpec((None, tm, p3d.shape[2]), lambda i: (layer, i, 0)), stacked(wpp),
            resident(gfin.shape),
        ],
        out_specs=rows(d),
        out_shape=jax.ShapeDtypeStruct((m, d), F32),
        scratch_shapes=[pltpu.VMEM((tm, d_ff), BF16)],
        compiler_params=pltpu.CompilerParams(
            dimension_semantics=("arbitrary",), vmem_limit_bytes=VMEM_LIMIT),
        name="mlp",
    )(h2d, *ys, wout, gffn, wg, wu, wd, gple, wpg, p3d, wpp, gfin)


def _rel_bucket_table(n_dist):
    n = np.arange(n_dist)
    nf = np.maximum(n, 1).astype(np.float32)
    large = REL_MAX_EXACT + (np.log(nf / REL_MAX_EXACT) / math.log(REL_MAX_DIST / REL_MAX_EXACT)
                             * (N_REL_BUCKETS - REL_MAX_EXACT)).astype(np.int32)
    large = np.minimum(large, N_REL_BUCKETS - 1)
    return np.where(n < REL_MAX_EXACT, n, large)


def _bias_tiles(rel_bias):
    t = ATTN_TILE
    bucket = _rel_bucket_table(2 * t)
    assert (bucket[t + 1:] == N_REL_BUCKETS - 1).all()
    per_dist = (rel_bias[bucket, :] - rel_bias[N_REL_BUCKETS - 1][None, :]).T * LOG2E
    own = jnp.concatenate([per_dist[:, :t], jnp.full_like(per_dist[:, :t], NEG)], axis=1)
    prev = jnp.concatenate([per_dist[:, t:], per_dist[:, :t]], axis=1)
    vec = jnp.stack([own, prev], axis=1)
    rolled = jnp.tile(vec, (1, 1, t))[:, :, :t * (2 * t - 1)].reshape(-1, 2, t, 2 * t - 1)
    return rolled[:, :, :, :t]


def _row(v, width=None):
    v = v.astype(F32).reshape(1, -1)
    if width is not None and v.shape[1] < width:
        v = jnp.pad(v, ((0, 0), (0, width - v.shape[1])))
    return v


def kernel(x, p, norm_mix, w_in, rel_bias, diff_lambda, diff_norm, gdn_conv, gdn_a_log, gdn_dt_bias,
           gdn_norm, gla_w_alpha, gla_b_alpha, gla_norm, w_out, norm_ffn, w_gate, w_up, w_down,
           norm_ple, w_ple_gate, w_ple_proj, final_norm):
    b, s, d = x.shape
    depth = w_in.shape[0]
    m = b * s
    tm = ROW_TILE
    hg = N_HEADS_G
    assert ATTN_TILE == MOBA_BLOCK
    assert s % tm == 0 and tm % ATTN_TILE == 0 and s >= 2 * ATTN_TILE
    assert s % REC_SEQ_TILE == 0 and REC_SEQ_TILE % CHUNK == 0

    bias_t = _bias_tiles(rel_bias.astype(F32))
    bias_a, bias_b = bias_t[:hg], bias_t[hg:]
    gw = GROUP_WIDTH
    col_scale = np.ones((1, N_QK), np.float32)
    col_scale[:, 0:gw] = HEAD_DIM ** -0.5
    col_scale[:, gw:2 * gw] = LOG2E
    col_scale[:, 2 * gw:3 * gw] = DIFF_QK_DIM ** -0.5 * LOG2E
    col_scale = jnp.asarray(col_scale)

    n_main = 6 * gw + N_GDN
    ab = 2 * hg
    gla_lo = n_main + ab
    gla_hi = gla_lo + N_GLA
    fin = _row(final_norm)
    w_out_bf, w_gate_bf, w_up_bf, w_down_bf, w_ple_gate_bf, w_ple_proj_bf = (
        w.astype(BF16) for w in (w_out, w_gate, w_up, w_down, w_ple_gate, w_ple_proj))
    p3d = p.reshape(depth, m, p.shape[-1])
    w1 = jnp.concatenate(
        [w_in[:, :, 0:2 * gw], w_in[:, :, 3 * gw:5 * gw], w_in[:, :, 6 * gw:n_main],
         w_in[:, :, gla_lo:gla_hi], w_in[:, :, n_main:gla_lo], w_in[:, :, gla_hi:],
         jnp.zeros((depth, d, N_MISC - ab - GLA_GATE_RANK), w_in.dtype)], axis=2).astype(BF16)
    wvt = jnp.swapaxes(jnp.concatenate([w_in[:, :, 2 * gw:3 * gw], w_in[:, :, 5 * gw:6 * gw]],
                                       axis=2), 1, 2).astype(BF16)
    h = x.reshape(m, d)
    for l in range(depth):
        qk, vt, gdn, gla, misc, kmean = _inproj(h, _row(norm_mix[l]), l, w1, wvt, col_scale, tm, s)
        qk = qk.reshape(b, s, N_QK)
        misc = misc.reshape(b, s, N_MISC)
        kmean = kmean.reshape(b, s // MOBA_BLOCK, GROUP_WIDTH)

        y_a = _moba(qk, vt, kmean, bias_a)

        lam_init = 0.8 - 0.6 * math.exp(-0.3 * l)
        lam_p = jnp.concatenate([diff_lambda[l].astype(F32),
                                 jnp.full((1, DIFF_QK_DIM), lam_init, F32)], axis=0)
        y_b = _diff(qk, vt, lam_p, bias_b, _row(jnp.tile(diff_norm[l], hg)))

        hp = jnp.concatenate([_row(gdn_a_log[l], LANES), _row(gdn_dt_bias[l], LANES)], axis=0)
        y_c = _gdn(gdn.reshape(b, s, N_GDN), misc, gdn_conv[l].astype(F32), hp,
                   _row(jnp.tile(gdn_norm[l], hg)))

        walpha = jnp.zeros((N_MISC, hg * GLA_DK), F32).at[MISC_LR:MISC_LR + GLA_GATE_RANK].set(
            gla_w_alpha[l]).astype(BF16)
        y_d = _gla(gla.reshape(b, s, N_GLA), misc, walpha, _row(gla_b_alpha[l]),
                   _row(jnp.tile(gla_norm[l], hg)))

        ys = [y.reshape(m, GROUP_WIDTH) for y in (y_a, y_b, y_c, y_d)]
        h = _mlp(h, ys, l, w_out_bf, _row(norm_ffn[l]), w_gate_bf, w_up_bf, w_down_bf,
                 _row(norm_ple[l]), w_ple_gate_bf, p3d, w_ple_proj_bf, fin, tm,
                 final=(l == depth - 1))
    return h.reshape(b, s, d)
```

```python
import functools
import math

import numpy as np
import jax
import jax.numpy as jnp
from jax import lax
from jax.experimental import pallas as pl
from jax.experimental.pallas import tpu as pltpu

F32 = jnp.float32
BF16 = jnp.bfloat16

HEAD_DIM = 64
N_HEADS_G = 4
GROUP_WIDTH = HEAD_DIM * N_HEADS_G
MOBA_BLOCK = 256
MOBA_TOPK = 3
N_REL_BUCKETS = 32
REL_MAX_EXACT = 16
REL_MAX_DIST = 128
DIFF_QK_DIM = HEAD_DIM // 2
CONV_WIDTH = 4
CHUNK = 64
GLA_DK = HEAD_DIM // 2
GLA_GATE_RANK = 16
GLA_TAU = 16.0
EPS = 1e-6

LANES = 128
F32_SUBLANES = 8
BF16_SUBLANES = 16
VMEM_LIMIT = 56 * 1024 * 1024
ROW_TILE = 512
ATTN_TILE = 256
ACC_ROWS = HEAD_DIM + BF16_SUBLANES
PIPE_CHAIN_TILES = 8
REC_SEQ_TILE = 256
REC_ROWS = 8
NEG = -1e30
LOG2E = math.log2(math.e)

N_QK = 4 * GROUP_WIDTH
N_VT = 2 * GROUP_WIDTH
N_GDN = 4 * GROUP_WIDTH
N_GLA = 2 * N_HEADS_G * GLA_DK + 2 * GROUP_WIDTH
N_MISC = LANES
MISC_A, MISC_B, MISC_LR = 0, N_HEADS_G, 2 * N_HEADS_G

_NT = (((1,), (1,)), ((), ()))
_TN = (((0,), (0,)), ((), ()))


def _dot(a, b):
    return jnp.dot(a, b, preferred_element_type=F32)


def _dot_nt(a, b):
    return lax.dot_general(a, b, _NT, preferred_element_type=F32)


def _dot_tn(a, b):
    return lax.dot_general(a, b, _TN, preferred_element_type=F32)


def _split_bf16(x, n):
    parts = []
    r = x
    for _ in range(n):
        hi = r.astype(BF16)
        parts.append(hi)
        r = r - hi.astype(F32)
    return parts


def _dot_exact_rhs(x, w, n):
    acc = None
    for part in _split_bf16(x, n):
        t = _dot(part, w)
        acc = t if acc is None else acc + t
    return acc


def _dot_exact_lhs(w, x, n):
    acc = None
    for part in _split_bf16(x, n):
        t = _dot(w, part)
        acc = t if acc is None else acc + t
    return acc


def _rms(x, g):
    return x * lax.rsqrt(jnp.mean(x * x, axis=-1, keepdims=True) + EPS) * g


def _sigmoid(x):
    return 1.0 / (1.0 + jnp.exp(-x))


def _silu(x):
    return x * _sigmoid(x)


def _inproj_kernel(x_ref, g_ref, w_ref, wvt_ref, scale_ref,
                   qk_ref, vt_ref, gdn_ref, gla_ref, misc_ref, kmean_ref):
    tm = x_ref.shape[0]
    xn = _rms(x_ref[...], g_ref[...]).astype(BF16)
    qk = _dot(xn, w_ref[:, 0:N_QK])
    k_moba = qk[:, GROUP_WIDTH:2 * GROUP_WIDTH]
    kmean_ref[0] = jnp.mean(k_moba.reshape(tm // MOBA_BLOCK, MOBA_BLOCK, GROUP_WIDTH), axis=1)
    qk_ref[...] = (qk * scale_ref[...]).astype(BF16)
    vt = _dot_nt(wvt_ref[...], xn).astype(BF16)
    for i in range(tm // ATTN_TILE):
        vt_ref[0, i] = vt[:, i * ATTN_TILE:(i + 1) * ATTN_TILE]
    o = N_QK
    gdn_ref[...] = _dot(xn, w_ref[:, o:o + N_GDN])
    o += N_GDN
    gla_ref[...] = _dot(xn, w_ref[:, o:o + N_GLA])
    o += N_GLA
    misc_ref[...] = _dot(xn, w_ref[:, o:o + N_MISC])


def _inproj(h2d, g, layer, w, wvt, scale, tm, seq):
    m, d = h2d.shape
    n_all = w.shape[2]
    per_seq = seq // tm
    const = lambda i: (0, 0)
    return pl.pallas_call(
        _inproj_kernel,
        grid=(m // tm,),
        in_specs=[
            pl.BlockSpec((tm, d), lambda i: (i, 0)),
            pl.BlockSpec((1, d), const),
            pl.BlockSpec((None, d, n_all), lambda i: (layer, 0, 0)),
            pl.BlockSpec((None, N_VT, d), lambda i: (layer, 0, 0)),
            pl.BlockSpec((1, N_QK), const),
        ],
        out_specs=[
            pl.BlockSpec((tm, N_QK), lambda i: (i, 0)),
            pl.BlockSpec((1, tm // ATTN_TILE, N_VT, ATTN_TILE),
                         lambda i: (i // per_seq, i % per_seq, 0, 0)),
            pl.BlockSpec((tm, N_GDN), lambda i: (i, 0)),
            pl.BlockSpec((tm, N_GLA), lambda i: (i, 0)),
            pl.BlockSpec((tm, N_MISC), lambda i: (i, 0)),
            pl.BlockSpec((1, tm // MOBA_BLOCK, GROUP_WIDTH), lambda i: (i, 0, 0)),
        ],
        out_shape=[
            jax.ShapeDtypeStruct((m, N_QK), BF16),
            jax.ShapeDtypeStruct((m // seq, seq // ATTN_TILE, N_VT, ATTN_TILE), BF16),
            jax.ShapeDtypeStruct((m, N_GDN), F32),
            jax.ShapeDtypeStruct((m, N_GLA), F32),
            jax.ShapeDtypeStruct((m, N_MISC), F32),
            jax.ShapeDtypeStruct((m // tm, tm // MOBA_BLOCK, GROUP_WIDTH), F32),
        ],
        compiler_params=pltpu.CompilerParams(
            dimension_semantics=("arbitrary",), vmem_limit_bytes=VMEM_LIMIT),
        name="inproj",
    )(h2d, g, w, wvt, scale)


def _lane_mask(lo, width):
    lane = lax.broadcasted_iota(jnp.int32, (1, LANES), 1)
    return (lane >= lo) & (lane < lo + width)


def _values_with_ones(vt, hh):
    head = vt[HEAD_DIM * hh:HEAD_DIM * (hh + 1), :]
    return jnp.concatenate([head, jnp.ones((ACC_ROWS - HEAD_DIM, vt.shape[1]), vt.dtype)], axis=0)


def _tile_max(scores):
    return [jnp.max(s, axis=0, keepdims=True) for s in scores]


def _softmax_stage(ms, scores, tile_maxes, valids=None):
    ms_new, alphas, ps = [], [], []
    for c, (m, s, m_tile) in enumerate(zip(ms, scores, tile_maxes)):
        if valids is None or valids[c] is None:
            m_new = jnp.maximum(m, m_tile)
            shift = m_new
        else:
            ok = valids[c] > 0.5
            m_new = jnp.maximum(m, jnp.where(ok, m_tile, NEG))
            shift = jnp.where(ok, m_new, -NEG)
        ms_new.append(m_new)
        alphas.append(jnp.exp2(m - m_new))
        ps.append(jnp.exp2(s - shift).astype(BF16))
    return ms_new, alphas, ps


def _value_products(vts, ps):
    return [_dot(vt, p) for vt, p in zip(vts, ps)]


def _attend_all(n, t, n_far, far_scores, far_valid, prev_scores, prev_valid, own_scores, values,
                j_prev, j_own, s_ref, p_ref, acc_ref, first_scores=None):
    chains = range(n)
    n_slots = jnp.maximum(n_far, 2)

    def slot_valid(j):
        exists = jnp.where(j < n_far, 1.0, 0.0)
        rows = far_valid(j)
        if rows is None:
            return [jnp.full((1, t), exists, F32)] * n
        return [row * exists for row in rows]

    s0, s1 = first_scores if first_scores is not None else (far_scores(0), far_scores(1))
    ms, alphas, ps = _softmax_stage([jnp.full((1, t), NEG, F32)] * n, s0, _tile_max(s0),
                                    slot_valid(0))
    tmax = _tile_max(s1)
    for c in chains:
        p_ref[c] = ps[c]
        s_ref[c] = s1[c]
        acc_ref[c] = jnp.zeros((ACC_ROWS, t), F32)

    def steps(first, unroll, carry):
        ms, alphas, tmax = carry
        ps = [p_ref[c] for c in chains]
        s_cur = [s_ref[c] for c in chains]
        for u in range(unroll):
            vts = values(first + u)
            late = n // 4 if unroll == 1 else 0
            pv = _value_products(vts[:n - late], ps[:n - late])
            s_new = far_scores(first + u + 2)
            pv += _value_products(vts[n - late:], ps[n - late:])
            tmax_new = _tile_max(s_new)
            ms, alphas_next, ps = _softmax_stage(ms, s_cur, tmax, far_valid(first + u + 1))
            for c in chains:
                acc_ref[c] = alphas[c] * acc_ref[c] + pv[c]
            alphas, s_cur, tmax = alphas_next, s_new, tmax_new
        for c in chains:
            p_ref[c] = ps[c]
            s_ref[c] = s_cur[c]
        return ms, alphas, tmax

    n_steps = n_slots - 2
    unroll = max(1, PIPE_CHAIN_TILES // n)
    n_blocks = n_steps // unroll
    carry = lax.fori_loop(0, n_blocks, lambda i, cr: steps(i * unroll, unroll, cr),
                          (ms, alphas, tmax))
    if unroll > 1:
        carry = lax.fori_loop(n_blocks * unroll, n_steps, lambda i, cr: steps(i, 1, cr), carry)
    ms, alphas, tmax = carry

    pv = _value_products(values(n_slots - 2), [p_ref[c] for c in chains])
    s_prev = prev_scores()
    tmax_prev = _tile_max(s_prev)
    ms, alphas1, ps1 = _softmax_stage(ms, [s_ref[c] for c in chains], tmax,
                                      slot_valid(n_slots - 1))
    accs = [alphas[c] * acc_ref[c] + pv[c] for c in chains]
    pv = _value_products(values(n_slots - 1), ps1)
    s_own = own_scores()
    tmax_own = _tile_max(s_own)
    ms, alphas2, ps2 = _softmax_stage(ms, s_prev, tmax_prev, prev_valid())
    accs = [alphas1[c] * accs[c] + pv[c] for c in chains]
    pv = _value_products(values(j_prev), ps2)
    ms, alphas3, ps3 = _softmax_stage(ms, s_own, tmax_own)
    accs = [alphas2[c] * accs[c] + pv[c] for c in chains]
    pv = _value_products(values(j_own), ps3)
    return [alphas3[c] * accs[c] + pv[c] for c in chains]


def _normalised(acc):
    return acc[0:HEAD_DIM, :] / acc[HEAD_DIM:HEAD_DIM + 1, :]


def _moba_kernel(q_ref, k_ref, vt_ref, kmean_ref, bias_ref, o_ref, sel_ref, s_ref, p_ref, acc_ref):
    t = ATTN_TILE
    qi = pl.program_id(2)
    nblk = kmean_ref.shape[1]
    heads = range(q_ref.shape[2] // HEAD_DIM)
    q = q_ref[0]
    kmean = kmean_ref[0].astype(BF16)

    def pair_lanes(x, h):
        lo = LANES * (h // 2)
        return x[:, lo:lo + LANES]

    qms = []
    for h in heads:
        qp = pair_lanes(q, h)
        qms.append(jnp.where(_lane_mask(HEAD_DIM * (h % 2), HEAD_DIM), qp, jnp.zeros_like(qp)))

    def keys(j):
        return k_ref[0, pl.ds(pl.multiple_of(j * t, t), t), :]

    def far_scores(j):
        k = keys(j)
        return [_dot_nt(pair_lanes(k, h), qms[h]) for h in heads]

    gates = [_dot_nt(pair_lanes(kmean, h), qms[h]) for h in heads]
    first_scores = far_scores(0), far_scores(1)

    row = lax.broadcasted_iota(jnp.int32, (nblk, t), 0)
    past = row < qi
    for h in heads:
        gate = jnp.where(past, gates[h], -jnp.inf)
        sel_t = jnp.zeros((nblk, t), F32)
        for _ in range(MOBA_TOPK):
            top = jnp.max(gate, axis=0, keepdims=True)
            first = jnp.min(jnp.where(gate == top, row, nblk), axis=0, keepdims=True)
            pick = row == first
            sel_t = jnp.where(pick, 1.0, sel_t)
            gate = jnp.where(pick, -jnp.inf, gate)
        sel_t = jnp.where(past, sel_t, 0.0)
        for j in range(nblk):
            sel_ref[h * nblk + j] = sel_t[j:j + 1, :]

    def values(j):
        vt = vt_ref[0, j]
        return [_values_with_ones(vt, h) for h in heads]

    def far_valid(j):
        return [sel_ref[h * nblk + j] for h in heads]

    jp = jnp.maximum(qi - 1, 0)

    def prev_scores():
        k = keys(jp)
        return [_dot_nt(pair_lanes(k, h), qms[h]) + bias_ref[h, 1] for h in heads]

    def own_scores():
        k = keys(qi)
        return [_dot_nt(pair_lanes(k, h), qms[h]) + bias_ref[h, 0] for h in heads]

    accs = _attend_all(len(heads), t, jnp.maximum(qi - 1, 0), far_scores, far_valid, prev_scores,
                       lambda: far_valid(jp), own_scores, values, jp, qi, s_ref, p_ref, acc_ref,
                       first_scores)
    out_t = jnp.concatenate([_normalised(acc) for acc in accs], axis=0)
    o_ref[0] = out_t.T.astype(o_ref.dtype)


def _attn_specs(s, t, mixer, width):
    nkv = s // t
    per = GROUP_WIDTH // width
    qcol = mixer * 2 * per
    kcol = qcol + per
    vrow = mixer * per
    return [
        pl.BlockSpec((1, t, width), lambda bi, g, qi: (bi, qi, qcol + g)),
        pl.BlockSpec((1, s, width), lambda bi, g, qi: (bi, 0, kcol + g)),
        pl.BlockSpec((1, nkv, width, t), lambda bi, g, qi: (bi, 0, vrow + g, 0)),
    ]


def _attn_scratch(n_chains, t):
    return [pltpu.VMEM((n_chains, t, t), F32),
            pltpu.VMEM((n_chains, t, t), BF16),
            pltpu.VMEM((n_chains, ACC_ROWS, t), F32)]


def _moba(qk, vt, kmean, bias_t):
    b, s, _ = qk.shape
    t = ATTN_TILE
    nq = s // t
    nblk = kmean.shape[1]
    n_heads = N_HEADS_G
    width = n_heads * HEAD_DIM
    return pl.pallas_call(
        _moba_kernel,
        grid=(b, GROUP_WIDTH // width, nq),
        in_specs=_attn_specs(s, t, 0, width) + [
            pl.BlockSpec((1, nblk, width), lambda bi, g, qi: (bi, 0, g)),
            pl.BlockSpec((n_heads, 2, t, t), lambda bi, g, qi: (g, 0, 0, 0)),
        ],
        out_specs=pl.BlockSpec((1, t, width), lambda bi, g, qi: (bi, qi, g)),
        out_shape=jax.ShapeDtypeStruct((b, s, GROUP_WIDTH), BF16),
        scratch_shapes=[pltpu.VMEM((n_heads * nblk, 1, t), F32)]
        + _attn_scratch(n_heads, t),
        compiler_params=pltpu.CompilerParams(
            dimension_semantics=("arbitrary", "arbitrary", "arbitrary"),
            vmem_limit_bytes=VMEM_LIMIT),
        name="moba",
    )(qk, qk, vt, kmean, bias_t)


def _diff_kernel(lam_ref, q_ref, k_ref, vt_ref, bias_ref, gnorm_ref, o_ref, s_ref, p_ref, acc_ref):
    t = ATTN_TILE
    qi = pl.program_id(2)
    heads = range(q_ref.shape[2] // HEAD_DIM)
    chains = [(h, mm) for h in heads for mm in range(2)]
    q = q_ref[0]

    def pair_lanes(x, h):
        lo = LANES * (h // 2)
        return x[:, lo:lo + LANES]

    qms = []
    for h, mm in chains:
        qp = pair_lanes(q, h)
        mask = _lane_mask(HEAD_DIM * (h % 2) + DIFF_QK_DIM * mm, DIFF_QK_DIM)
        qms.append(jnp.where(mask, qp, jnp.zeros_like(qp)))

    def keys(j):
        return k_ref[0, pl.ds(pl.multiple_of(j * t, t), t), :]

    def values(j):
        vt = vt_ref[0, j]
        vts = [_values_with_ones(vt, h) for h in heads]
        return [vts[h] for h, _ in chains]

    def far_scores(j):
        k = keys(j)
        return [_dot_nt(pair_lanes(k, h), qm) for (h, _), qm in zip(chains, qms)]

    jp = jnp.maximum(qi - 1, 0)

    def prev_scores():
        return [s + bias_ref[h, 1] for (h, _), s in zip(chains, far_scores(jp))]

    def prev_valid():
        return [jnp.full((1, t), jnp.where(qi >= 1, 1.0, 0.0), F32)] * len(chains)

    def own_scores():
        return [s + bias_ref[h, 0] for (h, _), s in zip(chains, far_scores(qi))]

    lam_p = lam_ref[...]
    lam_init = lam_p[4:5, 0:1]
    lam = (jnp.exp(jnp.sum(lam_p[0:1] * lam_p[1:2], axis=-1, keepdims=True))
           - jnp.exp(jnp.sum(lam_p[2:3] * lam_p[3:4], axis=-1, keepdims=True)) + lam_init)

    accs = _attend_all(len(chains), t, jnp.maximum(qi - 1, 0), far_scores, lambda j: None,
                       prev_scores, prev_valid, own_scores, values, jp, qi, s_ref, p_ref, acc_ref)
    outs = []
    for h in heads:
        o = _normalised(accs[2 * h]) - lam * _normalised(accs[2 * h + 1])
        ms = jnp.mean(o * o, axis=0, keepdims=True)
        outs.append(o * lax.rsqrt(ms + EPS))
    y_t = jnp.concatenate(outs, axis=0)
    o_ref[0] = (y_t.T * gnorm_ref[...] * (1.0 - lam_init)).astype(o_ref.dtype)


def _diff(qk, vt, lam_p, bias_t, gnorm):
    b, s, _ = qk.shape
    t = ATTN_TILE
    nq = s // t
    n_heads = N_HEADS_G
    width = n_heads * HEAD_DIM
    return pl.pallas_call(
        _diff_kernel,
        grid=(b, GROUP_WIDTH // width, nq),
        in_specs=[pl.BlockSpec(lam_p.shape, lambda bi, g, qi: (0, 0))]
        + _attn_specs(s, t, 1, width) + [
            pl.BlockSpec((n_heads, 2, t, t), lambda bi, g, qi: (g, 0, 0, 0)),
            pl.BlockSpec((1, width), lambda bi, g, qi: (0, g)),
        ],
        out_specs=pl.BlockSpec((1, t, width), lambda bi, g, qi: (bi, qi, g)),
        out_shape=jax.ShapeDtypeStruct((b, s, GROUP_WIDTH), BF16),
        scratch_shapes=_attn_scratch(2 * n_heads, t),
        compiler_params=pltpu.CompilerParams(
            dimension_semantics=("arbitrary", "arbitrary", "arbitrary"),
            vmem_limit_bytes=VMEM_LIMIT),
        name="diff_attn",
    )(lam_p, qk, qk, vt, bias_t, gnorm)


def _head_of_lane(n_lanes, width):
    return lax.broadcasted_iota(jnp.int32, (1, n_lanes), 1) // width


def _block_rows(x, lane_head, n_heads=N_HEADS_G):
    zero = jnp.zeros_like(x)
    return jnp.concatenate([jnp.where(lane_head == h, x, zero) for h in range(n_heads)], axis=0)


def _gdn_kernel(x_ref, misc_ref, conv_ref, hp_ref, gnorm_ref, o_ref, s_ref, tail_ref, g_ref, beta_ref):
    L = CHUNK
    W = GROUP_WIDTH
    rows = range(x_ref.shape[0])
    seq = x_ref.shape[1]
    n_chunks = seq // L
    head_w = _head_of_lane(W, HEAD_DIM)
    ri = lax.broadcasted_iota(jnp.int32, (L, W), 0)
    cj = lax.broadcasted_iota(jnp.int32, (L, W), 1) % HEAD_DIM
    lower = ri >= cj
    strict = ri > cj
    ident = jnp.where(ri == cj, 1.0, 0.0)
    ones_blk = (lax.broadcasted_iota(jnp.int32, (W, W), 0) // HEAD_DIM
                == lax.broadcasted_iota(jnp.int32, (W, W), 1) // HEAD_DIM)
    ones_seg = ones_blk.astype(BF16)
    tril = (lax.broadcasted_iota(jnp.int32, (L, L), 0)
            >= lax.broadcasted_iota(jnp.int32, (L, L), 1)).astype(BF16)

    hp = hp_ref[...]
    exp_g = (lax.broadcasted_iota(jnp.int32, (LANES, W), 0) - MISC_A
             == lax.broadcasted_iota(jnp.int32, (LANES, W), 1) // HEAD_DIM).astype(BF16)
    exp_b = (lax.broadcasted_iota(jnp.int32, (LANES, W), 0) - MISC_B
             == lax.broadcasted_iota(jnp.int32, (LANES, W), 1) // HEAD_DIM).astype(BF16)
    for r in rows:
        misc = misc_ref[r]
        sp_in = misc + hp[1:2]
        softplus = jnp.maximum(sp_in, 0.0) + jnp.log(1.0 + jnp.exp(-jnp.abs(sp_in)))
        g_tok = -jnp.exp(hp[0:1]) * softplus
        b_tok = _sigmoid(misc)
        g_ref[r] = _dot_exact_rhs(g_tok, exp_g, 2)
        beta_ref[r] = _dot_exact_rhs(b_tok, exp_b, 2)

    @pl.when(pl.program_id(1) == 0)
    def _():
        s_ref[...] = jnp.zeros_like(s_ref)
        tail_ref[...] = jnp.zeros_like(tail_ref)

    cw = conv_ref[...]
    blk = lambda a: _block_rows(a.astype(BF16), head_w)
    bf = lambda a: a.astype(BF16)

    def prepare(r, r0):
        x = x_ref[r, pl.ds(r0, L), :]
        qkv = x[:, 0:3 * W]
        xx = jnp.concatenate([tail_ref[r], qkv], axis=0)
        tail_ref[r] = qkv[L - F32_SUBLANES:L, :]
        conv = cw[CONV_WIDTH - 1:CONV_WIDTH] * qkv
        for i in range(CONV_WIDTH - 1):
            lo = F32_SUBLANES - (CONV_WIDTH - 1) + i
            conv = conv + cw[i:i + 1] * xx[lo:lo + L, :]
        conv = _silu(conv)
        return conv[:, 0:W], conv[:, W:2 * W], conv[:, 2 * W:3 * W], x[:, 3 * W:4 * W]

    def decays(gc):
        gc_col = jnp.sum(gc * ident, axis=0, keepdims=True)
        return jnp.exp(jnp.where(lower, gc - gc_col, -jnp.inf))

    def chunk(c, _):
        r0 = pl.multiple_of(c * L, L)
        q, k, v, z = zip(*[prepare(r, r0) for r in rows])
        ssq = [_dot_exact_rhs(jnp.concatenate([q[r] * q[r], k[r] * k[r]], axis=0), ones_seg, 1)
               for r in rows]
        q = [q[r] * lax.rsqrt(ssq[r][0:L] + EPS) * (HEAD_DIM ** -0.5) for r in rows]
        k = [k[r] * lax.rsqrt(ssq[r][L:2 * L] + EPS) for r in rows]
        beta = [beta_ref[r, pl.ds(r0, L), :] for r in rows]
        gc = [_dot_exact_lhs(tril, g_ref[r, pl.ds(r0, L), :], 2) for r in rows]
        egc = [jnp.exp(gc[r]) for r in rows]
        gc_last = [gc[r][L - 1:L, :] for r in rows]
        decay = [decays(gc[r]) for r in rows]
        kb = [k[r] * beta[r] for r in rows]
        both = [_dot_nt(bf(jnp.concatenate([kb[r], q[r]], axis=0)), blk(k[r])) for r in rows]
        qk = [jnp.where(lower, both[r][L:2 * L] * decay[r], 0.0) for r in rows]

        p = [jnp.where(strict, -both[r][0:L] * decay[r], 0.0) for r in rows]
        t_inv = [ident + p[r] for r in rows]
        p = [_dot(bf(p[r]), blk(p[r])) for r in rows]
        for _ in range(4):
            prod = [_dot(bf(jnp.concatenate([t_inv[r], p[r]], axis=0)), blk(p[r])) for r in rows]
            t_inv = [t_inv[r] + prod[r][0:L] for r in rows]
            p = [prod[r][L:2 * L] for r in rows]
        t_inv = [t_inv[r] + _dot(bf(t_inv[r]), blk(p[r])) for r in rows]

        u = [_dot(bf(t_inv[r]), blk(v[r] * beta[r])) for r in rows]
        w = [_dot(bf(t_inv[r]), blk(kb[r] * egc[r])) for r in rows]

        state = [s_ref[r] for r in rows]
        ws_qs = [_dot(bf(jnp.concatenate([w[r], q[r] * egc[r]], axis=0)), bf(state[r])) for r in rows]
        v_new = [u[r] - ws_qs[r][0:L] for r in rows]
        o = [ws_qs[r][L:2 * L] + _dot(bf(qk[r]), blk(v_new[r])) for r in rows]
        upd = [_dot_tn(bf(k[r] * jnp.exp(gc_last[r] - gc[r])), bf(v_new[r])) for r in rows]
        for r in rows:
            s_ref[r] = state[r] * jnp.exp(gc_last[r]) + jnp.where(ones_blk, upd[r], 0.0)

        ms = [_dot_exact_rhs(o[r] * o[r], ones_seg, 1) * (1.0 / HEAD_DIM) for r in rows]
        for r in rows:
            y = o[r] * lax.rsqrt(ms[r] + EPS) * gnorm_ref[...] * _silu(z[r])
            o_ref[r, pl.ds(r0, L), :] = y.astype(o_ref.dtype)
        return 0

    lax.fori_loop(0, n_chunks, chunk, 0)


def _gdn(gdn, misc, conv_w, hp, gnorm):
    b, s, _ = gdn.shape
    ts = min(s, REC_SEQ_TILE)
    nr = math.gcd(b, REC_ROWS)
    const = lambda bi, si: (0, 0)
    return pl.pallas_call(
        _gdn_kernel,
        grid=(b // nr, s // ts),
        in_specs=[
            pl.BlockSpec((nr, ts, N_GDN), lambda bi, si: (bi, si, 0)),
            pl.BlockSpec((nr, ts, N_MISC), lambda bi, si: (bi, si, 0)),
            pl.BlockSpec(conv_w.shape, const),
            pl.BlockSpec(hp.shape, const),
            pl.BlockSpec((1, GROUP_WIDTH), const),
        ],
        out_specs=pl.BlockSpec((nr, ts, GROUP_WIDTH), lambda bi, si: (bi, si, 0)),
        out_shape=jax.ShapeDtypeStruct((b, s, GROUP_WIDTH), BF16),
        scratch_shapes=[
            pltpu.VMEM((nr, GROUP_WIDTH, GROUP_WIDTH), F32),
            pltpu.VMEM((nr, F32_SUBLANES, 3 * GROUP_WIDTH), F32),
            pltpu.VMEM((nr, ts, GROUP_WIDTH), F32),
            pltpu.VMEM((nr, ts, GROUP_WIDTH), F32),
        ],
        compiler_params=pltpu.CompilerParams(
            dimension_semantics=("arbitrary", "arbitrary"), vmem_limit_bytes=VMEM_LIMIT),
        name="gdn",
    )(gdn, misc, conv_w, hp, gnorm)


def _gla_kernel(x_ref, misc_ref, walpha_ref, balpha_ref, gnorm_ref, o_ref, s_ref, la_ref):
    L = CHUNK
    W = GROUP_WIDTH
    KW = N_HEADS_G * GLA_DK
    rows = range(x_ref.shape[0])
    seq = x_ref.shape[1]
    n_chunks = seq // L
    head_k = _head_of_lane(KW, GLA_DK)
    head_v = _head_of_lane(W, HEAD_DIM)
    ri = lax.broadcasted_iota(jnp.int32, (L, W), 0)
    cj = lax.broadcasted_iota(jnp.int32, (L, W), 1) % HEAD_DIM
    lower = ri >= cj
    ones_blk = (lax.broadcasted_iota(jnp.int32, (W, W), 0) // HEAD_DIM
                == lax.broadcasted_iota(jnp.int32, (W, W), 1) // HEAD_DIM).astype(BF16)
    state_mask = (lax.broadcasted_iota(jnp.int32, (W, KW), 0) // HEAD_DIM
                  == lax.broadcasted_iota(jnp.int32, (W, KW), 1) // GLA_DK)
    tril = (lax.broadcasted_iota(jnp.int32, (L, L), 0)
            >= lax.broadcasted_iota(jnp.int32, (L, L), 1)).astype(BF16)

    for r in rows:
        pre = _dot(misc_ref[r].astype(BF16), walpha_ref[...]) + balpha_ref[...]
        log_sig = jnp.minimum(pre, 0.0) - jnp.log(1.0 + jnp.exp(-jnp.abs(pre)))
        la_ref[r] = log_sig * (1.0 / GLA_TAU)

    @pl.when(pl.program_id(1) == 0)
    def _():
        s_ref[...] = jnp.zeros_like(s_ref)

    bf = lambda a: a.astype(BF16)

    def chunk(c, _):
        r0 = pl.multiple_of(c * L, L)
        x = [x_ref[r, pl.ds(r0, L), :] for r in rows]
        k = [x[r][:, KW:2 * KW] for r in rows]
        v = [bf(x[r][:, 2 * KW:2 * KW + W]) for r in rows]
        bc = [_dot_exact_lhs(tril, la_ref[r, pl.ds(r0, L), :], 2) for r in rows]
        b_last = [bc[r][L - 1:L, :] for r in rows]
        q = [x[r][:, 0:KW] * (GLA_DK ** -0.5) for r in rows]
        qe = [bf(q[r] * jnp.exp(bc[r])) for r in rows]
        b_mid = [bc[r] - bc[r][L // 2:L // 2 + 1, :] for r in rows]
        qm = [bf(q[r] * jnp.exp(b_mid[r])) for r in rows]
        km = [bf(k[r] * jnp.exp(-b_mid[r])) for r in rows]
        a_mat = [jnp.where(lower, _dot_nt(qm[r], _block_rows(km[r], head_k)), 0.0) for r in rows]
        state_t = [s_ref[r] for r in rows]
        o = [_dot_nt(qe[r], bf(state_t[r])) for r in rows]
        o = [o[r] + _dot(bf(a_mat[r]), _block_rows(v[r], head_v)) for r in rows]
        upd = [_dot_tn(v[r], bf(k[r] * jnp.exp(b_last[r] - bc[r]))) for r in rows]
        for r in rows:
            s_ref[r] = state_t[r] * jnp.exp(b_last[r]) + jnp.where(state_mask, upd[r], 0.0)

        ms = [_dot_exact_rhs(o[r] * o[r], ones_blk, 1) * (1.0 / HEAD_DIM) for r in rows]
        for r in rows:
            gate = _silu(x[r][:, 2 * KW + W:2 * KW + 2 * W])
            y = o[r] * lax.rsqrt(ms[r] + EPS) * gnorm_ref[...] * gate
            o_ref[r, pl.ds(r0, L), :] = y.astype(o_ref.dtype)
        return 0

    lax.fori_loop(0, n_chunks, chunk, 0)


def _gla(gla, misc, walpha, balpha, gnorm):
    b, s, _ = gla.shape
    ts = min(s, REC_SEQ_TILE)
    nr = math.gcd(b, REC_ROWS)
    const = lambda bi, si: (0, 0)
    return pl.pallas_call(
        _gla_kernel,
        grid=(b // nr, s // ts),
        in_specs=[
            pl.BlockSpec((nr, ts, N_GLA), lambda bi, si: (bi, si, 0)),
            pl.BlockSpec((nr, ts, N_MISC), lambda bi, si: (bi, si, 0)),
            pl.BlockSpec(walpha.shape, const),
            pl.BlockSpec(balpha.shape, const),
            pl.BlockSpec((1, GROUP_WIDTH), const),
        ],
        out_specs=pl.BlockSpec((nr, ts, GROUP_WIDTH), lambda bi, si: (bi, si, 0)),
        out_shape=jax.ShapeDtypeStruct((b, s, GROUP_WIDTH), BF16),
        scratch_shapes=[
            pltpu.VMEM((nr, GROUP_WIDTH, N_HEADS_G * GLA_DK), F32),
            pltpu.VMEM((nr, ts, N_HEADS_G * GLA_DK), F32),
        ],
        compiler_params=pltpu.CompilerParams(
            dimension_semantics=("arbitrary", "arbitrary"), vmem_limit_bytes=VMEM_LIMIT),
        name="gla",
    )(gla, misc, walpha, balpha, gnorm)


def _ffn_chunks(d_ff):
    step = 4 * GROUP_WIDTH
    return [(lo, min(step, d_ff - lo)) for lo in range(0, d_ff, step)]


def _mlp_kernel(h_ref, ya_ref, yb_ref, yc_ref, yd_ref, wout_ref, gffn_ref, wg_ref, wu_ref, wd_ref,
                gple_ref, wpg_ref, p_ref, wpp_ref, gfin_ref, o_ref, act_ref, *, final):
    mixed = jnp.concatenate([ya_ref[...], yb_ref[...], yc_ref[...], yd_ref[...]], axis=-1)
    h = h_ref[...] + _dot(mixed, wout_ref[...])
    hn = _rms(h, gffn_ref[...]).astype(BF16)
    for lo, width in _ffn_chunks(wg_ref.shape[1]):
        gate = _dot(hn, wg_ref[:, lo:lo + width])
        up = _dot(hn, wu_ref[:, lo:lo + width])
        act_ref[:, lo:lo + width] = (_silu(gate) * up).astype(BF16)
    h = h + _dot(act_ref[...], wd_ref[...])
    gate = _sigmoid(_dot(_rms(h, gple_ref[...]).astype(BF16), wpg_ref[...]))
    h = h + gate * _dot(p_ref[...].astype(BF16), wpp_ref[...])
    if final:
        h = _rms(h, gfin_ref[...])
    o_ref[...] = h


def _mlp(h2d, ys, layer, wout, gffn, wg, wu, wd, gple, wpg, p3d, wpp, gfin, tm, final):
    m, d = h2d.shape
    d_ff = wg.shape[2]
    const = lambda i: (0, 0)
    resident = lambda shape: pl.BlockSpec(shape, const, pipeline_mode=pl.Buffered(1))
    stacked = lambda a: pl.BlockSpec((None,) + a.shape[1:], lambda i: (layer, 0, 0),
                                     pipeline_mode=pl.Buffered(1))
    rows = lambda width: pl.BlockSpec((tm, width), lambda i: (i, 0))
    return pl.pallas_call(
        functools.partial(_mlp_kernel, final=final),
        grid=(m // tm,),
        in_specs=[rows(d)] + [rows(GROUP_WIDTH)] * 4 + [
            stacked(wout), resident(gffn.shape), stacked(wg), stacked(wu),
            stacked(wd), resident(gple.shape), stacked(wpg),
            pl.BlockSpec((None, tm, p3d.shape[2]), lambda i: (layer, i, 0)), stacked(wpp),
            resident(gfin.shape),
        ],
        out_specs=rows(d),
        out_shape=jax.ShapeDtypeStruct((m, d), F32),
        scratch_shapes=[pltpu.VMEM((tm, d_ff), BF16)],
        compiler_params=pltpu.CompilerParams(
            dimension_semantics=("arbitrary",), vmem_limit_bytes=VMEM_LIMIT),
        name="mlp",
    )(h2d, *ys, wout, gffn, wg, wu, wd, gple, wpg, p3d, wpp, gfin)


def _rel_bucket_table(n_dist):
    n = np.arange(n_dist)
    nf = np.maximum(n, 1).astype(np.float32)
    large = REL_MAX_EXACT + (np.log(nf / REL_MAX_EXACT) / math.log(REL_MAX_DIST / REL_MAX_EXACT)
                             * (N_REL_BUCKETS - REL_MAX_EXACT)).astype(np.int32)
    large = np.minimum(large, N_REL_BUCKETS - 1)
    return np.where(n < REL_MAX_EXACT, n, large)


def _bias_tiles(rel_bias):
    t = ATTN_TILE
    bucket = _rel_bucket_table(2 * t)
    assert (bucket[t + 1:] == N_REL_BUCKETS - 1).all()
    per_dist = (rel_bias[bucket, :] - rel_bias[N_REL_BUCKETS - 1][None, :]).T * LOG2E
    own = jnp.concatenate([per_dist[:, :t], jnp.full_like(per_dist[:, :t], NEG)], axis=1)
    prev = jnp.concatenate([per_dist[:, t:], per_dist[:, :t]], axis=1)
    vec = jnp.stack([own, prev], axis=1)
    rolled = jnp.tile(vec, (1, 1, t))[:, :, :t * (2 * t - 1)].reshape(-1, 2, t, 2 * t - 1)
    return rolled[:, :, :, :t]


def _row(v, width=None):
    v = v.astype(F32).reshape(1, -1)
    if width is not None and v.shape[1] < width:
        v = jnp.pad(v, ((0, 0), (0, width - v.shape[1])))
    return v


def kernel(x, p, norm_mix, w_in, rel_bias, diff_lambda, diff_norm, gdn_conv, gdn_a_log, gdn_dt_bias,
           gdn_norm, gla_w_alpha, gla_b_alpha, gla_norm, w_out, norm_ffn, w_gate, w_up, w_down,
           norm_ple, w_ple_gate, w_ple_proj, final_norm):
    b, s, d = x.shape
    depth = w_in.shape[0]
    m = b * s
    tm = ROW_TILE
    hg = N_HEADS_G
    assert ATTN_TILE == MOBA_BLOCK
    assert s % tm == 0 and tm % ATTN_TILE == 0 and s >= 2 * ATTN_TILE
    assert s % REC_SEQ_TILE == 0 and REC_SEQ_TILE % CHUNK == 0

    bias_t = _bias_tiles(rel_bias.astype(F32))
    bias_a, bias_b = bias_t[:hg], bias_t[hg:]
    gw = GROUP_WIDTH
    col_scale = np.ones((1, N_QK), np.float32)
    col_scale[:, 0:gw] = HEAD_DIM ** -0.5
    col_scale[:, gw:2 * gw] = LOG2E
    col_scale[:, 2 * gw:3 * gw] = DIFF_QK_DIM ** -0.5 * LOG2E
    col_scale = jnp.asarray(col_scale)

    n_main = 6 * gw + N_GDN
    ab = 2 * hg
    gla_lo = n_main + ab
    gla_hi = gla_lo + N_GLA
    fin = _row(final_norm)
    w_out_bf, w_gate_bf, w_up_bf, w_down_bf, w_ple_gate_bf, w_ple_proj_bf = (
        w.astype(BF16) for w in (w_out, w_gate, w_up, w_down, w_ple_gate, w_ple_proj))
    p3d = p.reshape(depth, m, p.shape[-1])
    w1 = jnp.concatenate(
        [w_in[:, :, 0:2 * gw], w_in[:, :, 3 * gw:5 * gw], w_in[:, :, 6 * gw:n_main],
         w_in[:, :, gla_lo:gla_hi], w_in[:, :, n_main:gla_lo], w_in[:, :, gla_hi:],
         jnp.zeros((depth, d, N_MISC - ab - GLA_GATE_RANK), w_in.dtype)], axis=2).astype(BF16)
    wvt = jnp.swapaxes(jnp.concatenate([w_in[:, :, 2 * gw:3 * gw], w_in[:, :, 5 * gw:6 * gw]],
                                       axis=2), 1, 2).astype(BF16)
    h = x.reshape(m, d)
    for l in range(depth):
        qk, vt, gdn, gla, misc, kmean = _inproj(h, _row(norm_mix[l]), l, w1, wvt, col_scale, tm, s)
        qk = qk.reshape(b, s, N_QK)
        misc = misc.reshape(b, s, N_MISC)
        kmean = kmean.reshape(b, s // MOBA_BLOCK, GROUP_WIDTH)

        y_a = _moba(qk, vt, kmean, bias_a)

        lam_init = 0.8 - 0.6 * math.exp(-0.3 * l)
        lam_p = jnp.concatenate([diff_lambda[l].astype(F32),
                                 jnp.full((1, DIFF_QK_DIM), lam_init, F32)], axis=0)
        y_b = _diff(qk, vt, lam_p, bias_b, _row(jnp.tile(diff_norm[l], hg)))

        hp = jnp.concatenate([_row(gdn_a_log[l], LANES), _row(gdn_dt_bias[l], LANES)], axis=0)
        y_c = _gdn(gdn.reshape(b, s, N_GDN), misc, gdn_conv[l].astype(F32), hp,
                   _row(jnp.tile(gdn_norm[l], hg)))

        walpha = jnp.zeros((N_MISC, hg * GLA_DK), F32).at[MISC_LR:MISC_LR + GLA_GATE_RANK].set(
            gla_w_alpha[l]).astype(BF16)
        y_d = _gla(gla.reshape(b, s, N_GLA), misc, walpha, _row(gla_b_alpha[l]),
                   _row(jnp.tile(gla_norm[l], hg)))

        ys = [y.reshape(m, GROUP_WIDTH) for y in (y_a, y_b, y_c, y_d)]
        h = _mlp(h, ys, l, w_out_bf, _row(norm_ffn[l]), w_gate_bf, w_up_bf, w_down_bf,
                 _row(norm_ple[l]), w_ple_gate_bf, p3d, w_ple_proj_bf, fin, tm,
                 final=(l == depth - 1))
    return h.reshape(b, s, d)
```

```python
import functools
import math

import numpy as np
import jax
import jax.numpy as jnp
from jax import lax
from jax.experimental import pallas as pl
from jax.experimental.pallas import tpu as pltpu

F32 = jnp.float32
BF16 = jnp.bfloat16

HEAD_DIM = 64
N_HEADS_G = 4
GROUP_WIDTH = HEAD_DIM * N_HEADS_G
MOBA_BLOCK = 256
MOBA_TOPK = 3
N_REL_BUCKETS = 32
REL_MAX_EXACT = 16
REL_MAX_DIST = 128
DIFF_QK_DIM = HEAD_DIM // 2
CONV_WIDTH = 4
CHUNK = 64
GLA_DK = HEAD_DIM // 2
GLA_GATE_RANK = 16
GLA_TAU = 16.0
EPS = 1e-6

LANES = 128
F32_SUBLANES = 8
BF16_SUBLANES = 16
VMEM_LIMIT = 56 * 1024 * 1024
ROW_TILE = 512
ATTN_TILE = 256
ACC_ROWS = HEAD_DIM + BF16_SUBLANES
PIPE_CHAIN_TILES = 8
REC_SEQ_TILE = 256
REC_ROWS = 8
NEG = -1e30
LOG2E = math.log2(math.e)

N_QK = 4 * GROUP_WIDTH
N_VT = 2 * GROUP_WIDTH
N_GDN = 4 * GROUP_WIDTH
N_GLA = 2 * N_HEADS_G * GLA_DK + 2 * GROUP_WIDTH
N_MISC = LANES
MISC_A, MISC_B, MISC_LR = 0, N_HEADS_G, 2 * N_HEADS_G

_NT = (((1,), (1,)), ((), ()))
_TN = (((0,), (0,)), ((), ()))


def _dot(a, b):
    return jnp.dot(a, b, preferred_element_type=F32)


def _dot_nt(a, b):
    return lax.dot_general(a, b, _NT, preferred_element_type=F32)


def _dot_tn(a, b):
    return lax.dot_general(a, b, _TN, preferred_element_type=F32)


def _split_bf16(x, n):
    parts = []
    r = x
    for _ in range(n):
        hi = r.astype(BF16)
        parts.append(hi)
        r = r - hi.astype(F32)
    return parts


def _dot_exact_rhs(x, w, n):
    acc = None
    for part in _split_bf16(x, n):
        t = _dot(part, w)
        acc = t if acc is None else acc + t
    return acc


def _dot_exact_lhs(w, x, n):
    acc = None
    for part in _split_bf16(x, n):
        t = _dot(w, part)
        acc = t if acc is None else acc + t
    return acc


def _rms(x, g):
    return x * lax.rsqrt(jnp.mean(x * x, axis=-1, keepdims=True) + EPS) * g


def _sigmoid(x):
    return 1.0 / (1.0 + jnp.exp(-x))


def _silu(x):
    return x * _sigmoid(x)


def _inproj_kernel(x_ref, g_ref, w_ref, wvt_ref, scale_ref,
                   qk_ref, vt_ref, gdn_ref, gla_ref, misc_ref, kmean_ref):
    tm = x_ref.shape[0]
    xn = _rms(x_ref[...], g_ref[...]).astype(BF16)
    qk = _dot(xn, w_ref[:, 0:N_QK])
    k_moba = qk[:, GROUP_WIDTH:2 * GROUP_WIDTH]
    kmean_ref[0] = jnp.mean(k_moba.reshape(tm // MOBA_BLOCK, MOBA_BLOCK, GROUP_WIDTH), axis=1)
    qk_ref[...] = (qk * scale_ref[...]).astype(BF16)
    vt = _dot_nt(wvt_ref[...], xn).astype(BF16)
    for i in range(tm // ATTN_TILE):
        vt_ref[0, i] = vt[:, i * ATTN_TILE:(i + 1) * ATTN_TILE]
    o = N_QK
    gdn_ref[...] = _dot(xn, w_ref[:, o:o + N_GDN])
    o += N_GDN
    gla_ref[...] = _dot(xn, w_ref[:, o:o + N_GLA])
    o += N_GLA
    misc_ref[...] = _dot(xn, w_ref[:, o:o + N_MISC])


def _inproj(h2d, g, layer, w, wvt, scale, tm, seq):
    m, d = h2d.shape
    n_all = w.shape[2]
    per_seq = seq // tm
    const = lambda i: (0, 0)
    return pl.pallas_call(
        _inproj_kernel,
        grid=(m // tm,),
        in_specs=[
            pl.BlockSpec((tm, d), lambda i: (i, 0)),
            pl.BlockSpec((1, d), const),
            pl.BlockSpec((None, d, n_all), lambda i: (layer, 0, 0)),
            pl.BlockSpec((None, N_VT, d), lambda i: (layer, 0, 0)),
            pl.BlockSpec((1, N_QK), const),
        ],
        out_specs=[
            pl.BlockSpec((tm, N_QK), lambda i: (i, 0)),
            pl.BlockSpec((1, tm // ATTN_TILE, N_VT, ATTN_TILE),
                         lambda i: (i // per_seq, i % per_seq, 0, 0)),
            pl.BlockSpec((tm, N_GDN), lambda i: (i, 0)),
            pl.BlockSpec((tm, N_GLA), lambda i: (i, 0)),
            pl.BlockSpec((tm, N_MISC), lambda i: (i, 0)),
            pl.BlockSpec((1, tm // MOBA_BLOCK, GROUP_WIDTH), lambda i: (i, 0, 0)),
        ],
        out_shape=[
            jax.ShapeDtypeStruct((m, N_QK), BF16),
            jax.ShapeDtypeStruct((m // seq, seq // ATTN_TILE, N_VT, ATTN_TILE), BF16),
            jax.ShapeDtypeStruct((m, N_GDN), F32),
            jax.ShapeDtypeStruct((m, N_GLA), F32),
            jax.ShapeDtypeStruct((m, N_MISC), F32),
            jax.ShapeDtypeStruct((m // tm, tm // MOBA_BLOCK, GROUP_WIDTH), F32),
        ],
        compiler_params=pltpu.CompilerParams(
            dimension_semantics=("arbitrary",), vmem_limit_bytes=VMEM_LIMIT),
        name="inproj",
    )(h2d, g, w, wvt, scale)


def _lane_mask(lo, width):
    lane = lax.broadcasted_iota(jnp.int32, (1, LANES), 1)
    return (lane >= lo) & (lane < lo + width)


def _values_with_ones(vt, hh):
    head = vt[HEAD_DIM * hh:HEAD_DIM * (hh + 1), :]
    return jnp.concatenate([head, jnp.ones((ACC_ROWS - HEAD_DIM, vt.shape[1]), vt.dtype)], axis=0)


def _tile_max(scores):
    return [jnp.max(s, axis=0, keepdims=True) for s in scores]


def _softmax_stage(ms, scores, tile_maxes, valids=None):
    ms_new, alphas, ps = [], [], []
    for c, (m, s, m_tile) in enumerate(zip(ms, scores, tile_maxes)):
        if valids is None or valids[c] is None:
            m_new = jnp.maximum(m, m_tile)
            shift = m_new
        else:
            ok = valids[c] > 0.5
            m_new = jnp.maximum(m, jnp.where(ok, m_tile, NEG))
            shift = jnp.where(ok, m_new, -NEG)
        ms_new.append(m_new)
        alphas.append(jnp.exp2(m - m_new))
        ps.append(jnp.exp2(s - shift).astype(BF16))
    return ms_new, alphas, ps


def _value_products(vts, ps):
    return [_dot(vt, p) for vt, p in zip(vts, ps)]


def _attend_all(n, t, n_far, far_scores, far_valid, prev_scores, prev_valid, own_scores, values,
                j_prev, j_own, s_ref, p_ref, acc_ref, first_scores=None):
    chains = range(n)

    def tile_of(pos):
        return jnp.where(pos == 0, j_prev, jnp.where(pos == 1, j_own, pos - 2))

    def valid_of(pos):
        rows = far_valid(jnp.maximum(pos - 2, 0))
        if rows is None:
            return None
        return [jnp.where(pos == 1, 1.0, row) for row in rows]

    s0, s1 = first_scores if first_scores is not None else (prev_scores(), own_scores())
    ms, alphas, ps = _softmax_stage([jnp.full((1, t), NEG, F32)] * n, s0, _tile_max(s0),
                                    prev_valid())
    tmax = _tile_max(s1)
    for c in chains:
        p_ref[c] = ps[c]
        s_ref[c] = s1[c]
        acc_ref[c] = jnp.zeros((ACC_ROWS, t), F32)

    def steps(first, unroll, carry):
        ms, alphas, tmax = carry
        ps = [p_ref[c] for c in chains]
        s_cur = [s_ref[c] for c in chains]
        for u in range(unroll):
            pos = first + u
            vts = values(tile_of(pos))
            late = n // 4 if unroll == 1 else 0
            pv = _value_products(vts[:n - late], ps[:n - late])
            s_new = far_scores(pos)
            pv += _value_products(vts[n - late:], ps[n - late:])
            tmax_new = _tile_max(s_new)
            ms, alphas_next, ps = _softmax_stage(ms, s_cur, tmax, valid_of(pos + 1))
            for c in chains:
                acc_ref[c] = alphas[c] * acc_ref[c] + pv[c]
            alphas, s_cur, tmax = alphas_next, s_new, tmax_new
        for c in chains:
            p_ref[c] = ps[c]
            s_ref[c] = s_cur[c]
        return ms, alphas, tmax

    n_steps = n_far
    unroll = max(1, PIPE_CHAIN_TILES // n)
    n_blocks = n_steps // unroll
    carry = lax.fori_loop(0, n_blocks, lambda i, cr: steps(i * unroll, unroll, cr),
                          (ms, alphas, tmax))
    if unroll > 1:
        carry = lax.fori_loop(n_blocks * unroll, n_steps, lambda i, cr: steps(i, 1, cr), carry)
    ms, alphas, tmax = carry

    pv = _value_products(values(tile_of(n_far)), [p_ref[c] for c in chains])
    ms, alphas1, ps1 = _softmax_stage(ms, [s_ref[c] for c in chains], tmax, valid_of(n_far + 1))
    accs = [alphas[c] * acc_ref[c] + pv[c] for c in chains]
    pv = _value_products(values(tile_of(n_far + 1)), ps1)
    return [alphas1[c] * accs[c] + pv[c] for c in chains]


def _normalised(acc):
    return acc[0:HEAD_DIM, :] / acc[HEAD_DIM:HEAD_DIM + 1, :]


def _moba_kernel(q_ref, k_ref, vt_ref, kmean_ref, bias_ref, o_ref, sel_ref, s_ref, p_ref, acc_ref):
    t = ATTN_TILE
    qi = pl.program_id(2)
    nblk = kmean_ref.shape[1]
    heads = range(q_ref.shape[2] // HEAD_DIM)
    q = q_ref[0]
    kmean = kmean_ref[0].astype(BF16)

    def pair_lanes(x, h):
        lo = LANES * (h // 2)
        return x[:, lo:lo + LANES]

    qms = []
    for h in heads:
        qp = pair_lanes(q, h)
        qms.append(jnp.where(_lane_mask(HEAD_DIM * (h % 2), HEAD_DIM), qp, jnp.zeros_like(qp)))

    def keys(j):
        return k_ref[0, pl.ds(pl.multiple_of(j * t, t), t), :]

    def far_scores(j):
        k = keys(j)
        return [_dot_nt(pair_lanes(k, h), qms[h]) for h in heads]

    jp = jnp.maximum(qi - 1, 0)

    def prev_scores():
        return [s + bias_ref[h, 1] for h, s in zip(heads, far_scores(jp))]

    def own_scores():
        return [s + bias_ref[h, 0] for h, s in zip(heads, far_scores(qi))]

    gates = [_dot_nt(pair_lanes(kmean, h), qms[h]) for h in heads]
    first_scores = prev_scores(), own_scores()

    row = lax.broadcasted_iota(jnp.int32, (nblk, t), 0)
    past = row < qi
    for h in heads:
        gate = jnp.where(past, gates[h], -jnp.inf)
        sel_t = jnp.zeros((nblk, t), F32)
        for _ in range(MOBA_TOPK):
            top = jnp.max(gate, axis=0, keepdims=True)
            first = jnp.min(jnp.where(gate == top, row, nblk), axis=0, keepdims=True)
            pick = row == first
            sel_t = jnp.where(pick, 1.0, sel_t)
            gate = jnp.where(pick, -jnp.inf, gate)
        sel_t = jnp.where(past, sel_t, 0.0)
        for j in range(nblk):
            sel_ref[h * nblk + j] = sel_t[j:j + 1, :]

    def values(j):
        vt = vt_ref[0, j]
        return [_values_with_ones(vt, h) for h in heads]

    def far_valid(j):
        return [sel_ref[h * nblk + j] for h in heads]

    accs = _attend_all(len(heads), t, jnp.maximum(qi - 1, 0), far_scores, far_valid, prev_scores,
                       lambda: far_valid(jp), own_scores, values, jp, qi, s_ref, p_ref, acc_ref,
                       first_scores)
    out_t = jnp.concatenate([_normalised(acc) for acc in accs], axis=0)
    o_ref[0] = out_t.T.astype(o_ref.dtype)


def _attn_specs(s, t, mixer, width):
    nkv = s // t
    per = GROUP_WIDTH // width
    qcol = mixer * 2 * per
    kcol = qcol + per
    vrow = mixer * per
    return [
        pl.BlockSpec((1, t, width), lambda bi, g, qi: (bi, qi, qcol + g)),
        pl.BlockSpec((1, s, width), lambda bi, g, qi: (bi, 0, kcol + g)),
        pl.BlockSpec((1, nkv, width, t), lambda bi, g, qi: (bi, 0, vrow + g, 0)),
    ]


def _attn_scratch(n_chains, t):
    return [pltpu.VMEM((n_chains, t, t), F32),
            pltpu.VMEM((n_chains, t, t), BF16),
            pltpu.VMEM((n_chains, ACC_ROWS, t), F32)]


def _moba(qk, vt, kmean, bias_t):
    b, s, _ = qk.shape
    t = ATTN_TILE
    nq = s // t
    nblk = kmean.shape[1]
    n_heads = N_HEADS_G
    width = n_heads * HEAD_DIM
    return pl.pallas_call(
        _moba_kernel,
        grid=(b, GROUP_WIDTH // width, nq),
        in_specs=_attn_specs(s, t, 0, width) + [
            pl.BlockSpec((1, nblk, width), lambda bi, g, qi: (bi, 0, g)),
            pl.BlockSpec((n_heads, 2, t, t), lambda bi, g, qi: (g, 0, 0, 0)),
        ],
        out_specs=pl.BlockSpec((1, t, width), lambda bi, g, qi: (bi, qi, g)),
        out_shape=jax.ShapeDtypeStruct((b, s, GROUP_WIDTH), BF16),
        scratch_shapes=[pltpu.VMEM((n_heads * nblk, 1, t), F32)]
        + _attn_scratch(n_heads, t),
        compiler_params=pltpu.CompilerParams(
            dimension_semantics=("arbitrary", "arbitrary", "arbitrary"),
            vmem_limit_bytes=VMEM_LIMIT),
        name="moba",
    )(qk, qk, vt, kmean, bias_t)


def _diff_kernel(lam_ref, q_ref, k_ref, vt_ref, bias_ref, gnorm_ref, o_ref, s_ref, p_ref, acc_ref):
    t = ATTN_TILE
    qi = pl.program_id(2)
    heads = range(q_ref.shape[2] // HEAD_DIM)
    chains = [(h, mm) for h in heads for mm in range(2)]
    q = q_ref[0]

    def pair_lanes(x, h):
        lo = LANES * (h // 2)
        return x[:, lo:lo + LANES]

    qms = []
    for h, mm in chains:
        qp = pair_lanes(q, h)
        mask = _lane_mask(HEAD_DIM * (h % 2) + DIFF_QK_DIM * mm, DIFF_QK_DIM)
        qms.append(jnp.where(mask, qp, jnp.zeros_like(qp)))

    def keys(j):
        return k_ref[0, pl.ds(pl.multiple_of(j * t, t), t), :]

    def values(j):
        vt = vt_ref[0, j]
        vts = [_values_with_ones(vt, h) for h in heads]
        return [vts[h] for h, _ in chains]

    def far_scores(j):
        k = keys(j)
        return [_dot_nt(pair_lanes(k, h), qm) for (h, _), qm in zip(chains, qms)]

    jp = jnp.maximum(qi - 1, 0)

    def prev_scores():
        return [s + bias_ref[h, 1] for (h, _), s in zip(chains, far_scores(jp))]

    def prev_valid():
        return [jnp.full((1, t), jnp.where(qi >= 1, 1.0, 0.0), F32)] * len(chains)

    def own_scores():
        return [s + bias_ref[h, 0] for (h, _), s in zip(chains, far_scores(qi))]

    lam_p = lam_ref[...]
    lam_init = lam_p[4:5, 0:1]
    lam = (jnp.exp(jnp.sum(lam_p[0:1] * lam_p[1:2], axis=-1, keepdims=True))
           - jnp.exp(jnp.sum(lam_p[2:3] * lam_p[3:4], axis=-1, keepdims=True)) + lam_init)

    accs = _attend_all(len(chains), t, jnp.maximum(qi - 1, 0), far_scores, lambda j: None,
                       prev_scores, prev_valid, own_scores, values, jp, qi, s_ref, p_ref, acc_ref)
    outs = []
    for h in heads:
        o = _normalised(accs[2 * h]) - lam * _normalised(accs[2 * h + 1])
        ms = jnp.mean(o * o, axis=0, keepdims=True)
        outs.append(o * lax.rsqrt(ms + EPS))
    y_t = jnp.concatenate(outs, axis=0)
    o_ref[0] = (y_t.T * gnorm_ref[...] * (1.0 - lam_init)).astype(o_ref.dtype)


def _diff(qk, vt, lam_p, bias_t, gnorm):
    b, s, _ = qk.shape
    t = ATTN_TILE
    nq = s // t
    n_heads = N_HEADS_G
    width = n_heads * HEAD_DIM
    return pl.pallas_call(
        _diff_kernel,
        grid=(b, GROUP_WIDTH // width, nq),
        in_specs=[pl.BlockSpec(lam_p.shape, lambda bi, g, qi: (0, 0))]
        + _attn_specs(s, t, 1, width) + [
            pl.BlockSpec((n_heads, 2, t, t), lambda bi, g, qi: (g, 0, 0, 0)),
            pl.BlockSpec((1, width), lambda bi, g, qi: (0, g)),
        ],
        out_specs=pl.BlockSpec((1, t, width), lambda bi, g, qi: (bi, qi, g)),
        out_shape=jax.ShapeDtypeStruct((b, s, GROUP_WIDTH), BF16),
        scratch_shapes=_attn_scratch(2 * n_heads, t),
        compiler_params=pltpu.CompilerParams(
            dimension_semantics=("arbitrary", "arbitrary", "arbitrary"),
            vmem_limit_bytes=VMEM_LIMIT),
        name="diff_attn",
    )(lam_p, qk, qk, vt, bias_t, gnorm)


def _head_of_lane(n_lanes, width):
    return lax.broadcasted_iota(jnp.int32, (1, n_lanes), 1) // width


def _block_rows(x, lane_head, n_heads=N_HEADS_G):
    zero = jnp.zeros_like(x)
    return jnp.concatenate([jnp.where(lane_head == h, x, zero) for h in range(n_heads)], axis=0)


def _gdn_kernel(x_ref, misc_ref, conv_ref, hp_ref, gnorm_ref, o_ref, s_ref, tail_ref, g_ref, beta_ref):
    L = CHUNK
    W = GROUP_WIDTH
    rows = range(x_ref.shape[0])
    seq = x_ref.shape[1]
    n_chunks = seq // L
    head_w = _head_of_lane(W, HEAD_DIM)
    ri = lax.broadcasted_iota(jnp.int32, (L, W), 0)
    cj = lax.broadcasted_iota(jnp.int32, (L, W), 1) % HEAD_DIM
    lower = ri >= cj
    strict = ri > cj
    ident = jnp.where(ri == cj, 1.0, 0.0)
    ones_blk = (lax.broadcasted_iota(jnp.int32, (W, W), 0) // HEAD_DIM
                == lax.broadcasted_iota(jnp.int32, (W, W), 1) // HEAD_DIM)
    ones_seg = ones_blk.astype(BF16)
    tril = (lax.broadcasted_iota(jnp.int32, (L, L), 0)
            >= lax.broadcasted_iota(jnp.int32, (L, L), 1)).astype(BF16)

    hp = hp_ref[...]
    exp_g = (lax.broadcasted_iota(jnp.int32, (LANES, W), 0) - MISC_A
             == lax.broadcasted_iota(jnp.int32, (LANES, W), 1) // HEAD_DIM).astype(BF16)
    exp_b = (lax.broadcasted_iota(jnp.int32, (LANES, W), 0) - MISC_B
             == lax.broadcasted_iota(jnp.int32, (LANES, W), 1) // HEAD_DIM).astype(BF16)
    for r in rows:
        misc = misc_ref[r]
        sp_in = misc + hp[1:2]
        softplus = jnp.maximum(sp_in, 0.0) + jnp.log(1.0 + jnp.exp(-jnp.abs(sp_in)))
        g_tok = -jnp.exp(hp[0:1]) * softplus
        b_tok = _sigmoid(misc)
        g_ref[r] = _dot_exact_rhs(g_tok, exp_g, 2)
        beta_ref[r] = _dot_exact_rhs(b_tok, exp_b, 2)

    @pl.when(pl.program_id(1) == 0)
    def _():
        s_ref[...] = jnp.zeros_like(s_ref)
        tail_ref[...] = jnp.zeros_like(tail_ref)

    cw = conv_ref[...]
    blk = lambda a: _block_rows(a.astype(BF16), head_w)
    bf = lambda a: a.astype(BF16)

    def prepare(r, r0):
        x = x_ref[r, pl.ds(r0, L), :]
        qkv = x[:, 0:3 * W]
        xx = jnp.concatenate([tail_ref[r], qkv], axis=0)
        tail_ref[r] = qkv[L - F32_SUBLANES:L, :]
        conv = cw[CONV_WIDTH - 1:CONV_WIDTH] * qkv
        for i in range(CONV_WIDTH - 1):
            lo = F32_SUBLANES - (CONV_WIDTH - 1) + i
            conv = conv + cw[i:i + 1] * xx[lo:lo + L, :]
        conv = _silu(conv)
        return conv[:, 0:W], conv[:, W:2 * W], conv[:, 2 * W:3 * W], x[:, 3 * W:4 * W]

    def decays(gc):
        gc_col = jnp.sum(gc * ident, axis=0, keepdims=True)
        return jnp.exp(jnp.where(lower, gc - gc_col, -jnp.inf))

    def chunk(c, _):
        r0 = pl.multiple_of(c * L, L)
        q, k, v, z = zip(*[prepare(r, r0) for r in rows])
        ssq = [_dot_exact_rhs(jnp.concatenate([q[r] * q[r], k[r] * k[r]], axis=0), ones_seg, 1)
               for r in rows]
        q = [q[r] * lax.rsqrt(ssq[r][0:L] + EPS) * (HEAD_DIM ** -0.5) for r in rows]
        k = [k[r] * lax.rsqrt(ssq[r][L:2 * L] + EPS) for r in rows]
        beta = [beta_ref[r, pl.ds(r0, L), :] for r in rows]
        gc = [_dot_exact_lhs(tril, g_ref[r, pl.ds(r0, L), :], 2) for r in rows]
        egc = [jnp.exp(gc[r]) for r in rows]
        gc_last = [gc[r][L - 1:L, :] for r in rows]
        decay = [decays(gc[r]) for r in rows]
        kb = [k[r] * beta[r] for r in rows]
        both = [_dot_nt(bf(jnp.concatenate([kb[r], q[r]], axis=0)), blk(k[r])) for r in rows]
        qk = [jnp.where(lower, both[r][L:2 * L] * decay[r], 0.0) for r in rows]

        p = [jnp.where(strict, -both[r][0:L] * decay[r], 0.0) for r in rows]
        t_inv = [ident + p[r] for r in rows]
        p = [_dot(bf(p[r]), blk(p[r])) for r in rows]
        for _ in range(4):
            prod = [_dot(bf(jnp.concatenate([t_inv[r], p[r]], axis=0)), blk(p[r])) for r in rows]
            t_inv = [t_inv[r] + prod[r][0:L] for r in rows]
            p = [prod[r][L:2 * L] for r in rows]
        t_inv = [t_inv[r] + _dot(bf(t_inv[r]), blk(p[r])) for r in rows]

        u = [_dot(bf(t_inv[r]), blk(v[r] * beta[r])) for r in rows]
        w = [_dot(bf(t_inv[r]), blk(kb[r] * egc[r])) for r in rows]

        state = [s_ref[r] for r in rows]
        ws_qs = [_dot(bf(jnp.concatenate([w[r], q[r] * egc[r]], axis=0)), bf(state[r])) for r in rows]
        v_new = [u[r] - ws_qs[r][0:L] for r in rows]
        o = [ws_qs[r][L:2 * L] + _dot(bf(qk[r]), blk(v_new[r])) for r in rows]
        upd = [_dot_tn(bf(k[r] * jnp.exp(gc_last[r] - gc[r])), bf(v_new[r])) for r in rows]
        for r in rows:
            s_ref[r] = state[r] * jnp.exp(gc_last[r]) + jnp.where(ones_blk, upd[r], 0.0)

        ms = [_dot_exact_rhs(o[r] * o[r], ones_seg, 1) * (1.0 / HEAD_DIM) for r in rows]
        for r in rows:
            y = o[r] * lax.rsqrt(ms[r] + EPS) * gnorm_ref[...] * _silu(z[r])
            o_ref[r, pl.ds(r0, L), :] = y.astype(o_ref.dtype)
        return 0

    lax.fori_loop(0, n_chunks, chunk, 0)


def _gdn(gdn, misc, conv_w, hp, gnorm):
    b, s, _ = gdn.shape
    ts = min(s, REC_SEQ_TILE)
    nr = math.gcd(b, REC_ROWS)
    const = lambda bi, si: (0, 0)
    return pl.pallas_call(
        _gdn_kernel,
        grid=(b // nr, s // ts),
        in_specs=[
            pl.BlockSpec((nr, ts, N_GDN), lambda bi, si: (bi, si, 0)),
            pl.BlockSpec((nr, ts, N_MISC), lambda bi, si: (bi, si, 0)),
            pl.BlockSpec(conv_w.shape, const),
            pl.BlockSpec(hp.shape, const),
            pl.BlockSpec((1, GROUP_WIDTH), const),
        ],
        out_specs=pl.BlockSpec((nr, ts, GROUP_WIDTH), lambda bi, si: (bi, si, 0)),
        out_shape=jax.ShapeDtypeStruct((b, s, GROUP_WIDTH), BF16),
        scratch_shapes=[
            pltpu.VMEM((nr, GROUP_WIDTH, GROUP_WIDTH), F32),
            pltpu.VMEM((nr, F32_SUBLANES, 3 * GROUP_WIDTH), F32),
            pltpu.VMEM((nr, ts, GROUP_WIDTH), F32),
            pltpu.VMEM((nr, ts, GROUP_WIDTH), F32),
        ],
        compiler_params=pltpu.CompilerParams(
            dimension_semantics=("arbitrary", "arbitrary"), vmem_limit_bytes=VMEM_LIMIT),
        name="gdn",
    )(gdn, misc, conv_w, hp, gnorm)


def _gla_kernel(x_ref, misc_ref, walpha_ref, balpha_ref, gnorm_ref, o_ref, s_ref, la_ref):
    L = CHUNK
    W = GROUP_WIDTH
    KW = N_HEADS_G * GLA_DK
    rows = range(x_ref.shape[0])
    seq = x_ref.shape[1]
    n_chunks = seq // L
    head_k = _head_of_lane(KW, GLA_DK)
    head_v = _head_of_lane(W, HEAD_DIM)
    ri = lax.broadcasted_iota(jnp.int32, (L, W), 0)
    cj = lax.broadcasted_iota(jnp.int32, (L, W), 1) % HEAD_DIM
    lower = ri >= cj
    ones_blk = (lax.broadcasted_iota(jnp.int32, (W, W), 0) // HEAD_DIM
                == lax.broadcasted_iota(jnp.int32, (W, W), 1) // HEAD_DIM).astype(BF16)
    state_mask = (lax.broadcasted_iota(jnp.int32, (W, KW), 0) // HEAD_DIM
                  == lax.broadcasted_iota(jnp.int32, (W, KW), 1) // GLA_DK)
    tril = (lax.broadcasted_iota(jnp.int32, (L, L), 0)
            >= lax.broadcasted_iota(jnp.int32, (L, L), 1)).astype(BF16)

    for r in rows:
        pre = _dot(misc_ref[r].astype(BF16), walpha_ref[...]) + balpha_ref[...]
        log_sig = jnp.minimum(pre, 0.0) - jnp.log(1.0 + jnp.exp(-jnp.abs(pre)))
        la_ref[r] = log_sig * (1.0 / GLA_TAU)

    @pl.when(pl.program_id(1) == 0)
    def _():
        s_ref[...] = jnp.zeros_like(s_ref)

    bf = lambda a: a.astype(BF16)

    def chunk(c, _):
        r0 = pl.multiple_of(c * L, L)
        x = [x_ref[r, pl.ds(r0, L), :] for r in rows]
        k = [x[r][:, KW:2 * KW] for r in rows]
        v = [bf(x[r][:, 2 * KW:2 * KW + W]) for r in rows]
        bc = [_dot_exact_lhs(tril, la_ref[r, pl.ds(r0, L), :], 2) for r in rows]
        b_last = [bc[r][L - 1:L, :] for r in rows]
        q = [x[r][:, 0:KW] * (GLA_DK ** -0.5) for r in rows]
        qe = [bf(q[r] * jnp.exp(bc[r])) for r in rows]
        b_mid = [bc[r] - bc[r][L // 2:L // 2 + 1, :] for r in rows]
        qm = [bf(q[r] * jnp.exp(b_mid[r])) for r in rows]
        km = [bf(k[r] * jnp.exp(-b_mid[r])) for r in rows]
        a_mat = [jnp.where(lower, _dot_nt(qm[r], _block_rows(km[r], head_k)), 0.0) for r in rows]
        state_t = [s_ref[r] for r in rows]
        o = [_dot_nt(qe[r], bf(state_t[r])) for r in rows]
        o = [o[r] + _dot(bf(a_mat[r]), _block_rows(v[r], head_v)) for r in rows]
        upd = [_dot_tn(v[r], bf(k[r] * jnp.exp(b_last[r] - bc[r]))) for r in rows]
        for r in rows:
            s_ref[r] = state_t[r] * jnp.exp(b_last[r]) + jnp.where(state_mask, upd[r], 0.0)

        ms = [_dot_exact_rhs(o[r] * o[r], ones_blk, 1) * (1.0 / HEAD_DIM) for r in rows]
        for r in rows:
            gate = _silu(x[r][:, 2 * KW + W:2 * KW + 2 * W])
            y = o[r] * lax.rsqrt(ms[r] + EPS) * gnorm_ref[...] * gate
            o_ref[r, pl.ds(r0, L), :] = y.astype(o_ref.dtype)
        return 0

    lax.fori_loop(0, n_chunks, chunk, 0)


def _gla(gla, misc, walpha, balpha, gnorm):
    b, s, _ = gla.shape
    ts = min(s, REC_SEQ_TILE)
    nr = math.gcd(b, REC_ROWS)
    const = lambda bi, si: (0, 0)
    return pl.pallas_call(
        _gla_kernel,
        grid=(b // nr, s // ts),
        in_specs=[
            pl.BlockSpec((nr, ts, N_GLA), lambda bi, si: (bi, si, 0)),
            pl.BlockSpec((nr, ts, N_MISC), lambda bi, si: (bi, si, 0)),
            pl.BlockSpec(walpha.shape, const),
            pl.BlockSpec(balpha.shape, const),
            pl.BlockSpec((1, GROUP_WIDTH), const),
        ],
        out_specs=pl.BlockSpec((nr, ts, GROUP_WIDTH), lambda bi, si: (bi, si, 0)),
        out_shape=jax.ShapeDtypeStruct((b, s, GROUP_WIDTH), BF16),
        scratch_shapes=[
            pltpu.VMEM((nr, GROUP_WIDTH, N_HEADS_G * GLA_DK), F32),
            pltpu.VMEM((nr, ts, N_HEADS_G * GLA_DK), F32),
        ],
        compiler_params=pltpu.CompilerParams(
            dimension_semantics=("arbitrary", "arbitrary"), vmem_limit_bytes=VMEM_LIMIT),
        name="gla",
    )(gla, misc, walpha, balpha, gnorm)


def _ffn_chunks(d_ff):
    step = 4 * GROUP_WIDTH
    return [(lo, min(step, d_ff - lo)) for lo in range(0, d_ff, step)]


def _mlp_kernel(h_ref, ya_ref, yb_ref, yc_ref, yd_ref, wout_ref, gffn_ref, wg_ref, wu_ref, wd_ref,
                gple_ref, wpg_ref, p_ref, wpp_ref, gfin_ref, o_ref, act_ref, *, final):
    mixed = jnp.concatenate([ya_ref[...], yb_ref[...], yc_ref[...], yd_ref[...]], axis=-1)
    h = h_ref[...] + _dot(mixed, wout_ref[...])
    hn = _rms(h, gffn_ref[...]).astype(BF16)
    for lo, width in _ffn_chunks(wg_ref.shape[1]):
        gate = _dot(hn, wg_ref[:, lo:lo + width])
        up = _dot(hn, wu_ref[:, lo:lo + width])
        act_ref[:, lo:lo + width] = (_silu(gate) * up).astype(BF16)
    h = h + _dot(act_ref[...], wd_ref[...])
    gate = _sigmoid(_dot(_rms(h, gple_ref[...]).astype(BF16), wpg_ref[...]))
    h = h + gate * _dot(p_ref[...].astype(BF16), wpp_ref[...])
    if final:
        h = _rms(h, gfin_ref[...])
    o_ref[...] = h


def _mlp(h2d, ys, layer, wout, gffn, wg, wu, wd, gple, wpg, p3d, wpp, gfin, tm, final):
    m, d = h2d.shape
    d_ff = wg.shape[2]
    const = lambda i: (0, 0)
    resident = lambda shape: pl.BlockSpec(shape, const, pipeline_mode=pl.Buffered(1))
    stacked = lambda a: pl.BlockSpec((None,) + a.shape[1:], lambda i: (layer, 0, 0),
                                     pipeline_mode=pl.Buffered(1))
    rows = lambda width: pl.BlockSpec((tm, width), lambda i: (i, 0))
    return pl.pallas_call(
        functools.partial(_mlp_kernel, final=final),
        grid=(m // tm,),
        in_specs=[rows(d)] + [rows(GROUP_WIDTH)] * 4 + [
            stacked(wout), resident(gffn.shape), stacked(wg), stacked(wu),
            stacked(wd), resident(gple.shape), stacked(wpg),
            pl.BlockSpec((None, tm, p3d.shape[2]), lambda i: (layer, i, 0)), stacked(wpp),
            resident(gfin.shape),
        ],
        out_specs=rows(d),
        out_shape=jax.ShapeDtypeStruct((m, d), F32),
        scratch_shapes=[pltpu.VMEM((tm, d_ff), BF16)],
        compiler_params=pltpu.CompilerParams(
            dimension_semantics=("arbitrary",), vmem_limit_bytes=VMEM_LIMIT),
        name="mlp",
    )(h2d, *ys, wout, gffn, wg, wu, wd, gple, wpg, p3d, wpp, gfin)


def _rel_bucket_table(n_dist):
    n = np.arange(n_dist)
    nf = np.maximum(n, 1).astype(np.float32)
    large = REL_MAX_EXACT + (np.log(nf / REL_MAX_EXACT) / math.log(REL_MAX_DIST / REL_MAX_EXACT)
                             * (N_REL_BUCKETS - REL_MAX_EXACT)).astype(np.int32)
    large = np.minimum(large, N_REL_BUCKETS - 1)
    return np.where(n < REL_MAX_EXACT, n, large)


def _bias_tiles(rel_bias):
    t = ATTN_TILE
    bucket = _rel_bucket_table(2 * t)
    assert (bucket[t + 1:] == N_REL_BUCKETS - 1).all()
    per_dist = (rel_bias[bucket, :] - rel_bias[N_REL_BUCKETS - 1][None, :]).T * LOG2E
    own = jnp.concatenate([per_dist[:, :t], jnp.full_like(per_dist[:, :t], NEG)], axis=1)
    prev = jnp.concatenate([per_dist[:, t:], per_dist[:, :t]], axis=1)
    vec = jnp.stack([own, prev], axis=1)
    rolled = jnp.tile(vec, (1, 1, t))[:, :, :t * (2 * t - 1)].reshape(-1, 2, t, 2 * t - 1)
    return rolled[:, :, :, :t]


def _row(v, width=None):
    v = v.astype(F32).reshape(1, -1)
    if width is not None and v.shape[1] < width:
        v = jnp.pad(v, ((0, 0), (0, width - v.shape[1])))
    return v


def kernel(x, p, norm_mix, w_in, rel_bias, diff_lambda, diff_norm, gdn_conv, gdn_a_log, gdn_dt_bias,
           gdn_norm, gla_w_alpha, gla_b_alpha, gla_norm, w_out, norm_ffn, w_gate, w_up, w_down,
           norm_ple, w_ple_gate, w_ple_proj, final_norm):
    b, s, d = x.shape
    depth = w_in.shape[0]
    m = b * s
    tm = ROW_TILE
    hg = N_HEADS_G
    assert ATTN_TILE == MOBA_BLOCK
    assert s % tm == 0 and tm % ATTN_TILE == 0 and s >= 2 * ATTN_TILE
    assert s % REC_SEQ_TILE == 0 and REC_SEQ_TILE % CHUNK == 0

    bias_t = _bias_tiles(rel_bias.astype(F32))
    bias_a, bias_b = bias_t[:hg], bias_t[hg:]
    gw = GROUP_WIDTH
    col_scale = np.ones((1, N_QK), np.float32)
    col_scale[:, 0:gw] = HEAD_DIM ** -0.5
    col_scale[:, gw:2 * gw] = LOG2E
    col_scale[:, 2 * gw:3 * gw] = DIFF_QK_DIM ** -0.5 * LOG2E
    col_scale = jnp.asarray(col_scale)

    n_main = 6 * gw + N_GDN
    ab = 2 * hg
    gla_lo = n_main + ab
    gla_hi = gla_lo + N_GLA
    fin = _row(final_norm)
    w_out_bf, w_gate_bf, w_up_bf, w_down_bf, w_ple_gate_bf, w_ple_proj_bf = (
        w.astype(BF16) for w in (w_out, w_gate, w_up, w_down, w_ple_gate, w_ple_proj))
    p3d = p.reshape(depth, m, p.shape[-1])
    w1 = jnp.concatenate(
        [w_in[:, :, 0:2 * gw], w_in[:, :, 3 * gw:5 * gw], w_in[:, :, 6 * gw:n_main],
         w_in[:, :, gla_lo:gla_hi], w_in[:, :, n_main:gla_lo], w_in[:, :, gla_hi:],
         jnp.zeros((depth, d, N_MISC - ab - GLA_GATE_RANK), w_in.dtype)], axis=2).astype(BF16)
    wvt = jnp.swapaxes(jnp.concatenate([w_in[:, :, 2 * gw:3 * gw], w_in[:, :, 5 * gw:6 * gw]],
                                       axis=2), 1, 2).astype(BF16)
    h = x.reshape(m, d)
    for l in range(depth):
        qk, vt, gdn, gla, misc, kmean = _inproj(h, _row(norm_mix[l]), l, w1, wvt, col_scale, tm, s)
        qk = qk.reshape(b, s, N_QK)
        misc = misc.reshape(b, s, N_MISC)
        kmean = kmean.reshape(b, s // MOBA_BLOCK, GROUP_WIDTH)

        y_a = _moba(qk, vt, kmean, bias_a)

        lam_init = 0.8 - 0.6 * math.exp(-0.3 * l)
        lam_p = jnp.concatenate([diff_lambda[l].astype(F32),
                                 jnp.full((1, DIFF_QK_DIM), lam_init, F32)], axis=0)
        y_b = _diff(qk, vt, lam_p, bias_b, _row(jnp.tile(diff_norm[l], hg)))

        hp = jnp.concatenate([_row(gdn_a_log[l], LANES), _row(gdn_dt_bias[l], LANES)], axis=0)
        y_c = _gdn(gdn.reshape(b, s, N_GDN), misc, gdn_conv[l].astype(F32), hp,
                   _row(jnp.tile(gdn_norm[l], hg)))

        walpha = jnp.zeros((N_MISC, hg * GLA_DK), F32).at[MISC_LR:MISC_LR + GLA_GATE_RANK].set(
            gla_w_alpha[l]).astype(BF16)
        y_d = _gla(gla.reshape(b, s, N_GLA), misc, walpha, _row(gla_b_alpha[l]),
                   _row(jnp.tile(gla_norm[l], hg)))

        ys = [y.reshape(m, GROUP_WIDTH) for y in (y_a, y_b, y_c, y_d)]
        h = _mlp(h, ys, l, w_out_bf, _row(norm_ffn[l]), w_gate_bf, w_up_bf, w_down_bf,
                 _row(norm_ple[l]), w_ple_gate_bf, p3d, w_ple_proj_bf, fin, tm,
                 final=(l == depth - 1))
    return h.reshape(b, s, d)
```

```python
import functools
import math

import numpy as np
import jax
import jax.numpy as jnp
from jax import lax
from jax.experimental import pallas as pl
from jax.experimental.pallas import tpu as pltpu

F32 = jnp.float32
BF16 = jnp.bfloat16

HEAD_DIM = 64
N_HEADS_G = 4
GROUP_WIDTH = HEAD_DIM * N_HEADS_G
MOBA_BLOCK = 256
MOBA_TOPK = 3
N_REL_BUCKETS = 32
REL_MAX_EXACT = 16
REL_MAX_DIST = 128
DIFF_QK_DIM = HEAD_DIM // 2
CONV_WIDTH = 4
CHUNK = 64
GLA_DK = HEAD_DIM // 2
GLA_GATE_RANK = 16
GLA_TAU = 16.0
EPS = 1e-6

LANES = 128
F32_SUBLANES = 8
BF16_SUBLANES = 16
VMEM_LIMIT = 56 * 1024 * 1024
ROW_TILE = 512
ATTN_TILE = 256
ACC_ROWS = HEAD_DIM + BF16_SUBLANES
PIPE_CHAIN_TILES = 12
REC_SEQ_TILE = 256
REC_ROWS = 8
NEG = -1e30
LOG2E = math.log2(math.e)

N_QK = 4 * GROUP_WIDTH
N_VT = 2 * GROUP_WIDTH
N_GDN = 4 * GROUP_WIDTH
N_GLA = 2 * N_HEADS_G * GLA_DK + 2 * GROUP_WIDTH
N_MISC = LANES
MISC_A, MISC_B, MISC_LR = 0, N_HEADS_G, 2 * N_HEADS_G

_NT = (((1,), (1,)), ((), ()))
_TN = (((0,), (0,)), ((), ()))


def _dot(a, b):
    return jnp.dot(a, b, preferred_element_type=F32)


def _dot_nt(a, b):
    return lax.dot_general(a, b, _NT, preferred_element_type=F32)


def _dot_tn(a, b):
    return lax.dot_general(a, b, _TN, preferred_element_type=F32)


def _split_bf16(x, n):
    parts = []
    r = x
    for _ in range(n):
        hi = r.astype(BF16)
        parts.append(hi)
        r = r - hi.astype(F32)
    return parts


def _dot_exact_rhs(x, w, n):
    acc = None
    for part in _split_bf16(x, n):
        t = _dot(part, w)
        acc = t if acc is None else acc + t
    return acc


def _dot_exact_lhs(w, x, n):
    acc = None
    for part in _split_bf16(x, n):
        t = _dot(w, part)
        acc = t if acc is None else acc + t
    return acc


def _rms(x, g):
    return x * lax.rsqrt(jnp.mean(x * x, axis=-1, keepdims=True) + EPS) * g


def _sigmoid(x):
    return 1.0 / (1.0 + jnp.exp(-x))


def _silu(x):
    return x * _sigmoid(x)


def _inproj_kernel(x_ref, g_ref, w_ref, wvt_ref, scale_ref,
                   qk_ref, vt_ref, gdn_ref, gla_ref, misc_ref, kmean_ref):
    tm = x_ref.shape[0]
    xn = _rms(x_ref[...], g_ref[...]).astype(BF16)
    qk = _dot(xn, w_ref[:, 0:N_QK])
    k_moba = qk[:, GROUP_WIDTH:2 * GROUP_WIDTH]
    kmean_ref[0] = jnp.mean(k_moba.reshape(tm // MOBA_BLOCK, MOBA_BLOCK, GROUP_WIDTH), axis=1)
    qk_ref[...] = (qk * scale_ref[...]).astype(BF16)
    vt = _dot_nt(wvt_ref[...], xn).astype(BF16)
    for i in range(tm // ATTN_TILE):
        vt_ref[0, i] = vt[:, i * ATTN_TILE:(i + 1) * ATTN_TILE]
    o = N_QK
    gdn_ref[...] = _dot(xn, w_ref[:, o:o + N_GDN])
    o += N_GDN
    gla_ref[...] = _dot(xn, w_ref[:, o:o + N_GLA])
    o += N_GLA
    misc_ref[...] = _dot(xn, w_ref[:, o:o + N_MISC])


def _inproj(h2d, g, layer, w, wvt, scale, tm, seq):
    m, d = h2d.shape
    n_all = w.shape[2]
    per_seq = seq // tm
    const = lambda i: (0, 0)
    return pl.pallas_call(
        _inproj_kernel,
        grid=(m // tm,),
        in_specs=[
            pl.BlockSpec((tm, d), lambda i: (i, 0)),
            pl.BlockSpec((1, d), const),
            pl.BlockSpec((None, d, n_all), lambda i: (layer, 0, 0)),
            pl.BlockSpec((None, N_VT, d), lambda i: (layer, 0, 0)),
            pl.BlockSpec((1, N_QK), const),
        ],
        out_specs=[
            pl.BlockSpec((tm, N_QK), lambda i: (i, 0)),
            pl.BlockSpec((1, tm // ATTN_TILE, N_VT, ATTN_TILE),
                         lambda i: (i // per_seq, i % per_seq, 0, 0)),
            pl.BlockSpec((tm, N_GDN), lambda i: (i, 0)),
            pl.BlockSpec((tm, N_GLA), lambda i: (i, 0)),
            pl.BlockSpec((tm, N_MISC), lambda i: (i, 0)),
            pl.BlockSpec((1, tm // MOBA_BLOCK, GROUP_WIDTH), lambda i: (i, 0, 0)),
        ],
        out_shape=[
            jax.ShapeDtypeStruct((m, N_QK), BF16),
            jax.ShapeDtypeStruct((m // seq, seq // ATTN_TILE, N_VT, ATTN_TILE), BF16),
            jax.ShapeDtypeStruct((m, N_GDN), F32),
            jax.ShapeDtypeStruct((m, N_GLA), F32),
            jax.ShapeDtypeStruct((m, N_MISC), F32),
            jax.ShapeDtypeStruct((m // tm, tm // MOBA_BLOCK, GROUP_WIDTH), F32),
        ],
        compiler_params=pltpu.CompilerParams(
            dimension_semantics=("arbitrary",), vmem_limit_bytes=VMEM_LIMIT),
        name="inproj",
    )(h2d, g, w, wvt, scale)


def _lane_mask(lo, width):
    lane = lax.broadcasted_iota(jnp.int32, (1, LANES), 1)
    return (lane >= lo) & (lane < lo + width)


def _values_with_ones(vt, hh):
    head = vt[HEAD_DIM * hh:HEAD_DIM * (hh + 1), :]
    return jnp.concatenate([head, jnp.ones((ACC_ROWS - HEAD_DIM, vt.shape[1]), vt.dtype)], axis=0)


def _tile_max(scores):
    return [jnp.max(s, axis=0, keepdims=True) for s in scores]


def _softmax_stage(ms, scores, tile_maxes, valids=None):
    ms_new, alphas, ps = [], [], []
    for c, (m, s, m_tile) in enumerate(zip(ms, scores, tile_maxes)):
        if valids is None or valids[c] is None:
            m_new = jnp.maximum(m, m_tile)
            shift = m_new
        else:
            ok = valids[c] > 0.5
            m_new = jnp.maximum(m, jnp.where(ok, m_tile, NEG))
            shift = jnp.where(ok, m_new, -NEG)
        ms_new.append(m_new)
        alphas.append(jnp.exp2(m - m_new))
        ps.append(jnp.exp2(s - shift).astype(BF16))
    return ms_new, alphas, ps


def _value_products(vts, ps):
    return [_dot(vt, p) for vt, p in zip(vts, ps)]


def _attend_all(n, t, n_far, far_scores, far_valid, prev_scores, prev_valid, own_scores, values,
                j_prev, j_own, s_ref, p_ref, acc_ref, first_scores=None):
    chains = range(n)

    def tile_of(pos):
        return jnp.where(pos == 0, j_prev, jnp.where(pos == 1, j_own, pos - 2))

    def valid_of(pos):
        rows = far_valid(jnp.maximum(pos - 2, 0))
        if rows is None:
            return None
        return [jnp.where(pos == 1, 1.0, row) for row in rows]

    s0, s1 = first_scores if first_scores is not None else (prev_scores(), own_scores())
    ms, alphas, ps = _softmax_stage([jnp.full((1, t), NEG, F32)] * n, s0, _tile_max(s0),
                                    prev_valid())
    tmax = _tile_max(s1)
    for c in chains:
        p_ref[c] = ps[c]
        s_ref[c] = s1[c]
        acc_ref[c] = jnp.zeros((ACC_ROWS, t), F32)

    def steps(first, unroll, carry):
        ms, alphas, tmax = carry
        ps = [p_ref[c] for c in chains]
        s_cur = [s_ref[c] for c in chains]
        for u in range(unroll):
            pos = first + u
            vts = values(tile_of(pos))
            late = 2 if unroll == 1 else 1
            pv = _value_products(vts[:n - late], ps[:n - late])
            s_new = far_scores(pos)
            pv += _value_products(vts[n - late:], ps[n - late:])
            tmax_new = _tile_max(s_new)
            ms, alphas_next, ps = _softmax_stage(ms, s_cur, tmax, valid_of(pos + 1))
            for c in chains:
                acc_ref[c] = alphas[c] * acc_ref[c] + pv[c]
            alphas, s_cur, tmax = alphas_next, s_new, tmax_new
        for c in chains:
            p_ref[c] = ps[c]
            s_ref[c] = s_cur[c]
        return ms, alphas, tmax

    n_steps = n_far
    unroll = max(1, PIPE_CHAIN_TILES // n)
    n_blocks = n_steps // unroll
    carry = lax.fori_loop(0, n_blocks, lambda i, cr: steps(i * unroll, unroll, cr),
                          (ms, alphas, tmax))
    if unroll > 1:
        carry = lax.fori_loop(n_blocks * unroll, n_steps, lambda i, cr: steps(i, 1, cr), carry)
    ms, alphas, tmax = carry

    pv = _value_products(values(tile_of(n_far)), [p_ref[c] for c in chains])
    ms, alphas1, ps1 = _softmax_stage(ms, [s_ref[c] for c in chains], tmax, valid_of(n_far + 1))
    accs = [alphas[c] * acc_ref[c] + pv[c] for c in chains]
    pv = _value_products(values(tile_of(n_far + 1)), ps1)
    return [alphas1[c] * accs[c] + pv[c] for c in chains]


def _normalised(acc):
    return acc[0:HEAD_DIM, :] / acc[HEAD_DIM:HEAD_DIM + 1, :]


def _moba_kernel(q_ref, k_ref, vt_ref, kmean_ref, bias_ref, o_ref, sel_ref, s_ref, p_ref, acc_ref):
    t = ATTN_TILE
    qi = pl.program_id(2)
    nblk = kmean_ref.shape[1]
    heads = range(q_ref.shape[2] // HEAD_DIM)
    q = q_ref[0]
    kmean = kmean_ref[0].astype(BF16)

    def pair_lanes(x, h):
        lo = LANES * (h // 2)
        return x[:, lo:lo + LANES]

    qms = []
    for h in heads:
        qp = pair_lanes(q, h)
        qms.append(jnp.where(_lane_mask(HEAD_DIM * (h % 2), HEAD_DIM), qp, jnp.zeros_like(qp)))

    def keys(j):
        return k_ref[0, pl.ds(pl.multiple_of(j * t, t), t), :]

    def far_scores(j):
        k = keys(j)
        return [_dot_nt(pair_lanes(k, h), qms[h]) for h in heads]

    jp = jnp.maximum(qi - 1, 0)

    def prev_scores():
        return [s + bias_ref[h, 1] for h, s in zip(heads, far_scores(jp))]

    def own_scores():
        return [s + bias_ref[h, 0] for h, s in zip(heads, far_scores(qi))]

    gates = [_dot_nt(pair_lanes(kmean, h), qms[h]) for h in heads]
    first_scores = prev_scores(), own_scores()

    row = lax.broadcasted_iota(jnp.int32, (nblk, t), 0)
    past = row < qi
    for h in heads:
        gate = jnp.where(past, gates[h], -jnp.inf)
        sel_t = jnp.zeros((nblk, t), F32)
        for _ in range(MOBA_TOPK):
            top = jnp.max(gate, axis=0, keepdims=True)
            first = jnp.min(jnp.where(gate == top, row, nblk), axis=0, keepdims=True)
            pick = row == first
            sel_t = jnp.where(pick, 1.0, sel_t)
            gate = jnp.where(pick, -jnp.inf, gate)
        sel_t = jnp.where(past, sel_t, 0.0)
        for j in range(nblk):
            sel_ref[h * nblk + j] = sel_t[j:j + 1, :]

    def values(j):
        vt = vt_ref[0, j]
        return [_values_with_ones(vt, h) for h in heads]

    def far_valid(j):
        return [sel_ref[h * nblk + j] for h in heads]

    accs = _attend_all(len(heads), t, jnp.maximum(qi - 1, 0), far_scores, far_valid, prev_scores,
                       lambda: far_valid(jp), own_scores, values, jp, qi, s_ref, p_ref, acc_ref,
                       first_scores)
    out_t = jnp.concatenate([_normalised(acc) for acc in accs], axis=0)
    o_ref[0] = out_t.T.astype(o_ref.dtype)


def _attn_specs(s, t, mixer, width):
    nkv = s // t
    per = GROUP_WIDTH // width
    qcol = mixer * 2 * per
    kcol = qcol + per
    vrow = mixer * per
    return [
        pl.BlockSpec((1, t, width), lambda bi, g, qi: (bi, qi, qcol + g)),
        pl.BlockSpec((1, s, width), lambda bi, g, qi: (bi, 0, kcol + g)),
        pl.BlockSpec((1, nkv, width, t), lambda bi, g, qi: (bi, 0, vrow + g, 0)),
    ]


def _attn_scratch(n_chains, t):
    return [pltpu.VMEM((n_chains, t, t), F32),
            pltpu.VMEM((n_chains, t, t), BF16),
            pltpu.VMEM((n_chains, ACC_ROWS, t), F32)]


def _moba(qk, vt, kmean, bias_t):
    b, s, _ = qk.shape
    t = ATTN_TILE
    nq = s // t
    nblk = kmean.shape[1]
    n_heads = N_HEADS_G
    width = n_heads * HEAD_DIM
    return pl.pallas_call(
        _moba_kernel,
        grid=(b, GROUP_WIDTH // width, nq),
        in_specs=_attn_specs(s, t, 0, width) + [
            pl.BlockSpec((1, nblk, width), lambda bi, g, qi: (bi, 0, g)),
            pl.BlockSpec((n_heads, 2, t, t), lambda bi, g, qi: (g, 0, 0, 0)),
        ],
        out_specs=pl.BlockSpec((1, t, width), lambda bi, g, qi: (bi, qi, g)),
        out_shape=jax.ShapeDtypeStruct((b, s, GROUP_WIDTH), BF16),
        scratch_shapes=[pltpu.VMEM((n_heads * nblk, 1, t), F32)]
        + _attn_scratch(n_heads, t),
        compiler_params=pltpu.CompilerParams(
            dimension_semantics=("arbitrary", "arbitrary", "arbitrary"),
            vmem_limit_bytes=VMEM_LIMIT),
        name="moba",
    )(qk, qk, vt, kmean, bias_t)


def _diff_kernel(lam_ref, q_ref, k_ref, vt_ref, bias_ref, gnorm_ref, o_ref, s_ref, p_ref, acc_ref):
    t = ATTN_TILE
    qi = pl.program_id(2)
    heads = range(q_ref.shape[2] // HEAD_DIM)
    chains = [(h, mm) for h in heads for mm in range(2)]
    q = q_ref[0]

    def pair_lanes(x, h):
        lo = LANES * (h // 2)
        return x[:, lo:lo + LANES]

    qms = []
    for h, mm in chains:
        qp = pair_lanes(q, h)
        mask = _lane_mask(HEAD_DIM * (h % 2) + DIFF_QK_DIM * mm, DIFF_QK_DIM)
        qms.append(jnp.where(mask, qp, jnp.zeros_like(qp)))

    def keys(j):
        return k_ref[0, pl.ds(pl.multiple_of(j * t, t), t), :]

    def values(j):
        vt = vt_ref[0, j]
        vts = [_values_with_ones(vt, h) for h in heads]
        return [vts[h] for h, _ in chains]

    def far_scores(j):
        k = keys(j)
        return [_dot_nt(pair_lanes(k, h), qm) for (h, _), qm in zip(chains, qms)]

    jp = jnp.maximum(qi - 1, 0)

    def prev_scores():
        return [s + bias_ref[h, 1] for (h, _), s in zip(chains, far_scores(jp))]

    def prev_valid():
        return [jnp.full((1, t), jnp.where(qi >= 1, 1.0, 0.0), F32)] * len(chains)

    def own_scores():
        return [s + bias_ref[h, 0] for (h, _), s in zip(chains, far_scores(qi))]

    lam_p = lam_ref[...]
    lam_init = lam_p[4:5, 0:1]
    lam = (jnp.exp(jnp.sum(lam_p[0:1] * lam_p[1:2], axis=-1, keepdims=True))
           - jnp.exp(jnp.sum(lam_p[2:3] * lam_p[3:4], axis=-1, keepdims=True)) + lam_init)

    accs = _attend_all(len(chains), t, jnp.maximum(qi - 1, 0), far_scores, lambda j: None,
                       prev_scores, prev_valid, own_scores, values, jp, qi, s_ref, p_ref, acc_ref)
    outs = []
    for h in heads:
        o = _normalised(accs[2 * h]) - lam * _normalised(accs[2 * h + 1])
        ms = jnp.mean(o * o, axis=0, keepdims=True)
        outs.append(o * lax.rsqrt(ms + EPS))
    y_t = jnp.concatenate(outs, axis=0)
    o_ref[0] = (y_t.T * gnorm_ref[...] * (1.0 - lam_init)).astype(o_ref.dtype)


def _diff(qk, vt, lam_p, bias_t, gnorm):
    b, s, _ = qk.shape
    t = ATTN_TILE
    nq = s // t
    n_heads = N_HEADS_G
    width = n_heads * HEAD_DIM
    return pl.pallas_call(
        _diff_kernel,
        grid=(b, GROUP_WIDTH // width, nq),
        in_specs=[pl.BlockSpec(lam_p.shape, lambda bi, g, qi: (0, 0))]
        + _attn_specs(s, t, 1, width) + [
            pl.BlockSpec((n_heads, 2, t, t), lambda bi, g, qi: (g, 0, 0, 0)),
            pl.BlockSpec((1, width), lambda bi, g, qi: (0, g)),
        ],
        out_specs=pl.BlockSpec((1, t, width), lambda bi, g, qi: (bi, qi, g)),
        out_shape=jax.ShapeDtypeStruct((b, s, GROUP_WIDTH), BF16),
        scratch_shapes=_attn_scratch(2 * n_heads, t),
        compiler_params=pltpu.CompilerParams(
            dimension_semantics=("arbitrary", "arbitrary", "arbitrary"),
            vmem_limit_bytes=VMEM_LIMIT),
        name="diff_attn",
    )(lam_p, qk, qk, vt, bias_t, gnorm)


def _head_of_lane(n_lanes, width):
    return lax.broadcasted_iota(jnp.int32, (1, n_lanes), 1) // width


def _block_rows(x, lane_head, n_heads=N_HEADS_G):
    zero = jnp.zeros_like(x)
    return jnp.concatenate([jnp.where(lane_head == h, x, zero) for h in range(n_heads)], axis=0)


def _gdn_kernel(x_ref, misc_ref, conv_ref, hp_ref, gnorm_ref, o_ref, s_ref, tail_ref, g_ref, beta_ref):
    L = CHUNK
    W = GROUP_WIDTH
    rows = range(x_ref.shape[0])
    seq = x_ref.shape[1]
    n_chunks = seq // L
    head_w = _head_of_lane(W, HEAD_DIM)
    ri = lax.broadcasted_iota(jnp.int32, (L, W), 0)
    cj = lax.broadcasted_iota(jnp.int32, (L, W), 1) % HEAD_DIM
    lower = ri >= cj
    strict = ri > cj
    ident = jnp.where(ri == cj, 1.0, 0.0)
    ones_blk = (lax.broadcasted_iota(jnp.int32, (W, W), 0) // HEAD_DIM
                == lax.broadcasted_iota(jnp.int32, (W, W), 1) // HEAD_DIM)
    ones_seg = ones_blk.astype(BF16)
    tril = (lax.broadcasted_iota(jnp.int32, (L, L), 0)
            >= lax.broadcasted_iota(jnp.int32, (L, L), 1)).astype(BF16)

    hp = hp_ref[...]
    exp_g = (lax.broadcasted_iota(jnp.int32, (LANES, W), 0) - MISC_A
             == lax.broadcasted_iota(jnp.int32, (LANES, W), 1) // HEAD_DIM).astype(BF16)
    exp_b = (lax.broadcasted_iota(jnp.int32, (LANES, W), 0) - MISC_B
             == lax.broadcasted_iota(jnp.int32, (LANES, W), 1) // HEAD_DIM).astype(BF16)
    for r in rows:
        misc = misc_ref[r]
        sp_in = misc + hp[1:2]
        softplus = jnp.maximum(sp_in, 0.0) + jnp.log(1.0 + jnp.exp(-jnp.abs(sp_in)))
        g_tok = -jnp.exp(hp[0:1]) * softplus
        b_tok = _sigmoid(misc)
        g_ref[r] = _dot_exact_rhs(g_tok, exp_g, 2)
        beta_ref[r] = _dot_exact_rhs(b_tok, exp_b, 2)

    @pl.when(pl.program_id(1) == 0)
    def _():
        s_ref[...] = jnp.zeros_like(s_ref)
        tail_ref[...] = jnp.zeros_like(tail_ref)

    cw = conv_ref[...]
    blk = lambda a: _block_rows(a.astype(BF16), head_w)
    bf = lambda a: a.astype(BF16)

    def prepare(r, r0):
        x = x_ref[r, pl.ds(r0, L), :]
        qkv = x[:, 0:3 * W]
        xx = jnp.concatenate([tail_ref[r], qkv], axis=0)
        tail_ref[r] = qkv[L - F32_SUBLANES:L, :]
        conv = cw[CONV_WIDTH - 1:CONV_WIDTH] * qkv
        for i in range(CONV_WIDTH - 1):
            lo = F32_SUBLANES - (CONV_WIDTH - 1) + i
            conv = conv + cw[i:i + 1] * xx[lo:lo + L, :]
        conv = _silu(conv)
        return conv[:, 0:W], conv[:, W:2 * W], conv[:, 2 * W:3 * W], x[:, 3 * W:4 * W]

    def decays(gc):
        gc_col = jnp.sum(gc * ident, axis=0, keepdims=True)
        return jnp.exp(jnp.where(lower, gc - gc_col, -jnp.inf))

    def chunk(c, _):
        r0 = pl.multiple_of(c * L, L)
        q, k, v, z = zip(*[prepare(r, r0) for r in rows])
        ssq = [_dot_exact_rhs(jnp.concatenate([q[r] * q[r], k[r] * k[r]], axis=0), ones_seg, 1)
               for r in rows]
        q = [q[r] * lax.rsqrt(ssq[r][0:L] + EPS) * (HEAD_DIM ** -0.5) for r in rows]
        k = [k[r] * lax.rsqrt(ssq[r][L:2 * L] + EPS) for r in rows]
        beta = [beta_ref[r, pl.ds(r0, L), :] for r in rows]
        gc = [_dot_exact_lhs(tril, g_ref[r, pl.ds(r0, L), :], 2) for r in rows]
        egc = [jnp.exp(gc[r]) for r in rows]
        gc_last = [gc[r][L - 1:L, :] for r in rows]
        decay = [decays(gc[r]) for r in rows]
        kb = [k[r] * beta[r] for r in rows]
        both = [_dot_nt(bf(jnp.concatenate([kb[r], q[r]], axis=0)), blk(k[r])) for r in rows]
        qk = [jnp.where(lower, both[r][L:2 * L] * decay[r], 0.0) for r in rows]

        p = [jnp.where(strict, -both[r][0:L] * decay[r], 0.0) for r in rows]
        t_inv = [ident + p[r] for r in rows]
        p = [_dot(bf(p[r]), blk(p[r])) for r in rows]
        for _ in range(4):
            prod = [_dot(bf(jnp.concatenate([t_inv[r], p[r]], axis=0)), blk(p[r])) for r in rows]
            t_inv = [t_inv[r] + prod[r][0:L] for r in rows]
            p = [prod[r][L:2 * L] for r in rows]
        t_inv = [t_inv[r] + _dot(bf(t_inv[r]), blk(p[r])) for r in rows]

        u = [_dot(bf(t_inv[r]), blk(v[r] * beta[r])) for r in rows]
        w = [_dot(bf(t_inv[r]), blk(kb[r] * egc[r])) for r in rows]

        state = [s_ref[r] for r in rows]
        ws_qs = [_dot(bf(jnp.concatenate([w[r], q[r] * egc[r]], axis=0)), bf(state[r])) for r in rows]
        v_new = [u[r] - ws_qs[r][0:L] for r in rows]
        o = [ws_qs[r][L:2 * L] + _dot(bf(qk[r]), blk(v_new[r])) for r in rows]
        upd = [_dot_tn(bf(k[r] * jnp.exp(gc_last[r] - gc[r])), bf(v_new[r])) for r in rows]
        for r in rows:
            s_ref[r] = state[r] * jnp.exp(gc_last[r]) + jnp.where(ones_blk, upd[r], 0.0)

        ms = [_dot_exact_rhs(o[r] * o[r], ones_seg, 1) * (1.0 / HEAD_DIM) for r in rows]
        for r in rows:
            y = o[r] * lax.rsqrt(ms[r] + EPS) * gnorm_ref[...] * _silu(z[r])
            o_ref[r, pl.ds(r0, L), :] = y.astype(o_ref.dtype)
        return 0

    lax.fori_loop(0, n_chunks, chunk, 0)


def _gdn(gdn, misc, conv_w, hp, gnorm):
    b, s, _ = gdn.shape
    ts = min(s, REC_SEQ_TILE)
    nr = math.gcd(b, REC_ROWS)
    const = lambda bi, si: (0, 0)
    return pl.pallas_call(
        _gdn_kernel,
        grid=(b // nr, s // ts),
        in_specs=[
            pl.BlockSpec((nr, ts, N_GDN), lambda bi, si: (bi, si, 0)),
            pl.BlockSpec((nr, ts, N_MISC), lambda bi, si: (bi, si, 0)),
            pl.BlockSpec(conv_w.shape, const),
            pl.BlockSpec(hp.shape, const),
            pl.BlockSpec((1, GROUP_WIDTH), const),
        ],
        out_specs=pl.BlockSpec((nr, ts, GROUP_WIDTH), lambda bi, si: (bi, si, 0)),
        out_shape=jax.ShapeDtypeStruct((b, s, GROUP_WIDTH), BF16),
        scratch_shapes=[
            pltpu.VMEM((nr, GROUP_WIDTH, GROUP_WIDTH), F32),
            pltpu.VMEM((nr, F32_SUBLANES, 3 * GROUP_WIDTH), F32),
            pltpu.VMEM((nr, ts, GROUP_WIDTH), F32),
            pltpu.VMEM((nr, ts, GROUP_WIDTH), F32),
        ],
        compiler_params=pltpu.CompilerParams(
            dimension_semantics=("arbitrary", "arbitrary"), vmem_limit_bytes=VMEM_LIMIT),
        name="gdn",
    )(gdn, misc, conv_w, hp, gnorm)


def _gla_kernel(x_ref, misc_ref, walpha_ref, balpha_ref, gnorm_ref, o_ref, s_ref, la_ref):
    L = CHUNK
    W = GROUP_WIDTH
    KW = N_HEADS_G * GLA_DK
    rows = range(x_ref.shape[0])
    seq = x_ref.shape[1]
    n_chunks = seq // L
    head_k = _head_of_lane(KW, GLA_DK)
    head_v = _head_of_lane(W, HEAD_DIM)
    ri = lax.broadcasted_iota(jnp.int32, (L, W), 0)
    cj = lax.broadcasted_iota(jnp.int32, (L, W), 1) % HEAD_DIM
    lower = ri >= cj
    ones_blk = (lax.broadcasted_iota(jnp.int32, (W, W), 0) // HEAD_DIM
                == lax.broadcasted_iota(jnp.int32, (W, W), 1) // HEAD_DIM).astype(BF16)
    state_mask = (lax.broadcasted_iota(jnp.int32, (W, KW), 0) // HEAD_DIM
                  == lax.broadcasted_iota(jnp.int32, (W, KW), 1) // GLA_DK)
    tril = (lax.broadcasted_iota(jnp.int32, (L, L), 0)
            >= lax.broadcasted_iota(jnp.int32, (L, L), 1)).astype(BF16)

    for r in rows:
        pre = _dot(misc_ref[r].astype(BF16), walpha_ref[...]) + balpha_ref[...]
        log_sig = jnp.minimum(pre, 0.0) - jnp.log(1.0 + jnp.exp(-jnp.abs(pre)))
        la_ref[r] = log_sig * (1.0 / GLA_TAU)

    @pl.when(pl.program_id(1) == 0)
    def _():
        s_ref[...] = jnp.zeros_like(s_ref)

    bf = lambda a: a.astype(BF16)

    def chunk(c, _):
        r0 = pl.multiple_of(c * L, L)
        x = [x_ref[r, pl.ds(r0, L), :] for r in rows]
        k = [x[r][:, KW:2 * KW] for r in rows]
        v = [bf(x[r][:, 2 * KW:2 * KW + W]) for r in rows]
        bc = [_dot_exact_lhs(tril, la_ref[r, pl.ds(r0, L), :], 2) for r in rows]
        b_last = [bc[r][L - 1:L, :] for r in rows]
        q = [x[r][:, 0:KW] * (GLA_DK ** -0.5) for r in rows]
        qe = [bf(q[r] * jnp.exp(bc[r])) for r in rows]
        b_mid = [bc[r] - bc[r][L // 2:L // 2 + 1, :] for r in rows]
        qm = [bf(q[r] * jnp.exp(b_mid[r])) for r in rows]
        km = [bf(k[r] * jnp.exp(-b_mid[r])) for r in rows]
        a_mat = [jnp.where(lower, _dot_nt(qm[r], _block_rows(km[r], head_k)), 0.0) for r in rows]
        state_t = [s_ref[r] for r in rows]
        o = [_dot_nt(qe[r], bf(state_t[r])) for r in rows]
        o = [o[r] + _dot(bf(a_mat[r]), _block_rows(v[r], head_v)) for r in rows]
        upd = [_dot_tn(v[r], bf(k[r] * jnp.exp(b_last[r] - bc[r]))) for r in rows]
        for r in rows:
            s_ref[r] = state_t[r] * jnp.exp(b_last[r]) + jnp.where(state_mask, upd[r], 0.0)

        ms = [_dot_exact_rhs(o[r] * o[r], ones_blk, 1) * (1.0 / HEAD_DIM) for r in rows]
        for r in rows:
            gate = _silu(x[r][:, 2 * KW + W:2 * KW + 2 * W])
            y = o[r] * lax.rsqrt(ms[r] + EPS) * gnorm_ref[...] * gate
            o_ref[r, pl.ds(r0, L), :] = y.astype(o_ref.dtype)
        return 0

    lax.fori_loop(0, n_chunks, chunk, 0)


def _gla(gla, misc, walpha, balpha, gnorm):
    b, s, _ = gla.shape
    ts = min(s, REC_SEQ_TILE)
    nr = math.gcd(b, REC_ROWS)
    const = lambda bi, si: (0, 0)
    return pl.pallas_call(
        _gla_kernel,
        grid=(b // nr, s // ts),
        in_specs=[
            pl.BlockSpec((nr, ts, N_GLA), lambda bi, si: (bi, si, 0)),
            pl.BlockSpec((nr, ts, N_MISC), lambda bi, si: (bi, si, 0)),
            pl.BlockSpec(walpha.shape, const),
            pl.BlockSpec(balpha.shape, const),
            pl.BlockSpec((1, GROUP_WIDTH), const),
        ],
        out_specs=pl.BlockSpec((nr, ts, GROUP_WIDTH), lambda bi, si: (bi, si, 0)),
        out_shape=jax.ShapeDtypeStruct((b, s, GROUP_WIDTH), BF16),
        scratch_shapes=[
            pltpu.VMEM((nr, GROUP_WIDTH, N_HEADS_G * GLA_DK), F32),
            pltpu.VMEM((nr, ts, N_HEADS_G * GLA_DK), F32),
        ],
        compiler_params=pltpu.CompilerParams(
            dimension_semantics=("arbitrary", "arbitrary"), vmem_limit_bytes=VMEM_LIMIT),
        name="gla",
    )(gla, misc, walpha, balpha, gnorm)


def _ffn_chunks(d_ff):
    step = 4 * GROUP_WIDTH
    return [(lo, min(step, d_ff - lo)) for lo in range(0, d_ff, step)]


def _mlp_kernel(h_ref, ya_ref, yb_ref, yc_ref, yd_ref, wout_ref, gffn_ref, wg_ref, wu_ref, wd_ref,
                gple_ref, wpg_ref, p_ref, wpp_ref, gfin_ref, o_ref, act_ref, *, final):
    mixed = jnp.concatenate([ya_ref[...], yb_ref[...], yc_ref[...], yd_ref[...]], axis=-1)
    h = h_ref[...] + _dot(mixed, wout_ref[...])
    hn = _rms(h, gffn_ref[...]).astype(BF16)
    for lo, width in _ffn_chunks(wg_ref.shape[1]):
        gate = _dot(hn, wg_ref[:, lo:lo + width])
        up = _dot(hn, wu_ref[:, lo:lo + width])
        act_ref[:, lo:lo + width] = (_silu(gate) * up).astype(BF16)
    h = h + _dot(act_ref[...], wd_ref[...])
    gate = _sigmoid(_dot(_rms(h, gple_ref[...]).astype(BF16), wpg_ref[...]))
    h = h + gate * _dot(p_ref[...].astype(BF16), wpp_ref[...])
    if final:
        h = _rms(h, gfin_ref[...])
    o_ref[...] = h


def _mlp(h2d, ys, layer, wout, gffn, wg, wu, wd, gple, wpg, p3d, wpp, gfin, tm, final):
    m, d = h2d.shape
    d_ff = wg.shape[2]
    const = lambda i: (0, 0)
    resident = lambda shape: pl.BlockSpec(shape, const, pipeline_mode=pl.Buffered(1))
    stacked = lambda a: pl.BlockSpec((None,) + a.shape[1:], lambda i: (layer, 0, 0),
                                     pipeline_mode=pl.Buffered(1))
    rows = lambda width: pl.BlockSpec((tm, width), lambda i: (i, 0))
    return pl.pallas_call(
        functools.partial(_mlp_kernel, final=final),
        grid=(m // tm,),
        in_specs=[rows(d)] + [rows(GROUP_WIDTH)] * 4 + [
            stacked(wout), resident(gffn.shape), stacked(wg), stacked(wu),
            stacked(wd), resident(gple.shape), stacked(wpg),
            pl.BlockSpec((None, tm, p3d.shape[2]), lambda i: (layer, i, 0)), stacked(wpp),
            resident(gfin.shape),
        ],
        out_specs=rows(d),
        out_shape=jax.ShapeDtypeStruct((m, d), F32),
        scratch_shapes=[pltpu.VMEM((tm, d_ff), BF16)],
        compiler_params=pltpu.CompilerParams(
            dimension_semantics=("arbitrary",), vmem_limit_bytes=VMEM_LIMIT),
        name="mlp",
    )(h2d, *ys, wout, gffn, wg, wu, wd, gple, wpg, p3d, wpp, gfin)


def _rel_bucket_table(n_dist):
    n = np.arange(n_dist)
    nf = np.maximum(n, 1).astype(np.float32)
    large = REL_MAX_EXACT + (np.log(nf / REL_MAX_EXACT) / math.log(REL_MAX_DIST / REL_MAX_EXACT)
                             * (N_REL_BUCKETS - REL_MAX_EXACT)).astype(np.int32)
    large = np.minimum(large, N_REL_BUCKETS - 1)
    return np.where(n < REL_MAX_EXACT, n, large)


def _bias_tiles(rel_bias):
    t = ATTN_TILE
    bucket = _rel_bucket_table(2 * t)
    assert (bucket[t + 1:] == N_REL_BUCKETS - 1).all()
    per_dist = (rel_bias[bucket, :] - rel_bias[N_REL_BUCKETS - 1][None, :]).T * LOG2E
    own = jnp.concatenate([per_dist[:, :t], jnp.full_like(per_dist[:, :t], NEG)], axis=1)
    prev = jnp.concatenate([per_dist[:, t:], per_dist[:, :t]], axis=1)
    vec = jnp.stack([own, prev], axis=1)
    rolled = jnp.tile(vec, (1, 1, t))[:, :, :t * (2 * t - 1)].reshape(-1, 2, t, 2 * t - 1)
    return rolled[:, :, :, :t]


def _row(v, width=None):
    v = v.astype(F32).reshape(1, -1)
    if width is not None and v.shape[1] < width:
        v = jnp.pad(v, ((0, 0), (0, width - v.shape[1])))
    return v


def kernel(x, p, norm_mix, w_in, rel_bias, diff_lambda, diff_norm, gdn_conv, gdn_a_log, gdn_dt_bias,
           gdn_norm, gla_w_alpha, gla_b_alpha, gla_norm, w_out, norm_ffn, w_gate, w_up, w_down,
           norm_ple, w_ple_gate, w_ple_proj, final_norm):
    b, s, d = x.shape
    depth = w_in.shape[0]
    m = b * s
    tm = ROW_TILE
    hg = N_HEADS_G
    assert ATTN_TILE == MOBA_BLOCK
    assert s % tm == 0 and tm % ATTN_TILE == 0 and s >= 2 * ATTN_TILE
    assert s % REC_SEQ_TILE == 0 and REC_SEQ_TILE % CHUNK == 0

    bias_t = _bias_tiles(rel_bias.astype(F32))
    bias_a, bias_b = bias_t[:hg], bias_t[hg:]
    gw = GROUP_WIDTH
    col_scale = np.ones((1, N_QK), np.float32)
    col_scale[:, 0:gw] = HEAD_DIM ** -0.5
    col_scale[:, gw:2 * gw] = LOG2E
    col_scale[:, 2 * gw:3 * gw] = DIFF_QK_DIM ** -0.5 * LOG2E
    col_scale = jnp.asarray(col_scale)

    n_main = 6 * gw + N_GDN
    ab = 2 * hg
    gla_lo = n_main + ab
    gla_hi = gla_lo + N_GLA
    fin = _row(final_norm)
    w_out_bf, w_gate_bf, w_up_bf, w_down_bf, w_ple_gate_bf, w_ple_proj_bf = (
        w.astype(BF16) for w in (w_out, w_gate, w_up, w_down, w_ple_gate, w_ple_proj))
    p3d = p.reshape(depth, m, p.shape[-1])
    w1 = jnp.concatenate(
        [w_in[:, :, 0:2 * gw], w_in[:, :, 3 * gw:5 * gw], w_in[:, :, 6 * gw:n_main],
         w_in[:, :, gla_lo:gla_hi], w_in[:, :, n_main:gla_lo], w_in[:, :, gla_hi:],
         jnp.zeros((depth, d, N_MISC - ab - GLA_GATE_RANK), w_in.dtype)], axis=2).astype(BF16)
    wvt = jnp.swapaxes(jnp.concatenate([w_in[:, :, 2 * gw:3 * gw], w_in[:, :, 5 * gw:6 * gw]],
                                       axis=2), 1, 2).astype(BF16)
    h = x.reshape(m, d)
    for l in range(depth):
        qk, vt, gdn, gla, misc, kmean = _inproj(h, _row(norm_mix[l]), l, w1, wvt, col_scale, tm, s)
        qk = qk.reshape(b, s, N_QK)
        misc = misc.reshape(b, s, N_MISC)
        kmean = kmean.reshape(b, s // MOBA_BLOCK, GROUP_WIDTH)

        y_a = _moba(qk, vt, kmean, bias_a)

        lam_init = 0.8 - 0.6 * math.exp(-0.3 * l)
        lam_p = jnp.concatenate([diff_lambda[l].astype(F32),
                                 jnp.full((1, DIFF_QK_DIM), lam_init, F32)], axis=0)
        y_b = _diff(qk, vt, lam_p, bias_b, _row(jnp.tile(diff_norm[l], hg)))

        hp = jnp.concatenate([_row(gdn_a_log[l], LANES), _row(gdn_dt_bias[l], LANES)], axis=0)
        y_c = _gdn(gdn.reshape(b, s, N_GDN), misc, gdn_conv[l].astype(F32), hp,
                   _row(jnp.tile(gdn_norm[l], hg)))

        walpha = jnp.zeros((N_MISC, hg * GLA_DK), F32).at[MISC_LR:MISC_LR + GLA_GATE_RANK].set(
            gla_w_alpha[l]).astype(BF16)
        y_d = _gla(gla.reshape(b, s, N_GLA), misc, walpha, _row(gla_b_alpha[l]),
                   _row(jnp.tile(gla_norm[l], hg)))

        ys = [y.reshape(m, GROUP_WIDTH) for y in (y_a, y_b, y_c, y_d)]
        h = _mlp(h, ys, l, w_out_bf, _row(norm_ffn[l]), w_gate_bf, w_up_bf, w_down_bf,
                 _row(norm_ple[l]), w_ple_gate_bf, p3d, w_ple_proj_bf, fin, tm,
                 final=(l == depth - 1))
    return h.reshape(b, s, d)
```

```python
import functools
import math

import numpy as np
import jax
import jax.numpy as jnp
from jax import lax
from jax.experimental import pallas as pl
from jax.experimental.pallas import tpu as pltpu

F32 = jnp.float32
BF16 = jnp.bfloat16

HEAD_DIM = 64
N_HEADS_G = 4
GROUP_WIDTH = HEAD_DIM * N_HEADS_G
MOBA_BLOCK = 256
MOBA_TOPK = 3
N_REL_BUCKETS = 32
REL_MAX_EXACT = 16
REL_MAX_DIST = 128
DIFF_QK_DIM = HEAD_DIM // 2
CONV_WIDTH = 4
CHUNK = 64
GLA_DK = HEAD_DIM // 2
GLA_GATE_RANK = 16
GLA_TAU = 16.0
EPS = 1e-6

LANES = 128
F32_SUBLANES = 8
BF16_SUBLANES = 16
VMEM_LIMIT = 56 * 1024 * 1024
ROW_TILE = 512
ATTN_TILE = 256
ACC_ROWS = HEAD_DIM + BF16_SUBLANES
PIPE_CHAIN_TILES = 12
REC_SEQ_TILE = 256
REC_ROWS = 8
NEG = -1e30
LOG2E = math.log2(math.e)

N_QK = 4 * GROUP_WIDTH
N_VT = 2 * GROUP_WIDTH
N_GDN = 4 * GROUP_WIDTH
N_GLA = 2 * N_HEADS_G * GLA_DK + 2 * GROUP_WIDTH
N_MISC = LANES
MISC_A, MISC_B, MISC_LR = 0, N_HEADS_G, 2 * N_HEADS_G

_NT = (((1,), (1,)), ((), ()))
_TN = (((0,), (0,)), ((), ()))


def _dot(a, b):
    return jnp.dot(a, b, preferred_element_type=F32)


def _dot_nt(a, b):
    return lax.dot_general(a, b, _NT, preferred_element_type=F32)


def _dot_tn(a, b):
    return lax.dot_general(a, b, _TN, preferred_element_type=F32)


def _split_bf16(x, n):
    parts = []
    r = x
    for _ in range(n):
        hi = r.astype(BF16)
        parts.append(hi)
        r = r - hi.astype(F32)
    return parts


def _dot_exact_rhs(x, w, n):
    acc = None
    for part in _split_bf16(x, n):
        t = _dot(part, w)
        acc = t if acc is None else acc + t
    return acc


def _dot_exact_lhs(w, x, n):
    acc = None
    for part in _split_bf16(x, n):
        t = _dot(w, part)
        acc = t if acc is None else acc + t
    return acc


def _rms(x, g):
    return x * lax.rsqrt(jnp.mean(x * x, axis=-1, keepdims=True) + EPS) * g


def _sigmoid(x):
    return 1.0 / (1.0 + jnp.exp(-x))


def _silu(x):
    return x * _sigmoid(x)


def _inproj_kernel(x_ref, g_ref, w_ref, wvt_ref, scale_ref,
                   qk_ref, vt_ref, gdn_ref, gla_ref, misc_ref, kmean_ref):
    tm = x_ref.shape[0]
    xn = _rms(x_ref[...], g_ref[...]).astype(BF16)
    qk = _dot(xn, w_ref[:, 0:N_QK])
    k_moba = qk[:, GROUP_WIDTH:2 * GROUP_WIDTH]
    kmean_ref[0] = jnp.mean(k_moba.reshape(tm // MOBA_BLOCK, MOBA_BLOCK, GROUP_WIDTH), axis=1)
    qk_ref[...] = (qk * scale_ref[...]).astype(BF16)
    vt = _dot_nt(wvt_ref[...], xn).astype(BF16)
    for i in range(tm // ATTN_TILE):
        vt_ref[0, i] = vt[:, i * ATTN_TILE:(i + 1) * ATTN_TILE]
    o = N_QK
    gdn_ref[...] = _dot(xn, w_ref[:, o:o + N_GDN])
    o += N_GDN
    gla_ref[...] = _dot(xn, w_ref[:, o:o + N_GLA])
    o += N_GLA
    misc_ref[...] = _dot(xn, w_ref[:, o:o + N_MISC])


def _inproj(h2d, g, layer, w, wvt, scale, tm, seq):
    m, d = h2d.shape
    n_all = w.shape[2]
    per_seq = seq // tm
    const = lambda i: (0, 0)
    return pl.pallas_call(
        _inproj_kernel,
        grid=(m // tm,),
        in_specs=[
            pl.BlockSpec((tm, d), lambda i: (i, 0)),
            pl.BlockSpec((1, d), const),
            pl.BlockSpec((None, d, n_all), lambda i: (layer, 0, 0)),
            pl.BlockSpec((None, N_VT, d), lambda i: (layer, 0, 0)),
            pl.BlockSpec((1, N_QK), const),
        ],
        out_specs=[
            pl.BlockSpec((tm, N_QK), lambda i: (i, 0)),
            pl.BlockSpec((1, tm // ATTN_TILE, N_VT, ATTN_TILE),
                         lambda i: (i // per_seq, i % per_seq, 0, 0)),
            pl.BlockSpec((tm, N_GDN), lambda i: (i, 0)),
            pl.BlockSpec((tm, N_GLA), lambda i: (i, 0)),
            pl.BlockSpec((tm, N_MISC), lambda i: (i, 0)),
            pl.BlockSpec((1, tm // MOBA_BLOCK, GROUP_WIDTH), lambda i: (i, 0, 0)),
        ],
        out_shape=[
            jax.ShapeDtypeStruct((m, N_QK), BF16),
            jax.ShapeDtypeStruct((m // seq, seq // ATTN_TILE, N_VT, ATTN_TILE), BF16),
            jax.ShapeDtypeStruct((m, N_GDN), F32),
            jax.ShapeDtypeStruct((m, N_GLA), F32),
            jax.ShapeDtypeStruct((m, N_MISC), F32),
            jax.ShapeDtypeStruct((m // tm, tm // MOBA_BLOCK, GROUP_WIDTH), F32),
        ],
        compiler_params=pltpu.CompilerParams(
            dimension_semantics=("arbitrary",), vmem_limit_bytes=VMEM_LIMIT),
        name="inproj",
    )(h2d, g, w, wvt, scale)


def _lane_mask(lo, width):
    lane = lax.broadcasted_iota(jnp.int32, (1, LANES), 1)
    return (lane >= lo) & (lane < lo + width)


def _values_with_ones(vt, hh):
    head = vt[HEAD_DIM * hh:HEAD_DIM * (hh + 1), :]
    return jnp.concatenate([head, jnp.ones((ACC_ROWS - HEAD_DIM, vt.shape[1]), vt.dtype)], axis=0)


def _tile_max(scores):
    return [jnp.max(s, axis=0, keepdims=True) for s in scores]


def _softmax_stage(ms, scores, tile_maxes, valids=None):
    ms_new, alphas, ps = [], [], []
    for c, (m, s, m_tile) in enumerate(zip(ms, scores, tile_maxes)):
        if valids is None or valids[c] is None:
            m_new = jnp.maximum(m, m_tile)
            shift = m_new
        else:
            ok = valids[c] > 0.5
            m_new = jnp.maximum(m, jnp.where(ok, m_tile, NEG))
            shift = jnp.where(ok, m_new, -NEG)
        ms_new.append(m_new)
        alphas.append(jnp.exp2(m - m_new))
        ps.append(jnp.exp2(s - shift).astype(BF16))
    return ms_new, alphas, ps


def _value_products(vts, ps):
    return [_dot(vt, p) for vt, p in zip(vts, ps)]


def _attend_all(n, t, n_far, far_scores, far_valid, prev_scores, prev_valid, own_scores, values,
                j_prev, j_own, s_ref, p_ref, acc_ref, first_scores=None):
    chains = range(n)

    def tile_of(pos):
        return jnp.where(pos == 0, j_prev, jnp.where(pos == 1, j_own, pos - 2))

    def valid_of(pos):
        rows = far_valid(jnp.maximum(pos - 2, 0))
        if rows is None:
            return None
        return [jnp.where(pos == 1, 1.0, row) for row in rows]

    s0, s1 = first_scores if first_scores is not None else (prev_scores(), own_scores())
    ms, alphas, ps = _softmax_stage([jnp.full((1, t), NEG, F32)] * n, s0, _tile_max(s0),
                                    prev_valid())
    tmax = _tile_max(s1)
    for c in chains:
        p_ref[c] = ps[c]
        s_ref[c] = s1[c]
        acc_ref[c] = jnp.zeros((ACC_ROWS, t), F32)

    def steps(first, unroll, carry):
        ms, alphas, tmax = carry
        ps = [p_ref[c] for c in chains]
        s_cur = [s_ref[c] for c in chains]
        for u in range(unroll):
            pos = first + u
            vts = values(tile_of(pos))
            late = 2 if unroll == 1 else 1
            pv = _value_products(vts[:n - late], ps[:n - late])
            s_new = far_scores(pos)
            pv += _value_products(vts[n - late:], ps[n - late:])
            tmax_new = _tile_max(s_new)
            ms, alphas_next, ps = _softmax_stage(ms, s_cur, tmax, valid_of(pos + 1))
            for c in chains:
                acc_ref[c] = alphas[c] * acc_ref[c] + pv[c]
            alphas, s_cur, tmax = alphas_next, s_new, tmax_new
        for c in chains:
            p_ref[c] = ps[c]
            s_ref[c] = s_cur[c]
        return ms, alphas, tmax

    n_steps = n_far
    unroll = max(1, PIPE_CHAIN_TILES // n)
    n_blocks = n_steps // unroll
    carry = lax.fori_loop(0, n_blocks, lambda i, cr: steps(i * unroll, unroll, cr),
                          (ms, alphas, tmax))
    if unroll > 1:
        carry = lax.fori_loop(n_blocks * unroll, n_steps, lambda i, cr: steps(i, 1, cr), carry)
    ms, alphas, tmax = carry

    pv = _value_products(values(tile_of(n_far)), [p_ref[c] for c in chains])
    ms, alphas1, ps1 = _softmax_stage(ms, [s_ref[c] for c in chains], tmax, valid_of(n_far + 1))
    accs = [alphas[c] * acc_ref[c] + pv[c] for c in chains]
    pv = _value_products(values(tile_of(n_far + 1)), ps1)
    return [alphas1[c] * accs[c] + pv[c] for c in chains]


def _normalised(acc):
    return acc[0:HEAD_DIM, :] / acc[HEAD_DIM:HEAD_DIM + 1, :]


def _moba_kernel(q_ref, k_ref, vt_ref, kmean_ref, bias_ref, o_ref, sel_ref, s_ref, p_ref, acc_ref):
    t = ATTN_TILE
    qi = pl.program_id(2)
    nblk = kmean_ref.shape[1]
    heads = range(q_ref.shape[2] // HEAD_DIM)
    q = q_ref[0]
    kmean = kmean_ref[0].astype(BF16)

    def pair_lanes(x, h):
        lo = LANES * (h // 2)
        return x[:, lo:lo + LANES]

    qms = []
    for h in heads:
        qp = pair_lanes(q, h)
        qms.append(jnp.where(_lane_mask(HEAD_DIM * (h % 2), HEAD_DIM), qp, jnp.zeros_like(qp)))

    def keys(j):
        return k_ref[0, pl.ds(pl.multiple_of(j * t, t), t), :]

    def far_scores(j):
        k = keys(j)
        return [_dot_nt(pair_lanes(k, h), qms[h]) for h in heads]

    jp = jnp.maximum(qi - 1, 0)

    def prev_scores():
        return [s + bias_ref[h, 1] for h, s in zip(heads, far_scores(jp))]

    def own_scores():
        return [s + bias_ref[h, 0] for h, s in zip(heads, far_scores(qi))]

    gates = [_dot_nt(pair_lanes(kmean, h), qms[h]) for h in heads]
    first_scores = prev_scores(), own_scores()

    row = lax.broadcasted_iota(jnp.int32, (nblk, t), 0)
    past = row < qi
    for h in heads:
        gate = jnp.where(past, gates[h], -jnp.inf)
        sel_t = jnp.zeros((nblk, t), F32)
        for _ in range(MOBA_TOPK):
            top = jnp.max(gate, axis=0, keepdims=True)
            first = jnp.min(jnp.where(gate == top, row, nblk), axis=0, keepdims=True)
            pick = row == first
            sel_t = jnp.where(pick, 1.0, sel_t)
            gate = jnp.where(pick, -jnp.inf, gate)
        sel_t = jnp.where(past, sel_t, 0.0)
        for j in range(nblk):
            sel_ref[h * nblk + j] = sel_t[j:j + 1, :]

    def values(j):
        vt = vt_ref[0, j]
        return [_values_with_ones(vt, h) for h in heads]

    def far_valid(j):
        return [sel_ref[h * nblk + j] for h in heads]

    accs = _attend_all(len(heads), t, jnp.maximum(qi - 1, 0), far_scores, far_valid, prev_scores,
                       lambda: far_valid(jp), own_scores, values, jp, qi, s_ref, p_ref, acc_ref,
                       first_scores)
    out_t = jnp.concatenate([_normalised(acc) for acc in accs], axis=0)
    o_ref[0] = out_t.T.astype(o_ref.dtype)


def _attn_specs(s, t, mixer, width):
    nkv = s // t
    per = GROUP_WIDTH // width
    qcol = mixer * 2 * per
    kcol = qcol + per
    vrow = mixer * per
    return [
        pl.BlockSpec((1, t, width), lambda bi, g, qi: (bi, qi, qcol + g)),
        pl.BlockSpec((1, s, width), lambda bi, g, qi: (bi, 0, kcol + g)),
        pl.BlockSpec((1, nkv, width, t), lambda bi, g, qi: (bi, 0, vrow + g, 0)),
    ]


def _attn_scratch(n_chains, t):
    return [pltpu.VMEM((n_chains, t, t), F32),
            pltpu.VMEM((n_chains, t, t), BF16),
            pltpu.VMEM((n_chains, ACC_ROWS, t), F32)]


def _moba(qk, vt, kmean, bias_t):
    b, s, _ = qk.shape
    t = ATTN_TILE
    nq = s // t
    nblk = kmean.shape[1]
    n_heads = N_HEADS_G
    width = n_heads * HEAD_DIM
    return pl.pallas_call(
        _moba_kernel,
        grid=(b, GROUP_WIDTH // width, nq),
        in_specs=_attn_specs(s, t, 0, width) + [
            pl.BlockSpec((1, nblk, width), lambda bi, g, qi: (bi, 0, g)),
            pl.BlockSpec((n_heads, 2, t, t), lambda bi, g, qi: (g, 0, 0, 0)),
        ],
        out_specs=pl.BlockSpec((1, t, width), lambda bi, g, qi: (bi, qi, g)),
        out_shape=jax.ShapeDtypeStruct((b, s, GROUP_WIDTH), BF16),
        scratch_shapes=[pltpu.VMEM((n_heads * nblk, 1, t), F32)]
        + _attn_scratch(n_heads, t),
        compiler_params=pltpu.CompilerParams(
            dimension_semantics=("arbitrary", "arbitrary", "arbitrary"),
            vmem_limit_bytes=VMEM_LIMIT),
        name="moba",
    )(qk, qk, vt, kmean, bias_t)


def _diff_kernel(lam_ref, q_ref, k_ref, vt_ref, bias_ref, gnorm_ref, o_ref, s_ref, p_ref, acc_ref):
    t = ATTN_TILE
    qi = pl.program_id(2)
    heads = range(q_ref.shape[2] // HEAD_DIM)
    chains = [(h, mm) for h in heads for mm in range(2)]
    q = q_ref[0]

    def pair_lanes(x, h):
        lo = LANES * (h // 2)
        return x[:, lo:lo + LANES]

    qms = []
    for h, mm in chains:
        qp = pair_lanes(q, h)
        mask = _lane_mask(HEAD_DIM * (h % 2) + DIFF_QK_DIM * mm, DIFF_QK_DIM)
        qms.append(jnp.where(mask, qp, jnp.zeros_like(qp)))

    def keys(j):
        return k_ref[0, pl.ds(pl.multiple_of(j * t, t), t), :]

    def values(j):
        vt = vt_ref[0, j]
        vts = [_values_with_ones(vt, h) for h in heads]
        return [vts[h] for h, _ in chains]

    def far_scores(j):
        k = keys(j)
        return [_dot_nt(pair_lanes(k, h), qm) for (h, _), qm in zip(chains, qms)]

    jp = jnp.maximum(qi - 1, 0)

    def prev_scores():
        return [s + bias_ref[h, 1] for (h, _), s in zip(chains, far_scores(jp))]

    def prev_valid():
        return [jnp.full((1, t), jnp.where(qi >= 1, 1.0, 0.0), F32)] * len(chains)

    def own_scores():
        return [s + bias_ref[h, 0] for (h, _), s in zip(chains, far_scores(qi))]

    lam_p = lam_ref[...]
    lam_init = lam_p[4:5, 0:1]
    lam = (jnp.exp(jnp.sum(lam_p[0:1] * lam_p[1:2], axis=-1, keepdims=True))
           - jnp.exp(jnp.sum(lam_p[2:3] * lam_p[3:4], axis=-1, keepdims=True)) + lam_init)

    accs = _attend_all(len(chains), t, jnp.maximum(qi - 1, 0), far_scores, lambda j: None,
                       prev_scores, prev_valid, own_scores, values, jp, qi, s_ref, p_ref, acc_ref)
    outs = []
    for h in heads:
        o = _normalised(accs[2 * h]) - lam * _normalised(accs[2 * h + 1])
        ms = jnp.mean(o * o, axis=0, keepdims=True)
        outs.append(o * lax.rsqrt(ms + EPS))
    y_t = jnp.concatenate(outs, axis=0)
    o_ref[0] = (y_t.T * gnorm_ref[...] * (1.0 - lam_init)).astype(o_ref.dtype)


def _diff(qk, vt, lam_p, bias_t, gnorm):
    b, s, _ = qk.shape
    t = ATTN_TILE
    nq = s // t
    n_heads = N_HEADS_G
    width = n_heads * HEAD_DIM
    return pl.pallas_call(
        _diff_kernel,
        grid=(b, GROUP_WIDTH // width, nq),
        in_specs=[pl.BlockSpec(lam_p.shape, lambda bi, g, qi: (0, 0))]
        + _attn_specs(s, t, 1, width) + [
            pl.BlockSpec((n_heads, 2, t, t), lambda bi, g, qi: (g, 0, 0, 0)),
            pl.BlockSpec((1, width), lambda bi, g, qi: (0, g)),
        ],
        out_specs=pl.BlockSpec((1, t, width), lambda bi, g, qi: (bi, qi, g)),
        out_shape=jax.ShapeDtypeStruct((b, s, GROUP_WIDTH), BF16),
        scratch_shapes=_attn_scratch(2 * n_heads, t),
        compiler_params=pltpu.CompilerParams(
            dimension_semantics=("arbitrary", "arbitrary", "arbitrary"),
            vmem_limit_bytes=VMEM_LIMIT),
        name="diff_attn",
    )(lam_p, qk, qk, vt, bias_t, gnorm)


def _softmax_mixers_kernel(qm_ref, km_ref, vtm_ref, kmean_ref, biasm_ref,
                           lam_ref, qd_ref, kd_ref, vtd_ref, biasd_ref, gnorm_ref,
                           om_ref, od_ref, sel_ref, sm_ref, pm_ref, accm_ref, sd_ref, pd_ref, accd_ref):
    _moba_kernel(qm_ref, km_ref, vtm_ref, kmean_ref, biasm_ref, om_ref, sel_ref, sm_ref, pm_ref,
                 accm_ref)
    _diff_kernel(lam_ref, qd_ref, kd_ref, vtd_ref, biasd_ref, gnorm_ref, od_ref, sd_ref, pd_ref,
                 accd_ref)


def _softmax_mixers(qk, vt, kmean, bias_m, lam_p, bias_d, gnorm):
    b, s, _ = qk.shape
    t = ATTN_TILE
    nq = s // t
    nblk = kmean.shape[1]
    n_heads = N_HEADS_G
    width = n_heads * HEAD_DIM
    bias_spec = pl.BlockSpec((n_heads, 2, t, t), lambda bi, g, qi: (g, 0, 0, 0))
    out_spec = pl.BlockSpec((1, t, width), lambda bi, g, qi: (bi, qi, g))
    out = jax.ShapeDtypeStruct((b, s, GROUP_WIDTH), BF16)
    return pl.pallas_call(
        _softmax_mixers_kernel,
        grid=(b, GROUP_WIDTH // width, nq),
        in_specs=_attn_specs(s, t, 0, width) + [
            pl.BlockSpec((1, nblk, width), lambda bi, g, qi: (bi, 0, g)), bias_spec,
            pl.BlockSpec(lam_p.shape, lambda bi, g, qi: (0, 0)),
        ] + _attn_specs(s, t, 1, width) + [
            bias_spec, pl.BlockSpec((1, width), lambda bi, g, qi: (0, g)),
        ],
        out_specs=[out_spec, out_spec],
        out_shape=[out, out],
        scratch_shapes=[pltpu.VMEM((n_heads * nblk, 1, t), F32)]
        + _attn_scratch(n_heads, t) + _attn_scratch(2 * n_heads, t),
        compiler_params=pltpu.CompilerParams(
            dimension_semantics=("arbitrary", "arbitrary", "arbitrary"),
            vmem_limit_bytes=VMEM_LIMIT),
        name="softmax_mixers",
    )(qk, qk, vt, kmean, bias_m, lam_p, qk, qk, vt, bias_d, gnorm)


def _head_of_lane(n_lanes, width):
    return lax.broadcasted_iota(jnp.int32, (1, n_lanes), 1) // width


def _block_rows(x, lane_head, n_heads=N_HEADS_G):
    zero = jnp.zeros_like(x)
    return jnp.concatenate([jnp.where(lane_head == h, x, zero) for h in range(n_heads)], axis=0)


def _gdn_kernel(x_ref, misc_ref, conv_ref, hp_ref, gnorm_ref, o_ref, s_ref, tail_ref, g_ref, beta_ref):
    L = CHUNK
    W = GROUP_WIDTH
    rows = range(x_ref.shape[0])
    seq = x_ref.shape[1]
    n_chunks = seq // L
    head_w = _head_of_lane(W, HEAD_DIM)
    ri = lax.broadcasted_iota(jnp.int32, (L, W), 0)
    cj = lax.broadcasted_iota(jnp.int32, (L, W), 1) % HEAD_DIM
    lower = ri >= cj
    strict = ri > cj
    ident = jnp.where(ri == cj, 1.0, 0.0)
    ones_blk = (lax.broadcasted_iota(jnp.int32, (W, W), 0) // HEAD_DIM
                == lax.broadcasted_iota(jnp.int32, (W, W), 1) // HEAD_DIM)
    ones_seg = ones_blk.astype(BF16)
    tril = (lax.broadcasted_iota(jnp.int32, (L, L), 0)
            >= lax.broadcasted_iota(jnp.int32, (L, L), 1)).astype(BF16)

    hp = hp_ref[...]
    exp_g = (lax.broadcasted_iota(jnp.int32, (LANES, W), 0) - MISC_A
             == lax.broadcasted_iota(jnp.int32, (LANES, W), 1) // HEAD_DIM).astype(BF16)
    exp_b = (lax.broadcasted_iota(jnp.int32, (LANES, W), 0) - MISC_B
             == lax.broadcasted_iota(jnp.int32, (LANES, W), 1) // HEAD_DIM).astype(BF16)
    for r in rows:
        misc = misc_ref[r]
        sp_in = misc + hp[1:2]
        softplus = jnp.maximum(sp_in, 0.0) + jnp.log(1.0 + jnp.exp(-jnp.abs(sp_in)))
        g_tok = -jnp.exp(hp[0:1]) * softplus
        b_tok = _sigmoid(misc)
        g_ref[r] = _dot_exact_rhs(g_tok, exp_g, 2)
        beta_ref[r] = _dot_exact_rhs(b_tok, exp_b, 2)

    @pl.when(pl.program_id(1) == 0)
    def _():
        s_ref[...] = jnp.zeros_like(s_ref)
        tail_ref[...] = jnp.zeros_like(tail_ref)

    cw = conv_ref[...]
    blk = lambda a: _block_rows(a.astype(BF16), head_w)
    bf = lambda a: a.astype(BF16)

    def prepare(r, r0):
        x = x_ref[r, pl.ds(r0, L), :]
        qkv = x[:, 0:3 * W]
        xx = jnp.concatenate([tail_ref[r], qkv], axis=0)
        tail_ref[r] = qkv[L - F32_SUBLANES:L, :]
        conv = cw[CONV_WIDTH - 1:CONV_WIDTH] * qkv
        for i in range(CONV_WIDTH - 1):
            lo = F32_SUBLANES - (CONV_WIDTH - 1) + i
            conv = conv + cw[i:i + 1] * xx[lo:lo + L, :]
        conv = _silu(conv)
        return conv[:, 0:W], conv[:, W:2 * W], conv[:, 2 * W:3 * W], x[:, 3 * W:4 * W]

    def decays(gc):
        gc_col = jnp.sum(gc * ident, axis=0, keepdims=True)
        return jnp.exp(jnp.where(lower, gc - gc_col, -jnp.inf))

    def chunk(c, _):
        r0 = pl.multiple_of(c * L, L)
        q, k, v, z = zip(*[prepare(r, r0) for r in rows])
        ssq = [_dot_exact_rhs(jnp.concatenate([q[r] * q[r], k[r] * k[r]], axis=0), ones_seg, 1)
               for r in rows]
        q = [q[r] * lax.rsqrt(ssq[r][0:L] + EPS) * (HEAD_DIM ** -0.5) for r in rows]
        k = [k[r] * lax.rsqrt(ssq[r][L:2 * L] + EPS) for r in rows]
        beta = [beta_ref[r, pl.ds(r0, L), :] for r in rows]
        gc = [_dot_exact_lhs(tril, g_ref[r, pl.ds(r0, L), :], 2) for r in rows]
        egc = [jnp.exp(gc[r]) for r in rows]
        gc_last = [gc[r][L - 1:L, :] for r in rows]
        decay = [decays(gc[r]) for r in rows]
        kb = [k[r] * beta[r] for r in rows]
        both = [_dot_nt(bf(jnp.concatenate([kb[r], q[r]], axis=0)), blk(k[r])) for r in rows]
        qk = [jnp.where(lower, both[r][L:2 * L] * decay[r], 0.0) for r in rows]

        p = [jnp.where(strict, -both[r][0:L] * decay[r], 0.0) for r in rows]
        t_inv = [ident + p[r] for r in rows]
        p = [_dot(bf(p[r]), blk(p[r])) for r in rows]
        for _ in range(4):
            prod = [_dot(bf(jnp.concatenate([t_inv[r], p[r]], axis=0)), blk(p[r])) for r in rows]
            t_inv = [t_inv[r] + prod[r][0:L] for r in rows]
            p = [prod[r][L:2 * L] for r in rows]
        t_inv = [t_inv[r] + _dot(bf(t_inv[r]), blk(p[r])) for r in rows]

        u = [_dot(bf(t_inv[r]), blk(v[r] * beta[r])) for r in rows]
        w = [_dot(bf(t_inv[r]), blk(kb[r] * egc[r])) for r in rows]

        state = [s_ref[r] for r in rows]
        ws_qs = [_dot(bf(jnp.concatenate([w[r], q[r] * egc[r]], axis=0)), bf(state[r])) for r in rows]
        v_new = [u[r] - ws_qs[r][0:L] for r in rows]
        o = [ws_qs[r][L:2 * L] + _dot(bf(qk[r]), blk(v_new[r])) for r in rows]
        upd = [_dot_tn(bf(k[r] * jnp.exp(gc_last[r] - gc[r])), bf(v_new[r])) for r in rows]
        for r in rows:
            s_ref[r] = state[r] * jnp.exp(gc_last[r]) + jnp.where(ones_blk, upd[r], 0.0)

        ms = [_dot_exact_rhs(o[r] * o[r], ones_seg, 1) * (1.0 / HEAD_DIM) for r in rows]
        for r in rows:
            y = o[r] * lax.rsqrt(ms[r] + EPS) * gnorm_ref[...] * _silu(z[r])
            o_ref[r, pl.ds(r0, L), :] = y.astype(o_ref.dtype)
        return 0

    lax.fori_loop(0, n_chunks, chunk, 0)


def _gdn(gdn, misc, conv_w, hp, gnorm):
    b, s, _ = gdn.shape
    ts = min(s, REC_SEQ_TILE)
    nr = math.gcd(b, REC_ROWS)
    const = lambda bi, si: (0, 0)
    return pl.pallas_call(
        _gdn_kernel,
        grid=(b // nr, s // ts),
        in_specs=[
            pl.BlockSpec((nr, ts, N_GDN), lambda bi, si: (bi, si, 0)),
            pl.BlockSpec((nr, ts, N_MISC), lambda bi, si: (bi, si, 0)),
            pl.BlockSpec(conv_w.shape, const),
            pl.BlockSpec(hp.shape, const),
            pl.BlockSpec((1, GROUP_WIDTH), const),
        ],
        out_specs=pl.BlockSpec((nr, ts, GROUP_WIDTH), lambda bi, si: (bi, si, 0)),
        out_shape=jax.ShapeDtypeStruct((b, s, GROUP_WIDTH), BF16),
        scratch_shapes=[
            pltpu.VMEM((nr, GROUP_WIDTH, GROUP_WIDTH), F32),
            pltpu.VMEM((nr, F32_SUBLANES, 3 * GROUP_WIDTH), F32),
            pltpu.VMEM((nr, ts, GROUP_WIDTH), F32),
            pltpu.VMEM((nr, ts, GROUP_WIDTH), F32),
        ],
        compiler_params=pltpu.CompilerParams(
            dimension_semantics=("arbitrary", "arbitrary"), vmem_limit_bytes=VMEM_LIMIT),
        name="gdn",
    )(gdn, misc, conv_w, hp, gnorm)


def _gla_kernel(x_ref, misc_ref, walpha_ref, balpha_ref, gnorm_ref, o_ref, s_ref, la_ref):
    L = CHUNK
    W = GROUP_WIDTH
    KW = N_HEADS_G * GLA_DK
    rows = range(x_ref.shape[0])
    seq = x_ref.shape[1]
    n_chunks = seq // L
    head_k = _head_of_lane(KW, GLA_DK)
    head_v = _head_of_lane(W, HEAD_DIM)
    ri = lax.broadcasted_iota(jnp.int32, (L, W), 0)
    cj = lax.broadcasted_iota(jnp.int32, (L, W), 1) % HEAD_DIM
    lower = ri >= cj
    ones_blk = (lax.broadcasted_iota(jnp.int32, (W, W), 0) // HEAD_DIM
                == lax.broadcasted_iota(jnp.int32, (W, W), 1) // HEAD_DIM).astype(BF16)
    state_mask = (lax.broadcasted_iota(jnp.int32, (W, KW), 0) // HEAD_DIM
                  == lax.broadcasted_iota(jnp.int32, (W, KW), 1) // GLA_DK)
    tril = (lax.broadcasted_iota(jnp.int32, (L, L), 0)
            >= lax.broadcasted_iota(jnp.int32, (L, L), 1)).astype(BF16)

    for r in rows:
        pre = _dot(misc_ref[r].astype(BF16), walpha_ref[...]) + balpha_ref[...]
        log_sig = jnp.minimum(pre, 0.0) - jnp.log(1.0 + jnp.exp(-jnp.abs(pre)))
        la_ref[r] = log_sig * (1.0 / GLA_TAU)

    @pl.when(pl.program_id(1) == 0)
    def _():
        s_ref[...] = jnp.zeros_like(s_ref)

    bf = lambda a: a.astype(BF16)

    def chunk(c, _):
        r0 = pl.multiple_of(c * L, L)
        x = [x_ref[r, pl.ds(r0, L), :] for r in rows]
        k = [x[r][:, KW:2 * KW] for r in rows]
        v = [bf(x[r][:, 2 * KW:2 * KW + W]) for r in rows]
        bc = [_dot_exact_lhs(tril, la_ref[r, pl.ds(r0, L), :], 2) for r in rows]
        b_last = [bc[r][L - 1:L, :] for r in rows]
        q = [x[r][:, 0:KW] * (GLA_DK ** -0.5) for r in rows]
        qe = [bf(q[r] * jnp.exp(bc[r])) for r in rows]
        b_mid = [bc[r] - bc[r][L // 2:L // 2 + 1, :] for r in rows]
        qm = [bf(q[r] * jnp.exp(b_mid[r])) for r in rows]
        km = [bf(k[r] * jnp.exp(-b_mid[r])) for r in rows]
        a_mat = [jnp.where(lower, _dot_nt(qm[r], _block_rows(km[r], head_k)), 0.0) for r in rows]
        state_t = [s_ref[r] for r in rows]
        o = [_dot_nt(qe[r], bf(state_t[r])) for r in rows]
        o = [o[r] + _dot(bf(a_mat[r]), _block_rows(v[r], head_v)) for r in rows]
        upd = [_dot_tn(v[r], bf(k[r] * jnp.exp(b_last[r] - bc[r]))) for r in rows]
        for r in rows:
            s_ref[r] = state_t[r] * jnp.exp(b_last[r]) + jnp.where(state_mask, upd[r], 0.0)

        ms = [_dot_exact_rhs(o[r] * o[r], ones_blk, 1) * (1.0 / HEAD_DIM) for r in rows]
        for r in rows:
            gate = _silu(x[r][:, 2 * KW + W:2 * KW + 2 * W])
            y = o[r] * lax.rsqrt(ms[r] + EPS) * gnorm_ref[...] * gate
            o_ref[r, pl.ds(r0, L), :] = y.astype(o_ref.dtype)
        return 0

    lax.fori_loop(0, n_chunks, chunk, 0)


def _gla(gla, misc, walpha, balpha, gnorm):
    b, s, _ = gla.shape
    ts = min(s, REC_SEQ_TILE)
    nr = math.gcd(b, REC_ROWS)
    const = lambda bi, si: (0, 0)
    return pl.pallas_call(
        _gla_kernel,
        grid=(b // nr, s // ts),
        in_specs=[
            pl.BlockSpec((nr, ts, N_GLA), lambda bi, si: (bi, si, 0)),
            pl.BlockSpec((nr, ts, N_MISC), lambda bi, si: (bi, si, 0)),
            pl.BlockSpec(walpha.shape, const),
            pl.BlockSpec(balpha.shape, const),
            pl.BlockSpec((1, GROUP_WIDTH), const),
        ],
        out_specs=pl.BlockSpec((nr, ts, GROUP_WIDTH), lambda bi, si: (bi, si, 0)),
        out_shape=jax.ShapeDtypeStruct((b, s, GROUP_WIDTH), BF16),
        scratch_shapes=[
            pltpu.VMEM((nr, GROUP_WIDTH, N_HEADS_G * GLA_DK), F32),
            pltpu.VMEM((nr, ts, N_HEADS_G * GLA_DK), F32),
        ],
        compiler_params=pltpu.CompilerParams(
            dimension_semantics=("arbitrary", "arbitrary"), vmem_limit_bytes=VMEM_LIMIT),
        name="gla",
    )(gla, misc, walpha, balpha, gnorm)


def _ffn_chunks(d_ff):
    step = 4 * GROUP_WIDTH
    return [(lo, min(step, d_ff - lo)) for lo in range(0, d_ff, step)]


def _mlp_kernel(h_ref, ya_ref, yb_ref, yc_ref, yd_ref, wout_ref, gffn_ref, wg_ref, wu_ref, wd_ref,
                gple_ref, wpg_ref, p_ref, wpp_ref, gfin_ref, o_ref, act_ref, *, final):
    mixed = jnp.concatenate([ya_ref[...], yb_ref[...], yc_ref[...], yd_ref[...]], axis=-1)
    h = h_ref[...] + _dot(mixed, wout_ref[...])
    hn = _rms(h, gffn_ref[...]).astype(BF16)
    for lo, width in _ffn_chunks(wg_ref.shape[1]):
        gate = _dot(hn, wg_ref[:, lo:lo + width])
        up = _dot(hn, wu_ref[:, lo:lo + width])
        act_ref[:, lo:lo + width] = (_silu(gate) * up).astype(BF16)
    h = h + _dot(act_ref[...], wd_ref[...])
    gate = _sigmoid(_dot(_rms(h, gple_ref[...]).astype(BF16), wpg_ref[...]))
    h = h + gate * _dot(p_ref[...].astype(BF16), wpp_ref[...])
    if final:
        h = _rms(h, gfin_ref[...])
    o_ref[...] = h


def _mlp(h2d, ys, layer, wout, gffn, wg, wu, wd, gple, wpg, p3d, wpp, gfin, tm, final):
    m, d = h2d.shape
    d_ff = wg.shape[2]
    const = lambda i: (0, 0)
    resident = lambda shape: pl.BlockSpec(shape, const, pipeline_mode=pl.Buffered(1))
    stacked = lambda a: pl.BlockSpec((None,) + a.shape[1:], lambda i: (layer, 0, 0),
                                     pipeline_mode=pl.Buffered(1))
    rows = lambda width: pl.BlockSpec((tm, width), lambda i: (i, 0))
    return pl.pallas_call(
        functools.partial(_mlp_kernel, final=final),
        grid=(m // tm,),
        in_specs=[rows(d)] + [rows(GROUP_WIDTH)] * 4 + [
            stacked(wout), resident(gffn.shape), stacked(wg), stacked(wu),
            stacked(wd), resident(gple.shape), stacked(wpg),
            pl.BlockSpec((None, tm, p3d.shape[2]), lambda i: (layer, i, 0)), stacked(wpp),
            resident(gfin.shape),
        ],
        out_specs=rows(d),
        out_shape=jax.ShapeDtypeStruct((m, d), F32),
        scratch_shapes=[pltpu.VMEM((tm, d_ff), BF16)],
        compiler_params=pltpu.CompilerParams(
            dimension_semantics=("arbitrary",), vmem_limit_bytes=VMEM_LIMIT),
        name="mlp",
    )(h2d, *ys, wout, gffn, wg, wu, wd, gple, wpg, p3d, wpp, gfin)


def _rel_bucket_table(n_dist):
    n = np.arange(n_dist)
    nf = np.maximum(n, 1).astype(np.float32)
    large = REL_MAX_EXACT + (np.log(nf / REL_MAX_EXACT) / math.log(REL_MAX_DIST / REL_MAX_EXACT)
                             * (N_REL_BUCKETS - REL_MAX_EXACT)).astype(np.int32)
    large = np.minimum(large, N_REL_BUCKETS - 1)
    return np.where(n < REL_MAX_EXACT, n, large)


def _bias_tiles(rel_bias):
    t = ATTN_TILE
    bucket = _rel_bucket_table(2 * t)
    assert (bucket[t + 1:] == N_REL_BUCKETS - 1).all()
    per_dist = (rel_bias[bucket, :] - rel_bias[N_REL_BUCKETS - 1][None, :]).T * LOG2E
    own = jnp.concatenate([per_dist[:, :t], jnp.full_like(per_dist[:, :t], NEG)], axis=1)
    prev = jnp.concatenate([per_dist[:, t:], per_dist[:, :t]], axis=1)
    vec = jnp.stack([own, prev], axis=1)
    rolled = jnp.tile(vec, (1, 1, t))[:, :, :t * (2 * t - 1)].reshape(-1, 2, t, 2 * t - 1)
    return rolled[:, :, :, :t]


def _row(v, width=None):
    v = v.astype(F32).reshape(1, -1)
    if width is not None and v.shape[1] < width:
        v = jnp.pad(v, ((0, 0), (0, width - v.shape[1])))
    return v


def kernel(x, p, norm_mix, w_in, rel_bias, diff_lambda, diff_norm, gdn_conv, gdn_a_log, gdn_dt_bias,
           gdn_norm, gla_w_alpha, gla_b_alpha, gla_norm, w_out, norm_ffn, w_gate, w_up, w_down,
           norm_ple, w_ple_gate, w_ple_proj, final_norm):
    b, s, d = x.shape
    depth = w_in.shape[0]
    m = b * s
    tm = ROW_TILE
    hg = N_HEADS_G
    assert ATTN_TILE == MOBA_BLOCK
    assert s % tm == 0 and tm % ATTN_TILE == 0 and s >= 2 * ATTN_TILE
    assert s % REC_SEQ_TILE == 0 and REC_SEQ_TILE % CHUNK == 0

    bias_t = _bias_tiles(rel_bias.astype(F32))
    bias_a, bias_b = bias_t[:hg], bias_t[hg:]
    gw = GROUP_WIDTH
    col_scale = np.ones((1, N_QK), np.float32)
    col_scale[:, 0:gw] = HEAD_DIM ** -0.5
    col_scale[:, gw:2 * gw] = LOG2E
    col_scale[:, 2 * gw:3 * gw] = DIFF_QK_DIM ** -0.5 * LOG2E
    col_scale = jnp.asarray(col_scale)

    n_main = 6 * gw + N_GDN
    ab = 2 * hg
    gla_lo = n_main + ab
    gla_hi = gla_lo + N_GLA
    fin = _row(final_norm)
    w_out_bf, w_gate_bf, w_up_bf, w_down_bf, w_ple_gate_bf, w_ple_proj_bf = (
        w.astype(BF16) for w in (w_out, w_gate, w_up, w_down, w_ple_gate, w_ple_proj))
    p3d = p.reshape(depth, m, p.shape[-1])
    w1 = jnp.concatenate(
        [w_in[:, :, 0:2 * gw], w_in[:, :, 3 * gw:5 * gw], w_in[:, :, 6 * gw:n_main],
         w_in[:, :, gla_lo:gla_hi], w_in[:, :, n_main:gla_lo], w_in[:, :, gla_hi:],
         jnp.zeros((depth, d, N_MISC - ab - GLA_GATE_RANK), w_in.dtype)], axis=2).astype(BF16)
    wvt = jnp.swapaxes(jnp.concatenate([w_in[:, :, 2 * gw:3 * gw], w_in[:, :, 5 * gw:6 * gw]],
                                       axis=2), 1, 2).astype(BF16)
    h = x.reshape(m, d)
    for l in range(depth):
        qk, vt, gdn, gla, misc, kmean = _inproj(h, _row(norm_mix[l]), l, w1, wvt, col_scale, tm, s)
        qk = qk.reshape(b, s, N_QK)
        misc = misc.reshape(b, s, N_MISC)
        kmean = kmean.reshape(b, s // MOBA_BLOCK, GROUP_WIDTH)

        lam_init = 0.8 - 0.6 * math.exp(-0.3 * l)
        lam_p = jnp.concatenate([diff_lambda[l].astype(F32),
                                 jnp.full((1, DIFF_QK_DIM), lam_init, F32)], axis=0)
        y_a, y_b = _softmax_mixers(qk, vt, kmean, bias_a, lam_p, bias_b,
                                   _row(jnp.tile(diff_norm[l], hg)))

        hp = jnp.concatenate([_row(gdn_a_log[l], LANES), _row(gdn_dt_bias[l], LANES)], axis=0)
        y_c = _gdn(gdn.reshape(b, s, N_GDN), misc, gdn_conv[l].astype(F32), hp,
                   _row(jnp.tile(gdn_norm[l], hg)))

        walpha = jnp.zeros((N_MISC, hg * GLA_DK), F32).at[MISC_LR:MISC_LR + GLA_GATE_RANK].set(
            gla_w_alpha[l]).astype(BF16)
        y_d = _gla(gla.reshape(b, s, N_GLA), misc, walpha, _row(gla_b_alpha[l]),
                   _row(jnp.tile(gla_norm[l], hg)))

        ys = [y.reshape(m, GROUP_WIDTH) for y in (y_a, y_b, y_c, y_d)]
        h = _mlp(h, ys, l, w_out_bf, _row(norm_ffn[l]), w_gate_bf, w_up_bf, w_down_bf,
                 _row(norm_ple[l]), w_ple_gate_bf, p3d, w_ple_proj_bf, fin, tm,
                 final=(l == depth - 1))
    return h.reshape(b, s, d)
```
